```python
import math
import jax
import jax.numpy as jnp
from jax import lax
import numpy as np

D_MODEL = 1024
BATCH = 2
SEQ = 8192
DEPTH = 2

CTX_LEN = 256
GRID_W = 64
HEAD_DIM = 64
BRANCH_W = 512
N_BRANCH = 4
BLOCK = 128
A_HEADS = 8
A_KV = 2
A_WINDOW = 128
B_HEADS = 8
NB_ROWS = 8
NB_COLS = 16
C_HEADS = 8
C_KV = 2
D_HEADS = 4
D_HEAD_DIM = 64
ROPE_THETA = 10000.0
EPS = 1e-6
NEG_INF = -1e30

SPLIT_SIZES = (
    A_HEADS * HEAD_DIM, A_KV * HEAD_DIM, A_KV * HEAD_DIM, BRANCH_W,
    B_HEADS * HEAD_DIM, B_HEADS * HEAD_DIM, B_HEADS * HEAD_DIM, BRANCH_W,
    C_HEADS * HEAD_DIM, C_KV * HEAD_DIM, C_KV * HEAD_DIM, BRANCH_W,
    D_HEADS * 2 * D_HEAD_DIM, D_HEADS * 2 * D_HEAD_DIM, D_HEADS * 2 * D_HEAD_DIM, BRANCH_W,
    N_BRANCH * D_MODEL,
)
SPLIT_POINTS = tuple(sum(SPLIT_SIZES[: i + 1]) for i in range(len(SPLIT_SIZES) - 1))
IN_COLS = sum(SPLIT_SIZES)

kernel_name = "hybrid_gated_dit_block"


def rms_norm(x, gain):
    xf = x.astype(jnp.float32)
    y = xf * lax.rsqrt(jnp.mean(xf * xf, axis=-1, keepdims=True) + EPS)
    return (y * gain.astype(jnp.float32)).astype(x.dtype)


def split_heads(t, n, tail=(HEAD_DIM,)):
    return t.reshape(t.shape[:2] + (n,) + tuple(tail))


def axial_rope_tables(n, dtype):
    t = jnp.arange(n, dtype=jnp.int32)
    pos = jnp.stack([t // GRID_W, t % GRID_W], axis=-1).astype(jnp.float32)
    n_freq = HEAD_DIM // 4
    freqs = ROPE_THETA ** (-jnp.arange(n_freq, dtype=jnp.float32) / n_freq)
    ang = pos[:, :, None] * freqs[None, None, :]
    ang = jnp.concatenate([ang, ang], axis=-1).reshape(n, HEAD_DIM)
    return jnp.cos(ang).astype(dtype), jnp.sin(ang).astype(dtype)


def rotate_half_axial(x):
    xa = x.reshape(x.shape[:-1] + (2, 2, HEAD_DIM // 4))
    return jnp.stack([-xa[..., 1, :], xa[..., 0, :]], axis=-2).reshape(x.shape)


def apply_rope(x, cos, sin):
    shp = (1, x.shape[1]) + (1,) * (x.ndim - 3) + (HEAD_DIM,)
    return x * cos.reshape(shp) + rotate_half_axial(x) * sin.reshape(shp)


def qk_prep(t, n, gain, cos=None, sin=None, tail=(HEAD_DIM,)):
    t = rms_norm(split_heads(t, n, tail), gain)
    if cos is not None:
        t = apply_rope(t, cos, sin)
    return t


def attend_sets(q, kv_sets, sink=None):
    scale = HEAD_DIM ** -0.5
    logits = [jnp.einsum('bqkgd,bjkd->bkgqj', q, k, preferred_element_type=jnp.float32) * scale
              for k, _ in kv_sets]
    if sink is not None:
        kv, g = q.shape[2], q.shape[3]
        logits.append(jnp.broadcast_to(sink.astype(jnp.float32).reshape(1, kv, g, 1, 1),
                                       logits[0].shape[:-1] + (1,)))
    p = jax.nn.softmax(jnp.concatenate(logits, axis=-1), axis=-1)
    out, off = None, 0
    for k, v in kv_sets:
        n = k.shape[1]
        term = jnp.einsum('bkgqj,bjkd->bqkgd', p[..., off:off + n].astype(v.dtype), v)
        out = term if out is None else out + term
        off += n
    return out


def diff_attend_sets(q, kv_sets, lam):
    scale = D_HEAD_DIM ** -0.5
    logits = [jnp.einsum('bqhcd,bjhcd->bhcqj', q, k, preferred_element_type=jnp.float32) * scale
              for k, _ in kv_sets]
    p = jax.nn.softmax(jnp.concatenate(logits, axis=-1), axis=-1)
    pd = p[:, :, 0] - lam * p[:, :, 1]
    out, off = None, 0
    for k, v in kv_sets:
        n = k.shape[1]
        term = jnp.einsum('bhqj,bjhe->bqhe', pd[..., off:off + n].astype(v.dtype), v)
        out = term if out is None else out + term
        off += n
    return out


def diff_lambda(lam, lam_init):
    lf = lam.astype(jnp.float32)
    return jnp.exp(jnp.sum(lf[0] * lf[1])) - jnp.exp(jnp.sum(lf[2] * lf[3])) + lam_init


def finish_diff(o, gain, lam_init):
    return (rms_norm(o, gain) * (1.0 - lam_init)).reshape(o.shape[:2] + (-1,))


def window_sink_attention(q, k, v, kc, vc, sink):
    bsz, seq = q.shape[:2]
    nblk = seq // BLOCK
    g = A_HEADS // A_KV
    scale = HEAD_DIM ** -0.5
    qb = q.reshape(bsz, nblk, BLOCK, A_KV, g, HEAD_DIM)

    def band(t):
        tp = jnp.pad(t, ((0, 0), (BLOCK, BLOCK), (0, 0), (0, 0)))
        tp = tp.reshape(bsz, nblk + 2, BLOCK, A_KV, HEAD_DIM)
        return jnp.concatenate([tp[:, :-2], tp[:, 1:-1], tp[:, 2:]], axis=2)

    kw, vw = band(k), band(v)
    nw = 3 * BLOCK
    s_win = jnp.einsum('bnqkgd,bnjkd->bnkgqj', qb, kw, preferred_element_type=jnp.float32) * scale
    rel = jnp.arange(nw)[None, :] - BLOCK - jnp.arange(BLOCK)[:, None]
    kpos = jnp.arange(nblk)[:, None] * BLOCK - BLOCK + jnp.arange(nw)[None, :]
    mask = (jnp.abs(rel) <= A_WINDOW)[None] & ((kpos >= 0) & (kpos < seq))[:, None, :]
    s_win = jnp.where(mask[None, :, None, None], s_win, NEG_INF)
    s_ctx = jnp.einsum('bnqkgd,bjkd->bnkgqj', qb, kc, preferred_element_type=jnp.float32) * scale
    s_sink = jnp.broadcast_to(sink.astype(jnp.float32).reshape(1, 1, A_KV, g, 1, 1),
                              s_win.shape[:-1] + (1,))
    p = jax.nn.softmax(jnp.concatenate([s_win, s_ctx, s_sink], axis=-1), axis=-1)
    n_ctx = kc.shape[1]
    o = (jnp.einsum('bnkgqj,bnjkd->bnqkgd', p[..., :nw].astype(v.dtype), vw)
         + jnp.einsum('bnkgqj,bjkd->bnqkgd', p[..., nw:nw + n_ctx].astype(v.dtype), vc))
    return o.reshape(bsz, seq, A_HEADS * HEAD_DIM)


def neighborhood_attention(q, k, v, kc, vc, rpb):
    bsz, seq = q.shape[:2]
    rows = seq // GRID_W
    kr = min(NB_ROWS, rows)
    scale = HEAD_DIM ** -0.5
    qg = q.reshape(bsz, rows, GRID_W, B_HEADS, HEAD_DIM)
    r = jnp.arange(rows)
    rstart = jnp.clip(r - kr // 2, 0, rows - kr)
    ridx = rstart[:, None] + jnp.arange(kr)[None, :]
    krows = k.reshape(bsz, rows, GRID_W, B_HEADS, HEAD_DIM)[:, ridx]
    vrows = v.reshape(bsz, rows, GRID_W, B_HEADS, HEAD_DIM)[:, ridx]
    dr = ridx - r[:, None] + (NB_ROWS - 1)
    span = 2 * NB_COLS
    n_win = kr * span
    outs = []
    for c0 in range(0, GRID_W, NB_COLS):
        start = min(max(c0 - NB_COLS // 2, 0), GRID_W - span)
        qcol = c0 + jnp.arange(NB_COLS)
        kcol = start + jnp.arange(span)
        cstart = jnp.clip(qcol - NB_COLS // 2, 0, GRID_W - NB_COLS)
        colmask = (kcol[None, :] >= cstart[:, None]) & (kcol[None, :] < cstart[:, None] + NB_COLS)
        dc = jnp.clip(kcol[None, :] - qcol[:, None], -(NB_COLS - 1), NB_COLS - 1) + (NB_COLS - 1)
        bias = rpb[:, dr[:, None, :, None], dc[None, :, None, :]]
        bias = jnp.moveaxis(bias, 0, 1).astype(jnp.float32)
        qb = qg[:, :, c0:c0 + NB_COLS]
        kb = krows[:, :, :, start:start + span]
        vb = vrows[:, :, :, start:start + span]
        s = jnp.einsum('brqhd,brkjhd->brhqkj', qb, kb, preferred_element_type=jnp.float32) * scale
        s = jnp.where(colmask[:, None, :], s + bias[None], NEG_INF)
        s = s.reshape(bsz, rows, B_HEADS, NB_COLS, n_win)
        s_ctx = jnp.einsum('brqhd,bjhd->brhqj', qb, kc, preferred_element_type=jnp.float32) * scale
        p = jax.nn.softmax(jnp.concatenate([s, s_ctx], axis=-1), axis=-1)
        pw = p[..., :n_win].reshape(bsz, rows, B_HEADS, NB_COLS, kr, span).astype(v.dtype)
        o = (jnp.einsum('brhqkj,brkjhd->brqhd', pw, vb)
             + jnp.einsum('brhqj,bjhd->brqhd', p[..., n_win:].astype(v.dtype), vc))
        outs.append(o)
    return jnp.concatenate(outs, axis=2).reshape(bsz, seq, B_HEADS * HEAD_DIM)


def dense_block_attention(q, k, v, kc, vc):
    bsz, seq = q.shape[:2]
    nblk = seq // BLOCK
    qb = jnp.moveaxis(q.reshape(bsz, nblk, BLOCK, C_KV, C_HEADS // C_KV, HEAD_DIM), 1, 0)
    o = lax.map(lambda qblk: attend_sets(qblk, [(k, v), (kc, vc)]), qb)
    return jnp.moveaxis(o, 0, 1).reshape(bsz, seq, C_HEADS * HEAD_DIM)


def differential_attention(q, k, v, kc, vc, lam):
    bsz, seq = q.shape[:2]
    nblk = seq // BLOCK
    qb = jnp.moveaxis(q.reshape((bsz, nblk, BLOCK) + q.shape[2:]), 1, 0)
    o = lax.map(lambda qblk: diff_attend_sets(qblk, [(k, v), (kc, vc)], lam), qb)
    return jnp.moveaxis(o, 0, 1).reshape((bsz, seq) + o.shape[3:])


def gated_merge(branches, gate_logits, w_br_l, w_out_l):
    ys = jnp.stack(branches, axis=2)
    proj = jnp.einsum('btnw,nwd->btnd', ys, w_br_l)
    g = jax.nn.sigmoid(gate_logits.reshape(gate_logits.shape[:2] + (N_BRANCH, D_MODEL)))
    return jnp.sum(g * proj, axis=2) @ w_out_l


def setup_inputs(seed: int = 0) -> dict:
    key = jax.random.key(seed)
    ks = jax.random.split(key, 15)

    def nrm(k, shape, s):
        return jax.random.normal(k, shape, jnp.float32) * s

    return {
        "x": nrm(ks[0], (BATCH, SEQ, D_MODEL), 1.0),
        "c": nrm(ks[1], (BATCH, D_MODEL), 1.0),
        "ctx": nrm(ks[2], (BATCH, CTX_LEN, D_MODEL), 1.0),
        "c_ctx": nrm(ks[3], (D_MODEL,), 1.0),
        "norm_w": 1.0 + nrm(ks[4], (DEPTH, D_MODEL), 0.02),
        "w_ada": nrm(ks[5], (DEPTH, D_MODEL, 3 * D_MODEL), 0.5 * D_MODEL ** -0.5),
        "b_ada": nrm(ks[6], (DEPTH, 3 * D_MODEL), 0.01),
        "w_in": nrm(ks[7], (DEPTH, D_MODEL, IN_COLS), D_MODEL ** -0.5),
        "qk_gain": 1.0 + nrm(ks[8], (DEPTH, N_BRANCH, 2, HEAD_DIM), 0.02),
        "sink_a": nrm(ks[9], (DEPTH, A_HEADS), 0.5),
        "rpb_b": nrm(ks[10], (DEPTH, B_HEADS, 2 * NB_ROWS - 1, 2 * NB_COLS - 1), 0.1),
        "lam_d": nrm(ks[11], (DEPTH, 4, D_HEAD_DIM), 0.1),
        "subln_d": 1.0 + nrm(ks[12], (DEPTH, 2 * D_HEAD_DIM), 0.02),
        "w_br": nrm(ks[13], (DEPTH, N_BRANCH, BRANCH_W, D_MODEL), BRANCH_W ** -0.5),
        "w_out": nrm(ks[14], (DEPTH, D_MODEL, D_MODEL), D_MODEL ** -0.5),
    }


def reference(x, c, ctx, c_ctx, norm_w, w_ada, b_ada, w_in, qk_gain, sink_a, rpb_b, lam_d,
              subln_d, w_br, w_out):
    bsz, seq, _ = x.shape
    ctx_len = ctx.shape[1]
    cos, sin = axial_rope_tables(seq, x.dtype)
    ga = A_HEADS // A_KV
    gc = C_HEADS // C_KV
    dtail = (2, D_HEAD_DIM)
    for l in range(DEPTH):
        need_ctx = l < DEPTH - 1
        lam_init = 0.8 - 0.6 * math.exp(-0.3 * l)
        mod_x = jax.nn.silu(c) @ w_ada[l] + b_ada[l]
        mod_c = jax.nn.silu(c_ctx) @ w_ada[l] + b_ada[l]
        shift_x, scale_x, gate_x = jnp.split(mod_x[:, None, :], 3, axis=-1)
        shift_c, scale_c, gate_c = jnp.split(mod_c[None, None, :], 3, axis=-1)
        hx = rms_norm(x, norm_w[l]) * (1.0 + scale_x) + shift_x
        hc = rms_norm(ctx, norm_w[l]) * (1.0 + scale_c) + shift_c
        px = jnp.split(hx @ w_in[l], SPLIT_POINTS, axis=-1)
        pc = jnp.split(hc @ w_in[l], SPLIT_POINTS, axis=-1)
        g = qk_gain[l]
        lam = diff_lambda(lam_d[l], lam_init)

        ka_c = qk_prep(pc[1], A_KV, g[0, 1])
        va_c = split_heads(pc[2], A_KV)
        kb_c = qk_prep(pc[5], B_HEADS, g[1, 1])
        vb_c = split_heads(pc[6], B_HEADS)
        kc_c = qk_prep(pc[9], C_KV, g[2, 1])
        vc_c = split_heads(pc[10], C_KV)
        kd_c = qk_prep(pc[13], D_HEADS, g[3, 1], tail=dtail)
        vd_c = split_heads(pc[14], D_HEADS, (2 * D_HEAD_DIM,))

        qa = qk_prep(px[0], A_HEADS, g[0, 0], cos, sin)
        ka = qk_prep(px[1], A_KV, g[0, 1], cos, sin)
        va = split_heads(px[2], A_KV)
        y_a = window_sink_attention(qa, ka, va, ka_c, va_c, sink_a[l])
        qb = qk_prep(px[4], B_HEADS, g[1, 0])
        kb = qk_prep(px[5], B_HEADS, g[1, 1])
        vb = split_heads(px[6], B_HEADS)
        y_b = neighborhood_attention(qb, kb, vb, kb_c, vb_c, rpb_b[l])
        qc = qk_prep(px[8], C_HEADS, g[2, 0], cos, sin)
        kc = qk_prep(px[9], C_KV, g[2, 1], cos, sin)
        vc = split_heads(px[10], C_KV)
        y_c = dense_block_attention(qc, kc, vc, kc_c, vc_c)
        qd = qk_prep(px[12], D_HEADS, g[3, 0], cos, sin, tail=dtail)
        kd = qk_prep(px[13], D_HEADS, g[3, 1], cos, sin, tail=dtail)
        vd = split_heads(px[14], D_HEADS, (2 * D_HEAD_DIM,))
        y_d = finish_diff(differential_attention(qd, kd, vd, kd_c, vd_c, lam), subln_d[l], lam_init)

        branches_x = [y_a * jax.nn.silu(px[3]), y_b * jax.nn.silu(px[7]),
                      y_c * jax.nn.silu(px[11]), y_d * jax.nn.silu(px[15])]
        x_new = x + gate_x * gated_merge(branches_x, px[16], w_br[l], w_out[l])

        if need_ctx:
            qa_c = qk_prep(pc[0], A_HEADS, g[0, 0])
            ya_c = attend_sets(qa_c.reshape(bsz, ctx_len, A_KV, ga, HEAD_DIM), [(ka_c, va_c)],
                               sink_a[l]).reshape(bsz, ctx_len, -1)
            qb_c = qk_prep(pc[4], B_HEADS, g[1, 0])
            yb_c = attend_sets(qb_c[:, :, :, None], [(kb_c, vb_c)]).reshape(bsz, ctx_len, -1)
            qc_c = qk_prep(pc[8], C_HEADS, g[2, 0])
            yc_c = attend_sets(qc_c.reshape(bsz, ctx_len, C_KV, gc, HEAD_DIM),
                               [(kc_c, vc_c)]).reshape(bsz, ctx_len, -1)
            qd_c = qk_prep(pc[12], D_HEADS, g[3, 0], tail=dtail)
            yd_c = finish_diff(diff_attend_sets(qd_c, [(kd_c, vd_c)], lam), subln_d[l], lam_init)
            branches_c = [ya_c * jax.nn.silu(pc[3]), yb_c * jax.nn.silu(pc[7]),
                          yc_c * jax.nn.silu(pc[11]), yd_c * jax.nn.silu(pc[15])]
            ctx = ctx + gate_c * gated_merge(branches_c, pc[16], w_br[l], w_out[l])
        x = x_new
    return x
```

```python
import functools
import math

import numpy as np
import jax
import jax.numpy as jnp
from jax import lax
from jax.experimental import pallas as pl
from jax.experimental.pallas import tpu as pltpu

F32 = jnp.float32
BF16 = jnp.bfloat16

D_MODEL = 1024
GRID_W = 64
HEAD_DIM = 64
BRANCH_W = 512
N_BRANCH = 4
A_HEADS, A_KV, A_WINDOW = 8, 2, 128
B_HEADS, NB_ROWS, NB_COLS = 8, 8, 16
C_HEADS, C_KV = 8, 2
D_HEADS = 4
ROPE_THETA = 10000.0
EPS = 1e-6
NEG_INF = -1e30
QK_SCALE = HEAD_DIM ** -0.5

V7X_VMEM_LIMIT_BYTES = 56 * 1024 * 1024
LANES = 128

_ORIG = dict(aq=(0, 512), ak=(512, 640), av=(640, 768), ag=(768, 1280),
             bq=(1280, 1792), bk=(1792, 2304), bv=(2304, 2816), bg=(2816, 3328),
             cq=(3328, 3840), ck=(3840, 3968), cv=(3968, 4096), cg=(4096, 4608),
             dq=(4608, 5120), dk=(5120, 5632), dv=(5632, 6144), dg=(6144, 6656),
             mg=(6656, 10752))
_QK_ORDER = ("aq", "cq", "dq", "bq", "bk", "dk", "ak", "ck")
_REST_ORDER = ("mg", "ag", "bg", "cg", "dg", "av", "cv", "bv", "dv")


def _offsets(order):
    off, out = 0, {}
    for name in order:
        lo, hi = _ORIG[name]
        out[name] = off
        off += hi - lo
    return out, off


QK_OFF, QK_COLS = _offsets(_QK_ORDER)
REST_OFF, REST_COLS = _offsets(_REST_ORDER)
PROJ_TN = 256
_ROPE_LO_END = QK_OFF["bq"] // PROJ_TN
_ROPE_HI_START = QK_OFF["dk"] // PROJ_TN
_MG_BLOCKS = (REST_OFF["ag"]) // PROJ_TN
_GP_BLOCKS_END = REST_OFF["av"] // PROJ_TN


def _cparams(sem):
    return pltpu.CompilerParams(dimension_semantics=sem, vmem_limit_bytes=V7X_VMEM_LIMIT_BYTES)


def _sigmoid(x):
    return 1.0 / (1.0 + jnp.exp(-x))


def _dot_nt(a, b):
    return lax.dot_general(a, b, (((1,), (1,)), ((), ())), preferred_element_type=F32)


def _dot(a, b):
    return jnp.dot(a, b, preferred_element_type=F32)


def _ada_kernel(c_ref, w_ref, b_ref, o_ref):
    c = c_ref[...]
    o_ref[...] = _dot(c * _sigmoid(c), w_ref[...]) + b_ref[...]


def _ada(cvec, w_ada, b_ada):
    depth = w_ada.shape[0]
    tn = 512
    return pl.pallas_call(
        _ada_kernel,
        grid=(depth, 3 * D_MODEL // tn),
        in_specs=[pl.BlockSpec((8, D_MODEL), lambda l, j: (0, 0)),
                  pl.BlockSpec((None, D_MODEL, tn), lambda l, j: (l, 0, j)),
                  pl.BlockSpec((None, 1, tn), lambda l, j: (l, 0, j))],
        out_specs=pl.BlockSpec((None, 8, tn), lambda l, j: (l, 0, j)),
        out_shape=jax.ShapeDtypeStruct((depth, 8, 3 * D_MODEL), F32),
        compiler_params=_cparams(("parallel", "parallel")),
        name="ada_mod",
    )(cvec, w_ada, b_ada.reshape(depth, 1, 3 * D_MODEL))


def _prenorm_kernel(x_ref, nw_ref, mod_ref, o_ref):
    x = x_ref[...]
    mod = mod_ref[...]
    shift, scale = mod[:, :D_MODEL], mod[:, D_MODEL:2 * D_MODEL]
    y = x * lax.rsqrt(jnp.mean(x * x, axis=-1, keepdims=True) + EPS) * nw_ref[...]
    o_ref[...] = (y * (1.0 + scale) + shift).astype(BF16)


def _prenorm(x2, norm_w, mod3, tm, row_of_tile):
    rows = x2.shape[0]
    return pl.pallas_call(
        _prenorm_kernel,
        grid=(rows // tm,),
        in_specs=[pl.BlockSpec((tm, D_MODEL), lambda i: (i, 0)),
                  pl.BlockSpec((1, D_MODEL), lambda i: (0, 0)),
                  pl.BlockSpec((None, 1, 3 * D_MODEL), lambda i: (row_of_tile(i), 0, 0))],
        out_specs=pl.BlockSpec((tm, D_MODEL), lambda i: (i, 0)),
        out_shape=jax.ShapeDtypeStruct((rows, D_MODEL), BF16),
        compiler_params=_cparams(("parallel",)),
        name="prenorm",
    )(x2, norm_w.reshape(1, D_MODEL), mod3)


def _rot_half_unsigned(n):
    lane = lax.broadcasted_iota(jnp.int32, n.shape, 1)
    return jnp.where((lane & 31) < 16, pltpu.roll(n, LANES - 16, 1), pltpu.roll(n, 16, 1))


def _proj_qk_kernel(*refs, rope):
    if rope:
        hx_ref, w_ref, gain_ref, ones_ref, cos_ref, sin_ref, o_ref = refs
    else:
        hx_ref, w_ref, gain_ref, ones_ref, o_ref = refs
    acc = _dot(hx_ref[...], w_ref[...])
    ss = _dot((acc * acc).astype(BF16), ones_ref[...])
    n = acc * lax.rsqrt(ss * (1.0 / HEAD_DIM) + EPS) * gain_ref[...]
    if rope:
        j = pl.program_id(1)
        use = jnp.logical_or(j < _ROPE_LO_END, j >= _ROPE_HI_START)
        cos = jnp.where(use, cos_ref[...], 1.0)
        sin = jnp.where(use, sin_ref[...], 0.0)
        rot = jnp.concatenate([_rot_half_unsigned(n[:, :LANES]), _rot_half_unsigned(n[:, LANES:])], axis=1)
        n = n * cos + rot * sin
    o_ref[...] = n.astype(BF16)


def _proj_qk(hx, w_qk, gain_row, ones_bd, tm, cos=None, sin=None, tiles_per_batch=None):
    rows = hx.shape[0]
    rope = cos is not None
    in_specs = [pl.BlockSpec((tm, D_MODEL), lambda i, j: (i, 0)),
                pl.BlockSpec((D_MODEL, PROJ_TN), lambda i, j: (0, j)),
                pl.BlockSpec((1, PROJ_TN), lambda i, j: (0, j)),
                pl.BlockSpec((PROJ_TN, PROJ_TN), lambda i, j: (0, 0))]
    args = [hx, w_qk, gain_row, ones_bd]
    if rope:
        in_specs += [pl.BlockSpec((tm, PROJ_TN), lambda i, j: (i % tiles_per_batch, 0))] * 2
        args += [cos, sin]
    return pl.pallas_call(
        functools.partial(_proj_qk_kernel, rope=rope),
        grid=(rows // tm, QK_COLS // PROJ_TN),
        in_specs=in_specs,
        out_specs=pl.BlockSpec((tm, PROJ_TN), lambda i, j: (i, j)),
        out_shape=jax.ShapeDtypeStruct((rows, QK_COLS), BF16),
        compiler_params=_cparams(("parallel", "arbitrary")),
        name="proj_qk_rope" if rope else "proj_qk",
    )(*args)


def _proj_rest_kernel(hx_ref, w_ref, o_ref):
    j = pl.program_id(1)
    acc = _dot(hx_ref[...], w_ref[...])
    sg = _sigmoid(acc)
    o_ref[...] = jnp.where(j < _MG_BLOCKS, sg, jnp.where(j < _GP_BLOCKS_END, acc * sg, acc)).astype(BF16)


def _proj_rest(hx, w_rest, tm):
    rows = hx.shape[0]
    return pl.pallas_call(
        _proj_rest_kernel,
        grid=(rows // tm, REST_COLS // PROJ_TN),
        in_specs=[pl.BlockSpec((tm, D_MODEL), lambda i, j: (i, 0)),
                  pl.BlockSpec((D_MODEL, PROJ_TN), lambda i, j: (0, j))],
        out_specs=pl.BlockSpec((tm, PROJ_TN), lambda i, j: (i, j)),
        out_shape=jax.ShapeDtypeStruct((rows, REST_COLS), BF16),
        compiler_params=_cparams(("parallel", "arbitrary")),
        name="proj_rest",
    )(hx, w_rest)


def _stack_heads(q, first_head, n):
    return jnp.concatenate([q[:, (first_head + g) * HEAD_DIM:(first_head + g + 1) * HEAD_DIM] for g in range(n)],
                           axis=0)


def _softmax_rows(s, extra=None):
    m = jnp.max(s, axis=-1, keepdims=True)
    if extra is not None:
        m = jnp.maximum(m, extra)
    p = jnp.exp(s - m)
    l = jnp.sum(p, axis=-1, keepdims=True)
    if extra is not None:
        l = l + jnp.exp(extra - m)
    return p, l


def _diff_lambda(lam_ref, lam_init):
    lf = lam_ref[...]
    a = jnp.sum(lf[0:1] * lf[1:2], axis=-1, keepdims=True)
    b = jnp.sum(lf[2:3] * lf[3:4], axis=-1, keepdims=True)
    return jnp.exp(a) - jnp.exp(b) + lam_init


def _finish_diff(o, subln_ref, lam_init):
    y = o * lax.rsqrt(jnp.mean(o * o, axis=-1, keepdims=True) + EPS) * subln_ref[...]
    return y * (1.0 - lam_init)


def _win_kernel(sink_ref, q_ref, kp_ref, kc_ref, kn_ref, kx_ref, vp_ref, vc_ref, vn_ref, vx_ref, o_ref, *,
                tq, seq):
    i = pl.program_id(1)
    g = A_HEADS // A_KV
    q = q_ref[...]
    k_all = jnp.concatenate([kp_ref[...], kc_ref[...], kn_ref[...], kx_ref[...]], axis=0)
    v_all = jnp.concatenate([vp_ref[...], vc_ref[...], vn_ref[...], vx_ref[...]], axis=0)
    nw = 2 * tq
    t = lax.broadcasted_iota(jnp.int32, (g * tq, nw), 0) & (tq - 1)
    j = lax.broadcasted_iota(jnp.int32, (g * tq, nw), 1)
    rel = j - tq // 2 - t
    kpos = i * tq - tq // 2 + j
    bad = jnp.where(jnp.abs(rel) > A_WINDOW, 1, 0) + jnp.where(kpos < 0, 1, 0) + jnp.where(kpos >= seq, 1, 0)
    for kv in range(A_KV):
        qs = _stack_heads(q, kv * g, g)
        kk = k_all[:, kv * HEAD_DIM:(kv + 1) * HEAD_DIM]
        vv = v_all[:, kv * HEAD_DIM:(kv + 1) * HEAD_DIM]
        s = _dot_nt(qs, kk)
        s_win = jnp.where(bad > 0, NEG_INF, s[:, :nw])
        s_ctx = s[:, nw:]
        sink = jnp.concatenate([jnp.full((tq, 1), sink_ref[kv * g + h], F32) for h in range(g)], axis=0)
        m = jnp.maximum(jnp.maximum(jnp.max(s_win, axis=-1, keepdims=True),
                                    jnp.max(s_ctx, axis=-1, keepdims=True)), sink)
        pw = jnp.exp(s_win - m)
        pc = jnp.exp(s_ctx - m)
        l = jnp.sum(pw, axis=-1, keepdims=True) + jnp.sum(pc, axis=-1, keepdims=True) + jnp.exp(sink - m)
        o = (_dot(pw.astype(BF16), vv[:nw]) + _dot(pc.astype(BF16), vv[nw:])) / l
        for h in range(g):
            c0 = (kv * g + h) * HEAD_DIM
            o_ref[:, c0:c0 + HEAD_DIM] = o[h * tq:(h + 1) * tq].astype(BF16)


def _window_attention(qk_lat, rest_lat, qk_ctx, rest_ctx, sink):
    bsz, seq, _ = qk_lat.shape
    ctx_len = qk_ctx.shape[1]
    tq = 2 * A_WINDOW
    nt = seq // tq
    half = tq // 2
    kcol = QK_OFF["ak"] // LANES
    vcol = REST_OFF["av"] // LANES
    n_half = seq // half

    def prev_map(col):
        return lambda b, i: (b, jnp.maximum(2 * i - 1, 0), col)

    def cur_map(col):
        return lambda b, i: (b, i, col)

    def next_map(col):
        return lambda b, i: (b, jnp.minimum(2 * i + 2, n_half - 1), col)

    def ctx_map(col):
        return lambda b, i: (b, 0, col)

    def kv_specs(col):
        return [pl.BlockSpec((None, half, LANES), prev_map(col)),
                pl.BlockSpec((None, tq, LANES), cur_map(col)),
                pl.BlockSpec((None, half, LANES), next_map(col)),
                pl.BlockSpec((None, ctx_len, LANES), ctx_map(col))]

    return pl.pallas_call(
        functools.partial(_win_kernel, tq=tq, seq=seq),
        grid=(bsz, nt),
        in_specs=[pl.BlockSpec(memory_space=pltpu.SMEM),
                  pl.BlockSpec((None, tq, A_HEADS * HEAD_DIM), lambda b, i: (b, i, QK_OFF["aq"] // 512))]
                 + kv_specs(kcol) + kv_specs(vcol),
        out_specs=pl.BlockSpec((None, tq, BRANCH_W), lambda b, i: (b, i, 0)),
        out_shape=jax.ShapeDtypeStruct((bsz, seq, BRANCH_W), BF16),
        compiler_params=_cparams(("parallel", "parallel")),
        name="mixer_a_window",
    )(sink, qk_lat, qk_lat, qk_lat, qk_lat, qk_ctx, rest_lat, rest_lat, rest_lat, rest_ctx)


def _nbr_kernel(q_ref, kp_ref, kc_ref, kn_ref, kx_ref, vp_ref, vc_ref, vn_ref, vx_ref, bias_ref, o_ref, *, tq):
    q = q_ref[...]
    k_all = jnp.concatenate([kp_ref[...], kc_ref[...], kn_ref[...], kx_ref[...]], axis=0)
    v_all = jnp.concatenate([vp_ref[...], vc_ref[...], vn_ref[...], vx_ref[...]], axis=0)
    nw = 3 * tq
    for hh in range(2):
        sl = slice(hh * HEAD_DIM, (hh + 1) * HEAD_DIM)
        s = _dot_nt(q[:, sl], k_all[:, sl])
        s_win = s[:, :nw] + bias_ref[hh]
        s_ctx = s[:, nw:]
        m = jnp.maximum(jnp.max(s_win, axis=-1, keepdims=True), jnp.max(s_ctx, axis=-1, keepdims=True))
        pw = jnp.exp(s_win - m)
        pc = jnp.exp(s_ctx - m)
        l = jnp.sum(pw, axis=-1, keepdims=True) + jnp.sum(pc, axis=-1, keepdims=True)
        vv = v_all[:, sl]
        o = (_dot(pw.astype(BF16), vv[:nw]) + _dot(pc.astype(BF16), vv[nw:])) / l
        o_ref[:, sl] = o.astype(BF16)


def _nbr_bias_tables(rpb, seq, tq):
    rows = seq // GRID_W
    nt = seq // tq
    kr = min(NB_ROWS, rows)
    rows_per_tile = tq // GRID_W
    tabs = []
    for i0 in (0, 1, nt - 1):
        t = np.arange(tq)
        r = i0 * rows_per_tile + t // GRID_W
        c = t % GRID_W
        j = np.arange(3 * tq)
        r2 = (i0 - 1) * rows_per_tile + j // GRID_W
        c2 = j % GRID_W
        rstart = np.clip(r - kr // 2, 0, rows - kr)
        cstart = np.clip(c - NB_COLS // 2, 0, GRID_W - NB_COLS)
        ok = ((r2[None, :] >= rstart[:, None]) & (r2[None, :] < rstart[:, None] + kr)
              & (r2[None, :] >= 0) & (r2[None, :] < rows)
              & (c2[None, :] >= cstart[:, None]) & (c2[None, :] < cstart[:, None] + NB_COLS))
        dr = np.clip(r2[None, :] - r[:, None] + (NB_ROWS - 1), 0, 2 * NB_ROWS - 2)
        dc = np.clip(c2[None, :] - c[:, None] + (NB_COLS - 1), 0, 2 * NB_COLS - 2)
        bias = rpb[:, dr, dc].astype(F32)
        tabs.append(jnp.where(ok[None], bias, NEG_INF))
    return jnp.stack(tabs, axis=0)


def _neighborhood_attention(qk_lat, rest_lat, qk_ctx, rest_ctx, bias_tabs):
    bsz, seq, _ = qk_lat.shape
    ctx_len = qk_ctx.shape[1]
    tq = bias_tabs.shape[2]
    nt = seq // tq
    qcol = QK_OFF["bq"] // LANES
    kcol = QK_OFF["bk"] // LANES
    vcol = REST_OFF["bv"] // LANES

    def kv_specs(col):
        return [pl.BlockSpec((None, tq, LANES), lambda hp, b, i: (b, jnp.maximum(i - 1, 0), col + hp)),
                pl.BlockSpec((None, tq, LANES), lambda hp, b, i: (b, i, col + hp)),
                pl.BlockSpec((None, tq, LANES), lambda hp, b, i: (b, jnp.minimum(i + 1, nt - 1), col + hp)),
                pl.BlockSpec((None, ctx_len, LANES), lambda hp, b, i: (b, 0, col + hp))]

    def bias_map(hp, b, i):
        return (jnp.where(i == 0, 0, jnp.where(i == nt - 1, 2, 1)), hp, 0, 0)

    return pl.pallas_call(
        functools.partial(_nbr_kernel, tq=tq),
        grid=(B_HEADS // 2, bsz, nt),
        in_specs=[pl.BlockSpec((None, tq, LANES), lambda hp, b, i: (b, i, qcol + hp))]
                 + kv_specs(kcol) + kv_specs(vcol)
                 + [pl.BlockSpec((None, 2, tq, 3 * tq), bias_map)],
        out_specs=pl.BlockSpec((None, tq, LANES), lambda hp, b, i: (b, i, hp)),
        out_shape=jax.ShapeDtypeStruct((bsz, seq, BRANCH_W), BF16),
        compiler_params=_cparams(("parallel", "parallel", "parallel")),
        name="mixer_b_neighbourhood",
    )(qk_lat, qk_lat, qk_lat, qk_lat, qk_ctx, rest_lat, rest_lat, rest_lat, rest_ctx, bias_tabs)


def _dense_c_kernel(q_ref, k_ref, v_ref, o_ref, qs_ref, m_ref, l_ref, acc_ref, *, tq, nk):
    kt = pl.program_id(2)
    g = C_HEADS // C_KV

    @pl.when(kt == 0)
    def _():
        q = q_ref[...]
        for kv in range(C_KV):
            qs_ref[kv] = _stack_heads(q, kv * g, g)
        m_ref[...] = jnp.full(m_ref.shape, -jnp.inf, F32)
        l_ref[...] = jnp.zeros(l_ref.shape, F32)
        acc_ref[...] = jnp.zeros(acc_ref.shape, F32)

    k = k_ref[...]
    v = v_ref[...]
    for kv in range(C_KV):
        sl = slice(kv * HEAD_DIM, (kv + 1) * HEAD_DIM)
        s = _dot_nt(qs_ref[kv], k[:, sl])
        m_prev = m_ref[kv]
        m_new = jnp.maximum(m_prev, jnp.max(s, axis=-1, keepdims=True))
        alpha = jnp.exp(m_prev - m_new)
        p = jnp.exp(s - m_new)
        l_ref[kv] = alpha * l_ref[kv] + jnp.sum(p, axis=-1, keepdims=True)
        acc_ref[kv] = alpha * acc_ref[kv] + _dot(p.astype(BF16), v[:, sl])
        m_ref[kv] = m_new

    @pl.when(kt == nk - 1)
    def _():
        for kv in range(C_KV):
            o = acc_ref[kv] / l_ref[kv]
            for h in range(g):
                c0 = (kv * g + h) * HEAD_DIM
                o_ref[:, c0:c0 + HEAD_DIM] = o[h * tq:(h + 1) * tq].astype(BF16)


def _dense_gqa(qk_lat, k_all, v_all, tq, tk):
    bsz, seq, _ = qk_lat.shape
    keys = k_all.shape[1]
    nk = keys // tk
    g = C_HEADS // C_KV
    return pl.pallas_call(
        functools.partial(_dense_c_kernel, tq=tq, nk=nk),
        grid=(bsz, seq // tq, nk),
        in_specs=[pl.BlockSpec((None, tq, C_HEADS * HEAD_DIM), lambda b, i, kt: (b, i, QK_OFF["cq"] // 512)),
                  pl.BlockSpec((None, tk, C_KV * HEAD_DIM), lambda b, i, kt: (b, kt, 0)),
                  pl.BlockSpec((None, tk, C_KV * HEAD_DIM), lambda b, i, kt: (b, kt, 0))],
        out_specs=pl.BlockSpec((None, tq, BRANCH_W), lambda b, i, kt: (b, i, 0)),
        out_shape=jax.ShapeDtypeStruct((bsz, seq, BRANCH_W), BF16),
        scratch_shapes=[pltpu.VMEM((C_KV, g * tq, HEAD_DIM), BF16),
                        pltpu.VMEM((C_KV, g * tq, 1), F32),
                        pltpu.VMEM((C_KV, g * tq, 1), F32),
                        pltpu.VMEM((C_KV, g * tq, HEAD_DIM), F32)],
        compiler_params=_cparams(("parallel", "parallel", "arbitrary")),
        name="mixer_c_dense",
    )(qk_lat, k_all, v_all)


def _dense_d_kernel(q_ref, k_ref, v_ref, lam_ref, subln_ref, o_ref, qs_ref, m_ref, l_ref, acc_ref, *,
                    nk, lam_init):
    kt = pl.program_id(2)
    n_sc = 2 * D_HEADS

    @pl.when(kt == 0)
    def _():
        q = q_ref[...]
        for idx in range(n_sc):
            qs_ref[idx] = q[:, idx * HEAD_DIM:(idx + 1) * HEAD_DIM]
        m_ref[...] = jnp.full(m_ref.shape, -jnp.inf, F32)
        l_ref[...] = jnp.zeros(l_ref.shape, F32)
        acc_ref[...] = jnp.zeros(acc_ref.shape, F32)

    k = k_ref[...]
    v = v_ref[...]
    for idx in range(n_sc):
        h = idx // 2
        s = _dot_nt(qs_ref[idx], k[:, idx * HEAD_DIM:(idx + 1) * HEAD_DIM])
        m_prev = m_ref[idx]
        m_new = jnp.maximum(m_prev, jnp.max(s, axis=-1, keepdims=True))
        alpha = jnp.exp(m_prev - m_new)
        p = jnp.exp(s - m_new)
        l_ref[idx] = alpha * l_ref[idx] + jnp.sum(p, axis=-1, keepdims=True)
        acc_ref[idx] = alpha * acc_ref[idx] + _dot(p.astype(BF16), v[:, h * 2 * HEAD_DIM:(h + 1) * 2 * HEAD_DIM])
        m_ref[idx] = m_new

    @pl.when(kt == nk - 1)
    def _():
        lam = _diff_lambda(lam_ref, lam_init)
        for h in range(D_HEADS):
            o = acc_ref[2 * h] / l_ref[2 * h] - lam * (acc_ref[2 * h + 1] / l_ref[2 * h + 1])
            o_ref[:, h * 2 * HEAD_DIM:(h + 1) * 2 * HEAD_DIM] = _finish_diff(o, subln_ref, lam_init).astype(BF16)


def _dense_diff(qk_lat, k_all, v_all, lam_d, subln, lam_init, tq, tk):
    bsz, seq, _ = qk_lat.shape
    keys = k_all.shape[1]
    nk = keys // tk
    width = D_HEADS * 2 * HEAD_DIM
    return pl.pallas_call(
        functools.partial(_dense_d_kernel, nk=nk, lam_init=lam_init),
        grid=(bsz, seq // tq, nk),
        in_specs=[pl.BlockSpec((None, tq, width), lambda b, i, kt: (b, i, QK_OFF["dq"] // 512)),
                  pl.BlockSpec((None, tk, width), lambda b, i, kt: (b, kt, 0)),
                  pl.BlockSpec((None, tk, width), lambda b, i, kt: (b, kt, 0)),
                  pl.BlockSpec((4, HEAD_DIM), lambda b, i, kt: (0, 0)),
                  pl.BlockSpec((1, 2 * HEAD_DIM), lambda b, i, kt: (0, 0))],
        out_specs=pl.BlockSpec((None, tq, BRANCH_W), lambda b, i, kt: (b, i, 0)),
        out_shape=jax.ShapeDtypeStruct((bsz, seq, BRANCH_W), BF16),
        scratch_shapes=[pltpu.VMEM((2 * D_HEADS, tq, HEAD_DIM), BF16),
                        pltpu.VMEM((2 * D_HEADS, tq, 1), F32),
                        pltpu.VMEM((2 * D_HEADS, tq, 1), F32),
                        pltpu.VMEM((2 * D_HEADS, tq, 2 * HEAD_DIM), F32)],
        compiler_params=_cparams(("parallel", "parallel", "arbitrary")),
        name="mixer_d_diff",
    )(qk_lat, k_all, v_all, lam_d, subln.reshape(1, 2 * HEAD_DIM))


def _ctx_kernel(sink_ref, qk_ref, v_ref, lam_ref, subln_ref, o_ref, *, lam_init):
    qk = qk_ref[...]
    vals = v_ref[...]
    ctx_len = qk.shape[0]
    v_off = {name: REST_OFF[name] - REST_OFF["av"] for name in ("av", "cv", "bv", "dv")}

    def cols(name, start, width):
        c0 = QK_OFF[name] + start
        return qk[:, c0:c0 + width]

    def gqa(qname, kname, vname, out_off, n_kv, with_sink):
        g = 8 // n_kv
        for kv in range(n_kv):
            qs = _stack_heads(cols(qname, 0, 8 * HEAD_DIM), kv * g, g)
            s = _dot_nt(qs, cols(kname, kv * HEAD_DIM, HEAD_DIM))
            extra = None
            if with_sink:
                extra = jnp.concatenate([jnp.full((ctx_len, 1), sink_ref[kv * g + h], F32) for h in range(g)], axis=0)
            p, l = _softmax_rows(s, extra)
            vv = vals[:, v_off[vname] + kv * HEAD_DIM:v_off[vname] + (kv + 1) * HEAD_DIM]
            o = _dot(p.astype(BF16), vv) / l
            for h in range(g):
                c0 = out_off + (kv * g + h) * HEAD_DIM
                o_ref[:, c0:c0 + HEAD_DIM] = o[h * ctx_len:(h + 1) * ctx_len].astype(BF16)

    gqa("aq", "ak", "av", 0 * BRANCH_W, A_KV, True)
    gqa("bq", "bk", "bv", 1 * BRANCH_W, B_HEADS, False)
    gqa("cq", "ck", "cv", 2 * BRANCH_W, C_KV, False)

    lam = _diff_lambda(lam_ref, lam_init)
    for h in range(D_HEADS):
        base = h * 2 * HEAD_DIM
        p1, l1 = _softmax_rows(_dot_nt(cols("dq", base, HEAD_DIM), cols("dk", base, HEAD_DIM)))
        p2, l2 = _softmax_rows(_dot_nt(cols("dq", base + HEAD_DIM, HEAD_DIM), cols("dk", base + HEAD_DIM, HEAD_DIM)))
        pd = p1 / l1 - lam * (p2 / l2)
        o = _dot(pd.astype(BF16), vals[:, v_off["dv"] + base:v_off["dv"] + base + 2 * HEAD_DIM])
        c0 = 3 * BRANCH_W + base
        o_ref[:, c0:c0 + 2 * HEAD_DIM] = _finish_diff(o, subln_ref, lam_init).astype(BF16)


def _ctx_attention(qk_ctx, v_ctx, sink, lam_d, subln, lam_init):
    bsz, ctx_len, _ = qk_ctx.shape
    return pl.pallas_call(
        functools.partial(_ctx_kernel, lam_init=lam_init),
        grid=(bsz,),
        in_specs=[pl.BlockSpec(memory_space=pltpu.SMEM),
                  pl.BlockSpec((None, ctx_len, QK_COLS), lambda b: (b, 0, 0)),
                  pl.BlockSpec((None, ctx_len, v_ctx.shape[2]), lambda b: (b, 0, 0)),
                  pl.BlockSpec((4, HEAD_DIM), lambda b: (0, 0)),
                  pl.BlockSpec((1, 2 * HEAD_DIM), lambda b: (0, 0))],
        out_specs=pl.BlockSpec((None, ctx_len, N_BRANCH * BRANCH_W), lambda b: (b, 0, 0)),
        out_shape=jax.ShapeDtypeStruct((bsz, ctx_len, N_BRANCH * BRANCH_W), BF16),
        compiler_params=_cparams(("parallel",)),
        name="ctx_attention",
    )(sink, qk_ctx, v_ctx, lam_d, subln.reshape(1, 2 * HEAD_DIM))


def _merge_kernel(x_ref, mod_ref, ya_ref, yb_ref, yc_ref, yd_ref, gp_ref, mg_ref, wbr_ref, wout_ref, o_ref):
    gate = mod_ref[...][:, 2 * D_MODEL:]
    merged = None
    for n, y_ref in enumerate((ya_ref, yb_ref, yc_ref, yd_ref)):
        yg = (y_ref[...].astype(F32) * gp_ref[:, n * BRANCH_W:(n + 1) * BRANCH_W].astype(F32)).astype(BF16)
        term = mg_ref[:, n * D_MODEL:(n + 1) * D_MODEL].astype(F32) * _dot(yg, wbr_ref[n])
        merged = term if merged is None else merged + term
    o_ref[...] = x_ref[...] + gate * _dot(merged.astype(BF16), wout_ref[...])


def _merge(x2, mod3, ys, y_cols, rest, w_br, w_out, tm, row_of_tile):
    rows = x2.shape[0]
    y_specs = [pl.BlockSpec((tm, BRANCH_W), (lambda i, c=c: (i, c))) for c in y_cols]
    return pl.pallas_call(
        _merge_kernel,
        grid=(rows // tm,),
        in_specs=[pl.BlockSpec((tm, D_MODEL), lambda i: (i, 0)),
                  pl.BlockSpec((None, 1, 3 * D_MODEL), lambda i: (row_of_tile(i), 0, 0))]
                 + y_specs
                 + [pl.BlockSpec((tm, N_BRANCH * BRANCH_W), lambda i: (i, REST_OFF["ag"] // (N_BRANCH * BRANCH_W))),
                    pl.BlockSpec((tm, N_BRANCH * D_MODEL), lambda i: (i, 0)),
                    pl.BlockSpec((N_BRANCH, BRANCH_W, D_MODEL), lambda i: (0, 0, 0)),
                    pl.BlockSpec((D_MODEL, D_MODEL), lambda i: (0, 0))],
        out_specs=pl.BlockSpec((tm, D_MODEL), lambda i: (i, 0)),
        out_shape=jax.ShapeDtypeStruct((rows, D_MODEL), F32),
        compiler_params=_cparams(("parallel",)),
        name="gated_merge",
    )(x2, mod3, *ys, rest, rest, w_br, w_out)


def _rope_tables(seq):
    t = jnp.arange(seq, dtype=jnp.int32)
    pos = jnp.stack([t // GRID_W, t % GRID_W], axis=-1).astype(F32)
    n_freq = HEAD_DIM // 4
    freqs = ROPE_THETA ** (-jnp.arange(n_freq, dtype=F32) / n_freq)
    ang = pos[:, :, None] * freqs[None, None, :]
    ang = jnp.concatenate([ang, ang], axis=-1).reshape(seq, HEAD_DIM)
    sign = np.where((np.arange(HEAD_DIM) % 32) < 16, -1.0, 1.0).astype(np.float32)
    reps = PROJ_TN // HEAD_DIM
    return jnp.tile(jnp.cos(ang), (1, reps)), jnp.tile(jnp.sin(ang) * sign, (1, reps))


def _regroup_cols(w, order):
    return jnp.concatenate([w[:, _ORIG[n][0]:_ORIG[n][1]] for n in order], axis=1)


def _qk_gain_row(g):
    parts = []
    for name in _QK_ORDER:
        mixer = "abcd".index(name[0])
        is_q = name[1] == "q"
        width = _ORIG[name][1] - _ORIG[name][0]
        gain = g[mixer, 0] * QK_SCALE if is_q else g[mixer, 1]
        parts.append(jnp.tile(gain, width // HEAD_DIM))
    return jnp.concatenate(parts).reshape(1, QK_COLS).astype(F32)


def kernel(x, c, ctx, c_ctx, norm_w, w_ada, b_ada, w_in, qk_gain, sink_a, rpb_b, lam_d, subln_d, w_br, w_out):
    bsz, seq, _ = x.shape
    ctx_len = ctx.shape[1]
    depth = w_ada.shape[0]
    assert seq % (2 * A_WINDOW) == 0 and ctx_len % LANES == 0 and bsz <= 6

    tm_lat = 1024
    tiles_per_batch = seq // tm_lat
    tq_dense, tk_dense = 256, 768
    assert (seq + ctx_len) % tk_dense == 0
    tm_merge = 512
    merge_tiles_per_batch = seq // tm_merge

    cvec = jnp.concatenate([c, c_ctx[None, :], jnp.zeros((8 - bsz - 1, D_MODEL), F32)], axis=0)
    mod_all = _ada(cvec, w_ada, b_ada)
    cos, sin = _rope_tables(seq)
    ones_bd = jnp.asarray(np.kron(np.eye(PROJ_TN // HEAD_DIM), np.ones((HEAD_DIM, HEAD_DIM))), BF16)

    x2 = x.reshape(bsz * seq, D_MODEL)
    c2 = ctx.reshape(bsz * ctx_len, D_MODEL)
    for l in range(depth):
        need_ctx = l < depth - 1
        lam_init = 0.8 - 0.6 * math.exp(-0.3 * l)
        mod3 = mod_all[l].reshape(8, 1, 3 * D_MODEL)
        w_qk = _regroup_cols(w_in[l], _QK_ORDER).astype(BF16)
        w_rest = _regroup_cols(w_in[l], _REST_ORDER).astype(BF16)
        gain_row = _qk_gain_row(qk_gain[l])

        hx = _prenorm(x2, norm_w[l], mod3, tm_lat, lambda i: i // tiles_per_batch)
        hc = _prenorm(c2, norm_w[l], mod3, bsz * ctx_len, lambda i: bsz)
        qk_lat = _proj_qk(hx, w_qk, gain_row, ones_bd, tm_lat, cos, sin, tiles_per_batch).reshape(bsz, seq, QK_COLS)
        qk_ctx = _proj_qk(hc, w_qk, gain_row, ones_bd, bsz * ctx_len).reshape(bsz, ctx_len, QK_COLS)
        rest_lat = _proj_rest(hx, w_rest, tm_lat).reshape(bsz, seq, REST_COLS)
        rest_ctx = _proj_rest(hc, w_rest, bsz * ctx_len).reshape(bsz, ctx_len, REST_COLS)

        def keys_of(arr_lat, arr_ctx, off, width):
            return jnp.concatenate([arr_lat[:, :, off:off + width], arr_ctx[:, :, off:off + width]], axis=1)

        y_a = _window_attention(qk_lat, rest_lat, qk_ctx, rest_ctx, sink_a[l])
        y_b = _neighborhood_attention(qk_lat, rest_lat, qk_ctx, rest_ctx,
                                      _nbr_bias_tables(rpb_b[l], seq, 2 * A_WINDOW))
        y_c = _dense_gqa(qk_lat, keys_of(qk_lat, qk_ctx, QK_OFF["ck"], C_KV * HEAD_DIM),
                         keys_of(rest_lat, rest_ctx, REST_OFF["cv"], C_KV * HEAD_DIM), tq_dense, tk_dense)
        y_d = _dense_diff(qk_lat, keys_of(qk_lat, qk_ctx, QK_OFF["dk"], D_HEADS * 2 * HEAD_DIM),
                          keys_of(rest_lat, rest_ctx, REST_OFF["dv"], D_HEADS * 2 * HEAD_DIM),
                          lam_d[l], subln_d[l], lam_init, tq_dense, tk_dense)

        w_br_l = w_br[l].astype(BF16)
        w_out_l = w_out[l].astype(BF16)
        ys = [y.reshape(bsz * seq, BRANCH_W) for y in (y_a, y_b, y_c, y_d)]
        x_new = _merge(x2, mod3, ys, (0, 0, 0, 0), rest_lat.reshape(bsz * seq, REST_COLS), w_br_l, w_out_l,
                       tm_merge, lambda i: i // merge_tiles_per_batch)
        if need_ctx:
            y_ctx = _ctx_attention(qk_ctx, rest_ctx[:, :, REST_OFF["av"]:], sink_a[l], lam_d[l], subln_d[l], lam_init)
            y_ctx2 = y_ctx.reshape(bsz * ctx_len, N_BRANCH * BRANCH_W)
            c2 = _merge(c2, mod3, [y_ctx2] * 4, (0, 1, 2, 3), rest_ctx.reshape(bsz * ctx_len, REST_COLS),
                        w_br_l, w_out_l, bsz * ctx_len, lambda i: bsz)
        x2 = x_new
    return x2.reshape(bsz, seq, D_MODEL)
```

```python
import functools
import math

import numpy as np
import jax
import jax.numpy as jnp
from jax import lax
from jax.experimental import pallas as pl
from jax.experimental.pallas import tpu as pltpu

F32 = jnp.float32
BF16 = jnp.bfloat16

D_MODEL = 1024
GRID_W = 64
HEAD_DIM = 64
BRANCH_W = 512
N_BRANCH = 4
A_HEADS, A_KV, A_WINDOW = 8, 2, 128
B_HEADS, NB_ROWS, NB_COLS = 8, 8, 16
C_HEADS, C_KV = 8, 2
D_HEADS = 4
ROPE_THETA = 10000.0
EPS = 1e-6
NEG_INF = -1e30
QK_SCALE = HEAD_DIM ** -0.5
LOG2E = math.log2(math.e)
DENSE_LOGIT_BOUND = 60.0

V7X_VMEM_LIMIT_BYTES = 56 * 1024 * 1024
LANES = 128

_ORIG = dict(aq=(0, 512), ak=(512, 640), av=(640, 768), ag=(768, 1280),
             bq=(1280, 1792), bk=(1792, 2304), bv=(2304, 2816), bg=(2816, 3328),
             cq=(3328, 3840), ck=(3840, 3968), cv=(3968, 4096), cg=(4096, 4608),
             dq=(4608, 5120), dk=(5120, 5632), dv=(5632, 6144), dg=(6144, 6656),
             mg=(6656, 10752))
_QK_ORDER = ("aq", "cq", "dq", "bq", "bk", "dk", "ak", "ck")
_REST_ORDER = ("mg", "ag", "bg", "cg", "dg", "av", "cv", "bv", "dv")


def _offsets(order):
    off, out = 0, {}
    for name in order:
        lo, hi = _ORIG[name]
        out[name] = off
        off += hi - lo
    return out, off


QK_OFF, QK_COLS = _offsets(_QK_ORDER)
REST_OFF, REST_COLS = _offsets(_REST_ORDER)
PROJ_TN = 256
_ROPE_LO_END = QK_OFF["bq"] // PROJ_TN
_ROPE_HI_START = QK_OFF["dk"] // PROJ_TN
_MG_BLOCKS = (REST_OFF["ag"]) // PROJ_TN
_GP_BLOCKS_END = REST_OFF["av"] // PROJ_TN


def _cparams(sem):
    return pltpu.CompilerParams(dimension_semantics=sem, vmem_limit_bytes=V7X_VMEM_LIMIT_BYTES)


def _sigmoid(x):
    return 1.0 / (1.0 + jnp.exp(-x))


def _dot_nt(a, b):
    return lax.dot_general(a, b, (((1,), (1,)), ((), ())), preferred_element_type=F32)


def _dot(a, b):
    return jnp.dot(a, b, preferred_element_type=F32)


def _ada_kernel(c_ref, w_ref, b_ref, o_ref):
    c = c_ref[...]
    o_ref[...] = _dot(c * _sigmoid(c), w_ref[...]) + b_ref[...]


def _ada(cvec, w_ada, b_ada):
    depth = w_ada.shape[0]
    tn = 512
    return pl.pallas_call(
        _ada_kernel,
        grid=(depth, 3 * D_MODEL // tn),
        in_specs=[pl.BlockSpec((8, D_MODEL), lambda l, j: (0, 0)),
                  pl.BlockSpec((None, D_MODEL, tn), lambda l, j: (l, 0, j)),
                  pl.BlockSpec((None, 1, tn), lambda l, j: (l, 0, j))],
        out_specs=pl.BlockSpec((None, 8, tn), lambda l, j: (l, 0, j)),
        out_shape=jax.ShapeDtypeStruct((depth, 8, 3 * D_MODEL), F32),
        compiler_params=_cparams(("parallel", "parallel")),
        name="ada_mod",
    )(cvec, w_ada, b_ada.reshape(depth, 1, 3 * D_MODEL))


def _prenorm_kernel(x_ref, nw_ref, mod_ref, o_ref):
    x = x_ref[...]
    mod = mod_ref[...]
    shift, scale = mod[:, :D_MODEL], mod[:, D_MODEL:2 * D_MODEL]
    y = x * lax.rsqrt(jnp.mean(x * x, axis=-1, keepdims=True) + EPS) * nw_ref[...]
    o_ref[...] = (y * (1.0 + scale) + shift).astype(BF16)


def _prenorm(x2, norm_w, mod3, tm, row_of_tile):
    rows = x2.shape[0]
    return pl.pallas_call(
        _prenorm_kernel,
        grid=(rows // tm,),
        in_specs=[pl.BlockSpec((tm, D_MODEL), lambda i: (i, 0)),
                  pl.BlockSpec((1, D_MODEL), lambda i: (0, 0)),
                  pl.BlockSpec((None, 1, 3 * D_MODEL), lambda i: (row_of_tile(i), 0, 0))],
        out_specs=pl.BlockSpec((tm, D_MODEL), lambda i: (i, 0)),
        out_shape=jax.ShapeDtypeStruct((rows, D_MODEL), BF16),
        compiler_params=_cparams(("parallel",)),
        name="prenorm",
    )(x2, norm_w.reshape(1, D_MODEL), mod3)


def _rot_half_unsigned(n):
    lane = lax.broadcasted_iota(jnp.int32, n.shape, 1)
    return jnp.where((lane & 31) < 16, pltpu.roll(n, LANES - 16, 1), pltpu.roll(n, 16, 1))


def _proj_qk_kernel(*refs, rope):
    if rope:
        hx_ref, w_ref, gain_ref, ones_ref, cos_ref, sin_ref, o_ref = refs
    else:
        hx_ref, w_ref, gain_ref, ones_ref, o_ref = refs
    acc = _dot(hx_ref[...], w_ref[...])
    ss = _dot((acc * acc).astype(BF16), ones_ref[...])
    n = acc * lax.rsqrt(ss * (1.0 / HEAD_DIM) + EPS) * gain_ref[...]
    if rope:
        j = pl.program_id(1)
        use = jnp.logical_or(j < _ROPE_LO_END, j >= _ROPE_HI_START)
        cos = jnp.where(use, cos_ref[...], 1.0)
        sin = jnp.where(use, sin_ref[...], 0.0)
        rot = jnp.concatenate([_rot_half_unsigned(n[:, :LANES]), _rot_half_unsigned(n[:, LANES:])], axis=1)
        n = n * cos + rot * sin
    o_ref[...] = n.astype(BF16)


def _proj_qk(hx, w_qk, gain_row, ones_bd, tm, cos=None, sin=None, tiles_per_batch=None):
    rows = hx.shape[0]
    rope = cos is not None
    in_specs = [pl.BlockSpec((tm, D_MODEL), lambda i, j: (i, 0)),
                pl.BlockSpec((D_MODEL, PROJ_TN), lambda i, j: (0, j)),
                pl.BlockSpec((1, PROJ_TN), lambda i, j: (0, j)),
                pl.BlockSpec((PROJ_TN, PROJ_TN), lambda i, j: (0, 0))]
    args = [hx, w_qk, gain_row, ones_bd]
    if rope:
        in_specs += [pl.BlockSpec((tm, PROJ_TN), lambda i, j: (i % tiles_per_batch, 0))] * 2
        args += [cos, sin]
    return pl.pallas_call(
        functools.partial(_proj_qk_kernel, rope=rope),
        grid=(rows // tm, QK_COLS // PROJ_TN),
        in_specs=in_specs,
        out_specs=pl.BlockSpec((tm, PROJ_TN), lambda i, j: (i, j)),
        out_shape=jax.ShapeDtypeStruct((rows, QK_COLS), BF16),
        compiler_params=_cparams(("parallel", "arbitrary")),
        name="proj_qk_rope" if rope else "proj_qk",
    )(*args)


def _proj_rest_kernel(hx_ref, w_ref, o_ref):
    j = pl.program_id(1)
    acc = _dot(hx_ref[...], w_ref[...])
    sg = _sigmoid(acc)
    o_ref[...] = jnp.where(j < _MG_BLOCKS, sg, jnp.where(j < _GP_BLOCKS_END, acc * sg, acc)).astype(BF16)


def _proj_rest(hx, w_rest, tm):
    rows = hx.shape[0]
    return pl.pallas_call(
        _proj_rest_kernel,
        grid=(rows // tm, REST_COLS // PROJ_TN),
        in_specs=[pl.BlockSpec((tm, D_MODEL), lambda i, j: (i, 0)),
                  pl.BlockSpec((D_MODEL, PROJ_TN), lambda i, j: (0, j))],
        out_specs=pl.BlockSpec((tm, PROJ_TN), lambda i, j: (i, j)),
        out_shape=jax.ShapeDtypeStruct((rows, REST_COLS), BF16),
        compiler_params=_cparams(("parallel", "arbitrary")),
        name="proj_rest",
    )(hx, w_rest)


def _stack_heads(q, first_head, n):
    return jnp.concatenate([q[:, (first_head + g) * HEAD_DIM:(first_head + g + 1) * HEAD_DIM] for g in range(n)],
                           axis=0)


def _softmax_rows(s, extra=None, exp=jnp.exp):
    m = jnp.max(s, axis=-1, keepdims=True)
    if extra is not None:
        m = jnp.maximum(m, extra)
    p = exp(s - m)
    l = jnp.sum(p, axis=-1, keepdims=True)
    if extra is not None:
        l = l + exp(extra - m)
    return p, l


def _diff_lambda(lam_ref, lam_init):
    lf = lam_ref[...]
    a = jnp.sum(lf[0:1] * lf[1:2], axis=-1, keepdims=True)
    b = jnp.sum(lf[2:3] * lf[3:4], axis=-1, keepdims=True)
    return jnp.exp(a) - jnp.exp(b) + lam_init


def _finish_diff(o, subln_ref, lam_init):
    y = o * lax.rsqrt(jnp.mean(o * o, axis=-1, keepdims=True) + EPS) * subln_ref[...]
    return y * (1.0 - lam_init)


def _win_kernel(sink_ref, q_ref, kp_ref, kc_ref, kn_ref, kx_ref, vp_ref, vc_ref, vn_ref, vx_ref, o_ref, *,
                tq, seq):
    i = pl.program_id(1)
    g = A_HEADS // A_KV
    q = q_ref[...]
    k_all = jnp.concatenate([kp_ref[...], kc_ref[...], kn_ref[...], kx_ref[...]], axis=0)
    v_all = jnp.concatenate([vp_ref[...], vc_ref[...], vn_ref[...], vx_ref[...]], axis=0)
    nw = 2 * tq
    t = lax.broadcasted_iota(jnp.int32, (g * tq, nw), 0) & (tq - 1)
    j = lax.broadcasted_iota(jnp.int32, (g * tq, nw), 1)
    rel = j - tq // 2 - t
    kpos = i * tq - tq // 2 + j
    bad = jnp.where(jnp.abs(rel) > A_WINDOW, 1, 0) + jnp.where(kpos < 0, 1, 0) + jnp.where(kpos >= seq, 1, 0)
    for kv in range(A_KV):
        qs = _stack_heads(q, kv * g, g)
        kk = k_all[:, kv * HEAD_DIM:(kv + 1) * HEAD_DIM]
        vv = v_all[:, kv * HEAD_DIM:(kv + 1) * HEAD_DIM]
        s = _dot_nt(qs, kk)
        s_win = jnp.where(bad > 0, NEG_INF, s[:, :nw])
        s_ctx = s[:, nw:]
        sink = jnp.concatenate([jnp.full((tq, 1), sink_ref[kv * g + h], F32) for h in range(g)], axis=0)
        m = jnp.maximum(jnp.maximum(jnp.max(s_win, axis=-1, keepdims=True),
                                    jnp.max(s_ctx, axis=-1, keepdims=True)), sink)
        pw = jnp.exp(s_win - m)
        pc = jnp.exp(s_ctx - m)
        l = jnp.sum(pw, axis=-1, keepdims=True) + jnp.sum(pc, axis=-1, keepdims=True) + jnp.exp(sink - m)
        o = (_dot(pw.astype(BF16), vv[:nw]) + _dot(pc.astype(BF16), vv[nw:])) / l
        for h in range(g):
            c0 = (kv * g + h) * HEAD_DIM
            o_ref[:, c0:c0 + HEAD_DIM] = o[h * tq:(h + 1) * tq].astype(BF16)


def _window_attention(qk_lat, rest_lat, qk_ctx, rest_ctx, sink):
    bsz, seq, _ = qk_lat.shape
    ctx_len = qk_ctx.shape[1]
    tq = 2 * A_WINDOW
    nt = seq // tq
    half = tq // 2
    kcol = QK_OFF["ak"] // LANES
    vcol = REST_OFF["av"] // LANES
    n_half = seq // half

    def prev_map(col):
        return lambda b, i: (b, jnp.maximum(2 * i - 1, 0), col)

    def cur_map(col):
        return lambda b, i: (b, i, col)

    def next_map(col):
        return lambda b, i: (b, jnp.minimum(2 * i + 2, n_half - 1), col)

    def ctx_map(col):
        return lambda b, i: (b, 0, col)

    def kv_specs(col):
        return [pl.BlockSpec((None, half, LANES), prev_map(col)),
                pl.BlockSpec((None, tq, LANES), cur_map(col)),
                pl.BlockSpec((None, half, LANES), next_map(col)),
                pl.BlockSpec((None, ctx_len, LANES), ctx_map(col))]

    return pl.pallas_call(
        functools.partial(_win_kernel, tq=tq, seq=seq),
        grid=(bsz, nt),
        in_specs=[pl.BlockSpec(memory_space=pltpu.SMEM),
                  pl.BlockSpec((None, tq, A_HEADS * HEAD_DIM), lambda b, i: (b, i, QK_OFF["aq"] // 512))]
                 + kv_specs(kcol) + kv_specs(vcol),
        out_specs=pl.BlockSpec((None, tq, BRANCH_W), lambda b, i: (b, i, 0)),
        out_shape=jax.ShapeDtypeStruct((bsz, seq, BRANCH_W), BF16),
        compiler_params=_cparams(("parallel", "parallel")),
        name="mixer_a_window",
    )(sink, qk_lat, qk_lat, qk_lat, qk_lat, qk_ctx, rest_lat, rest_lat, rest_lat, rest_ctx)


def _nbr_kernel(q_ref, kp_ref, kc_ref, kn_ref, kx_ref, vp_ref, vc_ref, vn_ref, vx_ref, bias_ref, o_ref, *, tq):
    q = q_ref[...]
    k_all = jnp.concatenate([kp_ref[...], kc_ref[...], kn_ref[...], kx_ref[...]], axis=0)
    v_all = jnp.concatenate([vp_ref[...], vc_ref[...], vn_ref[...], vx_ref[...]], axis=0)
    nw = 3 * tq
    for hh in range(2):
        sl = slice(hh * HEAD_DIM, (hh + 1) * HEAD_DIM)
        s = _dot_nt(q[:, sl], k_all[:, sl])
        s_win = s[:, :nw] + bias_ref[hh]
        s_ctx = s[:, nw:]
        m = jnp.maximum(jnp.max(s_win, axis=-1, keepdims=True), jnp.max(s_ctx, axis=-1, keepdims=True))
        pw = jnp.exp(s_win - m)
        pc = jnp.exp(s_ctx - m)
        l = jnp.sum(pw, axis=-1, keepdims=True) + jnp.sum(pc, axis=-1, keepdims=True)
        vv = v_all[:, sl]
        o = (_dot(pw.astype(BF16), vv[:nw]) + _dot(pc.astype(BF16), vv[nw:])) / l
        o_ref[:, sl] = o.astype(BF16)


def _nbr_bias_tables(rpb, seq, tq):
    rows = seq // GRID_W
    nt = seq // tq
    kr = min(NB_ROWS, rows)
    qr = tq // GRID_W
    n_heads = rpb.shape[0]
    assert NB_ROWS - 1 - qr - (qr - 1) >= 0 and NB_ROWS - 1 - qr + 3 * qr <= 2 * NB_ROWS - 1
    pad = GRID_W - 1
    rpb_pad = jnp.pad(rpb.astype(F32), ((0, 0), (0, 0), (pad, pad)))
    toep = jnp.stack([rpb_pad[:, :, pad + NB_COLS - 1 - c:pad + NB_COLS - 1 - c + GRID_W] for c in range(GRID_W)],
                     axis=2)
    c = np.arange(GRID_W)
    cstart = np.clip(c - NB_COLS // 2, 0, GRID_W - NB_COLS)
    col_ok = (c[None, :] >= cstart[:, None]) & (c[None, :] < cstart[:, None] + NB_COLS)
    toep = jnp.where(col_ok[None, None], toep, NEG_INF)
    per_rl = [jnp.transpose(toep[:, NB_ROWS - 1 - qr - rl:NB_ROWS - 1 - qr - rl + 3 * qr], (0, 2, 1, 3))
              for rl in range(qr)]
    base = jnp.stack(per_rl, axis=1)
    tabs = []
    for i0 in (0, 1, nt - 1):
        r = i0 * qr + np.arange(qr)
        r2 = (i0 - 1) * qr + np.arange(3 * qr)
        rstart = np.clip(r - kr // 2, 0, rows - kr)
        row_ok = ((r2[None, :] >= rstart[:, None]) & (r2[None, :] < rstart[:, None] + kr)
                  & (r2[None, :] >= 0) & (r2[None, :] < rows))
        tab = jnp.where(row_ok[None, :, None, :, None], base, NEG_INF)
        tabs.append(tab.reshape(n_heads, tq, 3 * tq))
    return jnp.stack(tabs, axis=0)


def _neighborhood_attention(qk_lat, rest_lat, qk_ctx, rest_ctx, bias_tabs):
    bsz, seq, _ = qk_lat.shape
    ctx_len = qk_ctx.shape[1]
    tq = bias_tabs.shape[2]
    nt = seq // tq
    qcol = QK_OFF["bq"] // LANES
    kcol = QK_OFF["bk"] // LANES
    vcol = REST_OFF["bv"] // LANES

    def kv_specs(col):
        return [pl.BlockSpec((None, tq, LANES), lambda hp, b, i: (b, jnp.maximum(i - 1, 0), col + hp)),
                pl.BlockSpec((None, tq, LANES), lambda hp, b, i: (b, i, col + hp)),
                pl.BlockSpec((None, tq, LANES), lambda hp, b, i: (b, jnp.minimum(i + 1, nt - 1), col + hp)),
                pl.BlockSpec((None, ctx_len, LANES), lambda hp, b, i: (b, 0, col + hp))]

    def bias_map(hp, b, i):
        return (jnp.where(i == 0, 0, jnp.where(i == nt - 1, 2, 1)), hp, 0, 0)

    return pl.pallas_call(
        functools.partial(_nbr_kernel, tq=tq),
        grid=(B_HEADS // 2, bsz, nt),
        in_specs=[pl.BlockSpec((None, tq, LANES), lambda hp, b, i: (b, i, qcol + hp))]
                 + kv_specs(kcol) + kv_specs(vcol)
                 + [pl.BlockSpec((None, 2, tq, 3 * tq), bias_map)],
        out_specs=pl.BlockSpec((None, tq, LANES), lambda hp, b, i: (b, i, hp)),
        out_shape=jax.ShapeDtypeStruct((bsz, seq, BRANCH_W), BF16),
        compiler_params=_cparams(("parallel", "parallel", "parallel")),
        name="mixer_b_neighbourhood",
    )(qk_lat, qk_lat, qk_lat, qk_lat, qk_ctx, rest_lat, rest_lat, rest_lat, rest_ctx, bias_tabs)


def _dense_c_kernel(q_ref, k_ref, v_ref, o_ref, qs_ref, m_ref, l_ref, acc_ref, *, tq, nk):
    kt = pl.program_id(2)
    g = C_HEADS // C_KV

    @pl.when(kt == 0)
    def _():
        q = q_ref[...]
        for kv in range(C_KV):
            qs_ref[kv] = _stack_heads(q, kv * g, g)
        m_ref[...] = jnp.full(m_ref.shape, -jnp.inf, F32)
        l_ref[...] = jnp.zeros(l_ref.shape, F32)
        acc_ref[...] = jnp.zeros(acc_ref.shape, F32)

    k = k_ref[...]
    v = v_ref[...]
    for kv in range(C_KV):
        sl = slice(kv * HEAD_DIM, (kv + 1) * HEAD_DIM)
        s = _dot_nt(qs_ref[kv], k[:, sl])
        m_prev = m_ref[kv]
        m_new = jnp.maximum(m_prev, jnp.max(s, axis=-1, keepdims=True))
        alpha = jnp.exp2(m_prev - m_new)
        p = jnp.exp2(s - m_new)
        l_ref[kv] = alpha * l_ref[kv] + jnp.sum(p, axis=-1, keepdims=True)
        acc_ref[kv] = alpha * acc_ref[kv] + _dot(p.astype(BF16), v[:, sl])
        m_ref[kv] = m_new

    @pl.when(kt == nk - 1)
    def _():
        for kv in range(C_KV):
            o = acc_ref[kv] / l_ref[kv]
            for h in range(g):
                c0 = (kv * g + h) * HEAD_DIM
                o_ref[:, c0:c0 + HEAD_DIM] = o[h * tq:(h + 1) * tq].astype(BF16)


def _dense_gqa(qk_lat, k_all, v_all, tq, tk):
    bsz, seq, _ = qk_lat.shape
    keys = k_all.shape[1]
    nk = keys // tk
    g = C_HEADS // C_KV
    return pl.pallas_call(
        functools.partial(_dense_c_kernel, tq=tq, nk=nk),
        grid=(bsz, seq // tq, nk),
        in_specs=[pl.BlockSpec((None, tq, C_HEADS * HEAD_DIM), lambda b, i, kt: (b, i, QK_OFF["cq"] // 512)),
                  pl.BlockSpec((None, tk, C_KV * HEAD_DIM), lambda b, i, kt: (b, kt, 0)),
                  pl.BlockSpec((None, tk, C_KV * HEAD_DIM), lambda b, i, kt: (b, kt, 0))],
        out_specs=pl.BlockSpec((None, tq, BRANCH_W), lambda b, i, kt: (b, i, 0)),
        out_shape=jax.ShapeDtypeStruct((bsz, seq, BRANCH_W), BF16),
        scratch_shapes=[pltpu.VMEM((C_KV, g * tq, HEAD_DIM), BF16),
                        pltpu.VMEM((C_KV, g * tq, 1), F32),
                        pltpu.VMEM((C_KV, g * tq, 1), F32),
                        pltpu.VMEM((C_KV, g * tq, HEAD_DIM), F32)],
        compiler_params=_cparams(("parallel", "parallel", "arbitrary")),
        name="mixer_c_dense",
    )(qk_lat, k_all, v_all)


def _dense_c_bounded_kernel(q_ref, kt_ref, v_ref, o_ref, qs_ref, acc_ref, *, tq, nk):
    kt = pl.program_id(2)
    g = C_HEADS // C_KV

    @pl.when(kt == 0)
    def _():
        q = q_ref[...]
        for kv in range(C_KV):
            qs_ref[kv] = _stack_heads(q, kv * g, g)
        acc_ref[...] = jnp.zeros(acc_ref.shape, F32)

    for kv in range(C_KV):
        p = jnp.exp2(_dot(qs_ref[kv], kt_ref[kv])).astype(BF16)
        acc_ref[kv] += _dot(p, v_ref[kv])

    @pl.when(kt == nk - 1)
    def _():
        for kv in range(C_KV):
            acc = acc_ref[kv]
            o = acc[:, :HEAD_DIM] / acc[:, HEAD_DIM:HEAD_DIM + 1]
            for h in range(g):
                c0 = (kv * g + h) * HEAD_DIM
                o_ref[:, c0:c0 + HEAD_DIM] = o[h * tq:(h + 1) * tq].astype(BF16)


def _dense_gqa_bounded(qk_lat, k_t, v_ones, tq, tk):
    bsz, seq, _ = qk_lat.shape
    keys = k_t.shape[3]
    nk = keys // tk
    g = C_HEADS // C_KV
    return pl.pallas_call(
        functools.partial(_dense_c_bounded_kernel, tq=tq, nk=nk),
        grid=(bsz, seq // tq, nk),
        in_specs=[pl.BlockSpec((None, tq, C_HEADS * HEAD_DIM), lambda b, i, kt: (b, i, QK_OFF["cq"] // 512)),
                  pl.BlockSpec((None, C_KV, HEAD_DIM, tk), lambda b, i, kt: (b, 0, 0, kt)),
                  pl.BlockSpec((None, C_KV, tk, LANES), lambda b, i, kt: (b, 0, kt, 0))],
        out_specs=pl.BlockSpec((None, tq, BRANCH_W), lambda b, i, kt: (b, i, 0)),
        out_shape=jax.ShapeDtypeStruct((bsz, seq, BRANCH_W), BF16),
        scratch_shapes=[pltpu.VMEM((C_KV, g * tq, HEAD_DIM), BF16),
                        pltpu.VMEM((C_KV, g * tq, LANES), F32)],
        compiler_params=_cparams(("parallel", "parallel", "arbitrary")),
        name="mixer_c_dense_bounded",
    )(qk_lat, k_t, v_ones)


def _dense_d_kernel(q_ref, k_ref, v_ref, lam_ref, subln_ref, o_ref, qs_ref, m_ref, l_ref, acc_ref, *,
                    nk, lam_init):
    kt = pl.program_id(2)
    n_sc = 2 * D_HEADS

    @pl.when(kt == 0)
    def _():
        q = q_ref[...]
        for idx in range(n_sc):
            qs_ref[idx] = q[:, idx * HEAD_DIM:(idx + 1) * HEAD_DIM]
        m_ref[...] = jnp.full(m_ref.shape, -jnp.inf, F32)
        l_ref[...] = jnp.zeros(l_ref.shape, F32)
        acc_ref[...] = jnp.zeros(acc_ref.shape, F32)

    k = k_ref[...]
    v = v_ref[...]
    for idx in range(n_sc):
        h = idx // 2
        s = _dot_nt(qs_ref[idx], k[:, idx * HEAD_DIM:(idx + 1) * HEAD_DIM])
        m_prev = m_ref[idx]
        m_new = jnp.maximum(m_prev, jnp.max(s, axis=-1, keepdims=True))
        alpha = jnp.exp2(m_prev - m_new)
        p = jnp.exp2(s - m_new)
        l_ref[idx] = alpha * l_ref[idx] + jnp.sum(p, axis=-1, keepdims=True)
        acc_ref[idx] = alpha * acc_ref[idx] + _dot(p.astype(BF16), v[:, h * 2 * HEAD_DIM:(h + 1) * 2 * HEAD_DIM])
        m_ref[idx] = m_new

    @pl.when(kt == nk - 1)
    def _():
        lam = _diff_lambda(lam_ref, lam_init)
        for h in range(D_HEADS):
            o = acc_ref[2 * h] / l_ref[2 * h] - lam * (acc_ref[2 * h + 1] / l_ref[2 * h + 1])
            o_ref[:, h * 2 * HEAD_DIM:(h + 1) * 2 * HEAD_DIM] = _finish_diff(o, subln_ref, lam_init).astype(BF16)


def _dense_diff(qk_lat, k_all, v_all, lam_d, subln, lam_init, tq, tk):
    bsz, seq, _ = qk_lat.shape
    keys = k_all.shape[1]
    nk = keys // tk
    width = D_HEADS * 2 * HEAD_DIM
    return pl.pallas_call(
        functools.partial(_dense_d_kernel, nk=nk, lam_init=lam_init),
        grid=(bsz, seq // tq, nk),
        in_specs=[pl.BlockSpec((None, tq, width), lambda b, i, kt: (b, i, QK_OFF["dq"] // 512)),
                  pl.BlockSpec((None, tk, width), lambda b, i, kt: (b, kt, 0)),
                  pl.BlockSpec((None, tk, width), lambda b, i, kt: (b, kt, 0)),
                  pl.BlockSpec((4, HEAD_DIM), lambda b, i, kt: (0, 0)),
                  pl.BlockSpec((1, 2 * HEAD_DIM), lambda b, i, kt: (0, 0))],
        out_specs=pl.BlockSpec((None, tq, BRANCH_W), lambda b, i, kt: (b, i, 0)),
        out_shape=jax.ShapeDtypeStruct((bsz, seq, BRANCH_W), BF16),
        scratch_shapes=[pltpu.VMEM((2 * D_HEADS, tq, HEAD_DIM), BF16),
                        pltpu.VMEM((2 * D_HEADS, tq, 1), F32),
                        pltpu.VMEM((2 * D_HEADS, tq, 1), F32),
                        pltpu.VMEM((2 * D_HEADS, tq, 2 * HEAD_DIM), F32)],
        compiler_params=_cparams(("parallel", "parallel", "arbitrary")),
        name="mixer_d_diff",
    )(qk_lat, k_all, v_all, lam_d, subln.reshape(1, 2 * HEAD_DIM))


def _dense_d_bounded_kernel(q_ref, kt_ref, v_ref, lam_ref, subln_ref, o_ref, qs_ref, l_ref, acc_ref, *,
                            nk, tk, lam_init):
    kt = pl.program_id(2)
    n_sc = 2 * D_HEADS

    @pl.when(kt == 0)
    def _():
        q = q_ref[...]
        for idx in range(n_sc):
            qs_ref[idx] = q[:, idx * HEAD_DIM:(idx + 1) * HEAD_DIM]
        l_ref[...] = jnp.zeros(l_ref.shape, F32)
        acc_ref[...] = jnp.zeros(acc_ref.shape, F32)

    for idx in range(n_sc):
        h = idx // 2
        p = jnp.exp2(_dot(qs_ref[idx], kt_ref[idx]))
        part = p[:, :LANES]
        for c in range(1, tk // LANES):
            part = part + p[:, c * LANES:(c + 1) * LANES]
        l_ref[idx] += part
        acc_ref[idx] += _dot(p.astype(BF16), v_ref[:, h * 2 * HEAD_DIM:(h + 1) * 2 * HEAD_DIM])

    @pl.when(kt == nk - 1)
    def _():
        lam = _diff_lambda(lam_ref, lam_init)
        for h in range(D_HEADS):
            l1 = jnp.sum(l_ref[2 * h], axis=-1, keepdims=True)
            l2 = jnp.sum(l_ref[2 * h + 1], axis=-1, keepdims=True)
            o = acc_ref[2 * h] / l1 - lam * (acc_ref[2 * h + 1] / l2)
            o_ref[:, h * 2 * HEAD_DIM:(h + 1) * 2 * HEAD_DIM] = _finish_diff(o, subln_ref, lam_init).astype(BF16)


def _dense_diff_bounded(qk_lat, k_t, v_all, lam_d, subln, lam_init, tq, tk):
    bsz, seq, _ = qk_lat.shape
    keys = k_t.shape[3]
    nk = keys // tk
    width = D_HEADS * 2 * HEAD_DIM
    return pl.pallas_call(
        functools.partial(_dense_d_bounded_kernel, nk=nk, tk=tk, lam_init=lam_init),
        grid=(bsz, seq // tq, nk),
        in_specs=[pl.BlockSpec((None, tq, width), lambda b, i, kt: (b, i, QK_OFF["dq"] // 512)),
                  pl.BlockSpec((None, 2 * D_HEADS, HEAD_DIM, tk), lambda b, i, kt: (b, 0, 0, kt)),
                  pl.BlockSpec((None, tk, width), lambda b, i, kt: (b, kt, 0)),
                  pl.BlockSpec((4, HEAD_DIM), lambda b, i, kt: (0, 0)),
                  pl.BlockSpec((1, 2 * HEAD_DIM), lambda b, i, kt: (0, 0))],
        out_specs=pl.BlockSpec((None, tq, BRANCH_W), lambda b, i, kt: (b, i, 0)),
        out_shape=jax.ShapeDtypeStruct((bsz, seq, BRANCH_W), BF16),
        scratch_shapes=[pltpu.VMEM((2 * D_HEADS, tq, HEAD_DIM), BF16),
                        pltpu.VMEM((2 * D_HEADS, tq, LANES), F32),
                        pltpu.VMEM((2 * D_HEADS, tq, 2 * HEAD_DIM), F32)],
        compiler_params=_cparams(("parallel", "parallel", "arbitrary")),
        name="mixer_d_diff_bounded",
    )(qk_lat, k_t, v_all, lam_d, subln.reshape(1, 2 * HEAD_DIM))


def _ctx_kernel(sink_ref, qk_ref, v_ref, lam_ref, subln_ref, o_ref, *, lam_init):
    qk = qk_ref[...]
    vals = v_ref[...]
    ctx_len = qk.shape[0]
    v_off = {name: REST_OFF[name] - REST_OFF["av"] for name in ("av", "cv", "bv", "dv")}

    def cols(name, start, width):
        c0 = QK_OFF[name] + start
        return qk[:, c0:c0 + width]

    def gqa(qname, kname, vname, out_off, n_kv, with_sink, exp):
        g = 8 // n_kv
        for kv in range(n_kv):
            qs = _stack_heads(cols(qname, 0, 8 * HEAD_DIM), kv * g, g)
            s = _dot_nt(qs, cols(kname, kv * HEAD_DIM, HEAD_DIM))
            extra = None
            if with_sink:
                extra = jnp.concatenate([jnp.full((ctx_len, 1), sink_ref[kv * g + h], F32) for h in range(g)], axis=0)
            p, l = _softmax_rows(s, extra, exp)
            vv = vals[:, v_off[vname] + kv * HEAD_DIM:v_off[vname] + (kv + 1) * HEAD_DIM]
            o = _dot(p.astype(BF16), vv) / l
            for h in range(g):
                c0 = out_off + (kv * g + h) * HEAD_DIM
                o_ref[:, c0:c0 + HEAD_DIM] = o[h * ctx_len:(h + 1) * ctx_len].astype(BF16)

    gqa("aq", "ak", "av", 0 * BRANCH_W, A_KV, True, jnp.exp)
    gqa("bq", "bk", "bv", 1 * BRANCH_W, B_HEADS, False, jnp.exp)
    gqa("cq", "ck", "cv", 2 * BRANCH_W, C_KV, False, jnp.exp2)

    lam = _diff_lambda(lam_ref, lam_init)
    for h in range(D_HEADS):
        base = h * 2 * HEAD_DIM
        p1, l1 = _softmax_rows(_dot_nt(cols("dq", base, HEAD_DIM), cols("dk", base, HEAD_DIM)), exp=jnp.exp2)
        p2, l2 = _softmax_rows(_dot_nt(cols("dq", base + HEAD_DIM, HEAD_DIM), cols("dk", base + HEAD_DIM, HEAD_DIM)),
                               exp=jnp.exp2)
        pd = p1 / l1 - lam * (p2 / l2)
        o = _dot(pd.astype(BF16), vals[:, v_off["dv"] + base:v_off["dv"] + base + 2 * HEAD_DIM])
        c0 = 3 * BRANCH_W + base
        o_ref[:, c0:c0 + 2 * HEAD_DIM] = _finish_diff(o, subln_ref, lam_init).astype(BF16)


def _ctx_attention(qk_ctx, v_ctx, sink, lam_d, subln, lam_init):
    bsz, ctx_len, _ = qk_ctx.shape
    return pl.pallas_call(
        functools.partial(_ctx_kernel, lam_init=lam_init),
        grid=(bsz,),
        in_specs=[pl.BlockSpec(memory_space=pltpu.SMEM),
                  pl.BlockSpec((None, ctx_len, QK_COLS), lambda b: (b, 0, 0)),
                  pl.BlockSpec((None, ctx_len, v_ctx.shape[2]), lambda b: (b, 0, 0)),
                  pl.BlockSpec((4, HEAD_DIM), lambda b: (0, 0)),
                  pl.BlockSpec((1, 2 * HEAD_DIM), lambda b: (0, 0))],
        out_specs=pl.BlockSpec((None, ctx_len, N_BRANCH * BRANCH_W), lambda b: (b, 0, 0)),
        out_shape=jax.ShapeDtypeStruct((bsz, ctx_len, N_BRANCH * BRANCH_W), BF16),
        compiler_params=_cparams(("parallel",)),
        name="ctx_attention",
    )(sink, qk_ctx, v_ctx, lam_d, subln.reshape(1, 2 * HEAD_DIM))


def _merge_kernel(x_ref, mod_ref, ya_ref, yb_ref, yc_ref, yd_ref, gp_ref, mg_ref, wbr_ref, wout_ref, o_ref):
    gate = mod_ref[...][:, 2 * D_MODEL:]
    merged = None
    for n, y_ref in enumerate((ya_ref, yb_ref, yc_ref, yd_ref)):
        yg = (y_ref[...].astype(F32) * gp_ref[:, n * BRANCH_W:(n + 1) * BRANCH_W].astype(F32)).astype(BF16)
        term = mg_ref[:, n * D_MODEL:(n + 1) * D_MODEL].astype(F32) * _dot(yg, wbr_ref[n])
        merged = term if merged is None else merged + term
    o_ref[...] = x_ref[...] + gate * _dot(merged.astype(BF16), wout_ref[...])


def _merge(x2, mod3, ys, y_cols, rest, w_br, w_out, tm, row_of_tile):
    rows = x2.shape[0]
    y_specs = [pl.BlockSpec((tm, BRANCH_W), (lambda i, c=c: (i, c))) for c in y_cols]
    return pl.pallas_call(
        _merge_kernel,
        grid=(rows // tm,),
        in_specs=[pl.BlockSpec((tm, D_MODEL), lambda i: (i, 0)),
                  pl.BlockSpec((None, 1, 3 * D_MODEL), lambda i: (row_of_tile(i), 0, 0))]
                 + y_specs
                 + [pl.BlockSpec((tm, N_BRANCH * BRANCH_W), lambda i: (i, REST_OFF["ag"] // (N_BRANCH * BRANCH_W))),
                    pl.BlockSpec((tm, N_BRANCH * D_MODEL), lambda i: (i, 0)),
                    pl.BlockSpec((N_BRANCH, BRANCH_W, D_MODEL), lambda i: (0, 0, 0)),
                    pl.BlockSpec((D_MODEL, D_MODEL), lambda i: (0, 0))],
        out_specs=pl.BlockSpec((tm, D_MODEL), lambda i: (i, 0)),
        out_shape=jax.ShapeDtypeStruct((rows, D_MODEL), F32),
        compiler_params=_cparams(("parallel",)),
        name="gated_merge",
    )(x2, mod3, *ys, rest, rest, w_br, w_out)


def _rope_tables(seq):
    t = jnp.arange(seq, dtype=jnp.int32)
    pos = jnp.stack([t // GRID_W, t % GRID_W], axis=-1).astype(F32)
    n_freq = HEAD_DIM // 4
    freqs = ROPE_THETA ** (-jnp.arange(n_freq, dtype=F32) / n_freq)
    ang = pos[:, :, None] * freqs[None, None, :]
    ang = jnp.concatenate([ang, ang], axis=-1).reshape(seq, HEAD_DIM)
    sign = np.where((np.arange(HEAD_DIM) % 32) < 16, -1.0, 1.0).astype(np.float32)
    reps = PROJ_TN // HEAD_DIM
    return jnp.tile(jnp.cos(ang), (1, reps)), jnp.tile(jnp.sin(ang) * sign, (1, reps))


def _regroup_cols(w, order):
    return jnp.concatenate([w[:, _ORIG[n][0]:_ORIG[n][1]] for n in order], axis=1)


def _qk_gain_row(g):
    parts = []
    for name in _QK_ORDER:
        mixer = "abcd".index(name[0])
        is_q = name[1] == "q"
        width = _ORIG[name][1] - _ORIG[name][0]
        gain = g[mixer, 1]
        if is_q:
            gain = g[mixer, 0] * (QK_SCALE * LOG2E if name in ("cq", "dq") else QK_SCALE)
        parts.append(jnp.tile(gain, width // HEAD_DIM))
    return jnp.concatenate(parts).reshape(1, QK_COLS).astype(F32)


def kernel(x, c, ctx, c_ctx, norm_w, w_ada, b_ada, w_in, qk_gain, sink_a, rpb_b, lam_d, subln_d, w_br, w_out):
    bsz, seq, _ = x.shape
    ctx_len = ctx.shape[1]
    depth = w_ada.shape[0]
    assert seq % (2 * A_WINDOW) == 0 and ctx_len % LANES == 0 and bsz <= 6

    tm_lat = 1024
    tiles_per_batch = seq // tm_lat
    tq_dense, tk_dense = 256, 768
    tq_bounded = 512
    assert (seq + ctx_len) % tk_dense == 0
    tm_merge = 512
    merge_tiles_per_batch = seq // tm_merge

    cvec = jnp.concatenate([c, c_ctx[None, :], jnp.zeros((8 - bsz - 1, D_MODEL), F32)], axis=0)
    mod_all = _ada(cvec, w_ada, b_ada)
    cos, sin = _rope_tables(seq)
    ones_bd = jnp.asarray(np.kron(np.eye(PROJ_TN // HEAD_DIM), np.ones((HEAD_DIM, HEAD_DIM))), BF16)

    x2 = x.reshape(bsz * seq, D_MODEL)
    c2 = ctx.reshape(bsz * ctx_len, D_MODEL)
    for l in range(depth):
        need_ctx = l < depth - 1
        lam_init = 0.8 - 0.6 * math.exp(-0.3 * l)
        mod3 = mod_all[l].reshape(8, 1, 3 * D_MODEL)
        w_qk = _regroup_cols(w_in[l], _QK_ORDER).astype(BF16)
        w_rest = _regroup_cols(w_in[l], _REST_ORDER).astype(BF16)
        gain_row = _qk_gain_row(qk_gain[l])

        hx = _prenorm(x2, norm_w[l], mod3, tm_lat, lambda i: i // tiles_per_batch)
        hc = _prenorm(c2, norm_w[l], mod3, bsz * ctx_len, lambda i: bsz)
        qk_lat = _proj_qk(hx, w_qk, gain_row, ones_bd, tm_lat, cos, sin, tiles_per_batch).reshape(bsz, seq, QK_COLS)
        qk_ctx = _proj_qk(hc, w_qk, gain_row, ones_bd, bsz * ctx_len).reshape(bsz, ctx_len, QK_COLS)
        rest_lat = _proj_rest(hx, w_rest, tm_lat).reshape(bsz, seq, REST_COLS)
        rest_ctx = _proj_rest(hc, w_rest, bsz * ctx_len).reshape(bsz, ctx_len, REST_COLS)

        def keys_of(arr_lat, arr_ctx, off, width):
            return jnp.concatenate([arr_lat[:, :, off:off + width], arr_ctx[:, :, off:off + width]], axis=1)

        y_a = _window_attention(qk_lat, rest_lat, qk_ctx, rest_ctx, sink_a[l])
        y_b = _neighborhood_attention(qk_lat, rest_lat, qk_ctx, rest_ctx,
                                      _nbr_bias_tables(rpb_b[l], seq, 2 * A_WINDOW))
        k_c = keys_of(qk_lat, qk_ctx, QK_OFF["ck"], C_KV * HEAD_DIM)
        v_c = keys_of(rest_lat, rest_ctx, REST_OFF["cv"], C_KV * HEAD_DIM)
        k_d = keys_of(qk_lat, qk_ctx, QK_OFF["dk"], D_HEADS * 2 * HEAD_DIM)
        v_d = keys_of(rest_lat, rest_ctx, REST_OFF["dv"], D_HEADS * 2 * HEAD_DIM)
        n_keys = seq + ctx_len

        def heads_t(k, n):
            return jnp.transpose(k.reshape(bsz, n_keys, n, HEAD_DIM), (0, 2, 3, 1))

        def gqa_bounded(qk_lat, k_c, v_c):
            v4 = jnp.transpose(v_c.reshape(bsz, n_keys, C_KV, HEAD_DIM), (0, 2, 1, 3))
            ones_col = jnp.zeros((bsz, C_KV, n_keys, LANES - HEAD_DIM), BF16).at[..., 0].set(1.0)
            return _dense_gqa_bounded(qk_lat, heads_t(k_c, C_KV), jnp.concatenate([v4, ones_col], axis=-1),
                                      tq_bounded, tk_dense)

        def logits_bounded(g):
            bound = HEAD_DIM * QK_SCALE * LOG2E * jnp.max(jnp.abs(g[0])) * jnp.max(jnp.abs(g[1]))
            return bound * 1.02 <= DENSE_LOGIT_BOUND

        y_c = lax.cond(logits_bounded(qk_gain[l, 2]), gqa_bounded,
                       lambda q, k, v: _dense_gqa(q, k, v, tq_dense, tk_dense), qk_lat, k_c, v_c)
        y_d = lax.cond(logits_bounded(qk_gain[l, 3]),
                       lambda q, k, v: _dense_diff_bounded(q, heads_t(k, 2 * D_HEADS), v, lam_d[l], subln_d[l],
                                                           lam_init, tq_bounded, tk_dense),
                       lambda q, k, v: _dense_diff(q, k, v, lam_d[l], subln_d[l], lam_init, tq_dense, tk_dense),
                       qk_lat, k_d, v_d)

        w_br_l = w_br[l].astype(BF16)
        w_out_l = w_out[l].astype(BF16)
        ys = [y.reshape(bsz * seq, BRANCH_W) for y in (y_a, y_b, y_c, y_d)]
        x_new = _merge(x2, mod3, ys, (0, 0, 0, 0), rest_lat.reshape(bsz * seq, REST_COLS), w_br_l, w_out_l,
                       tm_merge, lambda i: i // merge_tiles_per_batch)
        if need_ctx:
            y_ctx = _ctx_attention(qk_ctx, rest_ctx[:, :, REST_OFF["av"]:], sink_a[l], lam_d[l], subln_d[l], lam_init)
            y_ctx2 = y_ctx.reshape(bsz * ctx_len, N_BRANCH * BRANCH_W)
            c2 = _merge(c2, mod3, [y_ctx2] * 4, (0, 1, 2, 3), rest_ctx.reshape(bsz * ctx_len, REST_COLS),
                        w_br_l, w_out_l, bsz * ctx_len, lambda i: bsz)
        x2 = x_new
    return x2.reshape(bsz, seq, D_MODEL)
```

```python
import functools
import math

import numpy as np
import jax
import jax.numpy as jnp
from jax import lax
from jax.experimental import pallas as pl
from jax.experimental.pallas import tpu as pltpu

F32 = jnp.float32
BF16 = jnp.bfloat16

D_MODEL = 1024
GRID_W = 64
HEAD_DIM = 64
BRANCH_W = 512
N_BRANCH = 4
A_HEADS, A_KV, A_WINDOW = 8, 2, 128
B_HEADS, NB_ROWS, NB_COLS = 8, 8, 16
C_HEADS, C_KV = 8, 2
D_HEADS = 4
ROPE_THETA = 10000.0
EPS = 1e-6
NEG_INF = -1e30
QK_SCALE = HEAD_DIM ** -0.5
LOG2E = math.log2(math.e)
LOGIT_BOUND = 60.0

V7X_VMEM_LIMIT_BYTES = 56 * 1024 * 1024
LANES = 128

_ORIG = dict(aq=(0, 512), ak=(512, 640), av=(640, 768), ag=(768, 1280),
             bq=(1280, 1792), bk=(1792, 2304), bv=(2304, 2816), bg=(2816, 3328),
             cq=(3328, 3840), ck=(3840, 3968), cv=(3968, 4096), cg=(4096, 4608),
             dq=(4608, 5120), dk=(5120, 5632), dv=(5632, 6144), dg=(6144, 6656),
             mg=(6656, 10752))
_QK_ORDER = ("aq", "cq", "dq", "bq", "bk", "dk", "ak", "ck")
_REST_ORDER = ("mg", "ag", "bg", "cg", "dg", "av", "cv", "bv", "dv")


def _offsets(order):
    off, out = 0, {}
    for name in order:
        lo, hi = _ORIG[name]
        out[name] = off
        off += hi - lo
    return out, off


PROJ_TN = 512
NORM_GROUP = 256
QK_OFF, _QK_USED = _offsets(_QK_ORDER)
REST_OFF, _REST_USED = _offsets(_REST_ORDER)
QK_COLS = -(-_QK_USED // PROJ_TN) * PROJ_TN
REST_COLS = -(-_REST_USED // PROJ_TN) * PROJ_TN
_ROPE_LO_END = QK_OFF["bq"] // PROJ_TN
_ROPE_HI_START = QK_OFF["dk"] // PROJ_TN
_MG_BLOCKS = (REST_OFF["ag"]) // PROJ_TN
_GP_BLOCKS_END = REST_OFF["av"] // PROJ_TN
assert QK_OFF["bq"] % PROJ_TN == 0 and QK_OFF["dk"] % PROJ_TN == 0
assert REST_OFF["ag"] % PROJ_TN == 0 and REST_OFF["av"] % PROJ_TN == 0


def _cparams(sem):
    return pltpu.CompilerParams(dimension_semantics=sem, vmem_limit_bytes=V7X_VMEM_LIMIT_BYTES)


def _sigmoid(x):
    return 1.0 / (1.0 + jnp.exp(-x))


def _dot_nt(a, b):
    return lax.dot_general(a, b, (((1,), (1,)), ((), ())), preferred_element_type=F32)


def _dot(a, b):
    return jnp.dot(a, b, preferred_element_type=F32)


def _ada_kernel(c_ref, w_ref, b_ref, o_ref):
    c = c_ref[...]
    o_ref[...] = _dot(c * _sigmoid(c), w_ref[...]) + b_ref[...]


def _ada(cvec, w_ada, b_ada):
    depth = w_ada.shape[0]
    tn = 512
    return pl.pallas_call(
        _ada_kernel,
        grid=(depth, 3 * D_MODEL // tn),
        in_specs=[pl.BlockSpec((8, D_MODEL), lambda l, j: (0, 0)),
                  pl.BlockSpec((None, D_MODEL, tn), lambda l, j: (l, 0, j)),
                  pl.BlockSpec((None, 1, tn), lambda l, j: (l, 0, j))],
        out_specs=pl.BlockSpec((None, 8, tn), lambda l, j: (l, 0, j)),
        out_shape=jax.ShapeDtypeStruct((depth, 8, 3 * D_MODEL), F32),
        compiler_params=_cparams(("parallel", "parallel")),
        name="ada_mod",
    )(cvec, w_ada, b_ada.reshape(depth, 1, 3 * D_MODEL))


def _prenorm_kernel(x_ref, nw_ref, mod_ref, o_ref):
    x = x_ref[...]
    mod = mod_ref[...]
    shift, scale = mod[:, :D_MODEL], mod[:, D_MODEL:2 * D_MODEL]
    y = x * lax.rsqrt(jnp.mean(x * x, axis=-1, keepdims=True) + EPS) * nw_ref[...]
    o_ref[...] = (y * (1.0 + scale) + shift).astype(BF16)


def _prenorm(x2, norm_w, mod3, tm, row_of_tile):
    rows = x2.shape[0]
    return pl.pallas_call(
        _prenorm_kernel,
        grid=(rows // tm,),
        in_specs=[pl.BlockSpec((tm, D_MODEL), lambda i: (i, 0)),
                  pl.BlockSpec((1, D_MODEL), lambda i: (0, 0)),
                  pl.BlockSpec((None, 1, 3 * D_MODEL), lambda i: (row_of_tile(i), 0, 0))],
        out_specs=pl.BlockSpec((tm, D_MODEL), lambda i: (i, 0)),
        out_shape=jax.ShapeDtypeStruct((rows, D_MODEL), BF16),
        compiler_params=_cparams(("parallel",)),
        name="prenorm",
    )(x2, norm_w.reshape(1, D_MODEL), mod3)


def _rot_half_unsigned(n):
    lane = lax.broadcasted_iota(jnp.int32, n.shape, 1)
    return jnp.where((lane & 31) < 16, pltpu.roll(n, LANES - 16, 1), pltpu.roll(n, 16, 1))


def _proj_qk_kernel(*refs, rope):
    if rope:
        hx_ref, w_ref, gain_ref, ones_ref, cos_ref, sin_ref, o_ref = refs
    else:
        hx_ref, w_ref, gain_ref, ones_ref, o_ref = refs
    acc_all = _dot(hx_ref[...], w_ref[...])
    if rope:
        j = pl.program_id(1)
        use = jnp.logical_or(j < _ROPE_LO_END, j >= _ROPE_HI_START)
        cos = jnp.where(use, cos_ref[...], 1.0)
        sin = jnp.where(use, sin_ref[...], 0.0)
    for c in range(PROJ_TN // NORM_GROUP):
        sl = slice(c * NORM_GROUP, (c + 1) * NORM_GROUP)
        acc = acc_all[:, sl]
        ss = _dot((acc * acc).astype(BF16), ones_ref[...])
        n = acc * lax.rsqrt(ss * (1.0 / HEAD_DIM) + EPS) * gain_ref[:, sl]
        if rope:
            rot = jnp.concatenate([_rot_half_unsigned(n[:, :LANES]), _rot_half_unsigned(n[:, LANES:])], axis=1)
            n = n * cos + rot * sin
        o_ref[:, sl] = n.astype(BF16)


def _proj_qk(hx, w_qk, gain_row, ones_bd, tm, cos=None, sin=None, tiles_per_batch=None):
    rows = hx.shape[0]
    rope = cos is not None
    in_specs = [pl.BlockSpec((tm, D_MODEL), lambda i, j: (i, 0)),
                pl.BlockSpec((D_MODEL, PROJ_TN), lambda i, j: (0, j)),
                pl.BlockSpec((1, PROJ_TN), lambda i, j: (0, j)),
                pl.BlockSpec((NORM_GROUP, NORM_GROUP), lambda i, j: (0, 0))]
    args = [hx, w_qk, gain_row, ones_bd]
    if rope:
        in_specs += [pl.BlockSpec((tm, NORM_GROUP), lambda i, j: (i % tiles_per_batch, 0))] * 2
        args += [cos, sin]
    return pl.pallas_call(
        functools.partial(_proj_qk_kernel, rope=rope),
        grid=(rows // tm, QK_COLS // PROJ_TN),
        in_specs=in_specs,
        out_specs=pl.BlockSpec((tm, PROJ_TN), lambda i, j: (i, j)),
        out_shape=jax.ShapeDtypeStruct((rows, QK_COLS), BF16),
        compiler_params=_cparams(("parallel", "arbitrary")),
        name="proj_qk_rope" if rope else "proj_qk",
    )(*args)


def _proj_rest_kernel(hx_ref, w_ref, o_ref):
    j = pl.program_id(1)
    acc = _dot(hx_ref[...], w_ref[...])
    sg = _sigmoid(acc)
    o_ref[...] = jnp.where(j < _MG_BLOCKS, sg, jnp.where(j < _GP_BLOCKS_END, acc * sg, acc)).astype(BF16)


def _proj_rest(hx, w_rest, tm):
    rows = hx.shape[0]
    return pl.pallas_call(
        _proj_rest_kernel,
        grid=(rows // tm, REST_COLS // PROJ_TN),
        in_specs=[pl.BlockSpec((tm, D_MODEL), lambda i, j: (i, 0)),
                  pl.BlockSpec((D_MODEL, PROJ_TN), lambda i, j: (0, j))],
        out_specs=pl.BlockSpec((tm, PROJ_TN), lambda i, j: (i, j)),
        out_shape=jax.ShapeDtypeStruct((rows, REST_COLS), BF16),
        compiler_params=_cparams(("parallel", "arbitrary")),
        name="proj_rest",
    )(hx, w_rest)


def _stack_heads(q, first_head, n):
    return jnp.concatenate([q[:, (first_head + g) * HEAD_DIM:(first_head + g + 1) * HEAD_DIM] for g in range(n)],
                           axis=0)


def _softmax_rows(s, extra=None):
    m = jnp.max(s, axis=-1, keepdims=True)
    if extra is not None:
        m = jnp.maximum(m, extra)
    p = jnp.exp2(s - m)
    l = jnp.sum(p, axis=-1, keepdims=True)
    if extra is not None:
        l = l + jnp.exp2(extra - m)
    return p, l


def _diff_lambda(lam_ref, lam_init):
    lf = lam_ref[...]
    a = jnp.sum(lf[0:1] * lf[1:2], axis=-1, keepdims=True)
    b = jnp.sum(lf[2:3] * lf[3:4], axis=-1, keepdims=True)
    return jnp.exp(a) - jnp.exp(b) + lam_init


def _finish_diff(o, subln_ref, lam_init):
    y = o * lax.rsqrt(jnp.mean(o * o, axis=-1, keepdims=True) + EPS) * subln_ref[...]
    return y * (1.0 - lam_init)


def _win_kernel(sink_ref, q_ref, kp_ref, kc_ref, kn_ref, kx_ref, vp_ref, vc_ref, vn_ref, vx_ref, o_ref, *,
                tq, seq):
    i = pl.program_id(1)
    g = A_HEADS // A_KV
    q = q_ref[...]
    k_all = jnp.concatenate([kp_ref[...], kc_ref[...], kn_ref[...], kx_ref[...]], axis=0)
    v_all = jnp.concatenate([vp_ref[...], vc_ref[...], vn_ref[...], vx_ref[...]], axis=0)
    nw = 2 * tq
    t = lax.broadcasted_iota(jnp.int32, (g * tq, nw), 0) & (tq - 1)
    j = lax.broadcasted_iota(jnp.int32, (g * tq, nw), 1)
    rel = j - tq // 2 - t
    kpos = i * tq - tq // 2 + j
    bad = jnp.where(jnp.abs(rel) > A_WINDOW, 1, 0) + jnp.where(kpos < 0, 1, 0) + jnp.where(kpos >= seq, 1, 0)
    for kv in range(A_KV):
        qs = _stack_heads(q, kv * g, g)
        kk = k_all[:, kv * HEAD_DIM:(kv + 1) * HEAD_DIM]
        vv = v_all[:, kv * HEAD_DIM:(kv + 1) * HEAD_DIM]
        s = _dot_nt(qs, kk)
        s_win = jnp.where(bad > 0, NEG_INF, s[:, :nw])
        s_ctx = s[:, nw:]
        sink = jnp.concatenate([jnp.full((tq, 1), sink_ref[kv * g + h], F32) for h in range(g)], axis=0)
        m = jnp.maximum(jnp.maximum(jnp.max(s_win, axis=-1, keepdims=True),
                                    jnp.max(s_ctx, axis=-1, keepdims=True)), sink)
        pw = jnp.exp2(s_win - m)
        pc = jnp.exp2(s_ctx - m)
        l = jnp.sum(pw, axis=-1, keepdims=True) + jnp.sum(pc, axis=-1, keepdims=True) + jnp.exp2(sink - m)
        o = (_dot(pw.astype(BF16), vv[:nw]) + _dot(pc.astype(BF16), vv[nw:])) / l
        for h in range(g):
            c0 = (kv * g + h) * HEAD_DIM
            o_ref[:, c0:c0 + HEAD_DIM] = o[h * tq:(h + 1) * tq].astype(BF16)


def _window_attention(qk_lat, rest_lat, qk_ctx, rest_ctx, sink):
    bsz, seq, _ = qk_lat.shape
    ctx_len = qk_ctx.shape[1]
    tq = 2 * A_WINDOW
    nt = seq // tq
    half = tq // 2
    kcol = QK_OFF["ak"] // LANES
    vcol = REST_OFF["av"] // LANES
    n_half = seq // half

    def prev_map(col):
        return lambda b, i: (b, jnp.maximum(2 * i - 1, 0), col)

    def cur_map(col):
        return lambda b, i: (b, i, col)

    def next_map(col):
        return lambda b, i: (b, jnp.minimum(2 * i + 2, n_half - 1), col)

    def ctx_map(col):
        return lambda b, i: (b, 0, col)

    def kv_specs(col):
        return [pl.BlockSpec((None, half, LANES), prev_map(col)),
                pl.BlockSpec((None, tq, LANES), cur_map(col)),
                pl.BlockSpec((None, half, LANES), next_map(col)),
                pl.BlockSpec((None, ctx_len, LANES), ctx_map(col))]

    return pl.pallas_call(
        functools.partial(_win_kernel, tq=tq, seq=seq),
        grid=(bsz, nt),
        in_specs=[pl.BlockSpec(memory_space=pltpu.SMEM),
                  pl.BlockSpec((None, tq, A_HEADS * HEAD_DIM), lambda b, i: (b, i, QK_OFF["aq"] // 512))]
                 + kv_specs(kcol) + kv_specs(vcol),
        out_specs=pl.BlockSpec((None, tq, BRANCH_W), lambda b, i: (b, i, 0)),
        out_shape=jax.ShapeDtypeStruct((bsz, seq, BRANCH_W), BF16),
        compiler_params=_cparams(("parallel", "parallel")),
        name="mixer_a_window",
    )(sink, qk_lat, qk_lat, qk_lat, qk_lat, qk_ctx, rest_lat, rest_lat, rest_lat, rest_ctx)


def _win_bounded_kernel(sink_ref, q_ref, kp_ref, kc_ref, kn_ref, kx_ref, vp_ref, vc_ref, vn_ref, vx_ref, mask_ref,
                        o_ref, *, tq):
    g = A_HEADS // A_KV
    q = q_ref[...]
    mask = mask_ref[...]
    sink = sink_ref[...]
    for kv in range(A_KV):
        qs = _stack_heads(q, kv * g, g)
        k_t = jnp.concatenate([kp_ref[kv], kc_ref[kv], kn_ref[kv], kx_ref[kv]], axis=1)
        vv = jnp.concatenate([vp_ref[kv], vc_ref[kv], vn_ref[kv], vx_ref[kv]], axis=0)
        stab = [jnp.maximum(sink[:, kv * g + h:kv * g + h + 1], 0.0) for h in range(g)]
        shift = jnp.concatenate([mask - stab[h] for h in range(g)], axis=0)
        p = jnp.exp2(_dot(qs, k_t) + shift).astype(BF16)
        acc = _dot(p, vv)
        for h in range(g):
            a = acc[h * tq:(h + 1) * tq]
            l = a[:, HEAD_DIM:HEAD_DIM + 1] + jnp.exp2(sink[:, kv * g + h:kv * g + h + 1] - stab[h])
            c0 = (kv * g + h) * HEAD_DIM
            o_ref[:, c0:c0 + HEAD_DIM] = (a[:, :HEAD_DIM] / l).astype(BF16)


def _window_mask_tables(seq, tq, ctx_len):
    nt = seq // tq
    tabs = []
    for i0 in (0, 1, nt - 1):
        t = np.arange(tq)[:, None]
        j = np.arange(2 * tq)[None, :]
        kpos = i0 * tq - tq // 2 + j
        ok = (np.abs(j - tq // 2 - t) <= A_WINDOW) & (kpos >= 0) & (kpos < seq)
        win = np.where(ok, 0.0, NEG_INF).astype(np.float32)
        tabs.append(np.concatenate([win, np.zeros((tq, ctx_len), np.float32)], axis=1))
    return jnp.asarray(np.stack(tabs, axis=0))


def _window_attention_bounded(qk_lat, k_t, v_ones, sink2):
    bsz, seq, _ = qk_lat.shape
    ctx_len = k_t.shape[3] - seq
    tq = 2 * A_WINDOW
    nt = seq // tq
    half = tq // 2
    n_half = seq // half
    assert seq % ctx_len == 0 and ctx_len % LANES == 0
    ctx_blk = seq // ctx_len

    def tile_variant(i):
        return jnp.where(i == 0, 0, jnp.where(i == nt - 1, 2, 1))

    return pl.pallas_call(
        functools.partial(_win_bounded_kernel, tq=tq),
        grid=(bsz, nt),
        in_specs=[pl.BlockSpec((1, A_HEADS), lambda b, i: (0, 0)),
                  pl.BlockSpec((None, tq, A_HEADS * HEAD_DIM), lambda b, i: (b, i, QK_OFF["aq"] // 512)),
                  pl.BlockSpec((None, A_KV, HEAD_DIM, half), lambda b, i: (b, 0, 0, jnp.maximum(2 * i - 1, 0))),
                  pl.BlockSpec((None, A_KV, HEAD_DIM, tq), lambda b, i: (b, 0, 0, i)),
                  pl.BlockSpec((None, A_KV, HEAD_DIM, half), lambda b, i: (b, 0, 0, jnp.minimum(2 * i + 2, n_half - 1))),
                  pl.BlockSpec((None, A_KV, HEAD_DIM, ctx_len), lambda b, i: (b, 0, 0, ctx_blk)),
                  pl.BlockSpec((None, A_KV, half, LANES), lambda b, i: (b, 0, jnp.maximum(2 * i - 1, 0), 0)),
                  pl.BlockSpec((None, A_KV, tq, LANES), lambda b, i: (b, 0, i, 0)),
                  pl.BlockSpec((None, A_KV, half, LANES), lambda b, i: (b, 0, jnp.minimum(2 * i + 2, n_half - 1), 0)),
                  pl.BlockSpec((None, A_KV, ctx_len, LANES), lambda b, i: (b, 0, ctx_blk, 0)),
                  pl.BlockSpec((None, tq, 2 * tq + ctx_len), lambda b, i: (tile_variant(i), 0, 0))],
        out_specs=pl.BlockSpec((None, tq, BRANCH_W), lambda b, i: (b, i, 0)),
        out_shape=jax.ShapeDtypeStruct((bsz, seq, BRANCH_W), BF16),
        compiler_params=_cparams(("parallel", "parallel")),
        name="mixer_a_window_bounded",
    )(sink2.reshape(1, A_HEADS), qk_lat, k_t, k_t, k_t, k_t, v_ones, v_ones, v_ones, v_ones,
      _window_mask_tables(seq, tq, ctx_len))


def _nbr_kernel(q_ref, kp_ref, kc_ref, kn_ref, kx_ref, vp_ref, vc_ref, vn_ref, vx_ref, bias_ref, o_ref, *, tq):
    q = q_ref[...]
    k_all = jnp.concatenate([kp_ref[...], kc_ref[...], kn_ref[...], kx_ref[...]], axis=0)
    v_all = jnp.concatenate([vp_ref[...], vc_ref[...], vn_ref[...], vx_ref[...]], axis=0)
    nw = 3 * tq
    for hh in range(2):
        sl = slice(hh * HEAD_DIM, (hh + 1) * HEAD_DIM)
        s = _dot_nt(q[:, sl], k_all[:, sl])
        s_win = s[:, :nw] + bias_ref[hh]
        s_ctx = s[:, nw:]
        m = jnp.maximum(jnp.max(s_win, axis=-1, keepdims=True), jnp.max(s_ctx, axis=-1, keepdims=True))
        pw = jnp.exp2(s_win - m)
        pc = jnp.exp2(s_ctx - m)
        l = jnp.sum(pw, axis=-1, keepdims=True) + jnp.sum(pc, axis=-1, keepdims=True)
        vv = v_all[:, sl]
        o = (_dot(pw.astype(BF16), vv[:nw]) + _dot(pc.astype(BF16), vv[nw:])) / l
        o_ref[:, sl] = o.astype(BF16)


def _nbr_bias_tables(rpb, seq, tq):
    rows = seq // GRID_W
    nt = seq // tq
    kr = min(NB_ROWS, rows)
    qr = tq // GRID_W
    n_heads = rpb.shape[0]
    assert NB_ROWS - 1 - qr - (qr - 1) >= 0 and NB_ROWS - 1 - qr + 3 * qr <= 2 * NB_ROWS - 1
    pad = GRID_W - 1
    rpb_pad = jnp.pad(rpb.astype(F32), ((0, 0), (0, 0), (pad, pad)))
    toep = jnp.stack([rpb_pad[:, :, pad + NB_COLS - 1 - c:pad + NB_COLS - 1 - c + GRID_W] for c in range(GRID_W)],
                     axis=2)
    c = np.arange(GRID_W)
    cstart = np.clip(c - NB_COLS // 2, 0, GRID_W - NB_COLS)
    col_ok = (c[None, :] >= cstart[:, None]) & (c[None, :] < cstart[:, None] + NB_COLS)
    toep = jnp.where(col_ok[None, None], toep, NEG_INF)
    per_rl = [jnp.transpose(toep[:, NB_ROWS - 1 - qr - rl:NB_ROWS - 1 - qr - rl + 3 * qr], (0, 2, 1, 3))
              for rl in range(qr)]
    base = jnp.stack(per_rl, axis=1)
    tabs = []
    for i0 in (0, 1, nt - 1):
        r = i0 * qr + np.arange(qr)
        r2 = (i0 - 1) * qr + np.arange(3 * qr)
        rstart = np.clip(r - kr // 2, 0, rows - kr)
        row_ok = ((r2[None, :] >= rstart[:, None]) & (r2[None, :] < rstart[:, None] + kr)
                  & (r2[None, :] >= 0) & (r2[None, :] < rows))
        tab = jnp.where(row_ok[None, :, None, :, None], base, NEG_INF)
        tabs.append(tab.reshape(n_heads, tq, 3 * tq))
    return jnp.stack(tabs, axis=0)


def _neighborhood_attention(qk_lat, rest_lat, qk_ctx, rest_ctx, bias_tabs):
    bsz, seq, _ = qk_lat.shape
    ctx_len = qk_ctx.shape[1]
    tq = bias_tabs.shape[2]
    nt = seq // tq
    qcol = QK_OFF["bq"] // LANES
    kcol = QK_OFF["bk"] // LANES
    vcol = REST_OFF["bv"] // LANES

    def kv_specs(col):
        return [pl.BlockSpec((None, tq, LANES), lambda hp, b, i: (b, jnp.maximum(i - 1, 0), col + hp)),
                pl.BlockSpec((None, tq, LANES), lambda hp, b, i: (b, i, col + hp)),
                pl.BlockSpec((None, tq, LANES), lambda hp, b, i: (b, jnp.minimum(i + 1, nt - 1), col + hp)),
                pl.BlockSpec((None, ctx_len, LANES), lambda hp, b, i: (b, 0, col + hp))]

    def bias_map(hp, b, i):
        return (jnp.where(i == 0, 0, jnp.where(i == nt - 1, 2, 1)), hp, 0, 0)

    return pl.pallas_call(
        functools.partial(_nbr_kernel, tq=tq),
        grid=(B_HEADS // 2, bsz, nt),
        in_specs=[pl.BlockSpec((None, tq, LANES), lambda hp, b, i: (b, i, qcol + hp))]
                 + kv_specs(kcol) + kv_specs(vcol)
                 + [pl.BlockSpec((None, 2, tq, 3 * tq), bias_map)],
        out_specs=pl.BlockSpec((None, tq, LANES), lambda hp, b, i: (b, i, hp)),
        out_shape=jax.ShapeDtypeStruct((bsz, seq, BRANCH_W), BF16),
        compiler_params=_cparams(("parallel", "parallel", "parallel")),
        name="mixer_b_neighbourhood",
    )(qk_lat, qk_lat, qk_lat, qk_lat, qk_ctx, rest_lat, rest_lat, rest_lat, rest_ctx, bias_tabs)


def _nbr_bounded_kernel(q_ref, kp_ref, kc_ref, kn_ref, kx_ref, vp_ref, vc_ref, vn_ref, vx_ref, bias_ref, o_ref, *,
                        tq):
    q = q_ref[...]
    nw = 3 * tq
    for h in range(B_HEADS):
        k_t = jnp.concatenate([kp_ref[h], kc_ref[h], kn_ref[h], kx_ref[h]], axis=1)
        vv = jnp.concatenate([vp_ref[h], vc_ref[h], vn_ref[h], vx_ref[h]], axis=0)
        s = _dot(q[:, h * HEAD_DIM:(h + 1) * HEAD_DIM], k_t)
        p = jnp.exp2(jnp.concatenate([s[:, :nw] + bias_ref[h], s[:, nw:]], axis=1)).astype(BF16)
        acc = _dot(p, vv)
        o_ref[:, h * HEAD_DIM:(h + 1) * HEAD_DIM] = (acc[:, :HEAD_DIM] / acc[:, HEAD_DIM:HEAD_DIM + 1]).astype(BF16)


def _neighborhood_attention_bounded(qk_lat, k_t, v_ones, bias_tabs):
    bsz, seq, _ = qk_lat.shape
    ctx_len = k_t.shape[3] - seq
    tq = bias_tabs.shape[2]
    nt = seq // tq
    assert seq % ctx_len == 0 and tq == ctx_len
    ctx_blk = seq // ctx_len

    def prev(i):
        return jnp.maximum(i - 1, 0)

    def nxt(i):
        return jnp.minimum(i + 1, nt - 1)

    def bias_map(b, i):
        return (jnp.where(i == 0, 0, jnp.where(i == nt - 1, 2, 1)), 0, 0, 0)

    kt_blk = (None, B_HEADS, HEAD_DIM, tq)
    v_blk = (None, B_HEADS, tq, LANES)
    return pl.pallas_call(
        functools.partial(_nbr_bounded_kernel, tq=tq),
        grid=(bsz, nt),
        in_specs=[pl.BlockSpec((None, tq, B_HEADS * HEAD_DIM), lambda b, i: (b, i, QK_OFF["bq"] // 512)),
                  pl.BlockSpec(kt_blk, lambda b, i: (b, 0, 0, prev(i))),
                  pl.BlockSpec(kt_blk, lambda b, i: (b, 0, 0, i)),
                  pl.BlockSpec(kt_blk, lambda b, i: (b, 0, 0, nxt(i))),
                  pl.BlockSpec(kt_blk, lambda b, i: (b, 0, 0, ctx_blk)),
                  pl.BlockSpec(v_blk, lambda b, i: (b, 0, prev(i), 0)),
                  pl.BlockSpec(v_blk, lambda b, i: (b, 0, i, 0)),
                  pl.BlockSpec(v_blk, lambda b, i: (b, 0, nxt(i), 0)),
                  pl.BlockSpec(v_blk, lambda b, i: (b, 0, ctx_blk, 0)),
                  pl.BlockSpec((None, B_HEADS, tq, 3 * tq), bias_map)],
        out_specs=pl.BlockSpec((None, tq, BRANCH_W), lambda b, i: (b, i, 0)),
        out_shape=jax.ShapeDtypeStruct((bsz, seq, BRANCH_W), BF16),
        compiler_params=_cparams(("parallel", "parallel")),
        name="mixer_b_neighbourhood_bounded",
    )(qk_lat, k_t, k_t, k_t, k_t, v_ones, v_ones, v_ones, v_ones, bias_tabs)


def _dense_c_kernel(q_ref, k_ref, v_ref, o_ref, qs_ref, m_ref, l_ref, acc_ref, *, tq, nk):
    kt = pl.program_id(2)
    g = C_HEADS // C_KV

    @pl.when(kt == 0)
    def _():
        q = q_ref[...]
        for kv in range(C_KV):
            qs_ref[kv] = _stack_heads(q, kv * g, g)
        m_ref[...] = jnp.full(m_ref.shape, -jnp.inf, F32)
        l_ref[...] = jnp.zeros(l_ref.shape, F32)
        acc_ref[...] = jnp.zeros(acc_ref.shape, F32)

    k = k_ref[...]
    v = v_ref[...]
    for kv in range(C_KV):
        sl = slice(kv * HEAD_DIM, (kv + 1) * HEAD_DIM)
        s = _dot_nt(qs_ref[kv], k[:, sl])
        m_prev = m_ref[kv]
        m_new = jnp.maximum(m_prev, jnp.max(s, axis=-1, keepdims=True))
        alpha = jnp.exp2(m_prev - m_new)
        p = jnp.exp2(s - m_new)
        l_ref[kv] = alpha * l_ref[kv] + jnp.sum(p, axis=-1, keepdims=True)
        acc_ref[kv] = alpha * acc_ref[kv] + _dot(p.astype(BF16), v[:, sl])
        m_ref[kv] = m_new

    @pl.when(kt == nk - 1)
    def _():
        for kv in range(C_KV):
            o = acc_ref[kv] / l_ref[kv]
            for h in range(g):
                c0 = (kv * g + h) * HEAD_DIM
                o_ref[:, c0:c0 + HEAD_DIM] = o[h * tq:(h + 1) * tq].astype(BF16)


def _dense_gqa(qk_lat, k_all, v_all, tq, tk):
    bsz, seq, _ = qk_lat.shape
    keys = k_all.shape[1]
    nk = keys // tk
    g = C_HEADS // C_KV
    return pl.pallas_call(
        functools.partial(_dense_c_kernel, tq=tq, nk=nk),
        grid=(bsz, seq // tq, nk),
        in_specs=[pl.BlockSpec((None, tq, C_HEADS * HEAD_DIM), lambda b, i, kt: (b, i, QK_OFF["cq"] // 512)),
                  pl.BlockSpec((None, tk, C_KV * HEAD_DIM), lambda b, i, kt: (b, kt, 0)),
                  pl.BlockSpec((None, tk, C_KV * HEAD_DIM), lambda b, i, kt: (b, kt, 0))],
        out_specs=pl.BlockSpec((None, tq, BRANCH_W), lambda b, i, kt: (b, i, 0)),
        out_shape=jax.ShapeDtypeStruct((bsz, seq, BRANCH_W), BF16),
        scratch_shapes=[pltpu.VMEM((C_KV, g * tq, HEAD_DIM), BF16),
                        pltpu.VMEM((C_KV, g * tq, 1), F32),
                        pltpu.VMEM((C_KV, g * tq, 1), F32),
                        pltpu.VMEM((C_KV, g * tq, HEAD_DIM), F32)],
        compiler_params=_cparams(("parallel", "parallel", "arbitrary")),
        name="mixer_c_dense",
    )(qk_lat, k_all, v_all)


def _dense_c_bounded_kernel(q_ref, kt_ref, v_ref, o_ref, qs_ref, acc_ref, *, tq, nk):
    kt = pl.program_id(2)
    g = C_HEADS // C_KV

    @pl.when(kt == 0)
    def _():
        q = q_ref[...]
        for kv in range(C_KV):
            qs_ref[kv] = _stack_heads(q, kv * g, g)
        acc_ref[...] = jnp.zeros(acc_ref.shape, F32)

    for kv in range(C_KV):
        p = jnp.exp2(_dot(qs_ref[kv], kt_ref[kv])).astype(BF16)
        acc_ref[kv] += _dot(p, v_ref[kv])

    @pl.when(kt == nk - 1)
    def _():
        for kv in range(C_KV):
            acc = acc_ref[kv]
            o = acc[:, :HEAD_DIM] / acc[:, HEAD_DIM:HEAD_DIM + 1]
            for h in range(g):
                c0 = (kv * g + h) * HEAD_DIM
                o_ref[:, c0:c0 + HEAD_DIM] = o[h * tq:(h + 1) * tq].astype(BF16)


def _dense_gqa_bounded(qk_lat, k_t, v_ones, tq, tk):
    bsz, seq, _ = qk_lat.shape
    keys = k_t.shape[3]
    nk = keys // tk
    g = C_HEADS // C_KV
    return pl.pallas_call(
        functools.partial(_dense_c_bounded_kernel, tq=tq, nk=nk),
        grid=(bsz, seq // tq, nk),
        in_specs=[pl.BlockSpec((None, tq, C_HEADS * HEAD_DIM), lambda b, i, kt: (b, i, QK_OFF["cq"] // 512)),
                  pl.BlockSpec((None, C_KV, HEAD_DIM, tk), lambda b, i, kt: (b, 0, 0, kt)),
                  pl.BlockSpec((None, C_KV, tk, LANES), lambda b, i, kt: (b, 0, kt, 0))],
        out_specs=pl.BlockSpec((None, tq, BRANCH_W), lambda b, i, kt: (b, i, 0)),
        out_shape=jax.ShapeDtypeStruct((bsz, seq, BRANCH_W), BF16),
        scratch_shapes=[pltpu.VMEM((C_KV, g * tq, HEAD_DIM), BF16),
                        pltpu.VMEM((C_KV, g * tq, LANES), F32)],
        compiler_params=_cparams(("parallel", "parallel", "arbitrary")),
        name="mixer_c_dense_bounded",
    )(qk_lat, k_t, v_ones)


def _dense_d_kernel(q_ref, k_ref, v_ref, lam_ref, subln_ref, o_ref, qs_ref, m_ref, l_ref, acc_ref, *,
                    nk, lam_init):
    kt = pl.program_id(2)
    n_sc = 2 * D_HEADS

    @pl.when(kt == 0)
    def _():
        q = q_ref[...]
        for idx in range(n_sc):
            qs_ref[idx] = q[:, idx * HEAD_DIM:(idx + 1) * HEAD_DIM]
        m_ref[...] = jnp.full(m_ref.shape, -jnp.inf, F32)
        l_ref[...] = jnp.zeros(l_ref.shape, F32)
        acc_ref[...] = jnp.zeros(acc_ref.shape, F32)

    k = k_ref[...]
    v = v_ref[...]
    for idx in range(n_sc):
        h = idx // 2
        s = _dot_nt(qs_ref[idx], k[:, idx * HEAD_DIM:(idx + 1) * HEAD_DIM])
        m_prev = m_ref[idx]
        m_new = jnp.maximum(m_prev, jnp.max(s, axis=-1, keepdims=True))
        alpha = jnp.exp2(m_prev - m_new)
        p = jnp.exp2(s - m_new)
        l_ref[idx] = alpha * l_ref[idx] + jnp.sum(p, axis=-1, keepdims=True)
        acc_ref[idx] = alpha * acc_ref[idx] + _dot(p.astype(BF16), v[:, h * 2 * HEAD_DIM:(h + 1) * 2 * HEAD_DIM])
        m_ref[idx] = m_new

    @pl.when(kt == nk - 1)
    def _():
        lam = _diff_lambda(lam_ref, lam_init)
        for h in range(D_HEADS):
            o = acc_ref[2 * h] / l_ref[2 * h] - lam * (acc_ref[2 * h + 1] / l_ref[2 * h + 1])
            o_ref[:, h * 2 * HEAD_DIM:(h + 1) * 2 * HEAD_DIM] = _finish_diff(o, subln_ref, lam_init).astype(BF16)


def _dense_diff(qk_lat, k_all, v_all, lam_d, subln, lam_init, tq, tk):
    bsz, seq, _ = qk_lat.shape
    keys = k_all.shape[1]
    nk = keys // tk
    width = D_HEADS * 2 * HEAD_DIM
    return pl.pallas_call(
        functools.partial(_dense_d_kernel, nk=nk, lam_init=lam_init),
        grid=(bsz, seq // tq, nk),
        in_specs=[pl.BlockSpec((None, tq, width), lambda b, i, kt: (b, i, QK_OFF["dq"] // 512)),
                  pl.BlockSpec((None, tk, width), lambda b, i, kt: (b, kt, 0)),
                  pl.BlockSpec((None, tk, width), lambda b, i, kt: (b, kt, 0)),
                  pl.BlockSpec((4, HEAD_DIM), lambda b, i, kt: (0, 0)),
                  pl.BlockSpec((1, 2 * HEAD_DIM), lambda b, i, kt: (0, 0))],
        out_specs=pl.BlockSpec((None, tq, BRANCH_W), lambda b, i, kt: (b, i, 0)),
        out_shape=jax.ShapeDtypeStruct((bsz, seq, BRANCH_W), BF16),
        scratch_shapes=[pltpu.VMEM((2 * D_HEADS, tq, HEAD_DIM), BF16),
                        pltpu.VMEM((2 * D_HEADS, tq, 1), F32),
                        pltpu.VMEM((2 * D_HEADS, tq, 1), F32),
                        pltpu.VMEM((2 * D_HEADS, tq, 2 * HEAD_DIM), F32)],
        compiler_params=_cparams(("parallel", "parallel", "arbitrary")),
        name="mixer_d_diff",
    )(qk_lat, k_all, v_all, lam_d, subln.reshape(1, 2 * HEAD_DIM))


def _dense_d_bounded_kernel(q_ref, kt_ref, v_ref, lam_ref, subln_ref, o_ref, qs_ref, l_ref, acc_ref, *,
                            nk, tk, lam_init):
    kt = pl.program_id(2)
    n_sc = 2 * D_HEADS

    @pl.when(kt == 0)
    def _():
        q = q_ref[...]
        for idx in range(n_sc):
            qs_ref[idx] = q[:, idx * HEAD_DIM:(idx + 1) * HEAD_DIM]
        l_ref[...] = jnp.zeros(l_ref.shape, F32)
        acc_ref[...] = jnp.zeros(acc_ref.shape, F32)

    for idx in range(n_sc):
        h = idx // 2
        p = jnp.exp2(_dot(qs_ref[idx], kt_ref[idx]))
        part = p[:, :LANES]
        for c in range(1, tk // LANES):
            part = part + p[:, c * LANES:(c + 1) * LANES]
        l_ref[idx] += part
        acc_ref[idx] += _dot(p.astype(BF16), v_ref[:, h * 2 * HEAD_DIM:(h + 1) * 2 * HEAD_DIM])

    @pl.when(kt == nk - 1)
    def _():
        lam = _diff_lambda(lam_ref, lam_init)
        for h in range(D_HEADS):
            l1 = jnp.sum(l_ref[2 * h], axis=-1, keepdims=True)
            l2 = jnp.sum(l_ref[2 * h + 1], axis=-1, keepdims=True)
            o = acc_ref[2 * h] / l1 - lam * (acc_ref[2 * h + 1] / l2)
            o_ref[:, h * 2 * HEAD_DIM:(h + 1) * 2 * HEAD_DIM] = _finish_diff(o, subln_ref, lam_init).astype(BF16)


def _dense_diff_bounded(qk_lat, k_t, v_all, lam_d, subln, lam_init, tq, tk):
    bsz, seq, _ = qk_lat.shape
    keys = k_t.shape[3]
    nk = keys // tk
    width = D_HEADS * 2 * HEAD_DIM
    return pl.pallas_call(
        functools.partial(_dense_d_bounded_kernel, nk=nk, tk=tk, lam_init=lam_init),
        grid=(bsz, seq // tq, nk),
        in_specs=[pl.BlockSpec((None, tq, width), lambda b, i, kt: (b, i, QK_OFF["dq"] // 512)),
                  pl.BlockSpec((None, 2 * D_HEADS, HEAD_DIM, tk), lambda b, i, kt: (b, 0, 0, kt)),
                  pl.BlockSpec((None, tk, width), lambda b, i, kt: (b, kt, 0)),
                  pl.BlockSpec((4, HEAD_DIM), lambda b, i, kt: (0, 0)),
                  pl.BlockSpec((1, 2 * HEAD_DIM), lambda b, i, kt: (0, 0))],
        out_specs=pl.BlockSpec((None, tq, BRANCH_W), lambda b, i, kt: (b, i, 0)),
        out_shape=jax.ShapeDtypeStruct((bsz, seq, BRANCH_W), BF16),
        scratch_shapes=[pltpu.VMEM((2 * D_HEADS, tq, HEAD_DIM), BF16),
                        pltpu.VMEM((2 * D_HEADS, tq, LANES), F32),
                        pltpu.VMEM((2 * D_HEADS, tq, 2 * HEAD_DIM), F32)],
        compiler_params=_cparams(("parallel", "parallel", "arbitrary")),
        name="mixer_d_diff_bounded",
    )(qk_lat, k_t, v_all, lam_d, subln.reshape(1, 2 * HEAD_DIM))


def _ctx_kernel(sink_ref, qk_ref, v_ref, lam_ref, subln_ref, o_ref, *, lam_init):
    qk = qk_ref[...]
    vals = v_ref[...]
    ctx_len = qk.shape[0]
    v_off = {name: REST_OFF[name] - REST_OFF["av"] for name in ("av", "cv", "bv", "dv")}

    def cols(name, start, width):
        c0 = QK_OFF[name] + start
        return qk[:, c0:c0 + width]

    def gqa(qname, kname, vname, out_off, n_kv, with_sink):
        g = 8 // n_kv
        for kv in range(n_kv):
            qs = _stack_heads(cols(qname, 0, 8 * HEAD_DIM), kv * g, g)
            s = _dot_nt(qs, cols(kname, kv * HEAD_DIM, HEAD_DIM))
            extra = None
            if with_sink:
                extra = jnp.concatenate([jnp.full((ctx_len, 1), sink_ref[kv * g + h], F32) for h in range(g)], axis=0)
            p, l = _softmax_rows(s, extra)
            vv = vals[:, v_off[vname] + kv * HEAD_DIM:v_off[vname] + (kv + 1) * HEAD_DIM]
            o = _dot(p.astype(BF16), vv) / l
            for h in range(g):
                c0 = out_off + (kv * g + h) * HEAD_DIM
                o_ref[:, c0:c0 + HEAD_DIM] = o[h * ctx_len:(h + 1) * ctx_len].astype(BF16)

    gqa("aq", "ak", "av", 0 * BRANCH_W, A_KV, True)
    gqa("bq", "bk", "bv", 1 * BRANCH_W, B_HEADS, False)
    gqa("cq", "ck", "cv", 2 * BRANCH_W, C_KV, False)

    lam = _diff_lambda(lam_ref, lam_init)
    for h in range(D_HEADS):
        base = h * 2 * HEAD_DIM
        p1, l1 = _softmax_rows(_dot_nt(cols("dq", base, HEAD_DIM), cols("dk", base, HEAD_DIM)))
        p2, l2 = _softmax_rows(_dot_nt(cols("dq", base + HEAD_DIM, HEAD_DIM), cols("dk", base + HEAD_DIM, HEAD_DIM)))
        pd = p1 / l1 - lam * (p2 / l2)
        o = _dot(pd.astype(BF16), vals[:, v_off["dv"] + base:v_off["dv"] + base + 2 * HEAD_DIM])
        c0 = 3 * BRANCH_W + base
        o_ref[:, c0:c0 + 2 * HEAD_DIM] = _finish_diff(o, subln_ref, lam_init).astype(BF16)


def _ctx_attention(qk_ctx, v_ctx, sink, lam_d, subln, lam_init):
    bsz, ctx_len, _ = qk_ctx.shape
    return pl.pallas_call(
        functools.partial(_ctx_kernel, lam_init=lam_init),
        grid=(bsz,),
        in_specs=[pl.BlockSpec(memory_space=pltpu.SMEM),
                  pl.BlockSpec((None, ctx_len, QK_COLS), lambda b: (b, 0, 0)),
                  pl.BlockSpec((None, ctx_len, v_ctx.shape[2]), lambda b: (b, 0, 0)),
                  pl.BlockSpec((4, HEAD_DIM), lambda b: (0, 0)),
                  pl.BlockSpec((1, 2 * HEAD_DIM), lambda b: (0, 0))],
        out_specs=pl.BlockSpec((None, ctx_len, N_BRANCH * BRANCH_W), lambda b: (b, 0, 0)),
        out_shape=jax.ShapeDtypeStruct((bsz, ctx_len, N_BRANCH * BRANCH_W), BF16),
        compiler_params=_cparams(("parallel",)),
        name="ctx_attention",
    )(sink, qk_ctx, v_ctx, lam_d, subln.reshape(1, 2 * HEAD_DIM))


def _merge_kernel(x_ref, mod_ref, ya_ref, yb_ref, yc_ref, yd_ref, gp_ref, mg_ref, wbr_ref, wout_ref, o_ref):
    gate = mod_ref[...][:, 2 * D_MODEL:]
    merged = None
    for n, y_ref in enumerate((ya_ref, yb_ref, yc_ref, yd_ref)):
        yg = (y_ref[...].astype(F32) * gp_ref[:, n * BRANCH_W:(n + 1) * BRANCH_W].astype(F32)).astype(BF16)
        term = mg_ref[:, n * D_MODEL:(n + 1) * D_MODEL].astype(F32) * _dot(yg, wbr_ref[n])
        merged = term if merged is None else merged + term
    o_ref[...] = x_ref[...] + gate * _dot(merged.astype(BF16), wout_ref[...])


def _merge(x2, mod3, ys, y_cols, rest, w_br, w_out, tm, row_of_tile):
    rows = x2.shape[0]
    y_specs = [pl.BlockSpec((tm, BRANCH_W), (lambda i, c=c: (i, c))) for c in y_cols]
    return pl.pallas_call(
        _merge_kernel,
        grid=(rows // tm,),
        in_specs=[pl.BlockSpec((tm, D_MODEL), lambda i: (i, 0)),
                  pl.BlockSpec((None, 1, 3 * D_MODEL), lambda i: (row_of_tile(i), 0, 0))]
                 + y_specs
                 + [pl.BlockSpec((tm, N_BRANCH * BRANCH_W), lambda i: (i, REST_OFF["ag"] // (N_BRANCH * BRANCH_W))),
                    pl.BlockSpec((tm, N_BRANCH * D_MODEL), lambda i: (i, 0)),
                    pl.BlockSpec((N_BRANCH, BRANCH_W, D_MODEL), lambda i: (0, 0, 0)),
                    pl.BlockSpec((D_MODEL, D_MODEL), lambda i: (0, 0))],
        out_specs=pl.BlockSpec((tm, D_MODEL), lambda i: (i, 0)),
        out_shape=jax.ShapeDtypeStruct((rows, D_MODEL), F32),
        compiler_params=_cparams(("parallel",)),
        name="gated_merge",
    )(x2, mod3, *ys, rest, rest, w_br, w_out)


def _rope_tables(seq):
    t = jnp.arange(seq, dtype=jnp.int32)
    pos = jnp.stack([t // GRID_W, t % GRID_W], axis=-1).astype(F32)
    n_freq = HEAD_DIM // 4
    freqs = ROPE_THETA ** (-jnp.arange(n_freq, dtype=F32) / n_freq)
    ang = pos[:, :, None] * freqs[None, None, :]
    ang = jnp.concatenate([ang, ang], axis=-1).reshape(seq, HEAD_DIM)
    sign = np.where((np.arange(HEAD_DIM) % 32) < 16, -1.0, 1.0).astype(np.float32)
    reps = NORM_GROUP // HEAD_DIM
    return jnp.tile(jnp.cos(ang), (1, reps)), jnp.tile(jnp.sin(ang) * sign, (1, reps))


def _regroup_cols(w, order, total):
    parts = [w[:, _ORIG[n][0]:_ORIG[n][1]] for n in order]
    used = sum(p.shape[1] for p in parts)
    if total > used:
        parts.append(jnp.zeros((w.shape[0], total - used), w.dtype))
    return jnp.concatenate(parts, axis=1)


def _qk_gain_row(g):
    parts = []
    for name in _QK_ORDER:
        mixer = "abcd".index(name[0])
        is_q = name[1] == "q"
        width = _ORIG[name][1] - _ORIG[name][0]
        gain = g[mixer, 0] * (QK_SCALE * LOG2E) if is_q else g[mixer, 1]
        parts.append(jnp.tile(gain, width // HEAD_DIM))
    parts.append(jnp.ones((QK_COLS - _QK_USED,), g.dtype))
    return jnp.concatenate(parts).reshape(1, QK_COLS).astype(F32)


def kernel(x, c, ctx, c_ctx, norm_w, w_ada, b_ada, w_in, qk_gain, sink_a, rpb_b, lam_d, subln_d, w_br, w_out):
    bsz, seq, _ = x.shape
    ctx_len = ctx.shape[1]
    depth = w_ada.shape[0]
    assert seq % (2 * A_WINDOW) == 0 and ctx_len % LANES == 0 and bsz <= 6

    tm_lat = 2048
    tiles_per_batch = seq // tm_lat
    tq_dense, tk_dense = 256, 768
    tq_bounded = 512
    assert (seq + ctx_len) % tk_dense == 0
    tm_merge = 512
    merge_tiles_per_batch = seq // tm_merge

    cvec = jnp.concatenate([c, c_ctx[None, :], jnp.zeros((8 - bsz - 1, D_MODEL), F32)], axis=0)
    mod_all = _ada(cvec, w_ada, b_ada)
    cos, sin = _rope_tables(seq)
    ones_bd = jnp.asarray(np.kron(np.eye(NORM_GROUP // HEAD_DIM), np.ones((HEAD_DIM, HEAD_DIM))), BF16)

    x2 = x.reshape(bsz * seq, D_MODEL)
    c2 = ctx.reshape(bsz * ctx_len, D_MODEL)
    for l in range(depth):
        need_ctx = l < depth - 1
        lam_init = 0.8 - 0.6 * math.exp(-0.3 * l)
        mod3 = mod_all[l].reshape(8, 1, 3 * D_MODEL)
        w_qk = _regroup_cols(w_in[l], _QK_ORDER, QK_COLS).astype(BF16)
        w_rest = _regroup_cols(w_in[l], _REST_ORDER, REST_COLS).astype(BF16)
        gain_row = _qk_gain_row(qk_gain[l])

        hx = _prenorm(x2, norm_w[l], mod3, tm_lat, lambda i: i // tiles_per_batch)
        hc = _prenorm(c2, norm_w[l], mod3, bsz * ctx_len, lambda i: bsz)
        qk_lat = _proj_qk(hx, w_qk, gain_row, ones_bd, tm_lat, cos, sin, tiles_per_batch).reshape(bsz, seq, QK_COLS)
        qk_ctx = _proj_qk(hc, w_qk, gain_row, ones_bd, bsz * ctx_len).reshape(bsz, ctx_len, QK_COLS)
        rest_lat = _proj_rest(hx, w_rest, tm_lat).reshape(bsz, seq, REST_COLS)
        rest_ctx = _proj_rest(hc, w_rest, bsz * ctx_len).reshape(bsz, ctx_len, REST_COLS)

        n_keys = seq + ctx_len
        sink2 = sink_a[l] * LOG2E
        bias_tabs = _nbr_bias_tables(rpb_b[l] * LOG2E, seq, 2 * A_WINDOW)

        def keys_of(arr_lat, arr_ctx, off, width):
            return jnp.concatenate([arr_lat[:, :, off:off + width], arr_ctx[:, :, off:off + width]], axis=1)

        def heads_t(k, n):
            return jnp.transpose(k.reshape(bsz, n_keys, n, HEAD_DIM), (0, 2, 3, 1))

        def values_with_ones(v, n):
            v4 = jnp.transpose(v.reshape(bsz, n_keys, n, HEAD_DIM), (0, 2, 1, 3))
            ones_col = jnp.zeros((bsz, n, n_keys, LANES - HEAD_DIM), BF16).at[..., 0].set(1.0)
            return jnp.concatenate([v4, ones_col], axis=-1)

        def qk_bound(g):
            return HEAD_DIM * QK_SCALE * LOG2E * jnp.max(jnp.abs(g[0])) * jnp.max(jnp.abs(g[1])) * 1.02

        y_a = lax.cond(
            qk_bound(qk_gain[l, 0]) <= LOGIT_BOUND,
            lambda: _window_attention_bounded(
                qk_lat, heads_t(keys_of(qk_lat, qk_ctx, QK_OFF["ak"], A_KV * HEAD_DIM), A_KV),
                values_with_ones(keys_of(rest_lat, rest_ctx, REST_OFF["av"], A_KV * HEAD_DIM), A_KV), sink2),
            lambda: _window_attention(qk_lat, rest_lat, qk_ctx, rest_ctx, sink2))
        y_b = lax.cond(
            qk_bound(qk_gain[l, 1]) + jnp.max(jnp.abs(rpb_b[l])) * LOG2E <= LOGIT_BOUND,
            lambda: _neighborhood_attention_bounded(
                qk_lat, heads_t(keys_of(qk_lat, qk_ctx, QK_OFF["bk"], B_HEADS * HEAD_DIM), B_HEADS),
                values_with_ones(keys_of(rest_lat, rest_ctx, REST_OFF["bv"], B_HEADS * HEAD_DIM), B_HEADS), bias_tabs),
            lambda: _neighborhood_attention(qk_lat, rest_lat, qk_ctx, rest_ctx, bias_tabs))
        y_c = lax.cond(
            qk_bound(qk_gain[l, 2]) <= LOGIT_BOUND,
            lambda: _dense_gqa_bounded(
                qk_lat, heads_t(keys_of(qk_lat, qk_ctx, QK_OFF["ck"], C_KV * HEAD_DIM), C_KV),
                values_with_ones(keys_of(rest_lat, rest_ctx, REST_OFF["cv"], C_KV * HEAD_DIM), C_KV),
                tq_bounded, tk_dense),
            lambda: _dense_gqa(qk_lat, keys_of(qk_lat, qk_ctx, QK_OFF["ck"], C_KV * HEAD_DIM),
                               keys_of(rest_lat, rest_ctx, REST_OFF["cv"], C_KV * HEAD_DIM), tq_dense, tk_dense))
        k_d = keys_of(qk_lat, qk_ctx, QK_OFF["dk"], D_HEADS * 2 * HEAD_DIM)
        v_d = keys_of(rest_lat, rest_ctx, REST_OFF["dv"], D_HEADS * 2 * HEAD_DIM)
        y_d = lax.cond(
            qk_bound(qk_gain[l, 3]) <= LOGIT_BOUND,
            lambda: _dense_diff_bounded(qk_lat, heads_t(k_d, 2 * D_HEADS), v_d, lam_d[l], subln_d[l], lam_init,
                                        tq_bounded, tk_dense),
            lambda: _dense_diff(qk_lat, k_d, v_d, lam_d[l], subln_d[l], lam_init, tq_dense, tk_dense))

        w_br_l = w_br[l].astype(BF16)
        w_out_l = w_out[l].astype(BF16)
        ys = [y.reshape(bsz * seq, BRANCH_W) for y in (y_a, y_b, y_c, y_d)]
        x_new = _merge(x2, mod3, ys, (0, 0, 0, 0), rest_lat.reshape(bsz * seq, REST_COLS), w_br_l, w_out_l,
                       tm_merge, lambda i: i // merge_tiles_per_batch)
        if need_ctx:
            y_ctx = _ctx_attention(qk_ctx, rest_ctx[:, :, REST_OFF["av"]:], sink2, lam_d[l], subln_d[l], lam_init)
            y_ctx2 = y_ctx.reshape(bsz * ctx_len, N_BRANCH * BRANCH_W)
            c2 = _merge(c2, mod3, [y_ctx2] * 4, (0, 1, 2, 3), rest_ctx.reshape(bsz * ctx_len, REST_COLS),
                        w_br_l, w_out_l, bsz * ctx_len, lambda i: bsz)
        x2 = x_new
    return x2.reshape(bsz, seq, D_MODEL)
```

```python
import functools
import math

import numpy as np
import jax
import jax.numpy as jnp
from jax import lax
from jax.experimental import pallas as pl
from jax.experimental.pallas import tpu as pltpu

F32 = jnp.float32
BF16 = jnp.bfloat16

D_MODEL = 1024
GRID_W = 64
HEAD_DIM = 64
BRANCH_W = 512
N_BRANCH = 4
A_HEADS, A_KV, A_WINDOW = 8, 2, 128
B_HEADS, NB_ROWS, NB_COLS = 8, 8, 16
C_HEADS, C_KV = 8, 2
D_HEADS = 4
ROPE_THETA = 10000.0
EPS = 1e-6
NEG_INF = -1e30
QK_SCALE = HEAD_DIM ** -0.5
LOG2E = math.log2(math.e)
LOGIT_BOUND = 60.0

V7X_VMEM_LIMIT_BYTES = 56 * 1024 * 1024
LANES = 128

_ORIG = dict(aq=(0, 512), ak=(512, 640), av=(640, 768), ag=(768, 1280),
             bq=(1280, 1792), bk=(1792, 2304), bv=(2304, 2816), bg=(2816, 3328),
             cq=(3328, 3840), ck=(3840, 3968), cv=(3968, 4096), cg=(4096, 4608),
             dq=(4608, 5120), dk=(5120, 5632), dv=(5632, 6144), dg=(6144, 6656),
             mg=(6656, 10752))
_QK_ORDER = ("aq", "cq", "dq", "bq", "bk", "dk", "ak", "ck")
_REST_ORDER = ("mg", "ag", "bg", "cg", "dg", "dv", "bv", "av", "cv")
_FIRST_VALUE = "dv"


def _offsets(order):
    off, out = 0, {}
    for name in order:
        lo, hi = _ORIG[name]
        out[name] = off
        off += hi - lo
    return out, off


PROJ_TN = 512
NORM_GROUP = 256
QK_OFF, _QK_USED = _offsets(_QK_ORDER)
REST_OFF, _REST_USED = _offsets(_REST_ORDER)
QK_COLS = -(-_QK_USED // PROJ_TN) * PROJ_TN
REST_COLS = -(-_REST_USED // PROJ_TN) * PROJ_TN
_ROPE_LO_END = QK_OFF["bq"] // PROJ_TN
_ROPE_HI_START = QK_OFF["dk"] // PROJ_TN
_MG_BLOCKS = (REST_OFF["ag"]) // PROJ_TN
_GP_BLOCKS_END = REST_OFF[_FIRST_VALUE] // PROJ_TN
assert QK_OFF["bq"] % PROJ_TN == 0 and QK_OFF["dk"] % PROJ_TN == 0
assert REST_OFF["ag"] % PROJ_TN == 0 and REST_OFF[_FIRST_VALUE] % PROJ_TN == 0


def _cparams(sem):
    return pltpu.CompilerParams(dimension_semantics=sem, vmem_limit_bytes=V7X_VMEM_LIMIT_BYTES)


def _sigmoid(x):
    return 1.0 / (1.0 + jnp.exp(-x))


def _dot_nt(a, b):
    return lax.dot_general(a, b, (((1,), (1,)), ((), ())), preferred_element_type=F32)


def _dot(a, b):
    return jnp.dot(a, b, preferred_element_type=F32)


def _ada_kernel(c_ref, w_ref, b_ref, o_ref):
    c = c_ref[...]
    o_ref[...] = _dot(c * _sigmoid(c), w_ref[...]) + b_ref[...]


def _ada(cvec, w_ada, b_ada):
    depth = w_ada.shape[0]
    tn = 512
    return pl.pallas_call(
        _ada_kernel,
        grid=(depth, 3 * D_MODEL // tn),
        in_specs=[pl.BlockSpec((8, D_MODEL), lambda l, j: (0, 0)),
                  pl.BlockSpec((None, D_MODEL, tn), lambda l, j: (l, 0, j)),
                  pl.BlockSpec((None, 1, tn), lambda l, j: (l, 0, j))],
        out_specs=pl.BlockSpec((None, 8, tn), lambda l, j: (l, 0, j)),
        out_shape=jax.ShapeDtypeStruct((depth, 8, 3 * D_MODEL), F32),
        compiler_params=_cparams(("parallel", "parallel")),
        name="ada_mod",
    )(cvec, w_ada, b_ada.reshape(depth, 1, 3 * D_MODEL))


def _prenorm_kernel(x_ref, nw_ref, mod_ref, o_ref):
    x = x_ref[...]
    mod = mod_ref[...]
    shift, scale = mod[:, :D_MODEL], mod[:, D_MODEL:2 * D_MODEL]
    y = x * lax.rsqrt(jnp.mean(x * x, axis=-1, keepdims=True) + EPS) * nw_ref[...]
    o_ref[...] = (y * (1.0 + scale) + shift).astype(BF16)


def _prenorm(x2, norm_w, mod3, tm, row_of_tile):
    rows = x2.shape[0]
    return pl.pallas_call(
        _prenorm_kernel,
        grid=(rows // tm,),
        in_specs=[pl.BlockSpec((tm, D_MODEL), lambda i: (i, 0)),
                  pl.BlockSpec((1, D_MODEL), lambda i: (0, 0)),
                  pl.BlockSpec((None, 1, 3 * D_MODEL), lambda i: (row_of_tile(i), 0, 0))],
        out_specs=pl.BlockSpec((tm, D_MODEL), lambda i: (i, 0)),
        out_shape=jax.ShapeDtypeStruct((rows, D_MODEL), BF16),
        compiler_params=_cparams(("parallel",)),
        name="prenorm",
    )(x2, norm_w.reshape(1, D_MODEL), mod3)


def _rot_half_unsigned(n):
    lane = lax.broadcasted_iota(jnp.int32, n.shape, 1)
    return jnp.where((lane & 31) < 16, pltpu.roll(n, LANES - 16, 1), pltpu.roll(n, 16, 1))


def _proj_qk_kernel(*refs, rope):
    if rope:
        hx_ref, w_ref, gain_ref, ones_ref, cos_ref, sin_ref, o_ref = refs
    else:
        hx_ref, w_ref, gain_ref, ones_ref, o_ref = refs
    acc_all = _dot(hx_ref[...], w_ref[...])
    if rope:
        j = pl.program_id(1)
        use = jnp.logical_or(j < _ROPE_LO_END, j >= _ROPE_HI_START)
        cos = jnp.where(use, cos_ref[...], 1.0)
        sin = jnp.where(use, sin_ref[...], 0.0)
    for c in range(PROJ_TN // NORM_GROUP):
        sl = slice(c * NORM_GROUP, (c + 1) * NORM_GROUP)
        acc = acc_all[:, sl]
        ss = _dot((acc * acc).astype(BF16), ones_ref[...])
        n = acc * lax.rsqrt(ss * (1.0 / HEAD_DIM) + EPS) * gain_ref[:, sl]
        if rope:
            rot = jnp.concatenate([_rot_half_unsigned(n[:, :LANES]), _rot_half_unsigned(n[:, LANES:])], axis=1)
            n = n * cos + rot * sin
        o_ref[:, sl] = n.astype(BF16)


def _proj_qk(hx, w_qk, gain_row, ones_bd, tm, cos=None, sin=None, tiles_per_batch=None):
    rows = hx.shape[0]
    rope = cos is not None
    in_specs = [pl.BlockSpec((tm, D_MODEL), lambda i, j: (i, 0)),
                pl.BlockSpec((D_MODEL, PROJ_TN), lambda i, j: (0, j)),
                pl.BlockSpec((1, PROJ_TN), lambda i, j: (0, j)),
                pl.BlockSpec((NORM_GROUP, NORM_GROUP), lambda i, j: (0, 0))]
    args = [hx, w_qk, gain_row, ones_bd]
    if rope:
        in_specs += [pl.BlockSpec((tm, NORM_GROUP), lambda i, j: (i % tiles_per_batch, 0))] * 2
        args += [cos, sin]
    return pl.pallas_call(
        functools.partial(_proj_qk_kernel, rope=rope),
        grid=(rows // tm, QK_COLS // PROJ_TN),
        in_specs=in_specs,
        out_specs=pl.BlockSpec((tm, PROJ_TN), lambda i, j: (i, j)),
        out_shape=jax.ShapeDtypeStruct((rows, QK_COLS), BF16),
        compiler_params=_cparams(("parallel", "arbitrary")),
        name="proj_qk_rope" if rope else "proj_qk",
    )(*args)


def _proj_rest_kernel(hx_ref, w_ref, o_ref):
    j = pl.program_id(1)
    acc = _dot(hx_ref[...], w_ref[...])
    sg = _sigmoid(acc)
    o_ref[...] = jnp.where(j < _MG_BLOCKS, sg, jnp.where(j < _GP_BLOCKS_END, acc * sg, acc)).astype(BF16)


def _proj_rest(hx, w_rest, tm):
    rows = hx.shape[0]
    return pl.pallas_call(
        _proj_rest_kernel,
        grid=(rows // tm, REST_COLS // PROJ_TN),
        in_specs=[pl.BlockSpec((tm, D_MODEL), lambda i, j: (i, 0)),
                  pl.BlockSpec((D_MODEL, PROJ_TN), lambda i, j: (0, j))],
        out_specs=pl.BlockSpec((tm, PROJ_TN), lambda i, j: (i, j)),
        out_shape=jax.ShapeDtypeStruct((rows, REST_COLS), BF16),
        compiler_params=_cparams(("parallel", "arbitrary")),
        name="proj_rest",
    )(hx, w_rest)


def _stack_heads(q, first_head, n):
    return jnp.concatenate([q[:, (first_head + g) * HEAD_DIM:(first_head + g + 1) * HEAD_DIM] for g in range(n)],
                           axis=0)


def _softmax_rows(s, extra=None):
    m = jnp.max(s, axis=-1, keepdims=True)
    if extra is not None:
        m = jnp.maximum(m, extra)
    p = jnp.exp2(s - m)
    l = jnp.sum(p, axis=-1, keepdims=True)
    if extra is not None:
        l = l + jnp.exp2(extra - m)
    return p, l


def _with_ones_column(v):
    lane = lax.broadcasted_iota(jnp.int32, v.shape, 1)
    return jnp.concatenate([v, jnp.where(lane == 0, 1.0, 0.0).astype(v.dtype)], axis=1)


def _lane_partial_sums(p):
    part = p[:, :LANES]
    for c in range(1, p.shape[1] // LANES):
        part = part + p[:, c * LANES:(c + 1) * LANES]
    return part


def _diff_lambda(lam_ref, lam_init):
    lf = lam_ref[...]
    a = jnp.sum(lf[0:1] * lf[1:2], axis=-1, keepdims=True)
    b = jnp.sum(lf[2:3] * lf[3:4], axis=-1, keepdims=True)
    return jnp.exp(a) - jnp.exp(b) + lam_init


def _finish_diff(o, subln_ref, lam_init):
    y = o * lax.rsqrt(jnp.mean(o * o, axis=-1, keepdims=True) + EPS) * subln_ref[...]
    return y * (1.0 - lam_init)


def _win_kernel(sink_ref, q_ref, kp_ref, kc_ref, kn_ref, kx_ref, vp_ref, vc_ref, vn_ref, vx_ref, o_ref, *,
                tq, seq):
    i = pl.program_id(1)
    g = A_HEADS // A_KV
    q = q_ref[...]
    k_all = jnp.concatenate([kp_ref[...], kc_ref[...], kn_ref[...], kx_ref[...]], axis=0)
    v_all = jnp.concatenate([vp_ref[...], vc_ref[...], vn_ref[...], vx_ref[...]], axis=0)
    nw = 2 * tq
    t = lax.broadcasted_iota(jnp.int32, (g * tq, nw), 0) & (tq - 1)
    j = lax.broadcasted_iota(jnp.int32, (g * tq, nw), 1)
    rel = j - tq // 2 - t
    kpos = i * tq - tq // 2 + j
    bad = jnp.where(jnp.abs(rel) > A_WINDOW, 1, 0) + jnp.where(kpos < 0, 1, 0) + jnp.where(kpos >= seq, 1, 0)
    for kv in range(A_KV):
        qs = _stack_heads(q, kv * g, g)
        kk = k_all[:, kv * HEAD_DIM:(kv + 1) * HEAD_DIM]
        vv = v_all[:, kv * HEAD_DIM:(kv + 1) * HEAD_DIM]
        s = _dot_nt(qs, kk)
        s_win = jnp.where(bad > 0, NEG_INF, s[:, :nw])
        s_ctx = s[:, nw:]
        sink = jnp.concatenate([jnp.full((tq, 1), sink_ref[kv * g + h], F32) for h in range(g)], axis=0)
        m = jnp.maximum(jnp.maximum(jnp.max(s_win, axis=-1, keepdims=True),
                                    jnp.max(s_ctx, axis=-1, keepdims=True)), sink)
        pw = jnp.exp2(s_win - m)
        pc = jnp.exp2(s_ctx - m)
        l = jnp.sum(pw, axis=-1, keepdims=True) + jnp.sum(pc, axis=-1, keepdims=True) + jnp.exp2(sink - m)
        o = (_dot(pw.astype(BF16), vv[:nw]) + _dot(pc.astype(BF16), vv[nw:])) / l
        for h in range(g):
            c0 = (kv * g + h) * HEAD_DIM
            o_ref[:, c0:c0 + HEAD_DIM] = o[h * tq:(h + 1) * tq].astype(BF16)


def _win_bounded_kernel(sink_ref, q_ref, kp_ref, kc_ref, kn_ref, kx_ref, vp_ref, vc_ref, vn_ref, vx_ref, mask_ref,
                        o_ref, *, tq):
    g = A_HEADS // A_KV
    q = q_ref[...]
    mask = mask_ref[...]
    sink = sink_ref[...]
    k_all = jnp.concatenate([kp_ref[...], kc_ref[...], kn_ref[...], kx_ref[...]], axis=0)
    v_all = jnp.concatenate([vp_ref[...], vc_ref[...], vn_ref[...], vx_ref[...]], axis=0)
    for kv in range(A_KV):
        sl = slice(kv * HEAD_DIM, (kv + 1) * HEAD_DIM)
        qs = _stack_heads(q, kv * g, g)
        stab = [jnp.maximum(sink[:, kv * g + h:kv * g + h + 1], 0.0) for h in range(g)]
        shift = jnp.concatenate([mask - stab[h] for h in range(g)], axis=0)
        p = jnp.exp2(_dot_nt(qs, k_all[:, sl]) + shift).astype(BF16)
        acc = _dot(p, _with_ones_column(v_all[:, sl]))
        for h in range(g):
            a = acc[h * tq:(h + 1) * tq]
            l = a[:, HEAD_DIM:HEAD_DIM + 1] + jnp.exp2(sink[:, kv * g + h:kv * g + h + 1] - stab[h])
            c0 = (kv * g + h) * HEAD_DIM
            o_ref[:, c0:c0 + HEAD_DIM] = (a[:, :HEAD_DIM] * (1.0 / l)).astype(BF16)


def _window_mask_tables(seq, tq, ctx_len):
    nt = seq // tq
    tabs = []
    for i0 in (0, 1, nt - 1):
        t = np.arange(tq)[:, None]
        j = np.arange(2 * tq)[None, :]
        kpos = i0 * tq - tq // 2 + j
        ok = (np.abs(j - tq // 2 - t) <= A_WINDOW) & (kpos >= 0) & (kpos < seq)
        win = np.where(ok, 0.0, NEG_INF).astype(np.float32)
        tabs.append(np.concatenate([win, np.zeros((tq, ctx_len), np.float32)], axis=1))
    return jnp.asarray(np.stack(tabs, axis=0))


def _window_attention(qk_lat, rest_lat, qk_ctx, rest_ctx, sink2, bounded):
    bsz, seq, _ = qk_lat.shape
    ctx_len = qk_ctx.shape[1]
    tq = 2 * A_WINDOW
    nt = seq // tq
    half = tq // 2
    kcol = QK_OFF["ak"] // LANES
    vcol = REST_OFF["av"] // LANES
    n_half = seq // half

    def kv_specs(col):
        return [pl.BlockSpec((None, half, LANES), lambda b, i: (b, jnp.maximum(2 * i - 1, 0), col)),
                pl.BlockSpec((None, tq, LANES), lambda b, i: (b, i, col)),
                pl.BlockSpec((None, half, LANES), lambda b, i: (b, jnp.minimum(2 * i + 2, n_half - 1), col)),
                pl.BlockSpec((None, ctx_len, LANES), lambda b, i: (b, 0, col))]

    def tile_variant(i):
        return jnp.where(i == 0, 0, jnp.where(i == nt - 1, 2, 1))

    q_spec = pl.BlockSpec((None, tq, A_HEADS * HEAD_DIM), lambda b, i: (b, i, QK_OFF["aq"] // 512))
    kv_args = (qk_lat, qk_lat, qk_lat, qk_ctx, rest_lat, rest_lat, rest_lat, rest_ctx)
    if bounded:
        body = functools.partial(_win_bounded_kernel, tq=tq)
        in_specs = ([pl.BlockSpec((1, A_HEADS), lambda b, i: (0, 0)), q_spec] + kv_specs(kcol) + kv_specs(vcol)
                    + [pl.BlockSpec((None, tq, 2 * tq + ctx_len), lambda b, i: (tile_variant(i), 0, 0))])
        args = (sink2.reshape(1, A_HEADS), qk_lat) + kv_args + (_window_mask_tables(seq, tq, ctx_len),)
    else:
        body = functools.partial(_win_kernel, tq=tq, seq=seq)
        in_specs = [pl.BlockSpec(memory_space=pltpu.SMEM), q_spec] + kv_specs(kcol) + kv_specs(vcol)
        args = (sink2, qk_lat) + kv_args
    return pl.pallas_call(
        body,
        grid=(bsz, nt),
        in_specs=in_specs,
        out_specs=pl.BlockSpec((None, tq, BRANCH_W), lambda b, i: (b, i, 0)),
        out_shape=jax.ShapeDtypeStruct((bsz, seq, BRANCH_W), BF16),
        compiler_params=_cparams(("parallel", "parallel")),
        name="mixer_a_window_bounded" if bounded else "mixer_a_window",
    )(*args)


def _nbr_kernel(q_ref, kp_ref, kc_ref, kn_ref, kx_ref, vp_ref, vc_ref, vn_ref, vx_ref, bias_ref, o_ref, *, tq,
                bounded):
    q = q_ref[...]
    k_all = jnp.concatenate([kp_ref[...], kc_ref[...], kn_ref[...], kx_ref[...]], axis=0)
    v_all = jnp.concatenate([vp_ref[...], vc_ref[...], vn_ref[...], vx_ref[...]], axis=0)
    nw = 3 * tq
    for h in range(B_HEADS):
        sl = slice(h * HEAD_DIM, (h + 1) * HEAD_DIM)
        s = _dot_nt(q[:, sl], k_all[:, sl])
        s = jnp.concatenate([s[:, :nw] + bias_ref[h], s[:, nw:]], axis=1)
        if not bounded:
            s = s - jnp.max(s, axis=-1, keepdims=True)
        acc = _dot(jnp.exp2(s).astype(BF16), _with_ones_column(v_all[:, sl]))
        o_ref[:, sl] = (acc[:, :HEAD_DIM] * (1.0 / acc[:, HEAD_DIM:HEAD_DIM + 1])).astype(BF16)


def _nbr_bias_tables(rpb, seq, tq):
    rows = seq // GRID_W
    nt = seq // tq
    kr = min(NB_ROWS, rows)
    qr = tq // GRID_W
    n_heads = rpb.shape[0]
    assert NB_ROWS - 1 - qr - (qr - 1) >= 0 and NB_ROWS - 1 - qr + 3 * qr <= 2 * NB_ROWS - 1
    pad = GRID_W - 1
    rpb_pad = jnp.pad(rpb.astype(F32), ((0, 0), (0, 0), (pad, pad)))
    toep = jnp.stack([rpb_pad[:, :, pad + NB_COLS - 1 - c:pad + NB_COLS - 1 - c + GRID_W] for c in range(GRID_W)],
                     axis=2)
    c = np.arange(GRID_W)
    cstart = np.clip(c - NB_COLS // 2, 0, GRID_W - NB_COLS)
    col_ok = (c[None, :] >= cstart[:, None]) & (c[None, :] < cstart[:, None] + NB_COLS)
    toep = jnp.where(col_ok[None, None], toep, NEG_INF)
    per_rl = [jnp.transpose(toep[:, NB_ROWS - 1 - qr - rl:NB_ROWS - 1 - qr - rl + 3 * qr], (0, 2, 1, 3))
              for rl in range(qr)]
    base = jnp.stack(per_rl, axis=1)
    tabs = []
    for i0 in (0, 1, nt - 1):
        r = i0 * qr + np.arange(qr)
        r2 = (i0 - 1) * qr + np.arange(3 * qr)
        rstart = np.clip(r - kr // 2, 0, rows - kr)
        row_ok = ((r2[None, :] >= rstart[:, None]) & (r2[None, :] < rstart[:, None] + kr)
                  & (r2[None, :] >= 0) & (r2[None, :] < rows))
        tab = jnp.where(row_ok[None, :, None, :, None], base, NEG_INF)
        tabs.append(tab.reshape(n_heads, tq, 3 * tq))
    return jnp.stack(tabs, axis=0)


def _neighborhood_attention(qk_lat, rest_lat, qk_ctx, rest_ctx, bias_tabs, bounded):
    bsz, seq, _ = qk_lat.shape
    ctx_len = qk_ctx.shape[1]
    tq = bias_tabs.shape[2]
    nt = seq // tq
    width = B_HEADS * HEAD_DIM

    def kv_specs(col):
        return [pl.BlockSpec((None, tq, width), lambda b, i: (b, jnp.maximum(i - 1, 0), col)),
                pl.BlockSpec((None, tq, width), lambda b, i: (b, i, col)),
                pl.BlockSpec((None, tq, width), lambda b, i: (b, jnp.minimum(i + 1, nt - 1), col)),
                pl.BlockSpec((None, ctx_len, width), lambda b, i: (b, 0, col))]

    def bias_map(b, i):
        return (jnp.where(i == 0, 0, jnp.where(i == nt - 1, 2, 1)), 0, 0, 0)

    return pl.pallas_call(
        functools.partial(_nbr_kernel, tq=tq, bounded=bounded),
        grid=(bsz, nt),
        in_specs=[pl.BlockSpec((None, tq, width), lambda b, i: (b, i, QK_OFF["bq"] // width))]
                 + kv_specs(QK_OFF["bk"] // width) + kv_specs(REST_OFF["bv"] // width)
                 + [pl.BlockSpec((None, B_HEADS, tq, 3 * tq), bias_map)],
        out_specs=pl.BlockSpec((None, tq, BRANCH_W), lambda b, i: (b, i, 0)),
        out_shape=jax.ShapeDtypeStruct((bsz, seq, BRANCH_W), BF16),
        compiler_params=_cparams(("parallel", "parallel")),
        name="mixer_b_neighbourhood_bounded" if bounded else "mixer_b_neighbourhood",
    )(qk_lat, qk_lat, qk_lat, qk_lat, qk_ctx, rest_lat, rest_lat, rest_lat, rest_ctx, bias_tabs)


def _flash_update(s, v, m_ref, l_ref, acc_ref, idx, bounded):
    if bounded:
        p = jnp.exp2(s)
        if l_ref is not None:
            l_ref[idx] += _lane_partial_sums(p)
        acc_ref[idx] += _dot(p.astype(BF16), v)
    else:
        m_prev = m_ref[idx]
        m_new = jnp.maximum(m_prev, jnp.max(s, axis=-1, keepdims=True))
        alpha = jnp.exp2(m_prev - m_new)
        p = jnp.exp2(s - m_new)
        if l_ref is not None:
            l_ref[idx] = alpha * l_ref[idx] + _lane_partial_sums(p)
        acc_ref[idx] = alpha * acc_ref[idx] + _dot(p.astype(BF16), v)
        m_ref[idx] = m_new


def _dense_c_kernel(q_ref, kl_ref, vl_ref, kx_ref, vx_ref, o_ref, qs_ref, acc_ref, *maybe_m_ref,
                    tq, nk_lat, bounded):
    kt = pl.program_id(2)
    g = C_HEADS // C_KV
    m_ref = maybe_m_ref[0] if maybe_m_ref else None

    @pl.when(kt == 0)
    def _():
        q = q_ref[...]
        for kv in range(C_KV):
            qs_ref[kv] = _stack_heads(q, kv * g, g)
        acc_ref[...] = jnp.zeros(acc_ref.shape, F32)
        if m_ref is not None:
            m_ref[...] = jnp.full(m_ref.shape, -jnp.inf, F32)

    def step(k, v):
        for kv in range(C_KV):
            sl = slice(kv * HEAD_DIM, (kv + 1) * HEAD_DIM)
            _flash_update(_dot_nt(qs_ref[kv], k[:, sl]), _with_ones_column(v[:, sl]), m_ref, None, acc_ref, kv,
                          bounded)

    @pl.when(kt < nk_lat)
    def _():
        step(kl_ref[...], vl_ref[...])

    @pl.when(kt == nk_lat)
    def _():
        step(kx_ref[...], vx_ref[...])
        for kv in range(C_KV):
            acc = acc_ref[kv]
            o = acc[:, :HEAD_DIM] * (1.0 / acc[:, HEAD_DIM:HEAD_DIM + 1])
            for h in range(g):
                c0 = (kv * g + h) * HEAD_DIM
                o_ref[:, c0:c0 + HEAD_DIM] = o[h * tq:(h + 1) * tq].astype(BF16)


def _dense_gqa(qk_lat, rest_lat, qk_ctx, rest_ctx, tq, tk, bounded):
    bsz, seq, _ = qk_lat.shape
    ctx_len = qk_ctx.shape[1]
    nk_lat = seq // tk
    g = C_HEADS // C_KV
    kcol = QK_OFF["ck"] // LANES
    vcol = REST_OFF["cv"] // LANES
    scratch = [pltpu.VMEM((C_KV, g * tq, HEAD_DIM), BF16),
               pltpu.VMEM((C_KV, g * tq, LANES), F32)]
    if not bounded:
        scratch.append(pltpu.VMEM((C_KV, g * tq, 1), F32))
    return pl.pallas_call(
        functools.partial(_dense_c_kernel, tq=tq, nk_lat=nk_lat, bounded=bounded),
        grid=(bsz, seq // tq, nk_lat + 1),
        in_specs=[pl.BlockSpec((None, tq, C_HEADS * HEAD_DIM), lambda b, i, kt: (b, i, QK_OFF["cq"] // 512)),
                  pl.BlockSpec((None, tk, LANES), lambda b, i, kt: (b, jnp.minimum(kt, nk_lat - 1), kcol)),
                  pl.BlockSpec((None, tk, LANES), lambda b, i, kt: (b, jnp.minimum(kt, nk_lat - 1), vcol)),
                  pl.BlockSpec((None, ctx_len, LANES), lambda b, i, kt: (b, 0, kcol)),
                  pl.BlockSpec((None, ctx_len, LANES), lambda b, i, kt: (b, 0, vcol))],
        out_specs=pl.BlockSpec((None, tq, BRANCH_W), lambda b, i, kt: (b, i, 0)),
        out_shape=jax.ShapeDtypeStruct((bsz, seq, BRANCH_W), BF16),
        scratch_shapes=scratch,
        compiler_params=_cparams(("parallel", "parallel", "arbitrary")),
        name="mixer_c_dense_bounded" if bounded else "mixer_c_dense",
    )(qk_lat, qk_lat, rest_lat, qk_ctx, rest_ctx)


def _dense_d_kernel(q_ref, kl_ref, vl_ref, kx_ref, vx_ref, lam_ref, subln_ref, o_ref, qs_ref, l_ref, acc_ref,
                    *maybe_m_ref, nk_lat, lam_init, bounded):
    kt = pl.program_id(2)
    n_sc = 2 * D_HEADS
    m_ref = maybe_m_ref[0] if maybe_m_ref else None

    @pl.when(kt == 0)
    def _():
        q = q_ref[...]
        for idx in range(n_sc):
            qs_ref[idx] = q[:, idx * HEAD_DIM:(idx + 1) * HEAD_DIM]
        l_ref[...] = jnp.zeros(l_ref.shape, F32)
        acc_ref[...] = jnp.zeros(acc_ref.shape, F32)
        if m_ref is not None:
            m_ref[...] = jnp.full(m_ref.shape, -jnp.inf, F32)

    def step(k, v):
        for idx in range(n_sc):
            h = idx // 2
            s = _dot_nt(qs_ref[idx], k[:, idx * HEAD_DIM:(idx + 1) * HEAD_DIM])
            _flash_update(s, v[:, h * 2 * HEAD_DIM:(h + 1) * 2 * HEAD_DIM], m_ref, l_ref, acc_ref, idx, bounded)

    @pl.when(kt < nk_lat)
    def _():
        step(kl_ref[...], vl_ref[...])

    @pl.when(kt == nk_lat)
    def _():
        step(kx_ref[...], vx_ref[...])
        lam = _diff_lambda(lam_ref, lam_init)
        for h in range(D_HEADS):
            l1 = jnp.sum(l_ref[2 * h], axis=-1, keepdims=True)
            l2 = jnp.sum(l_ref[2 * h + 1], axis=-1, keepdims=True)
            o = acc_ref[2 * h] / l1 - lam * (acc_ref[2 * h + 1] / l2)
            o_ref[:, h * 2 * HEAD_DIM:(h + 1) * 2 * HEAD_DIM] = _finish_diff(o, subln_ref, lam_init).astype(BF16)


def _dense_diff(qk_lat, rest_lat, qk_ctx, rest_ctx, lam_d, subln, lam_init, tq, tk, bounded):
    bsz, seq, _ = qk_lat.shape
    ctx_len = qk_ctx.shape[1]
    nk_lat = seq // tk
    width = D_HEADS * 2 * HEAD_DIM
    kcol = QK_OFF["dk"] // width
    vcol = REST_OFF["dv"] // width
    scratch = [pltpu.VMEM((2 * D_HEADS, tq, HEAD_DIM), BF16),
               pltpu.VMEM((2 * D_HEADS, tq, LANES), F32),
               pltpu.VMEM((2 * D_HEADS, tq, 2 * HEAD_DIM), F32)]
    if not bounded:
        scratch.append(pltpu.VMEM((2 * D_HEADS, tq, 1), F32))
    return pl.pallas_call(
        functools.partial(_dense_d_kernel, nk_lat=nk_lat, lam_init=lam_init, bounded=bounded),
        grid=(bsz, seq // tq, nk_lat + 1),
        in_specs=[pl.BlockSpec((None, tq, width), lambda b, i, kt: (b, i, QK_OFF["dq"] // width)),
                  pl.BlockSpec((None, tk, width), lambda b, i, kt: (b, jnp.minimum(kt, nk_lat - 1), kcol)),
                  pl.BlockSpec((None, tk, width), lambda b, i, kt: (b, jnp.minimum(kt, nk_lat - 1), vcol)),
                  pl.BlockSpec((None, ctx_len, width), lambda b, i, kt: (b, 0, kcol)),
                  pl.BlockSpec((None, ctx_len, width), lambda b, i, kt: (b, 0, vcol)),
                  pl.BlockSpec((4, HEAD_DIM), lambda b, i, kt: (0, 0)),
                  pl.BlockSpec((1, 2 * HEAD_DIM), lambda b, i, kt: (0, 0))],
        out_specs=pl.BlockSpec((None, tq, BRANCH_W), lambda b, i, kt: (b, i, 0)),
        out_shape=jax.ShapeDtypeStruct((bsz, seq, BRANCH_W), BF16),
        scratch_shapes=scratch,
        compiler_params=_cparams(("parallel", "parallel", "arbitrary")),
        name="mixer_d_diff_bounded" if bounded else "mixer_d_diff",
    )(qk_lat, qk_lat, rest_lat, qk_ctx, rest_ctx, lam_d, subln.reshape(1, 2 * HEAD_DIM))


def _ctx_kernel(sink_ref, qk_ref, v_ref, lam_ref, subln_ref, o_ref, *, lam_init):
    qk = qk_ref[...]
    vals = v_ref[...]
    ctx_len = qk.shape[0]
    v_off = {name: REST_OFF[name] - REST_OFF[_FIRST_VALUE] for name in ("av", "cv", "bv", "dv")}

    def cols(name, start, width):
        c0 = QK_OFF[name] + start
        return qk[:, c0:c0 + width]

    def gqa(qname, kname, vname, out_off, n_kv, with_sink):
        g = 8 // n_kv
        for kv in range(n_kv):
            qs = _stack_heads(cols(qname, 0, 8 * HEAD_DIM), kv * g, g)
            s = _dot_nt(qs, cols(kname, kv * HEAD_DIM, HEAD_DIM))
            extra = None
            if with_sink:
                extra = jnp.concatenate([jnp.full((ctx_len, 1), sink_ref[kv * g + h], F32) for h in range(g)], axis=0)
            p, l = _softmax_rows(s, extra)
            vv = vals[:, v_off[vname] + kv * HEAD_DIM:v_off[vname] + (kv + 1) * HEAD_DIM]
            o = _dot(p.astype(BF16), vv) / l
            for h in range(g):
                c0 = out_off + (kv * g + h) * HEAD_DIM
                o_ref[:, c0:c0 + HEAD_DIM] = o[h * ctx_len:(h + 1) * ctx_len].astype(BF16)

    gqa("aq", "ak", "av", 0 * BRANCH_W, A_KV, True)
    gqa("bq", "bk", "bv", 1 * BRANCH_W, B_HEADS, False)
    gqa("cq", "ck", "cv", 2 * BRANCH_W, C_KV, False)

    lam = _diff_lambda(lam_ref, lam_init)
    for h in range(D_HEADS):
        base = h * 2 * HEAD_DIM
        p1, l1 = _softmax_rows(_dot_nt(cols("dq", base, HEAD_DIM), cols("dk", base, HEAD_DIM)))
        p2, l2 = _softmax_rows(_dot_nt(cols("dq", base + HEAD_DIM, HEAD_DIM), cols("dk", base + HEAD_DIM, HEAD_DIM)))
        pd = p1 / l1 - lam * (p2 / l2)
        o = _dot(pd.astype(BF16), vals[:, v_off["dv"] + base:v_off["dv"] + base + 2 * HEAD_DIM])
        c0 = 3 * BRANCH_W + base
        o_ref[:, c0:c0 + 2 * HEAD_DIM] = _finish_diff(o, subln_ref, lam_init).astype(BF16)


def _ctx_attention(qk_ctx, v_ctx, sink2, lam_d, subln, lam_init):
    bsz, ctx_len, _ = qk_ctx.shape
    return pl.pallas_call(
        functools.partial(_ctx_kernel, lam_init=lam_init),
        grid=(bsz,),
        in_specs=[pl.BlockSpec(memory_space=pltpu.SMEM),
                  pl.BlockSpec((None, ctx_len, QK_COLS), lambda b: (b, 0, 0)),
                  pl.BlockSpec((None, ctx_len, v_ctx.shape[2]), lambda b: (b, 0, 0)),
                  pl.BlockSpec((4, HEAD_DIM), lambda b: (0, 0)),
                  pl.BlockSpec((1, 2 * HEAD_DIM), lambda b: (0, 0))],
        out_specs=pl.BlockSpec((None, ctx_len, N_BRANCH * BRANCH_W), lambda b: (b, 0, 0)),
        out_shape=jax.ShapeDtypeStruct((bsz, ctx_len, N_BRANCH * BRANCH_W), BF16),
        compiler_params=_cparams(("parallel",)),
        name="ctx_attention",
    )(sink2, qk_ctx, v_ctx, lam_d, subln.reshape(1, 2 * HEAD_DIM))


def _merge_kernel(x_ref, mod_ref, ya_ref, yb_ref, yc_ref, yd_ref, gp_ref, mg_ref, wbr_ref, wout_ref, o_ref):
    gate = mod_ref[...][:, 2 * D_MODEL:]
    merged = None
    for n, y_ref in enumerate((ya_ref, yb_ref, yc_ref, yd_ref)):
        yg = (y_ref[...].astype(F32) * gp_ref[:, n * BRANCH_W:(n + 1) * BRANCH_W].astype(F32)).astype(BF16)
        term = mg_ref[:, n * D_MODEL:(n + 1) * D_MODEL].astype(F32) * _dot(yg, wbr_ref[n])
        merged = term if merged is None else merged + term
    o_ref[...] = x_ref[...] + gate * _dot(merged.astype(BF16), wout_ref[...])


def _merge(x2, mod3, ys, y_cols, rest, w_br, w_out, tm, row_of_tile):
    rows = x2.shape[0]
    y_specs = [pl.BlockSpec((tm, BRANCH_W), (lambda i, c=c: (i, c))) for c in y_cols]
    return pl.pallas_call(
        _merge_kernel,
        grid=(rows // tm,),
        in_specs=[pl.BlockSpec((tm, D_MODEL), lambda i: (i, 0)),
                  pl.BlockSpec((None, 1, 3 * D_MODEL), lambda i: (row_of_tile(i), 0, 0))]
                 + y_specs
                 + [pl.BlockSpec((tm, N_BRANCH * BRANCH_W), lambda i: (i, REST_OFF["ag"] // (N_BRANCH * BRANCH_W))),
                    pl.BlockSpec((tm, N_BRANCH * D_MODEL), lambda i: (i, 0)),
                    pl.BlockSpec((N_BRANCH, BRANCH_W, D_MODEL), lambda i: (0, 0, 0)),
                    pl.BlockSpec((D_MODEL, D_MODEL), lambda i: (0, 0))],
        out_specs=pl.BlockSpec((tm, D_MODEL), lambda i: (i, 0)),
        out_shape=jax.ShapeDtypeStruct((rows, D_MODEL), F32),
        compiler_params=_cparams(("parallel",)),
        name="gated_merge",
    )(x2, mod3, *ys, rest, rest, w_br, w_out)


def _rope_tables(seq):
    t = jnp.arange(seq, dtype=jnp.int32)
    pos = jnp.stack([t // GRID_W, t % GRID_W], axis=-1).astype(F32)
    n_freq = HEAD_DIM // 4
    freqs = ROPE_THETA ** (-jnp.arange(n_freq, dtype=F32) / n_freq)
    ang = pos[:, :, None] * freqs[None, None, :]
    ang = jnp.concatenate([ang, ang], axis=-1).reshape(seq, HEAD_DIM)
    sign = np.where((np.arange(HEAD_DIM) % 32) < 16, -1.0, 1.0).astype(np.float32)
    reps = NORM_GROUP // HEAD_DIM
    return jnp.tile(jnp.cos(ang), (1, reps)), jnp.tile(jnp.sin(ang) * sign, (1, reps))


def _regroup_cols(w, order, total):
    parts = [w[:, _ORIG[n][0]:_ORIG[n][1]] for n in order]
    used = sum(p.shape[1] for p in parts)
    if total > used:
        parts.append(jnp.zeros((w.shape[0], total - used), w.dtype))
    return jnp.concatenate(parts, axis=1)


def _qk_gain_row(g):
    parts = []
    for name in _QK_ORDER:
        mixer = "abcd".index(name[0])
        is_q = name[1] == "q"
        width = _ORIG[name][1] - _ORIG[name][0]
        gain = g[mixer, 0] * (QK_SCALE * LOG2E) if is_q else g[mixer, 1]
        parts.append(jnp.tile(gain, width // HEAD_DIM))
    parts.append(jnp.ones((QK_COLS - _QK_USED,), g.dtype))
    return jnp.concatenate(parts).reshape(1, QK_COLS).astype(F32)


def kernel(x, c, ctx, c_ctx, norm_w, w_ada, b_ada, w_in, qk_gain, sink_a, rpb_b, lam_d, subln_d, w_br, w_out):
    bsz, seq, _ = x.shape
    ctx_len = ctx.shape[1]
    depth = w_ada.shape[0]
    assert seq % (2 * A_WINDOW) == 0 and ctx_len % LANES == 0 and bsz <= 6

    tm_lat = 2048
    tiles_per_batch = seq // tm_lat
    tk_dense = 1024
    tq_bounded, tq_rowmax = 512, 256
    assert seq % tk_dense == 0
    tm_merge = 512
    merge_tiles_per_batch = seq // tm_merge

    cvec = jnp.concatenate([c, c_ctx[None, :], jnp.zeros((8 - bsz - 1, D_MODEL), F32)], axis=0)
    mod_all = _ada(cvec, w_ada, b_ada)
    cos, sin = _rope_tables(seq)
    ones_bd = jnp.asarray(np.kron(np.eye(NORM_GROUP // HEAD_DIM), np.ones((HEAD_DIM, HEAD_DIM))), BF16)

    x2 = x.reshape(bsz * seq, D_MODEL)
    c2 = ctx.reshape(bsz * ctx_len, D_MODEL)
    for l in range(depth):
        need_ctx = l < depth - 1
        lam_init = 0.8 - 0.6 * math.exp(-0.3 * l)
        mod3 = mod_all[l].reshape(8, 1, 3 * D_MODEL)
        w_qk = _regroup_cols(w_in[l], _QK_ORDER, QK_COLS).astype(BF16)
        w_rest = _regroup_cols(w_in[l], _REST_ORDER, REST_COLS).astype(BF16)
        gain_row = _qk_gain_row(qk_gain[l])

        hx = _prenorm(x2, norm_w[l], mod3, tm_lat, lambda i: i // tiles_per_batch)
        hc = _prenorm(c2, norm_w[l], mod3, bsz * ctx_len, lambda i: bsz)
        qk_lat = _proj_qk(hx, w_qk, gain_row, ones_bd, tm_lat, cos, sin, tiles_per_batch).reshape(bsz, seq, QK_COLS)
        qk_ctx = _proj_qk(hc, w_qk, gain_row, ones_bd, bsz * ctx_len).reshape(bsz, ctx_len, QK_COLS)
        rest_lat = _proj_rest(hx, w_rest, tm_lat).reshape(bsz, seq, REST_COLS)
        rest_ctx = _proj_rest(hc, w_rest, bsz * ctx_len).reshape(bsz, ctx_len, REST_COLS)

        sink2 = sink_a[l] * LOG2E
        bias_tabs = _nbr_bias_tables(rpb_b[l] * LOG2E, seq, 2 * A_WINDOW)
        proj = (qk_lat, rest_lat, qk_ctx, rest_ctx)

        def qk_bound(g):
            return HEAD_DIM * QK_SCALE * LOG2E * jnp.max(jnp.abs(g[0])) * jnp.max(jnp.abs(g[1])) * 1.02

        y_a = lax.cond(qk_bound(qk_gain[l, 0]) <= LOGIT_BOUND,
                       lambda: _window_attention(*proj, sink2, True),
                       lambda: _window_attention(*proj, sink2, False))
        y_b = lax.cond(qk_bound(qk_gain[l, 1]) + jnp.max(jnp.abs(rpb_b[l])) * LOG2E <= LOGIT_BOUND,
                       lambda: _neighborhood_attention(*proj, bias_tabs, True),
                       lambda: _neighborhood_attention(*proj, bias_tabs, False))
        y_c = lax.cond(qk_bound(qk_gain[l, 2]) <= LOGIT_BOUND,
                       lambda: _dense_gqa(*proj, tq_bounded, tk_dense, True),
                       lambda: _dense_gqa(*proj, tq_rowmax, tk_dense, False))
        y_d = lax.cond(qk_bound(qk_gain[l, 3]) <= LOGIT_BOUND,
                       lambda: _dense_diff(*proj, lam_d[l], subln_d[l], lam_init, tq_bounded, tk_dense, True),
                       lambda: _dense_diff(*proj, lam_d[l], subln_d[l], lam_init, tq_rowmax, tk_dense, False))

        w_br_l = w_br[l].astype(BF16)
        w_out_l = w_out[l].astype(BF16)
        ys = [y.reshape(bsz * seq, BRANCH_W) for y in (y_a, y_b, y_c, y_d)]
        x_new = _merge(x2, mod3, ys, (0, 0, 0, 0), rest_lat.reshape(bsz * seq, REST_COLS), w_br_l, w_out_l,
                       tm_merge, lambda i: i // merge_tiles_per_batch)
        if need_ctx:
            y_ctx = _ctx_attention(qk_ctx, rest_ctx[:, :, REST_OFF[_FIRST_VALUE]:], sink2, lam_d[l], subln_d[l],
                                   lam_init)
            y_ctx2 = y_ctx.reshape(bsz * ctx_len, N_BRANCH * BRANCH_W)
            c2 = _merge(c2, mod3, [y_ctx2] * 4, (0, 1, 2, 3), rest_ctx.reshape(bsz * ctx_len, REST_COLS),
                        w_br_l, w_out_l, bsz * ctx_len, lambda i: bsz)
        x2 = x_new
    return x2.reshape(bsz, seq, D_MODEL)
```

```python
import functools
import math

import numpy as np
import jax
import jax.numpy as jnp
from jax import lax
from jax.experimental import pallas as pl
from jax.experimental.pallas import tpu as pltpu

F32 = jnp.float32
BF16 = jnp.bfloat16

D_MODEL = 1024
GRID_W = 64
HEAD_DIM = 64
BRANCH_W = 512
N_BRANCH = 4
A_HEADS, A_KV, A_WINDOW = 8, 2, 128
B_HEADS, NB_ROWS, NB_COLS = 8, 8, 16
C_HEADS, C_KV = 8, 2
D_HEADS = 4
ROPE_THETA = 10000.0
EPS = 1e-6
NEG_INF = -1e30
QK_SCALE = HEAD_DIM ** -0.5
LOG2E = math.log2(math.e)
LOGIT_BOUND = 60.0

V7X_VMEM_LIMIT_BYTES = 56 * 1024 * 1024
LANES = 128

_ORIG = dict(aq=(0, 512), ak=(512, 640), av=(640, 768), ag=(768, 1280),
             bq=(1280, 1792), bk=(1792, 2304), bv=(2304, 2816), bg=(2816, 3328),
             cq=(3328, 3840), ck=(3840, 3968), cv=(3968, 4096), cg=(4096, 4608),
             dq=(4608, 5120), dk=(5120, 5632), dv=(5632, 6144), dg=(6144, 6656),
             mg=(6656, 10752))
_QK_ORDER = ("aq", "cq", "dq", "bq", "bk", "dk", "ak", "ck")
_REST_ORDER = ("mg", "ag", "bg", "cg", "dg", "dv", "bv", "av", "cv")
_FIRST_VALUE = "dv"


def _offsets(order):
    off, out = 0, {}
    for name in order:
        lo, hi = _ORIG[name]
        out[name] = off
        off += hi - lo
    return out, off


PROJ_TN = 512
PROJ_ROW_CHUNK = 512
VT_PAD = 16
NORM_GROUP = 256
QK_OFF, _QK_USED = _offsets(_QK_ORDER)
REST_OFF, _REST_USED = _offsets(_REST_ORDER)
QK_COLS = -(-_QK_USED // PROJ_TN) * PROJ_TN
REST_COLS = -(-_REST_USED // PROJ_TN) * PROJ_TN
_ROPE_LO_END = QK_OFF["bq"] // PROJ_TN
_ROPE_HI_START = QK_OFF["dk"] // PROJ_TN
_MG_BLOCKS = (REST_OFF["ag"]) // PROJ_TN
_GP_BLOCKS_END = REST_OFF[_FIRST_VALUE] // PROJ_TN
assert QK_OFF["bq"] % PROJ_TN == 0 and QK_OFF["dk"] % PROJ_TN == 0
assert REST_OFF["ag"] % PROJ_TN == 0 and REST_OFF[_FIRST_VALUE] % PROJ_TN == 0


def _cparams(sem):
    return pltpu.CompilerParams(dimension_semantics=sem, vmem_limit_bytes=V7X_VMEM_LIMIT_BYTES)


def _sigmoid(x):
    return 1.0 / (1.0 + jnp.exp(-x))


def _dot_nt(a, b):
    return lax.dot_general(a, b, (((1,), (1,)), ((), ())), preferred_element_type=F32)


def _dot(a, b):
    return jnp.dot(a, b, preferred_element_type=F32)


def _ada_kernel(c_ref, w_ref, b_ref, o_ref):
    c = c_ref[...]
    o_ref[...] = _dot(c * _sigmoid(c), w_ref[...]) + b_ref[...]


def _ada(cvec, w_ada, b_ada):
    depth = w_ada.shape[0]
    tn = 512
    return pl.pallas_call(
        _ada_kernel,
        grid=(depth, 3 * D_MODEL // tn),
        in_specs=[pl.BlockSpec((8, D_MODEL), lambda l, j: (0, 0)),
                  pl.BlockSpec((None, D_MODEL, tn), lambda l, j: (l, 0, j)),
                  pl.BlockSpec((None, 1, tn), lambda l, j: (l, 0, j))],
        out_specs=pl.BlockSpec((None, 8, tn), lambda l, j: (l, 0, j)),
        out_shape=jax.ShapeDtypeStruct((depth, 8, 3 * D_MODEL), F32),
        compiler_params=_cparams(("parallel", "parallel")),
        name="ada_mod",
    )(cvec, w_ada, b_ada.reshape(depth, 1, 3 * D_MODEL))


def _prenorm_kernel(x_ref, nw_ref, mod_ref, o_ref):
    x = x_ref[...]
    mod = mod_ref[...]
    shift, scale = mod[:, :D_MODEL], mod[:, D_MODEL:2 * D_MODEL]
    y = x * lax.rsqrt(jnp.mean(x * x, axis=-1, keepdims=True) + EPS) * nw_ref[...]
    o_ref[...] = (y * (1.0 + scale) + shift).astype(BF16)


def _prenorm(x2, norm_w, mod3, tm, row_of_tile):
    rows = x2.shape[0]
    return pl.pallas_call(
        _prenorm_kernel,
        grid=(rows // tm,),
        in_specs=[pl.BlockSpec((tm, D_MODEL), lambda i: (i, 0)),
                  pl.BlockSpec((1, D_MODEL), lambda i: (0, 0)),
                  pl.BlockSpec((None, 1, 3 * D_MODEL), lambda i: (row_of_tile(i), 0, 0))],
        out_specs=pl.BlockSpec((tm, D_MODEL), lambda i: (i, 0)),
        out_shape=jax.ShapeDtypeStruct((rows, D_MODEL), BF16),
        compiler_params=_cparams(("parallel",)),
        name="prenorm",
    )(x2, norm_w.reshape(1, D_MODEL), mod3)


def _rot_half_unsigned(n):
    lane = lax.broadcasted_iota(jnp.int32, n.shape, 1)
    return jnp.where((lane & 31) < 16, pltpu.roll(n, LANES - 16, 1), pltpu.roll(n, 16, 1))


def _proj_qk_kernel(*refs, rope):
    if rope:
        hx_ref, w_ref, gain_ref, ones_ref, cos_ref, sin_ref, o_ref = refs
    else:
        hx_ref, w_ref, gain_ref, ones_ref, o_ref = refs
    w = w_ref[...]
    ones_bd = ones_ref[...]
    rows = hx_ref.shape[0]
    chunk = min(rows, PROJ_ROW_CHUNK)
    if rope:
        j = pl.program_id(1)
        use = jnp.logical_or(j < _ROPE_LO_END, j >= _ROPE_HI_START)
    for r0 in range(0, rows, chunk):
        rs = slice(r0, r0 + chunk)
        acc_all = _dot(hx_ref[rs, :], w)
        if rope:
            cos = jnp.where(use, cos_ref[rs, :], 1.0)
            sin = jnp.where(use, sin_ref[rs, :], 0.0)
        for c in range(PROJ_TN // NORM_GROUP):
            sl = slice(c * NORM_GROUP, (c + 1) * NORM_GROUP)
            acc = acc_all[:, sl]
            ss = _dot((acc * acc).astype(BF16), ones_bd)
            n = acc * lax.rsqrt(ss * (1.0 / HEAD_DIM) + EPS) * gain_ref[:, sl]
            if rope:
                rot = jnp.concatenate([_rot_half_unsigned(n[:, :LANES]), _rot_half_unsigned(n[:, LANES:])], axis=1)
                n = n * cos + rot * sin
            o_ref[rs, sl] = n.astype(BF16)


def _proj_qk(hx, w_qk, gain_row, ones_bd, tm, cos=None, sin=None, tiles_per_batch=None):
    rows = hx.shape[0]
    rope = cos is not None
    in_specs = [pl.BlockSpec((tm, D_MODEL), lambda i, j: (i, 0)),
                pl.BlockSpec((D_MODEL, PROJ_TN), lambda i, j: (0, j)),
                pl.BlockSpec((1, PROJ_TN), lambda i, j: (0, j)),
                pl.BlockSpec((NORM_GROUP, NORM_GROUP), lambda i, j: (0, 0))]
    args = [hx, w_qk, gain_row, ones_bd]
    if rope:
        in_specs += [pl.BlockSpec((tm, NORM_GROUP), lambda i, j: (i % tiles_per_batch, 0))] * 2
        args += [cos, sin]
    return pl.pallas_call(
        functools.partial(_proj_qk_kernel, rope=rope),
        grid=(rows // tm, QK_COLS // PROJ_TN),
        in_specs=in_specs,
        out_specs=pl.BlockSpec((tm, PROJ_TN), lambda i, j: (i, j)),
        out_shape=jax.ShapeDtypeStruct((rows, QK_COLS), BF16),
        compiler_params=_cparams(("parallel", "arbitrary")),
        name="proj_qk_rope" if rope else "proj_qk",
    )(*args)


def _proj_rest_kernel(hx_ref, w_ref, o_ref):
    j = pl.program_id(1)
    w = w_ref[...]
    rows = hx_ref.shape[0]
    chunk = min(rows, PROJ_ROW_CHUNK)
    for r0 in range(0, rows, chunk):
        acc = _dot(hx_ref[r0:r0 + chunk, :], w)
        sg = _sigmoid(acc)
        o_ref[r0:r0 + chunk, :] = jnp.where(j < _MG_BLOCKS, sg,
                                            jnp.where(j < _GP_BLOCKS_END, acc * sg, acc)).astype(BF16)


def _proj_rest(hx, w_rest, tm):
    rows = hx.shape[0]
    return pl.pallas_call(
        _proj_rest_kernel,
        grid=(rows // tm, REST_COLS // PROJ_TN),
        in_specs=[pl.BlockSpec((tm, D_MODEL), lambda i, j: (i, 0)),
                  pl.BlockSpec((D_MODEL, PROJ_TN), lambda i, j: (0, j))],
        out_specs=pl.BlockSpec((tm, PROJ_TN), lambda i, j: (i, j)),
        out_shape=jax.ShapeDtypeStruct((rows, REST_COLS), BF16),
        compiler_params=_cparams(("parallel", "arbitrary")),
        name="proj_rest",
    )(hx, w_rest)


def _stack_heads(q, first_head, n):
    return jnp.concatenate([q[:, (first_head + g) * HEAD_DIM:(first_head + g + 1) * HEAD_DIM] for g in range(n)],
                           axis=0)


def _softmax_rows(s, extra=None):
    m = jnp.max(s, axis=-1, keepdims=True)
    if extra is not None:
        m = jnp.maximum(m, extra)
    p = jnp.exp2(s - m)
    l = jnp.sum(p, axis=-1, keepdims=True)
    if extra is not None:
        l = l + jnp.exp2(extra - m)
    return p, l


def _with_ones_column(v, pad=HEAD_DIM):
    lane = lax.broadcasted_iota(jnp.int32, (v.shape[0], pad), 1)
    return jnp.concatenate([v, jnp.where(lane == 0, 1.0, 0.0).astype(v.dtype)], axis=1)


def _lane_partial_sums(p):
    part = p[:, :LANES]
    for c in range(1, p.shape[1] // LANES):
        part = part + p[:, c * LANES:(c + 1) * LANES]
    return part


def _diff_lambda(lam_ref, lam_init):
    lf = lam_ref[...]
    a = jnp.sum(lf[0:1] * lf[1:2], axis=-1, keepdims=True)
    b = jnp.sum(lf[2:3] * lf[3:4], axis=-1, keepdims=True)
    return jnp.exp(a) - jnp.exp(b) + lam_init


def _finish_diff(o, subln_ref, lam_init):
    y = o * lax.rsqrt(jnp.mean(o * o, axis=-1, keepdims=True) + EPS) * subln_ref[...]
    return y * (1.0 - lam_init)


def _win_kernel(sink_ref, q_ref, kp_ref, kc_ref, kn_ref, kx_ref, vp_ref, vc_ref, vn_ref, vx_ref, o_ref, *,
                tq, seq):
    i = pl.program_id(1)
    g = A_HEADS // A_KV
    q = q_ref[...]
    k_all = jnp.concatenate([kp_ref[...], kc_ref[...], kn_ref[...], kx_ref[...]], axis=0)
    v_all = jnp.concatenate([vp_ref[...], vc_ref[...], vn_ref[...], vx_ref[...]], axis=0)
    nw = 2 * tq
    t = lax.broadcasted_iota(jnp.int32, (g * tq, nw), 0) & (tq - 1)
    j = lax.broadcasted_iota(jnp.int32, (g * tq, nw), 1)
    rel = j - tq // 2 - t
    kpos = i * tq - tq // 2 + j
    bad = jnp.where(jnp.abs(rel) > A_WINDOW, 1, 0) + jnp.where(kpos < 0, 1, 0) + jnp.where(kpos >= seq, 1, 0)
    for kv in range(A_KV):
        qs = _stack_heads(q, kv * g, g)
        kk = k_all[:, kv * HEAD_DIM:(kv + 1) * HEAD_DIM]
        vv = v_all[:, kv * HEAD_DIM:(kv + 1) * HEAD_DIM]
        s = _dot_nt(qs, kk)
        s_win = jnp.where(bad > 0, NEG_INF, s[:, :nw])
        s_ctx = s[:, nw:]
        sink = jnp.concatenate([jnp.full((tq, 1), sink_ref[kv * g + h], F32) for h in range(g)], axis=0)
        m = jnp.maximum(jnp.maximum(jnp.max(s_win, axis=-1, keepdims=True),
                                    jnp.max(s_ctx, axis=-1, keepdims=True)), sink)
        pw = jnp.exp2(s_win - m)
        pc = jnp.exp2(s_ctx - m)
        l = jnp.sum(pw, axis=-1, keepdims=True) + jnp.sum(pc, axis=-1, keepdims=True) + jnp.exp2(sink - m)
        o = (_dot(pw.astype(BF16), vv[:nw]) + _dot(pc.astype(BF16), vv[nw:])) / l
        for h in range(g):
            c0 = (kv * g + h) * HEAD_DIM
            o_ref[:, c0:c0 + HEAD_DIM] = o[h * tq:(h + 1) * tq].astype(BF16)


def _win_bounded_kernel(sink_ref, q_ref, kp_ref, kc_ref, kn_ref, kx_ref, vp_ref, vc_ref, vn_ref, vx_ref, mask_ref,
                        o_ref, *, tq):
    g = A_HEADS // A_KV
    q = q_ref[...]
    mask = mask_ref[...]
    sink = sink_ref[...]
    k_all = jnp.concatenate([kp_ref[...], kc_ref[...], kn_ref[...], kx_ref[...]], axis=0)
    v_all = jnp.concatenate([vp_ref[...], vc_ref[...], vn_ref[...], vx_ref[...]], axis=0)
    for kv in range(A_KV):
        sl = slice(kv * HEAD_DIM, (kv + 1) * HEAD_DIM)
        qs = _stack_heads(q, kv * g, g)
        stab = [jnp.maximum(sink[:, kv * g + h:kv * g + h + 1], 0.0) for h in range(g)]
        shift = jnp.concatenate([mask - stab[h] for h in range(g)], axis=0)
        p = jnp.exp2(_dot_nt(qs, k_all[:, sl]) + shift).astype(BF16)
        acc = _dot(p, _with_ones_column(v_all[:, sl]))
        for h in range(g):
            a = acc[h * tq:(h + 1) * tq]
            l = a[:, HEAD_DIM:HEAD_DIM + 1] + jnp.exp2(sink[:, kv * g + h:kv * g + h + 1] - stab[h])
            c0 = (kv * g + h) * HEAD_DIM
            o_ref[:, c0:c0 + HEAD_DIM] = (a[:, :HEAD_DIM] * (1.0 / l)).astype(BF16)


def _window_mask_tables(seq, tq, ctx_len):
    nt = seq // tq
    tabs = []
    for i0 in (0, 1, nt - 1):
        t = np.arange(tq)[:, None]
        j = np.arange(2 * tq)[None, :]
        kpos = i0 * tq - tq // 2 + j
        ok = (np.abs(j - tq // 2 - t) <= A_WINDOW) & (kpos >= 0) & (kpos < seq)
        win = np.where(ok, 0.0, NEG_INF).astype(np.float32)
        tabs.append(np.concatenate([win, np.zeros((tq, ctx_len), np.float32)], axis=1))
    return jnp.asarray(np.stack(tabs, axis=0))


def _window_attention(qk_lat, rest_lat, qk_ctx, rest_ctx, sink2, bounded):
    bsz, seq, _ = qk_lat.shape
    ctx_len = qk_ctx.shape[1]
    tq = 2 * A_WINDOW
    nt = seq // tq
    half = tq // 2
    kcol = QK_OFF["ak"] // LANES
    vcol = REST_OFF["av"] // LANES
    n_half = seq // half

    def kv_specs(col):
        return [pl.BlockSpec((None, half, LANES), lambda b, i: (b, jnp.maximum(2 * i - 1, 0), col)),
                pl.BlockSpec((None, tq, LANES), lambda b, i: (b, i, col)),
                pl.BlockSpec((None, half, LANES), lambda b, i: (b, jnp.minimum(2 * i + 2, n_half - 1), col)),
                pl.BlockSpec((None, ctx_len, LANES), lambda b, i: (b, 0, col))]

    def tile_variant(i):
        return jnp.where(i == 0, 0, jnp.where(i == nt - 1, 2, 1))

    q_spec = pl.BlockSpec((None, tq, A_HEADS * HEAD_DIM), lambda b, i: (b, i, QK_OFF["aq"] // 512))
    kv_args = (qk_lat, qk_lat, qk_lat, qk_ctx, rest_lat, rest_lat, rest_lat, rest_ctx)
    if bounded:
        body = functools.partial(_win_bounded_kernel, tq=tq)
        in_specs = ([pl.BlockSpec((1, A_HEADS), lambda b, i: (0, 0)), q_spec] + kv_specs(kcol) + kv_specs(vcol)
                    + [pl.BlockSpec((None, tq, 2 * tq + ctx_len), lambda b, i: (tile_variant(i), 0, 0))])
        args = (sink2.reshape(1, A_HEADS), qk_lat) + kv_args + (_window_mask_tables(seq, tq, ctx_len),)
    else:
        body = functools.partial(_win_kernel, tq=tq, seq=seq)
        in_specs = [pl.BlockSpec(memory_space=pltpu.SMEM), q_spec] + kv_specs(kcol) + kv_specs(vcol)
        args = (sink2, qk_lat) + kv_args
    return pl.pallas_call(
        body,
        grid=(bsz, nt),
        in_specs=in_specs,
        out_specs=pl.BlockSpec((None, tq, BRANCH_W), lambda b, i: (b, i, 0)),
        out_shape=jax.ShapeDtypeStruct((bsz, seq, BRANCH_W), BF16),
        compiler_params=_cparams(("parallel", "parallel")),
        name="mixer_a_window_bounded" if bounded else "mixer_a_window",
    )(*args)


def _nbr_kernel(q_ref, kp_ref, kc_ref, kn_ref, kx_ref, vp_ref, vc_ref, vn_ref, vx_ref, bias_ref, o_ref, *, tq,
                bounded):
    q = q_ref[...]
    k_all = jnp.concatenate([kp_ref[...], kc_ref[...], kn_ref[...], kx_ref[...]], axis=0)
    v_all = jnp.concatenate([vp_ref[...], vc_ref[...], vn_ref[...], vx_ref[...]], axis=0)
    nw = 3 * tq
    for h in range(B_HEADS):
        sl = slice(h * HEAD_DIM, (h + 1) * HEAD_DIM)
        s = _dot_nt(q[:, sl], k_all[:, sl])
        s = jnp.concatenate([s[:, :nw] + bias_ref[h], s[:, nw:]], axis=1)
        if not bounded:
            s = s - jnp.max(s, axis=-1, keepdims=True)
        acc = _dot(jnp.exp2(s).astype(BF16), _with_ones_column(v_all[:, sl]))
        o_ref[:, sl] = (acc[:, :HEAD_DIM] * (1.0 / acc[:, HEAD_DIM:HEAD_DIM + 1])).astype(BF16)


def _nbr_bias_tables(rpb, seq, tq):
    rows = seq // GRID_W
    nt = seq // tq
    kr = min(NB_ROWS, rows)
    qr = tq // GRID_W
    n_heads = rpb.shape[0]
    assert NB_ROWS - 1 - qr - (qr - 1) >= 0 and NB_ROWS - 1 - qr + 3 * qr <= 2 * NB_ROWS - 1
    pad = GRID_W - 1
    rpb_pad = jnp.pad(rpb.astype(F32), ((0, 0), (0, 0), (pad, pad)))
    toep = jnp.stack([rpb_pad[:, :, pad + NB_COLS - 1 - c:pad + NB_COLS - 1 - c + GRID_W] for c in range(GRID_W)],
                     axis=2)
    c = np.arange(GRID_W)
    cstart = np.clip(c - NB_COLS // 2, 0, GRID_W - NB_COLS)
    col_ok = (c[None, :] >= cstart[:, None]) & (c[None, :] < cstart[:, None] + NB_COLS)
    toep = jnp.where(col_ok[None, None], toep, NEG_INF)
    per_rl = [jnp.transpose(toep[:, NB_ROWS - 1 - qr - rl:NB_ROWS - 1 - qr - rl + 3 * qr], (0, 2, 1, 3))
              for rl in range(qr)]
    base = jnp.stack(per_rl, axis=1)
    tabs = []
    for i0 in (0, 1, nt - 1):
        r = i0 * qr + np.arange(qr)
        r2 = (i0 - 1) * qr + np.arange(3 * qr)
        rstart = np.clip(r - kr // 2, 0, rows - kr)
        row_ok = ((r2[None, :] >= rstart[:, None]) & (r2[None, :] < rstart[:, None] + kr)
                  & (r2[None, :] >= 0) & (r2[None, :] < rows))
        tab = jnp.where(row_ok[None, :, None, :, None], base, NEG_INF)
        tabs.append(tab.reshape(n_heads, tq, 3 * tq))
    return jnp.stack(tabs, axis=0)


def _neighborhood_attention(qk_lat, rest_lat, qk_ctx, rest_ctx, bias_tabs, bounded):
    bsz, seq, _ = qk_lat.shape
    ctx_len = qk_ctx.shape[1]
    tq = bias_tabs.shape[2]
    nt = seq // tq
    width = B_HEADS * HEAD_DIM

    def kv_specs(col):
        return [pl.BlockSpec((None, tq, width), lambda b, i: (b, jnp.maximum(i - 1, 0), col)),
                pl.BlockSpec((None, tq, width), lambda b, i: (b, i, col)),
                pl.BlockSpec((None, tq, width), lambda b, i: (b, jnp.minimum(i + 1, nt - 1), col)),
                pl.BlockSpec((None, ctx_len, width), lambda b, i: (b, 0, col))]

    def bias_map(b, i):
        return (jnp.where(i == 0, 0, jnp.where(i == nt - 1, 2, 1)), 0, 0, 0)

    return pl.pallas_call(
        functools.partial(_nbr_kernel, tq=tq, bounded=bounded),
        grid=(bsz, nt),
        in_specs=[pl.BlockSpec((None, tq, width), lambda b, i: (b, i, QK_OFF["bq"] // width))]
                 + kv_specs(QK_OFF["bk"] // width) + kv_specs(REST_OFF["bv"] // width)
                 + [pl.BlockSpec((None, B_HEADS, tq, 3 * tq), bias_map)],
        out_specs=pl.BlockSpec((None, tq, BRANCH_W), lambda b, i: (b, i, 0)),
        out_shape=jax.ShapeDtypeStruct((bsz, seq, BRANCH_W), BF16),
        compiler_params=_cparams(("parallel", "parallel")),
        name="mixer_b_neighbourhood_bounded" if bounded else "mixer_b_neighbourhood",
    )(qk_lat, qk_lat, qk_lat, qk_lat, qk_ctx, rest_lat, rest_lat, rest_lat, rest_ctx, bias_tabs)


def _flash_update(s, v, m_ref, l_ref, acc_ref, idx, bounded):
    if bounded:
        p = jnp.exp2(s)
        if l_ref is not None:
            l_ref[idx] += _lane_partial_sums(p)
        acc_ref[idx] += _dot(p.astype(BF16), v)
    else:
        m_prev = m_ref[idx]
        m_new = jnp.maximum(m_prev, jnp.max(s, axis=-1, keepdims=True))
        alpha = jnp.exp2(m_prev - m_new)
        p = jnp.exp2(s - m_new)
        if l_ref is not None:
            l_ref[idx] = alpha * l_ref[idx] + _lane_partial_sums(p)
        acc_ref[idx] = alpha * acc_ref[idx] + _dot(p.astype(BF16), v)
        m_ref[idx] = m_new


def _dense_c_kernel(q_ref, kl_ref, vl_ref, kx_ref, vx_ref, o_ref, qs_ref, acc_ref, *maybe_m_ref,
                    tq, nk_lat, bounded):
    kt = pl.program_id(2)
    g = C_HEADS // C_KV
    m_ref = maybe_m_ref[0] if maybe_m_ref else None

    @pl.when(kt == 0)
    def _():
        q = q_ref[...]
        for kv in range(C_KV):
            qs_ref[kv] = _stack_heads(q, kv * g, g)
        acc_ref[...] = jnp.zeros(acc_ref.shape, F32)
        if m_ref is not None:
            m_ref[...] = jnp.full(m_ref.shape, -jnp.inf, F32)

    def step(k, v):
        for kv in range(C_KV):
            sl = slice(kv * HEAD_DIM, (kv + 1) * HEAD_DIM)
            _flash_update(_dot_nt(qs_ref[kv], k[:, sl]), _with_ones_column(v[:, sl]), m_ref, None, acc_ref, kv,
                          bounded)

    @pl.when(kt < nk_lat)
    def _():
        step(kl_ref[...], vl_ref[...])

    @pl.when(kt == nk_lat)
    def _():
        step(kx_ref[...], vx_ref[...])
        for kv in range(C_KV):
            acc = acc_ref[kv]
            o = acc[:, :HEAD_DIM] * (1.0 / acc[:, HEAD_DIM:HEAD_DIM + 1])
            for h in range(g):
                c0 = (kv * g + h) * HEAD_DIM
                o_ref[:, c0:c0 + HEAD_DIM] = o[h * tq:(h + 1) * tq].astype(BF16)


def _dense_c_bounded_kernel(q_ref, kl_ref, vl_ref, kx_ref, vx_ref, o_ref, qs_ref, acc_ref, *, tq, nk_lat):
    kt = pl.program_id(2)
    g = C_HEADS // C_KV

    @pl.when(kt == 0)
    def _():
        q = q_ref[...]
        for kv in range(C_KV):
            qs_ref[kv] = _stack_heads(q, kv * g, g)
        acc_ref[...] = jnp.zeros(acc_ref.shape, F32)

    def step(k, v):
        for kv in range(C_KV):
            sl = slice(kv * HEAD_DIM, (kv + 1) * HEAD_DIM)
            p_t = jnp.exp2(_dot_nt(k[:, sl], qs_ref[kv])).astype(BF16)
            acc_ref[kv] += lax.dot_general(_with_ones_column(v[:, sl], VT_PAD), p_t, (((0,), (0,)), ((), ())),
                                           preferred_element_type=F32)

    @pl.when(kt < nk_lat)
    def _():
        step(kl_ref[...], vl_ref[...])

    @pl.when(kt == nk_lat)
    def _():
        step(kx_ref[...], vx_ref[...])
        for kv in range(C_KV):
            acc = acc_ref[kv]
            o = jnp.transpose(acc[:HEAD_DIM] * (1.0 / acc[HEAD_DIM:HEAD_DIM + 1]))
            for h in range(g):
                c0 = (kv * g + h) * HEAD_DIM
                o_ref[:, c0:c0 + HEAD_DIM] = o[h * tq:(h + 1) * tq].astype(BF16)


def _dense_gqa(qk_lat, rest_lat, qk_ctx, rest_ctx, tq, tk, bounded):
    bsz, seq, _ = qk_lat.shape
    ctx_len = qk_ctx.shape[1]
    nk_lat = seq // tk
    g = C_HEADS // C_KV
    kcol = QK_OFF["ck"] // LANES
    vcol = REST_OFF["cv"] // LANES
    if bounded:
        body = functools.partial(_dense_c_bounded_kernel, tq=tq, nk_lat=nk_lat)
        scratch = [pltpu.VMEM((C_KV, g * tq, HEAD_DIM), BF16),
                   pltpu.VMEM((C_KV, HEAD_DIM + VT_PAD, g * tq), F32)]
    else:
        body = functools.partial(_dense_c_kernel, tq=tq, nk_lat=nk_lat, bounded=False)
        scratch = [pltpu.VMEM((C_KV, g * tq, HEAD_DIM), BF16),
                   pltpu.VMEM((C_KV, g * tq, LANES), F32),
                   pltpu.VMEM((C_KV, g * tq, 1), F32)]
    return pl.pallas_call(
        body,
        grid=(bsz, seq // tq, nk_lat + 1),
        in_specs=[pl.BlockSpec((None, tq, C_HEADS * HEAD_DIM), lambda b, i, kt: (b, i, QK_OFF["cq"] // 512)),
                  pl.BlockSpec((None, tk, LANES), lambda b, i, kt: (b, jnp.minimum(kt, nk_lat - 1), kcol)),
                  pl.BlockSpec((None, tk, LANES), lambda b, i, kt: (b, jnp.minimum(kt, nk_lat - 1), vcol)),
                  pl.BlockSpec((None, ctx_len, LANES), lambda b, i, kt: (b, 0, kcol)),
                  pl.BlockSpec((None, ctx_len, LANES), lambda b, i, kt: (b, 0, vcol))],
        out_specs=pl.BlockSpec((None, tq, BRANCH_W), lambda b, i, kt: (b, i, 0)),
        out_shape=jax.ShapeDtypeStruct((bsz, seq, BRANCH_W), BF16),
        scratch_shapes=scratch,
        compiler_params=_cparams(("parallel", "parallel", "arbitrary")),
        name="mixer_c_dense_bounded" if bounded else "mixer_c_dense",
    )(qk_lat, qk_lat, rest_lat, qk_ctx, rest_ctx)


def _dense_d_kernel(q_ref, kl_ref, vl_ref, kx_ref, vx_ref, lam_ref, subln_ref, o_ref, qs_ref, l_ref, acc_ref,
                    *maybe_m_ref, nk_lat, lam_init, bounded):
    kt = pl.program_id(2)
    n_sc = 2 * D_HEADS
    m_ref = maybe_m_ref[0] if maybe_m_ref else None

    @pl.when(kt == 0)
    def _():
        q = q_ref[...]
        for idx in range(n_sc):
            qs_ref[idx] = q[:, idx * HEAD_DIM:(idx + 1) * HEAD_DIM]
        l_ref[...] = jnp.zeros(l_ref.shape, F32)
        acc_ref[...] = jnp.zeros(acc_ref.shape, F32)
        if m_ref is not None:
            m_ref[...] = jnp.full(m_ref.shape, -jnp.inf, F32)

    def step(k, v):
        for idx in range(n_sc):
            h = idx // 2
            s = _dot_nt(qs_ref[idx], k[:, idx * HEAD_DIM:(idx + 1) * HEAD_DIM])
            _flash_update(s, v[:, h * 2 * HEAD_DIM:(h + 1) * 2 * HEAD_DIM], m_ref, l_ref, acc_ref, idx, bounded)

    @pl.when(kt < nk_lat)
    def _():
        step(kl_ref[...], vl_ref[...])

    @pl.when(kt == nk_lat)
    def _():
        step(kx_ref[...], vx_ref[...])
        lam = _diff_lambda(lam_ref, lam_init)
        for h in range(D_HEADS):
            l1 = jnp.sum(l_ref[2 * h], axis=-1, keepdims=True)
            l2 = jnp.sum(l_ref[2 * h + 1], axis=-1, keepdims=True)
            o = acc_ref[2 * h] / l1 - lam * (acc_ref[2 * h + 1] / l2)
            o_ref[:, h * 2 * HEAD_DIM:(h + 1) * 2 * HEAD_DIM] = _finish_diff(o, subln_ref, lam_init).astype(BF16)


def _dense_d_bounded_kernel(q_ref, kl_ref, vl_ref, kx_ref, vx_ref, lam_ref, subln_ref, o_ref, qs_ref, acc_ref, *,
                            nk_lat, lam_init):
    kt = pl.program_id(2)
    n_sc = 2 * D_HEADS
    vw = 2 * HEAD_DIM

    @pl.when(kt == 0)
    def _():
        q = q_ref[...]
        for idx in range(n_sc):
            qs_ref[idx] = q[:, idx * HEAD_DIM:(idx + 1) * HEAD_DIM]
        acc_ref[...] = jnp.zeros(acc_ref.shape, F32)

    def step(k, v):
        lane = lax.broadcasted_iota(jnp.int32, (k.shape[0], VT_PAD), 1)
        ones_cols = jnp.where(lane == 0, 1.0, 0.0).astype(BF16)
        for h in range(D_HEADS):
            v1 = jnp.concatenate([v[:, h * vw:(h + 1) * vw], ones_cols], axis=1)
            for idx in (2 * h, 2 * h + 1):
                p_t = jnp.exp2(_dot_nt(k[:, idx * HEAD_DIM:(idx + 1) * HEAD_DIM], qs_ref[idx])).astype(BF16)
                acc_ref[idx] += lax.dot_general(v1, p_t, (((0,), (0,)), ((), ())), preferred_element_type=F32)

    @pl.when(kt < nk_lat)
    def _():
        step(kl_ref[...], vl_ref[...])

    @pl.when(kt == nk_lat)
    def _():
        step(kx_ref[...], vx_ref[...])
        lam = _diff_lambda(lam_ref, lam_init)
        for h in range(D_HEADS):
            a1, a2 = acc_ref[2 * h], acc_ref[2 * h + 1]
            o_t = a1[:vw] * (1.0 / a1[vw:vw + 1]) - lam * (a2[:vw] * (1.0 / a2[vw:vw + 1]))
            o_ref[:, h * vw:(h + 1) * vw] = _finish_diff(jnp.transpose(o_t), subln_ref, lam_init).astype(BF16)


def _dense_diff(qk_lat, rest_lat, qk_ctx, rest_ctx, lam_d, subln, lam_init, tq, tk, bounded):
    bsz, seq, _ = qk_lat.shape
    ctx_len = qk_ctx.shape[1]
    nk_lat = seq // tk
    width = D_HEADS * 2 * HEAD_DIM
    kcol = QK_OFF["dk"] // width
    vcol = REST_OFF["dv"] // width
    if bounded:
        body = functools.partial(_dense_d_bounded_kernel, nk_lat=nk_lat, lam_init=lam_init)
        scratch = [pltpu.VMEM((2 * D_HEADS, tq, HEAD_DIM), BF16),
                   pltpu.VMEM((2 * D_HEADS, 2 * HEAD_DIM + VT_PAD, tq), F32)]
    else:
        body = functools.partial(_dense_d_kernel, nk_lat=nk_lat, lam_init=lam_init, bounded=False)
        scratch = [pltpu.VMEM((2 * D_HEADS, tq, HEAD_DIM), BF16),
                   pltpu.VMEM((2 * D_HEADS, tq, LANES), F32),
                   pltpu.VMEM((2 * D_HEADS, tq, 2 * HEAD_DIM), F32),
                   pltpu.VMEM((2 * D_HEADS, tq, 1), F32)]
    return pl.pallas_call(
        body,
        grid=(bsz, seq // tq, nk_lat + 1),
        in_specs=[pl.BlockSpec((None, tq, width), lambda b, i, kt: (b, i, QK_OFF["dq"] // width)),
                  pl.BlockSpec((None, tk, width), lambda b, i, kt: (b, jnp.minimum(kt, nk_lat - 1), kcol)),
                  pl.BlockSpec((None, tk, width), lambda b, i, kt: (b, jnp.minimum(kt, nk_lat - 1), vcol)),
                  pl.BlockSpec((None, ctx_len, width), lambda b, i, kt: (b, 0, kcol)),
                  pl.BlockSpec((None, ctx_len, width), lambda b, i, kt: (b, 0, vcol)),
                  pl.BlockSpec((4, HEAD_DIM), lambda b, i, kt: (0, 0)),
                  pl.BlockSpec((1, 2 * HEAD_DIM), lambda b, i, kt: (0, 0))],
        out_specs=pl.BlockSpec((None, tq, BRANCH_W), lambda b, i, kt: (b, i, 0)),
        out_shape=jax.ShapeDtypeStruct((bsz, seq, BRANCH_W), BF16),
        scratch_shapes=scratch,
        compiler_params=_cparams(("parallel", "parallel", "arbitrary")),
        name="mixer_d_diff_bounded" if bounded else "mixer_d_diff",
    )(qk_lat, qk_lat, rest_lat, qk_ctx, rest_ctx, lam_d, subln.reshape(1, 2 * HEAD_DIM))


def _ctx_kernel(sink_ref, qk_ref, v_ref, lam_ref, subln_ref, o_ref, *, lam_init):
    qk = qk_ref[...]
    vals = v_ref[...]
    ctx_len = qk.shape[0]
    v_off = {name: REST_OFF[name] - REST_OFF[_FIRST_VALUE] for name in ("av", "cv", "bv", "dv")}

    def cols(name, start, width):
        c0 = QK_OFF[name] + start
        return qk[:, c0:c0 + width]

    def gqa(qname, kname, vname, out_off, n_kv, with_sink):
        g = 8 // n_kv
        for kv in range(n_kv):
            qs = _stack_heads(cols(qname, 0, 8 * HEAD_DIM), kv * g, g)
            s = _dot_nt(qs, cols(kname, kv * HEAD_DIM, HEAD_DIM))
            extra = None
            if with_sink:
                extra = jnp.concatenate([jnp.full((ctx_len, 1), sink_ref[kv * g + h], F32) for h in range(g)], axis=0)
            p, l = _softmax_rows(s, extra)
            vv = vals[:, v_off[vname] + kv * HEAD_DIM:v_off[vname] + (kv + 1) * HEAD_DIM]
            o = _dot(p.astype(BF16), vv) / l
            for h in range(g):
                c0 = out_off + (kv * g + h) * HEAD_DIM
                o_ref[:, c0:c0 + HEAD_DIM] = o[h * ctx_len:(h + 1) * ctx_len].astype(BF16)

    gqa("aq", "ak", "av", 0 * BRANCH_W, A_KV, True)
    gqa("bq", "bk", "bv", 1 * BRANCH_W, B_HEADS, False)
    gqa("cq", "ck", "cv", 2 * BRANCH_W, C_KV, False)

    lam = _diff_lambda(lam_ref, lam_init)
    for h in range(D_HEADS):
        base = h * 2 * HEAD_DIM
        p1, l1 = _softmax_rows(_dot_nt(cols("dq", base, HEAD_DIM), cols("dk", base, HEAD_DIM)))
        p2, l2 = _softmax_rows(_dot_nt(cols("dq", base + HEAD_DIM, HEAD_DIM), cols("dk", base + HEAD_DIM, HEAD_DIM)))
        pd = p1 / l1 - lam * (p2 / l2)
        o = _dot(pd.astype(BF16), vals[:, v_off["dv"] + base:v_off["dv"] + base + 2 * HEAD_DIM])
        c0 = 3 * BRANCH_W + base
        o_ref[:, c0:c0 + 2 * HEAD_DIM] = _finish_diff(o, subln_ref, lam_init).astype(BF16)


def _ctx_attention(qk_ctx, v_ctx, sink2, lam_d, subln, lam_init):
    bsz, ctx_len, _ = qk_ctx.shape
    return pl.pallas_call(
        functools.partial(_ctx_kernel, lam_init=lam_init),
        grid=(bsz,),
        in_specs=[pl.BlockSpec(memory_space=pltpu.SMEM),
                  pl.BlockSpec((None, ctx_len, QK_COLS), lambda b: (b, 0, 0)),
                  pl.BlockSpec((None, ctx_len, v_ctx.shape[2]), lambda b: (b, 0, 0)),
                  pl.BlockSpec((4, HEAD_DIM), lambda b: (0, 0)),
                  pl.BlockSpec((1, 2 * HEAD_DIM), lambda b: (0, 0))],
        out_specs=pl.BlockSpec((None, ctx_len, N_BRANCH * BRANCH_W), lambda b: (b, 0, 0)),
        out_shape=jax.ShapeDtypeStruct((bsz, ctx_len, N_BRANCH * BRANCH_W), BF16),
        compiler_params=_cparams(("parallel",)),
        name="ctx_attention",
    )(sink2, qk_ctx, v_ctx, lam_d, subln.reshape(1, 2 * HEAD_DIM))


def _merge_kernel(x_ref, mod_ref, ya_ref, yb_ref, yc_ref, yd_ref, gp_ref, mg_ref, wbr_ref, wout_ref, o_ref):
    gate = mod_ref[...][:, 2 * D_MODEL:]
    merged = None
    for n, y_ref in enumerate((ya_ref, yb_ref, yc_ref, yd_ref)):
        yg = (y_ref[...].astype(F32) * gp_ref[:, n * BRANCH_W:(n + 1) * BRANCH_W].astype(F32)).astype(BF16)
        term = mg_ref[:, n * D_MODEL:(n + 1) * D_MODEL].astype(F32) * _dot(yg, wbr_ref[n])
        merged = term if merged is None else merged + term
    o_ref[...] = x_ref[...] + gate * _dot(merged.astype(BF16), wout_ref[...])


def _merge(x2, mod3, ys, y_cols, rest, w_br, w_out, tm, row_of_tile):
    rows = x2.shape[0]
    y_specs = [pl.BlockSpec((tm, BRANCH_W), (lambda i, c=c: (i, c))) for c in y_cols]
    return pl.pallas_call(
        _merge_kernel,
        grid=(rows // tm,),
        in_specs=[pl.BlockSpec((tm, D_MODEL), lambda i: (i, 0)),
                  pl.BlockSpec((None, 1, 3 * D_MODEL), lambda i: (row_of_tile(i), 0, 0))]
                 + y_specs
                 + [pl.BlockSpec((tm, N_BRANCH * BRANCH_W), lambda i: (i, REST_OFF["ag"] // (N_BRANCH * BRANCH_W))),
                    pl.BlockSpec((tm, N_BRANCH * D_MODEL), lambda i: (i, 0)),
                    pl.BlockSpec((N_BRANCH, BRANCH_W, D_MODEL), lambda i: (0, 0, 0)),
                    pl.BlockSpec((D_MODEL, D_MODEL), lambda i: (0, 0))],
        out_specs=pl.BlockSpec((tm, D_MODEL), lambda i: (i, 0)),
        out_shape=jax.ShapeDtypeStruct((rows, D_MODEL), F32),
        compiler_params=_cparams(("parallel",)),
        name="gated_merge",
    )(x2, mod3, *ys, rest, rest, w_br, w_out)


def _rope_tables(seq):
    t = jnp.arange(seq, dtype=jnp.int32)
    pos = jnp.stack([t // GRID_W, t % GRID_W], axis=-1).astype(F32)
    n_freq = HEAD_DIM // 4
    freqs = ROPE_THETA ** (-jnp.arange(n_freq, dtype=F32) / n_freq)
    ang = pos[:, :, None] * freqs[None, None, :]
    ang = jnp.concatenate([ang, ang], axis=-1).reshape(seq, HEAD_DIM)
    sign = np.where((np.arange(HEAD_DIM) % 32) < 16, -1.0, 1.0).astype(np.float32)
    reps = NORM_GROUP // HEAD_DIM
    return jnp.tile(jnp.cos(ang), (1, reps)), jnp.tile(jnp.sin(ang) * sign, (1, reps))


def _regroup_cols(w, order, total):
    parts = [w[:, _ORIG[n][0]:_ORIG[n][1]] for n in order]
    used = sum(p.shape[1] for p in parts)
    if total > used:
        parts.append(jnp.zeros((w.shape[0], total - used), w.dtype))
    return jnp.concatenate(parts, axis=1)


def _qk_gain_row(g):
    parts = []
    for name in _QK_ORDER:
        mixer = "abcd".index(name[0])
        is_q = name[1] == "q"
        width = _ORIG[name][1] - _ORIG[name][0]
        gain = g[mixer, 0] * (QK_SCALE * LOG2E) if is_q else g[mixer, 1]
        parts.append(jnp.tile(gain, width // HEAD_DIM))
    parts.append(jnp.ones((QK_COLS - _QK_USED,), g.dtype))
    return jnp.concatenate(parts).reshape(1, QK_COLS).astype(F32)


def kernel(x, c, ctx, c_ctx, norm_w, w_ada, b_ada, w_in, qk_gain, sink_a, rpb_b, lam_d, subln_d, w_br, w_out):
    bsz, seq, _ = x.shape
    ctx_len = ctx.shape[1]
    depth = w_ada.shape[0]
    assert seq % (2 * A_WINDOW) == 0 and ctx_len % LANES == 0 and bsz <= 6

    tm_lat = 2048
    tiles_per_batch = seq // tm_lat
    tk_dense = 1024
    tq_bounded, tq_rowmax = 512, 256
    assert seq % tk_dense == 0
    tm_merge = 512
    merge_tiles_per_batch = seq // tm_merge

    cvec = jnp.concatenate([c, c_ctx[None, :], jnp.zeros((8 - bsz - 1, D_MODEL), F32)], axis=0)
    mod_all = _ada(cvec, w_ada, b_ada)
    cos, sin = _rope_tables(seq)
    ones_bd = jnp.asarray(np.kron(np.eye(NORM_GROUP // HEAD_DIM), np.ones((HEAD_DIM, HEAD_DIM))), BF16)

    x2 = x.reshape(bsz * seq, D_MODEL)
    c2 = ctx.reshape(bsz * ctx_len, D_MODEL)
    for l in range(depth):
        need_ctx = l < depth - 1
        lam_init = 0.8 - 0.6 * math.exp(-0.3 * l)
        mod3 = mod_all[l].reshape(8, 1, 3 * D_MODEL)
        w_qk = _regroup_cols(w_in[l], _QK_ORDER, QK_COLS).astype(BF16)
        w_rest = _regroup_cols(w_in[l], _REST_ORDER, REST_COLS).astype(BF16)
        gain_row = _qk_gain_row(qk_gain[l])

        hx = _prenorm(x2, norm_w[l], mod3, tm_lat, lambda i: i // tiles_per_batch)
        hc = _prenorm(c2, norm_w[l], mod3, bsz * ctx_len, lambda i: bsz)
        qk_lat = _proj_qk(hx, w_qk, gain_row, ones_bd, tm_lat, cos, sin, tiles_per_batch).reshape(bsz, seq, QK_COLS)
        qk_ctx = _proj_qk(hc, w_qk, gain_row, ones_bd, bsz * ctx_len).reshape(bsz, ctx_len, QK_COLS)
        rest_lat = _proj_rest(hx, w_rest, tm_lat).reshape(bsz, seq, REST_COLS)
        rest_ctx = _proj_rest(hc, w_rest, bsz * ctx_len).reshape(bsz, ctx_len, REST_COLS)

        sink2 = sink_a[l] * LOG2E
        bias_tabs = _nbr_bias_tables(rpb_b[l] * LOG2E, seq, 2 * A_WINDOW)
        proj = (qk_lat, rest_lat, qk_ctx, rest_ctx)

        def qk_bound(g):
            return HEAD_DIM * QK_SCALE * LOG2E * jnp.max(jnp.abs(g[0])) * jnp.max(jnp.abs(g[1])) * 1.02

        y_a = lax.cond(qk_bound(qk_gain[l, 0]) <= LOGIT_BOUND,
                       lambda: _window_attention(*proj, sink2, True),
                       lambda: _window_attention(*proj, sink2, False))
        y_b = lax.cond(qk_bound(qk_gain[l, 1]) + jnp.max(jnp.abs(rpb_b[l])) * LOG2E <= LOGIT_BOUND,
                       lambda: _neighborhood_attention(*proj, bias_tabs, True),
                       lambda: _neighborhood_attention(*proj, bias_tabs, False))
        y_c = lax.cond(qk_bound(qk_gain[l, 2]) <= LOGIT_BOUND,
                       lambda: _dense_gqa(*proj, tq_bounded, tk_dense, True),
                       lambda: _dense_gqa(*proj, tq_rowmax, tk_dense, False))
        y_d = lax.cond(qk_bound(qk_gain[l, 3]) <= LOGIT_BOUND,
                       lambda: _dense_diff(*proj, lam_d[l], subln_d[l], lam_init, tq_bounded, tk_dense, True),
                       lambda: _dense_diff(*proj, lam_d[l], subln_d[l], lam_init, tq_rowmax, tk_dense, False))

        w_br_l = w_br[l].astype(BF16)
        w_out_l = w_out[l].astype(BF16)
        ys = [y.reshape(bsz * seq, BRANCH_W) for y in (y_a, y_b, y_c, y_d)]
        x_new = _merge(x2, mod3, ys, (0, 0, 0, 0), rest_lat.reshape(bsz * seq, REST_COLS), w_br_l, w_out_l,
                       tm_merge, lambda i: i // merge_tiles_per_batch)
        if need_ctx:
            y_ctx = _ctx_attention(qk_ctx, rest_ctx[:, :, REST_OFF[_FIRST_VALUE]:], sink2, lam_d[l], subln_d[l],
                                   lam_init)
            y_ctx2 = y_ctx.reshape(bsz * ctx_len, N_BRANCH * BRANCH_W)
            c2 = _merge(c2, mod3, [y_ctx2] * 4, (0, 1, 2, 3), rest_ctx.reshape(bsz * ctx_len, REST_COLS),
                        w_br_l, w_out_l, bsz * ctx_len, lambda i: bsz)
        x2 = x_new
    return x2.reshape(bsz, seq, D_MODEL)
```

```python
import functools
import math

import numpy as np
import jax
import jax.numpy as jnp
from jax import lax
from jax.experimental import pallas as pl
from jax.experimental.pallas import tpu as pltpu

F32 = jnp.float32
BF16 = jnp.bfloat16

D_MODEL = 1024
GRID_W = 64
HEAD_DIM = 64
BRANCH_W = 512
N_BRANCH = 4
A_HEADS, A_KV, A_WINDOW = 8, 2, 128
B_HEADS, NB_ROWS, NB_COLS = 8, 8, 16
C_HEADS, C_KV = 8, 2
D_HEADS = 4
ROPE_THETA = 10000.0
EPS = 1e-6
NEG_INF = -1e30
QK_SCALE = HEAD_DIM ** -0.5
LOG2E = math.log2(math.e)
LOGIT_BOUND = 60.0

V7X_VMEM_LIMIT_BYTES = 56 * 1024 * 1024
LANES = 128

_ORIG = dict(aq=(0, 512), ak=(512, 640), av=(640, 768), ag=(768, 1280),
             bq=(1280, 1792), bk=(1792, 2304), bv=(2304, 2816), bg=(2816, 3328),
             cq=(3328, 3840), ck=(3840, 3968), cv=(3968, 4096), cg=(4096, 4608),
             dq=(4608, 5120), dk=(5120, 5632), dv=(5632, 6144), dg=(6144, 6656),
             mg=(6656, 10752))
_QK_ORDER = ("aq", "cq", "dq", "bq", "bk", "dk", "ak", "ck")
_REST_ORDER = ("mg", "ag", "bg", "cg", "dg", "dv", "bv", "av", "cv")
_FIRST_VALUE = "dv"


def _offsets(order):
    off, out = 0, {}
    for name in order:
        lo, hi = _ORIG[name]
        out[name] = off
        off += hi - lo
    return out, off


PROJ_TN = 512
PROJ_ROW_CHUNK = 512
VT_PAD = 16
DENSE_KEY_CHUNK = 512
NORM_GROUP = 256
QK_OFF, _QK_USED = _offsets(_QK_ORDER)
REST_OFF, _REST_USED = _offsets(_REST_ORDER)
QK_COLS = -(-_QK_USED // PROJ_TN) * PROJ_TN
REST_COLS = -(-_REST_USED // PROJ_TN) * PROJ_TN
_ROPE_LO_END = QK_OFF["bq"] // PROJ_TN
_ROPE_HI_START = QK_OFF["dk"] // PROJ_TN
_MG_BLOCKS = (REST_OFF["ag"]) // PROJ_TN
_GP_BLOCKS_END = REST_OFF[_FIRST_VALUE] // PROJ_TN
assert QK_OFF["bq"] % PROJ_TN == 0 and QK_OFF["dk"] % PROJ_TN == 0
assert REST_OFF["ag"] % PROJ_TN == 0 and REST_OFF[_FIRST_VALUE] % PROJ_TN == 0


def _cparams(sem):
    return pltpu.CompilerParams(dimension_semantics=sem, vmem_limit_bytes=V7X_VMEM_LIMIT_BYTES)


def _sigmoid(x):
    return 1.0 / (1.0 + jnp.exp(-x))


def _dot_nt(a, b):
    return lax.dot_general(a, b, (((1,), (1,)), ((), ())), preferred_element_type=F32)


def _dot(a, b):
    return jnp.dot(a, b, preferred_element_type=F32)


def _ada_kernel(c_ref, w_ref, b_ref, o_ref):
    c = c_ref[...]
    o_ref[...] = _dot(c * _sigmoid(c), w_ref[...]) + b_ref[...]


def _ada(cvec, w_ada, b_ada):
    depth = w_ada.shape[0]
    tn = 512
    return pl.pallas_call(
        _ada_kernel,
        grid=(depth, 3 * D_MODEL // tn),
        in_specs=[pl.BlockSpec((8, D_MODEL), lambda l, j: (0, 0)),
                  pl.BlockSpec((None, D_MODEL, tn), lambda l, j: (l, 0, j)),
                  pl.BlockSpec((None, 1, tn), lambda l, j: (l, 0, j))],
        out_specs=pl.BlockSpec((None, 8, tn), lambda l, j: (l, 0, j)),
        out_shape=jax.ShapeDtypeStruct((depth, 8, 3 * D_MODEL), F32),
        compiler_params=_cparams(("parallel", "parallel")),
        name="ada_mod",
    )(cvec, w_ada, b_ada.reshape(depth, 1, 3 * D_MODEL))


def _prenorm_kernel(x_ref, nw_ref, mod_ref, o_ref):
    x = x_ref[...]
    mod = mod_ref[...]
    shift, scale = mod[:, :D_MODEL], mod[:, D_MODEL:2 * D_MODEL]
    y = x * lax.rsqrt(jnp.mean(x * x, axis=-1, keepdims=True) + EPS) * nw_ref[...]
    o_ref[...] = (y * (1.0 + scale) + shift).astype(BF16)


def _prenorm(x2, norm_w, mod3, tm, row_of_tile):
    rows = x2.shape[0]
    return pl.pallas_call(
        _prenorm_kernel,
        grid=(rows // tm,),
        in_specs=[pl.BlockSpec((tm, D_MODEL), lambda i: (i, 0)),
                  pl.BlockSpec((1, D_MODEL), lambda i: (0, 0)),
                  pl.BlockSpec((None, 1, 3 * D_MODEL), lambda i: (row_of_tile(i), 0, 0))],
        out_specs=pl.BlockSpec((tm, D_MODEL), lambda i: (i, 0)),
        out_shape=jax.ShapeDtypeStruct((rows, D_MODEL), BF16),
        compiler_params=_cparams(("parallel",)),
        name="prenorm",
    )(x2, norm_w.reshape(1, D_MODEL), mod3)


def _rot_half_unsigned(n):
    lane = lax.broadcasted_iota(jnp.int32, n.shape, 1)
    return jnp.where((lane & 31) < 16, pltpu.roll(n, LANES - 16, 1), pltpu.roll(n, 16, 1))


def _proj_qk_kernel(*refs, rope):
    if rope:
        hx_ref, w_ref, gain_ref, ones_ref, cos_ref, sin_ref, o_ref = refs
    else:
        hx_ref, w_ref, gain_ref, ones_ref, o_ref = refs
    w = w_ref[...]
    ones_bd = ones_ref[...]
    rows = hx_ref.shape[0]
    chunk = min(rows, PROJ_ROW_CHUNK)
    if rope:
        j = pl.program_id(1)
        use = jnp.logical_or(j < _ROPE_LO_END, j >= _ROPE_HI_START)
    for r0 in range(0, rows, chunk):
        rs = slice(r0, r0 + chunk)
        acc_all = _dot(hx_ref[rs, :], w)
        if rope:
            cos = jnp.where(use, cos_ref[rs, :], 1.0)
            sin = jnp.where(use, sin_ref[rs, :], 0.0)
        for c in range(PROJ_TN // NORM_GROUP):
            sl = slice(c * NORM_GROUP, (c + 1) * NORM_GROUP)
            acc = acc_all[:, sl]
            ss = _dot((acc * acc).astype(BF16), ones_bd)
            n = acc * lax.rsqrt(ss * (1.0 / HEAD_DIM) + EPS) * gain_ref[:, sl]
            if rope:
                rot = jnp.concatenate([_rot_half_unsigned(n[:, :LANES]), _rot_half_unsigned(n[:, LANES:])], axis=1)
                n = n * cos + rot * sin
            o_ref[rs, sl] = n.astype(BF16)


def _proj_qk(hx, w_qk, gain_row, ones_bd, tm, cos=None, sin=None, tiles_per_batch=None):
    rows = hx.shape[0]
    rope = cos is not None
    in_specs = [pl.BlockSpec((tm, D_MODEL), lambda i, j: (i, 0)),
                pl.BlockSpec((D_MODEL, PROJ_TN), lambda i, j: (0, j)),
                pl.BlockSpec((1, PROJ_TN), lambda i, j: (0, j)),
                pl.BlockSpec((NORM_GROUP, NORM_GROUP), lambda i, j: (0, 0))]
    args = [hx, w_qk, gain_row, ones_bd]
    if rope:
        in_specs += [pl.BlockSpec((tm, NORM_GROUP), lambda i, j: (i % tiles_per_batch, 0))] * 2
        args += [cos, sin]
    return pl.pallas_call(
        functools.partial(_proj_qk_kernel, rope=rope),
        grid=(rows // tm, QK_COLS // PROJ_TN),
        in_specs=in_specs,
        out_specs=pl.BlockSpec((tm, PROJ_TN), lambda i, j: (i, j)),
        out_shape=jax.ShapeDtypeStruct((rows, QK_COLS), BF16),
        compiler_params=_cparams(("parallel", "arbitrary")),
        name="proj_qk_rope" if rope else "proj_qk",
    )(*args)


def _proj_rest_kernel(hx_ref, w_ref, o_ref):
    j = pl.program_id(1)
    w = w_ref[...]
    rows = hx_ref.shape[0]
    chunk = min(rows, PROJ_ROW_CHUNK)
    for r0 in range(0, rows, chunk):
        acc = _dot(hx_ref[r0:r0 + chunk, :], w)
        sg = _sigmoid(acc)
        o_ref[r0:r0 + chunk, :] = jnp.where(j < _MG_BLOCKS, sg,
                                            jnp.where(j < _GP_BLOCKS_END, acc * sg, acc)).astype(BF16)


def _proj_rest(hx, w_rest, tm):
    rows = hx.shape[0]
    return pl.pallas_call(
        _proj_rest_kernel,
        grid=(rows // tm, REST_COLS // PROJ_TN),
        in_specs=[pl.BlockSpec((tm, D_MODEL), lambda i, j: (i, 0)),
                  pl.BlockSpec((D_MODEL, PROJ_TN), lambda i, j: (0, j))],
        out_specs=pl.BlockSpec((tm, PROJ_TN), lambda i, j: (i, j)),
        out_shape=jax.ShapeDtypeStruct((rows, REST_COLS), BF16),
        compiler_params=_cparams(("parallel", "arbitrary")),
        name="proj_rest",
    )(hx, w_rest)


def _stack_heads(q, first_head, n):
    return jnp.concatenate([q[:, (first_head + g) * HEAD_DIM:(first_head + g + 1) * HEAD_DIM] for g in range(n)],
                           axis=0)


def _softmax_rows(s, extra=None):
    m = jnp.max(s, axis=-1, keepdims=True)
    if extra is not None:
        m = jnp.maximum(m, extra)
    p = jnp.exp2(s - m)
    l = jnp.sum(p, axis=-1, keepdims=True)
    if extra is not None:
        l = l + jnp.exp2(extra - m)
    return p, l


def _with_ones_column(v, pad=HEAD_DIM):
    lane = lax.broadcasted_iota(jnp.int32, (v.shape[0], pad), 1)
    return jnp.concatenate([v, jnp.where(lane == 0, 1.0, 0.0).astype(v.dtype)], axis=1)


def _lane_partial_sums(p):
    part = p[:, :LANES]
    for c in range(1, p.shape[1] // LANES):
        part = part + p[:, c * LANES:(c + 1) * LANES]
    return part


def _diff_lambda(lam_ref, lam_init):
    lf = lam_ref[...]
    a = jnp.sum(lf[0:1] * lf[1:2], axis=-1, keepdims=True)
    b = jnp.sum(lf[2:3] * lf[3:4], axis=-1, keepdims=True)
    return jnp.exp(a) - jnp.exp(b) + lam_init


def _finish_diff(o, subln_ref, lam_init):
    y = o * lax.rsqrt(jnp.mean(o * o, axis=-1, keepdims=True) + EPS) * subln_ref[...]
    return y * (1.0 - lam_init)


def _win_kernel(sink_ref, q_ref, kp_ref, kc_ref, kn_ref, kx_ref, vp_ref, vc_ref, vn_ref, vx_ref, o_ref, *,
                tq, seq):
    i = pl.program_id(1)
    g = A_HEADS // A_KV
    q = q_ref[...]
    k_all = jnp.concatenate([kp_ref[...], kc_ref[...], kn_ref[...], kx_ref[...]], axis=0)
    v_all = jnp.concatenate([vp_ref[...], vc_ref[...], vn_ref[...], vx_ref[...]], axis=0)
    nw = 2 * tq
    t = lax.broadcasted_iota(jnp.int32, (g * tq, nw), 0) & (tq - 1)
    j = lax.broadcasted_iota(jnp.int32, (g * tq, nw), 1)
    rel = j - tq // 2 - t
    kpos = i * tq - tq // 2 + j
    bad = jnp.where(jnp.abs(rel) > A_WINDOW, 1, 0) + jnp.where(kpos < 0, 1, 0) + jnp.where(kpos >= seq, 1, 0)
    for kv in range(A_KV):
        qs = _stack_heads(q, kv * g, g)
        kk = k_all[:, kv * HEAD_DIM:(kv + 1) * HEAD_DIM]
        vv = v_all[:, kv * HEAD_DIM:(kv + 1) * HEAD_DIM]
        s = _dot_nt(qs, kk)
        s_win = jnp.where(bad > 0, NEG_INF, s[:, :nw])
        s_ctx = s[:, nw:]
        sink = jnp.concatenate([jnp.full((tq, 1), sink_ref[kv * g + h], F32) for h in range(g)], axis=0)
        m = jnp.maximum(jnp.maximum(jnp.max(s_win, axis=-1, keepdims=True),
                                    jnp.max(s_ctx, axis=-1, keepdims=True)), sink)
        pw = jnp.exp2(s_win - m)
        pc = jnp.exp2(s_ctx - m)
        l = jnp.sum(pw, axis=-1, keepdims=True) + jnp.sum(pc, axis=-1, keepdims=True) + jnp.exp2(sink - m)
        o = (_dot(pw.astype(BF16), vv[:nw]) + _dot(pc.astype(BF16), vv[nw:])) / l
        for h in range(g):
            c0 = (kv * g + h) * HEAD_DIM
            o_ref[:, c0:c0 + HEAD_DIM] = o[h * tq:(h + 1) * tq].astype(BF16)


def _win_bounded_kernel(sink_ref, q_ref, kp_ref, kc_ref, kn_ref, kx_ref, vp_ref, vc_ref, vn_ref, vx_ref, mask_ref,
                        o_ref, *, tq):
    g = A_HEADS // A_KV
    q = q_ref[...]
    mask_t = mask_ref[...]
    sink = sink_ref[...]
    k_all = jnp.concatenate([kp_ref[...], kc_ref[...], kn_ref[...], kx_ref[...]], axis=0)
    v_all = jnp.concatenate([vp_ref[...], vc_ref[...], vn_ref[...], vx_ref[...]], axis=0)
    for kv in range(A_KV):
        sl = slice(kv * HEAD_DIM, (kv + 1) * HEAD_DIM)
        qs = _stack_heads(q, kv * g, g)
        stab = [jnp.maximum(sink[:, kv * g + h:kv * g + h + 1], 0.0) for h in range(g)]
        shift_t = jnp.concatenate([mask_t - stab[h] for h in range(g)], axis=1)
        p_t = jnp.exp2(_dot_nt(k_all[:, sl], qs) + shift_t).astype(BF16)
        acc = lax.dot_general(_with_ones_column(v_all[:, sl], VT_PAD), p_t, (((0,), (0,)), ((), ())),
                              preferred_element_type=F32)
        for h in range(g):
            a = acc[:, h * tq:(h + 1) * tq]
            l = a[HEAD_DIM:HEAD_DIM + 1] + jnp.exp2(sink[:, kv * g + h:kv * g + h + 1] - stab[h])
            c0 = (kv * g + h) * HEAD_DIM
            o_ref[:, c0:c0 + HEAD_DIM] = jnp.transpose(a[:HEAD_DIM] * (1.0 / l)).astype(BF16)


def _window_mask_tables(seq, tq, ctx_len):
    nt = seq // tq
    tabs = []
    for i0 in (0, 1, nt - 1):
        t = np.arange(tq)[None, :]
        j = np.arange(2 * tq)[:, None]
        kpos = i0 * tq - tq // 2 + j
        ok = (np.abs(j - tq // 2 - t) <= A_WINDOW) & (kpos >= 0) & (kpos < seq)
        win = np.where(ok, 0.0, NEG_INF).astype(np.float32)
        tabs.append(np.concatenate([win, np.zeros((ctx_len, tq), np.float32)], axis=0))
    return jnp.asarray(np.stack(tabs, axis=0))


def _window_attention(qk_lat, rest_lat, qk_ctx, rest_ctx, sink2, bounded):
    bsz, seq, _ = qk_lat.shape
    ctx_len = qk_ctx.shape[1]
    tq = 2 * A_WINDOW
    nt = seq // tq
    half = tq // 2
    kcol = QK_OFF["ak"] // LANES
    vcol = REST_OFF["av"] // LANES
    n_half = seq // half

    def kv_specs(col):
        return [pl.BlockSpec((None, half, LANES), lambda b, i: (b, jnp.maximum(2 * i - 1, 0), col)),
                pl.BlockSpec((None, tq, LANES), lambda b, i: (b, i, col)),
                pl.BlockSpec((None, half, LANES), lambda b, i: (b, jnp.minimum(2 * i + 2, n_half - 1), col)),
                pl.BlockSpec((None, ctx_len, LANES), lambda b, i: (b, 0, col))]

    def tile_variant(i):
        return jnp.where(i == 0, 0, jnp.where(i == nt - 1, 2, 1))

    q_spec = pl.BlockSpec((None, tq, A_HEADS * HEAD_DIM), lambda b, i: (b, i, QK_OFF["aq"] // 512))
    kv_args = (qk_lat, qk_lat, qk_lat, qk_ctx, rest_lat, rest_lat, rest_lat, rest_ctx)
    if bounded:
        body = functools.partial(_win_bounded_kernel, tq=tq)
        in_specs = ([pl.BlockSpec((1, A_HEADS), lambda b, i: (0, 0)), q_spec] + kv_specs(kcol) + kv_specs(vcol)
                    + [pl.BlockSpec((None, 2 * tq + ctx_len, tq), lambda b, i: (tile_variant(i), 0, 0))])
        args = (sink2.reshape(1, A_HEADS), qk_lat) + kv_args + (_window_mask_tables(seq, tq, ctx_len),)
    else:
        body = functools.partial(_win_kernel, tq=tq, seq=seq)
        in_specs = [pl.BlockSpec(memory_space=pltpu.SMEM), q_spec] + kv_specs(kcol) + kv_specs(vcol)
        args = (sink2, qk_lat) + kv_args
    return pl.pallas_call(
        body,
        grid=(bsz, nt),
        in_specs=in_specs,
        out_specs=pl.BlockSpec((None, tq, BRANCH_W), lambda b, i: (b, i, 0)),
        out_shape=jax.ShapeDtypeStruct((bsz, seq, BRANCH_W), BF16),
        compiler_params=_cparams(("parallel", "parallel")),
        name="mixer_a_window_bounded" if bounded else "mixer_a_window",
    )(*args)


def _nbr_kernel(q_ref, kp_ref, kc_ref, kn_ref, kx_ref, vp_ref, vc_ref, vn_ref, vx_ref, bias_ref, o_ref, *, tq,
                bounded):
    q = q_ref[...]
    k_all = jnp.concatenate([kp_ref[...], kc_ref[...], kn_ref[...], kx_ref[...]], axis=0)
    v_all = jnp.concatenate([vp_ref[...], vc_ref[...], vn_ref[...], vx_ref[...]], axis=0)
    nw = 3 * tq
    for h in range(B_HEADS):
        sl = slice(h * HEAD_DIM, (h + 1) * HEAD_DIM)
        s = _dot_nt(q[:, sl], k_all[:, sl])
        s = jnp.concatenate([s[:, :nw] + bias_ref[h], s[:, nw:]], axis=1)
        if not bounded:
            s = s - jnp.max(s, axis=-1, keepdims=True)
        acc = _dot(jnp.exp2(s).astype(BF16), _with_ones_column(v_all[:, sl]))
        o_ref[:, sl] = (acc[:, :HEAD_DIM] * (1.0 / acc[:, HEAD_DIM:HEAD_DIM + 1])).astype(BF16)


def _nbr_bias_tables(rpb, seq, tq):
    rows = seq // GRID_W
    nt = seq // tq
    kr = min(NB_ROWS, rows)
    qr = tq // GRID_W
    n_heads = rpb.shape[0]
    assert NB_ROWS - 1 - qr - (qr - 1) >= 0 and NB_ROWS - 1 - qr + 3 * qr <= 2 * NB_ROWS - 1
    pad = GRID_W - 1
    rpb_pad = jnp.pad(rpb.astype(F32), ((0, 0), (0, 0), (pad, pad)))
    toep = jnp.stack([rpb_pad[:, :, pad + NB_COLS - 1 - c:pad + NB_COLS - 1 - c + GRID_W] for c in range(GRID_W)],
                     axis=2)
    c = np.arange(GRID_W)
    cstart = np.clip(c - NB_COLS // 2, 0, GRID_W - NB_COLS)
    col_ok = (c[None, :] >= cstart[:, None]) & (c[None, :] < cstart[:, None] + NB_COLS)
    toep = jnp.where(col_ok[None, None], toep, NEG_INF)
    a0 = NB_ROWS - 1 - qr
    per_rl = [jnp.transpose(toep[:, a0 - rl:a0 - rl + 3 * qr], (0, 2, 1, 3)) for rl in range(qr)]
    base = jnp.stack(per_rl, axis=1)
    tabs = []
    for i0 in (0, 1, nt - 1):
        r = i0 * qr + np.arange(qr)
        r2 = (i0 - 1) * qr + np.arange(3 * qr)
        rstart = np.clip(r - kr // 2, 0, rows - kr)
        row_ok = ((r2[None, :] >= rstart[:, None]) & (r2[None, :] < rstart[:, None] + kr)
                  & (r2[None, :] >= 0) & (r2[None, :] < rows))
        tabs.append(jnp.where(row_ok[None, :, None, :, None], base, NEG_INF).reshape(n_heads, tq, 3 * tq))
    return jnp.stack(tabs, axis=0)


def _neighborhood_attention(qk_lat, rest_lat, qk_ctx, rest_ctx, rpb2, tq, bounded):
    bsz, seq, _ = qk_lat.shape
    ctx_len = qk_ctx.shape[1]
    nt = seq // tq
    width = B_HEADS * HEAD_DIM
    bias_tabs = _nbr_bias_tables(rpb2, seq, tq)

    def kv_specs(col):
        return [pl.BlockSpec((None, tq, width), lambda b, i: (b, jnp.maximum(i - 1, 0), col)),
                pl.BlockSpec((None, tq, width), lambda b, i: (b, i, col)),
                pl.BlockSpec((None, tq, width), lambda b, i: (b, jnp.minimum(i + 1, nt - 1), col)),
                pl.BlockSpec((None, ctx_len, width), lambda b, i: (b, 0, col))]

    def bias_map(b, i):
        return (jnp.where(i == 0, 0, jnp.where(i == nt - 1, 2, 1)), 0, 0, 0)

    return pl.pallas_call(
        functools.partial(_nbr_kernel, tq=tq, bounded=bounded),
        grid=(bsz, nt),
        in_specs=[pl.BlockSpec((None, tq, width), lambda b, i: (b, i, QK_OFF["bq"] // width))]
                 + kv_specs(QK_OFF["bk"] // width) + kv_specs(REST_OFF["bv"] // width)
                 + [pl.BlockSpec((None,) + bias_tabs.shape[1:], bias_map)],
        out_specs=pl.BlockSpec((None, tq, BRANCH_W), lambda b, i: (b, i, 0)),
        out_shape=jax.ShapeDtypeStruct((bsz, seq, BRANCH_W), BF16),
        compiler_params=_cparams(("parallel", "parallel")),
        name="mixer_b_neighbourhood_bounded" if bounded else "mixer_b_neighbourhood",
    )(qk_lat, qk_lat, qk_lat, qk_lat, qk_ctx, rest_lat, rest_lat, rest_lat, rest_ctx, bias_tabs)


def _flash_update(s, v, m_ref, l_ref, acc_ref, idx):
    m_prev = m_ref[idx]
    m_new = jnp.maximum(m_prev, jnp.max(s, axis=-1, keepdims=True))
    alpha = jnp.exp2(m_prev - m_new)
    p = jnp.exp2(s - m_new)
    if l_ref is not None:
        l_ref[idx] = alpha * l_ref[idx] + _lane_partial_sums(p)
    acc_ref[idx] = alpha * acc_ref[idx] + _dot(p.astype(BF16), v)
    m_ref[idx] = m_new


def _dense_c_kernel(q_ref, kl_ref, vl_ref, kx_ref, vx_ref, o_ref, qs_ref, acc_ref, m_ref, *, tq, nk_lat):
    kt = pl.program_id(2)
    g = C_HEADS // C_KV

    @pl.when(kt == 0)
    def _():
        q = q_ref[...]
        for kv in range(C_KV):
            qs_ref[kv] = _stack_heads(q, kv * g, g)
        acc_ref[...] = jnp.zeros(acc_ref.shape, F32)
        m_ref[...] = jnp.full(m_ref.shape, -jnp.inf, F32)

    def step(k, v):
        for kv in range(C_KV):
            sl = slice(kv * HEAD_DIM, (kv + 1) * HEAD_DIM)
            _flash_update(_dot_nt(qs_ref[kv], k[:, sl]), _with_ones_column(v[:, sl]), m_ref, None, acc_ref, kv)

    @pl.when(kt < nk_lat)
    def _():
        step(kl_ref[...], vl_ref[...])

    @pl.when(kt == nk_lat)
    def _():
        step(kx_ref[...], vx_ref[...])
        for kv in range(C_KV):
            acc = acc_ref[kv]
            o = acc[:, :HEAD_DIM] * (1.0 / acc[:, HEAD_DIM:HEAD_DIM + 1])
            for h in range(g):
                c0 = (kv * g + h) * HEAD_DIM
                o_ref[:, c0:c0 + HEAD_DIM] = o[h * tq:(h + 1) * tq].astype(BF16)


def _dense_c_bounded_kernel(q_ref, kl_ref, vl_ref, kx_ref, vx_ref, o_ref, qs_ref, acc_ref, *, tq, nk_lat):
    kt = pl.program_id(2)
    g = C_HEADS // C_KV

    @pl.when(kt == 0)
    def _():
        q = q_ref[...]
        for kv in range(C_KV):
            qs_ref[kv] = _stack_heads(q, kv * g, g)
        acc_ref[...] = jnp.zeros(acc_ref.shape, F32)

    def step(k, v):
        n = k.shape[0]
        chunk = min(n, DENSE_KEY_CHUNK)
        for kv in range(C_KV):
            sl = slice(kv * HEAD_DIM, (kv + 1) * HEAD_DIM)
            upd = None
            for k0 in range(0, n, chunk):
                ks = slice(k0, k0 + chunk)
                p_t = jnp.exp2(_dot_nt(k[ks, sl], qs_ref[kv])).astype(BF16)
                term = lax.dot_general(_with_ones_column(v[ks, sl], VT_PAD), p_t, (((0,), (0,)), ((), ())),
                                       preferred_element_type=F32)
                upd = term if upd is None else upd + term
            acc_ref[kv] += upd

    @pl.when(kt < nk_lat)
    def _():
        step(kl_ref[...], vl_ref[...])

    @pl.when(kt == nk_lat)
    def _():
        step(kx_ref[...], vx_ref[...])
        for kv in range(C_KV):
            acc = acc_ref[kv]
            o = jnp.transpose(acc[:HEAD_DIM] * (1.0 / acc[HEAD_DIM:HEAD_DIM + 1]))
            for h in range(g):
                c0 = (kv * g + h) * HEAD_DIM
                o_ref[:, c0:c0 + HEAD_DIM] = o[h * tq:(h + 1) * tq].astype(BF16)


def _dense_gqa(qk_lat, rest_lat, qk_ctx, rest_ctx, tq, tk, bounded):
    bsz, seq, _ = qk_lat.shape
    ctx_len = qk_ctx.shape[1]
    nk_lat = seq // tk
    g = C_HEADS // C_KV
    kcol = QK_OFF["ck"] // LANES
    vcol = REST_OFF["cv"] // LANES
    if bounded:
        body = functools.partial(_dense_c_bounded_kernel, tq=tq, nk_lat=nk_lat)
        scratch = [pltpu.VMEM((C_KV, g * tq, HEAD_DIM), BF16),
                   pltpu.VMEM((C_KV, HEAD_DIM + VT_PAD, g * tq), F32)]
    else:
        body = functools.partial(_dense_c_kernel, tq=tq, nk_lat=nk_lat)
        scratch = [pltpu.VMEM((C_KV, g * tq, HEAD_DIM), BF16),
                   pltpu.VMEM((C_KV, g * tq, LANES), F32),
                   pltpu.VMEM((C_KV, g * tq, 1), F32)]
    return pl.pallas_call(
        body,
        grid=(bsz, seq // tq, nk_lat + 1),
        in_specs=[pl.BlockSpec((None, tq, C_HEADS * HEAD_DIM), lambda b, i, kt: (b, i, QK_OFF["cq"] // 512)),
                  pl.BlockSpec((None, tk, LANES), lambda b, i, kt: (b, jnp.minimum(kt, nk_lat - 1), kcol)),
                  pl.BlockSpec((None, tk, LANES), lambda b, i, kt: (b, jnp.minimum(kt, nk_lat - 1), vcol)),
                  pl.BlockSpec((None, ctx_len, LANES), lambda b, i, kt: (b, 0, kcol)),
                  pl.BlockSpec((None, ctx_len, LANES), lambda b, i, kt: (b, 0, vcol))],
        out_specs=pl.BlockSpec((None, tq, BRANCH_W), lambda b, i, kt: (b, i, 0)),
        out_shape=jax.ShapeDtypeStruct((bsz, seq, BRANCH_W), BF16),
        scratch_shapes=scratch,
        compiler_params=_cparams(("parallel", "parallel", "arbitrary")),
        name="mixer_c_dense_bounded" if bounded else "mixer_c_dense",
    )(qk_lat, qk_lat, rest_lat, qk_ctx, rest_ctx)


def _dense_d_kernel(q_ref, kl_ref, vl_ref, kx_ref, vx_ref, lam_ref, subln_ref, o_ref, qs_ref, l_ref, acc_ref,
                    m_ref, *, nk_lat, lam_init):
    kt = pl.program_id(2)
    n_sc = 2 * D_HEADS

    @pl.when(kt == 0)
    def _():
        q = q_ref[...]
        for idx in range(n_sc):
            qs_ref[idx] = q[:, idx * HEAD_DIM:(idx + 1) * HEAD_DIM]
        l_ref[...] = jnp.zeros(l_ref.shape, F32)
        acc_ref[...] = jnp.zeros(acc_ref.shape, F32)
        m_ref[...] = jnp.full(m_ref.shape, -jnp.inf, F32)

    def step(k, v):
        for idx in range(n_sc):
            h = idx // 2
            s = _dot_nt(qs_ref[idx], k[:, idx * HEAD_DIM:(idx + 1) * HEAD_DIM])
            _flash_update(s, v[:, h * 2 * HEAD_DIM:(h + 1) * 2 * HEAD_DIM], m_ref, l_ref, acc_ref, idx)

    @pl.when(kt < nk_lat)
    def _():
        step(kl_ref[...], vl_ref[...])

    @pl.when(kt == nk_lat)
    def _():
        step(kx_ref[...], vx_ref[...])
        lam = _diff_lambda(lam_ref, lam_init)
        for h in range(D_HEADS):
            l1 = jnp.sum(l_ref[2 * h], axis=-1, keepdims=True)
            l2 = jnp.sum(l_ref[2 * h + 1], axis=-1, keepdims=True)
            o = acc_ref[2 * h] / l1 - lam * (acc_ref[2 * h + 1] / l2)
            o_ref[:, h * 2 * HEAD_DIM:(h + 1) * 2 * HEAD_DIM] = _finish_diff(o, subln_ref, lam_init).astype(BF16)


def _dense_d_bounded_kernel(q_ref, kl_ref, vl_ref, kx_ref, vx_ref, lam_ref, subln_ref, o_ref, qs_ref, acc_ref, *,
                            nk_lat, lam_init):
    kt = pl.program_id(2)
    n_sc = 2 * D_HEADS
    vw = 2 * HEAD_DIM

    @pl.when(kt == 0)
    def _():
        q = q_ref[...]
        for idx in range(n_sc):
            qs_ref[idx] = q[:, idx * HEAD_DIM:(idx + 1) * HEAD_DIM]
        acc_ref[...] = jnp.zeros(acc_ref.shape, F32)

    def step(k, v):
        n = k.shape[0]
        chunk = min(n, 2 * DENSE_KEY_CHUNK)
        lane = lax.broadcasted_iota(jnp.int32, (chunk, VT_PAD), 1)
        ones_cols = jnp.where(lane == 0, 1.0, 0.0).astype(BF16)
        for h in range(D_HEADS):
            upd = [None, None]
            for k0 in range(0, n, chunk):
                ks = slice(k0, k0 + chunk)
                v1 = jnp.concatenate([v[ks, h * vw:(h + 1) * vw], ones_cols], axis=1)
                for c in range(2):
                    idx = 2 * h + c
                    p_t = jnp.exp2(_dot_nt(k[ks, idx * HEAD_DIM:(idx + 1) * HEAD_DIM], qs_ref[idx])).astype(BF16)
                    term = lax.dot_general(v1, p_t, (((0,), (0,)), ((), ())), preferred_element_type=F32)
                    upd[c] = term if upd[c] is None else upd[c] + term
            for c in range(2):
                acc_ref[2 * h + c] += upd[c]

    @pl.when(kt < nk_lat)
    def _():
        step(kl_ref[...], vl_ref[...])

    @pl.when(kt == nk_lat)
    def _():
        step(kx_ref[...], vx_ref[...])
        lam = _diff_lambda(lam_ref, lam_init)
        for h in range(D_HEADS):
            a1, a2 = acc_ref[2 * h], acc_ref[2 * h + 1]
            o_t = a1[:vw] * (1.0 / a1[vw:vw + 1]) - lam * (a2[:vw] * (1.0 / a2[vw:vw + 1]))
            o_ref[:, h * vw:(h + 1) * vw] = _finish_diff(jnp.transpose(o_t), subln_ref, lam_init).astype(BF16)


def _dense_diff(qk_lat, rest_lat, qk_ctx, rest_ctx, lam_d, subln, lam_init, tq, tk, bounded):
    bsz, seq, _ = qk_lat.shape
    ctx_len = qk_ctx.shape[1]
    nk_lat = seq // tk
    width = D_HEADS * 2 * HEAD_DIM
    kcol = QK_OFF["dk"] // width
    vcol = REST_OFF["dv"] // width
    if bounded:
        body = functools.partial(_dense_d_bounded_kernel, nk_lat=nk_lat, lam_init=lam_init)
        scratch = [pltpu.VMEM((2 * D_HEADS, tq, HEAD_DIM), BF16),
                   pltpu.VMEM((2 * D_HEADS, 2 * HEAD_DIM + VT_PAD, tq), F32)]
    else:
        body = functools.partial(_dense_d_kernel, nk_lat=nk_lat, lam_init=lam_init)
        scratch = [pltpu.VMEM((2 * D_HEADS, tq, HEAD_DIM), BF16),
                   pltpu.VMEM((2 * D_HEADS, tq, LANES), F32),
                   pltpu.VMEM((2 * D_HEADS, tq, 2 * HEAD_DIM), F32),
                   pltpu.VMEM((2 * D_HEADS, tq, 1), F32)]
    return pl.pallas_call(
        body,
        grid=(bsz, seq // tq, nk_lat + 1),
        in_specs=[pl.BlockSpec((None, tq, width), lambda b, i, kt: (b, i, QK_OFF["dq"] // width)),
                  pl.BlockSpec((None, tk, width), lambda b, i, kt: (b, jnp.minimum(kt, nk_lat - 1), kcol)),
                  pl.BlockSpec((None, tk, width), lambda b, i, kt: (b, jnp.minimum(kt, nk_lat - 1), vcol)),
                  pl.BlockSpec((None, ctx_len, width), lambda b, i, kt: (b, 0, kcol)),
                  pl.BlockSpec((None, ctx_len, width), lambda b, i, kt: (b, 0, vcol)),
                  pl.BlockSpec((4, HEAD_DIM), lambda b, i, kt: (0, 0)),
                  pl.BlockSpec((1, 2 * HEAD_DIM), lambda b, i, kt: (0, 0))],
        out_specs=pl.BlockSpec((None, tq, BRANCH_W), lambda b, i, kt: (b, i, 0)),
        out_shape=jax.ShapeDtypeStruct((bsz, seq, BRANCH_W), BF16),
        scratch_shapes=scratch,
        compiler_params=_cparams(("parallel", "parallel", "arbitrary")),
        name="mixer_d_diff_bounded" if bounded else "mixer_d_diff",
    )(qk_lat, qk_lat, rest_lat, qk_ctx, rest_ctx, lam_d, subln.reshape(1, 2 * HEAD_DIM))


def _ctx_kernel(sink_ref, qk_ref, v_ref, lam_ref, subln_ref, o_ref, *, lam_init):
    qk = qk_ref[...]
    vals = v_ref[...]
    ctx_len = qk.shape[0]
    v_off = {name: REST_OFF[name] - REST_OFF[_FIRST_VALUE] for name in ("av", "cv", "bv", "dv")}

    def cols(name, start, width):
        c0 = QK_OFF[name] + start
        return qk[:, c0:c0 + width]

    def gqa(qname, kname, vname, out_off, n_kv, with_sink):
        g = 8 // n_kv
        for kv in range(n_kv):
            qs = _stack_heads(cols(qname, 0, 8 * HEAD_DIM), kv * g, g)
            s = _dot_nt(qs, cols(kname, kv * HEAD_DIM, HEAD_DIM))
            extra = None
            if with_sink:
                extra = jnp.concatenate([jnp.full((ctx_len, 1), sink_ref[kv * g + h], F32) for h in range(g)], axis=0)
            p, l = _softmax_rows(s, extra)
            vv = vals[:, v_off[vname] + kv * HEAD_DIM:v_off[vname] + (kv + 1) * HEAD_DIM]
            o = _dot(p.astype(BF16), vv) / l
            for h in range(g):
                c0 = out_off + (kv * g + h) * HEAD_DIM
                o_ref[:, c0:c0 + HEAD_DIM] = o[h * ctx_len:(h + 1) * ctx_len].astype(BF16)

    gqa("aq", "ak", "av", 0 * BRANCH_W, A_KV, True)
    gqa("bq", "bk", "bv", 1 * BRANCH_W, B_HEADS, False)
    gqa("cq", "ck", "cv", 2 * BRANCH_W, C_KV, False)

    lam = _diff_lambda(lam_ref, lam_init)
    for h in range(D_HEADS):
        base = h * 2 * HEAD_DIM
        p1, l1 = _softmax_rows(_dot_nt(cols("dq", base, HEAD_DIM), cols("dk", base, HEAD_DIM)))
        p2, l2 = _softmax_rows(_dot_nt(cols("dq", base + HEAD_DIM, HEAD_DIM), cols("dk", base + HEAD_DIM, HEAD_DIM)))
        pd = p1 / l1 - lam * (p2 / l2)
        o = _dot(pd.astype(BF16), vals[:, v_off["dv"] + base:v_off["dv"] + base + 2 * HEAD_DIM])
        c0 = 3 * BRANCH_W + base
        o_ref[:, c0:c0 + 2 * HEAD_DIM] = _finish_diff(o, subln_ref, lam_init).astype(BF16)


def _ctx_attention(qk_ctx, v_ctx, sink2, lam_d, subln, lam_init):
    bsz, ctx_len, _ = qk_ctx.shape
    return pl.pallas_call(
        functools.partial(_ctx_kernel, lam_init=lam_init),
        grid=(bsz,),
        in_specs=[pl.BlockSpec(memory_space=pltpu.SMEM),
                  pl.BlockSpec((None, ctx_len, QK_COLS), lambda b: (b, 0, 0)),
                  pl.BlockSpec((None, ctx_len, v_ctx.shape[2]), lambda b: (b, 0, 0)),
                  pl.BlockSpec((4, HEAD_DIM), lambda b: (0, 0)),
                  pl.BlockSpec((1, 2 * HEAD_DIM), lambda b: (0, 0))],
        out_specs=pl.BlockSpec((None, ctx_len, N_BRANCH * BRANCH_W), lambda b: (b, 0, 0)),
        out_shape=jax.ShapeDtypeStruct((bsz, ctx_len, N_BRANCH * BRANCH_W), BF16),
        compiler_params=_cparams(("parallel",)),
        name="ctx_attention",
    )(sink2, qk_ctx, v_ctx, lam_d, subln.reshape(1, 2 * HEAD_DIM))


def _merge_kernel(x_ref, mod_ref, ya_ref, yb_ref, yc_ref, yd_ref, gp_ref, mg_ref, wbr_ref, wout_ref, o_ref):
    gate = mod_ref[...][:, 2 * D_MODEL:]
    merged = None
    for n, y_ref in enumerate((ya_ref, yb_ref, yc_ref, yd_ref)):
        yg = (y_ref[...].astype(F32) * gp_ref[:, n * BRANCH_W:(n + 1) * BRANCH_W].astype(F32)).astype(BF16)
        term = mg_ref[:, n * D_MODEL:(n + 1) * D_MODEL].astype(F32) * _dot(yg, wbr_ref[n])
        merged = term if merged is None else merged + term
    o_ref[...] = x_ref[...] + gate * _dot(merged.astype(BF16), wout_ref[...])


def _merge(x2, mod3, ys, y_cols, rest, w_br, w_out, tm, row_of_tile):
    rows = x2.shape[0]
    y_specs = [pl.BlockSpec((tm, BRANCH_W), (lambda i, c=c: (i, c))) for c in y_cols]
    return pl.pallas_call(
        _merge_kernel,
        grid=(rows // tm,),
        in_specs=[pl.BlockSpec((tm, D_MODEL), lambda i: (i, 0)),
                  pl.BlockSpec((None, 1, 3 * D_MODEL), lambda i: (row_of_tile(i), 0, 0))]
                 + y_specs
                 + [pl.BlockSpec((tm, N_BRANCH * BRANCH_W), lambda i: (i, REST_OFF["ag"] // (N_BRANCH * BRANCH_W))),
                    pl.BlockSpec((tm, N_BRANCH * D_MODEL), lambda i: (i, 0)),
                    pl.BlockSpec((N_BRANCH, BRANCH_W, D_MODEL), lambda i: (0, 0, 0)),
                    pl.BlockSpec((D_MODEL, D_MODEL), lambda i: (0, 0))],
        out_specs=pl.BlockSpec((tm, D_MODEL), lambda i: (i, 0)),
        out_shape=jax.ShapeDtypeStruct((rows, D_MODEL), F32),
        compiler_params=_cparams(("parallel",)),
        name="gated_merge",
    )(x2, mod3, *ys, rest, rest, w_br, w_out)


def _rope_tables(seq):
    t = jnp.arange(seq, dtype=jnp.int32)
    pos = jnp.stack([t // GRID_W, t % GRID_W], axis=-1).astype(F32)
    n_freq = HEAD_DIM // 4
    freqs = ROPE_THETA ** (-jnp.arange(n_freq, dtype=F32) / n_freq)
    ang = pos[:, :, None] * freqs[None, None, :]
    ang = jnp.concatenate([ang, ang], axis=-1).reshape(seq, HEAD_DIM)
    sign = np.where((np.arange(HEAD_DIM) % 32) < 16, -1.0, 1.0).astype(np.float32)
    reps = NORM_GROUP // HEAD_DIM
    return jnp.tile(jnp.cos(ang), (1, reps)), jnp.tile(jnp.sin(ang) * sign, (1, reps))


def _regroup_cols(w, order, total):
    parts = [w[:, _ORIG[n][0]:_ORIG[n][1]] for n in order]
    used = sum(p.shape[1] for p in parts)
    if total > used:
        parts.append(jnp.zeros((w.shape[0], total - used), w.dtype))
    return jnp.concatenate(parts, axis=1)


def _qk_gain_row(g):
    parts = []
    for name in _QK_ORDER:
        mixer = "abcd".index(name[0])
        is_q = name[1] == "q"
        width = _ORIG[name][1] - _ORIG[name][0]
        gain = g[mixer, 0] * (QK_SCALE * LOG2E) if is_q else g[mixer, 1]
        parts.append(jnp.tile(gain, width // HEAD_DIM))
    parts.append(jnp.ones((QK_COLS - _QK_USED,), g.dtype))
    return jnp.concatenate(parts).reshape(1, QK_COLS).astype(F32)


def kernel(x, c, ctx, c_ctx, norm_w, w_ada, b_ada, w_in, qk_gain, sink_a, rpb_b, lam_d, subln_d, w_br, w_out):
    bsz, seq, _ = x.shape
    ctx_len = ctx.shape[1]
    depth = w_ada.shape[0]
    assert seq % (2 * A_WINDOW) == 0 and ctx_len % LANES == 0 and bsz <= 6

    tm_lat = 2048
    tiles_per_batch = seq // tm_lat
    tk_dense = 2048
    tq_bounded, tq_rowmax = 512, 256
    assert seq % tk_dense == 0
    tm_merge = 512
    merge_tiles_per_batch = seq // tm_merge

    cvec = jnp.concatenate([c, c_ctx[None, :], jnp.zeros((8 - bsz - 1, D_MODEL), F32)], axis=0)
    mod_all = _ada(cvec, w_ada, b_ada)
    cos, sin = _rope_tables(seq)
    ones_bd = jnp.asarray(np.kron(np.eye(NORM_GROUP // HEAD_DIM), np.ones((HEAD_DIM, HEAD_DIM))), BF16)

    x2 = x.reshape(bsz * seq, D_MODEL)
    c2 = ctx.reshape(bsz * ctx_len, D_MODEL)
    for l in range(depth):
        need_ctx = l < depth - 1
        lam_init = 0.8 - 0.6 * math.exp(-0.3 * l)
        mod3 = mod_all[l].reshape(8, 1, 3 * D_MODEL)
        w_qk = _regroup_cols(w_in[l], _QK_ORDER, QK_COLS).astype(BF16)
        w_rest = _regroup_cols(w_in[l], _REST_ORDER, REST_COLS).astype(BF16)
        gain_row = _qk_gain_row(qk_gain[l])

        hx = _prenorm(x2, norm_w[l], mod3, tm_lat, lambda i: i // tiles_per_batch)
        hc = _prenorm(c2, norm_w[l], mod3, bsz * ctx_len, lambda i: bsz)
        qk_lat = _proj_qk(hx, w_qk, gain_row, ones_bd, tm_lat, cos, sin, tiles_per_batch).reshape(bsz, seq, QK_COLS)
        qk_ctx = _proj_qk(hc, w_qk, gain_row, ones_bd, bsz * ctx_len).reshape(bsz, ctx_len, QK_COLS)
        rest_lat = _proj_rest(hx, w_rest, tm_lat).reshape(bsz, seq, REST_COLS)
        rest_ctx = _proj_rest(hc, w_rest, bsz * ctx_len).reshape(bsz, ctx_len, REST_COLS)

        sink2 = sink_a[l] * LOG2E
        rpb2 = rpb_b[l] * LOG2E
        tq_nbr = 2 * A_WINDOW
        proj = (qk_lat, rest_lat, qk_ctx, rest_ctx)

        def qk_bound(g):
            return HEAD_DIM * QK_SCALE * LOG2E * jnp.max(jnp.abs(g[0])) * jnp.max(jnp.abs(g[1])) * 1.02

        y_a = lax.cond(qk_bound(qk_gain[l, 0]) <= LOGIT_BOUND,
                       lambda: _window_attention(*proj, sink2, True),
                       lambda: _window_attention(*proj, sink2, False))
        y_b = lax.cond(qk_bound(qk_gain[l, 1]) + jnp.max(jnp.abs(rpb_b[l])) * LOG2E <= LOGIT_BOUND,
                       lambda: _neighborhood_attention(*proj, rpb2, tq_nbr, True),
                       lambda: _neighborhood_attention(*proj, rpb2, tq_nbr, False))
        y_c = lax.cond(qk_bound(qk_gain[l, 2]) <= LOGIT_BOUND,
                       lambda: _dense_gqa(*proj, tq_bounded, tk_dense, True),
                       lambda: _dense_gqa(*proj, tq_rowmax, tk_dense, False))
        y_d = lax.cond(qk_bound(qk_gain[l, 3]) <= LOGIT_BOUND,
                       lambda: _dense_diff(*proj, lam_d[l], subln_d[l], lam_init, tq_bounded, tk_dense, True),
                       lambda: _dense_diff(*proj, lam_d[l], subln_d[l], lam_init, tq_rowmax, tk_dense, False))

        w_br_l = w_br[l].astype(BF16)
        w_out_l = w_out[l].astype(BF16)
        ys = [y.reshape(bsz * seq, BRANCH_W) for y in (y_a, y_b, y_c, y_d)]
        x_new = _merge(x2, mod3, ys, (0, 0, 0, 0), rest_lat.reshape(bsz * seq, REST_COLS), w_br_l, w_out_l,
                       tm_merge, lambda i: i // merge_tiles_per_batch)
        if need_ctx:
            y_ctx = _ctx_attention(qk_ctx, rest_ctx[:, :, REST_OFF[_FIRST_VALUE]:], sink2, lam_d[l], subln_d[l],
                                   lam_init)
            y_ctx2 = y_ctx.reshape(bsz * ctx_len, N_BRANCH * BRANCH_W)
            c2 = _merge(c2, mod3, [y_ctx2] * 4, (0, 1, 2, 3), rest_ctx.reshape(bsz * ctx_len, REST_COLS),
                        w_br_l, w_out_l, bsz * ctx_len, lambda i: bsz)
        x2 = x_new
    return x2.reshape(bsz, seq, D_MODEL)
```

```python
import functools
import math

import numpy as np
import jax
import jax.numpy as jnp
from jax import lax
from jax.experimental import pallas as pl
from jax.experimental.pallas import tpu as pltpu

F32 = jnp.float32
BF16 = jnp.bfloat16

D_MODEL = 1024
GRID_W = 64
HEAD_DIM = 64
BRANCH_W = 512
N_BRANCH = 4
A_HEADS, A_KV, A_WINDOW = 8, 2, 128
B_HEADS, NB_ROWS, NB_COLS = 8, 8, 16
C_HEADS, C_KV = 8, 2
D_HEADS = 4
ROPE_THETA = 10000.0
EPS = 1e-6
NEG_INF = -1e30
QK_SCALE = HEAD_DIM ** -0.5
LOG2E = math.log2(math.e)
LOGIT_BOUND = 60.0

V7X_VMEM_LIMIT_BYTES = 56 * 1024 * 1024
LANES = 128

_ORIG = dict(aq=(0, 512), ak=(512, 640), av=(640, 768), ag=(768, 1280),
             bq=(1280, 1792), bk=(1792, 2304), bv=(2304, 2816), bg=(2816, 3328),
             cq=(3328, 3840), ck=(3840, 3968), cv=(3968, 4096), cg=(4096, 4608),
             dq=(4608, 5120), dk=(5120, 5632), dv=(5632, 6144), dg=(6144, 6656),
             mg=(6656, 10752))
_QK_ORDER = ("aq", "cq", "dq", "bq", "bk", "dk", "ak", "ck")
_REST_ORDER = ("mg", "ag", "bg", "cg", "dg", "dv", "bv", "av", "cv")
_FIRST_VALUE = "dv"


def _offsets(order):
    off, out = 0, {}
    for name in order:
        lo, hi = _ORIG[name]
        out[name] = off
        off += hi - lo
    return out, off


PROJ_TN = 512
PROJ_ROW_CHUNK = 512
REST_ROW_CHUNK = 1024
VT_PAD = 16
DENSE_KEY_CHUNK = 512
NORM_GROUP = 256
QK_OFF, _QK_USED = _offsets(_QK_ORDER)
REST_OFF, _REST_USED = _offsets(_REST_ORDER)
QK_COLS = -(-_QK_USED // PROJ_TN) * PROJ_TN
REST_COLS = -(-_REST_USED // PROJ_TN) * PROJ_TN
_ROPE_LO_END = QK_OFF["bq"] // PROJ_TN
_ROPE_HI_START = QK_OFF["dk"] // PROJ_TN
_MG_BLOCKS = (REST_OFF["ag"]) // PROJ_TN
_GP_BLOCKS_END = REST_OFF[_FIRST_VALUE] // PROJ_TN
assert QK_OFF["bq"] % PROJ_TN == 0 and QK_OFF["dk"] % PROJ_TN == 0
assert REST_OFF["ag"] % PROJ_TN == 0 and REST_OFF[_FIRST_VALUE] % PROJ_TN == 0


def _cparams(sem):
    return pltpu.CompilerParams(dimension_semantics=sem, vmem_limit_bytes=V7X_VMEM_LIMIT_BYTES)


def _sigmoid(x):
    return 0.5 * jnp.tanh(0.5 * x) + 0.5


def _dot_nt(a, b):
    return lax.dot_general(a, b, (((1,), (1,)), ((), ())), preferred_element_type=F32)


def _dot(a, b):
    return jnp.dot(a, b, preferred_element_type=F32)


def _ada_kernel(c_ref, w_ref, b_ref, o_ref):
    c = c_ref[...]
    o_ref[...] = _dot(c * _sigmoid(c), w_ref[...]) + b_ref[...]


def _ada(cvec, w_ada, b_ada):
    depth = w_ada.shape[0]
    tn = 512
    return pl.pallas_call(
        _ada_kernel,
        grid=(depth, 3 * D_MODEL // tn),
        in_specs=[pl.BlockSpec((8, D_MODEL), lambda l, j: (0, 0)),
                  pl.BlockSpec((None, D_MODEL, tn), lambda l, j: (l, 0, j)),
                  pl.BlockSpec((None, 1, tn), lambda l, j: (l, 0, j))],
        out_specs=pl.BlockSpec((None, 8, tn), lambda l, j: (l, 0, j)),
        out_shape=jax.ShapeDtypeStruct((depth, 8, 3 * D_MODEL), F32),
        compiler_params=_cparams(("parallel", "parallel")),
        name="ada_mod",
    )(cvec, w_ada, b_ada.reshape(depth, 1, 3 * D_MODEL))


def _prenorm_kernel(x_ref, nw_ref, mod_ref, o_ref):
    x = x_ref[...]
    mod = mod_ref[...]
    shift, scale = mod[:, :D_MODEL], mod[:, D_MODEL:2 * D_MODEL]
    y = x * lax.rsqrt(jnp.mean(x * x, axis=-1, keepdims=True) + EPS) * nw_ref[...]
    o_ref[...] = (y * (1.0 + scale) + shift).astype(BF16)


def _prenorm(x2, norm_w, mod3, tm, row_of_tile):
    rows = x2.shape[0]
    return pl.pallas_call(
        _prenorm_kernel,
        grid=(rows // tm,),
        in_specs=[pl.BlockSpec((tm, D_MODEL), lambda i: (i, 0)),
                  pl.BlockSpec((1, D_MODEL), lambda i: (0, 0)),
                  pl.BlockSpec((None, 1, 3 * D_MODEL), lambda i: (row_of_tile(i), 0, 0))],
        out_specs=pl.BlockSpec((tm, D_MODEL), lambda i: (i, 0)),
        out_shape=jax.ShapeDtypeStruct((rows, D_MODEL), BF16),
        compiler_params=_cparams(("parallel",)),
        name="prenorm",
    )(x2, norm_w.reshape(1, D_MODEL), mod3)


def _rot_half_unsigned(n):
    lane = lax.broadcasted_iota(jnp.int32, n.shape, 1)
    return jnp.where((lane & 31) < 16, pltpu.roll(n, LANES - 16, 1), pltpu.roll(n, 16, 1))


def _proj_qk_kernel(*refs, rope):
    if rope:
        hx_ref, w_ref, gain_ref, ones_ref, cos_ref, sin_ref, o_ref = refs
    else:
        hx_ref, w_ref, gain_ref, ones_ref, o_ref = refs
    w = w_ref[...]
    ones_bd = ones_ref[...]
    rows = hx_ref.shape[0]
    chunk = min(rows, PROJ_ROW_CHUNK)
    if rope:
        j = pl.program_id(1)
        use = jnp.logical_or(j < _ROPE_LO_END, j >= _ROPE_HI_START)
    for r0 in range(0, rows, chunk):
        rs = slice(r0, r0 + chunk)
        acc_all = _dot(hx_ref[rs, :], w)
        if rope:
            cos = jnp.where(use, cos_ref[rs, :], 1.0)
            sin = jnp.where(use, sin_ref[rs, :], 0.0)
        for c in range(PROJ_TN // NORM_GROUP):
            sl = slice(c * NORM_GROUP, (c + 1) * NORM_GROUP)
            acc = acc_all[:, sl]
            ss = _dot((acc * acc).astype(BF16), ones_bd)
            n = acc * lax.rsqrt(ss * (1.0 / HEAD_DIM) + EPS) * gain_ref[:, sl]
            if rope:
                rot = jnp.concatenate([_rot_half_unsigned(n[:, :LANES]), _rot_half_unsigned(n[:, LANES:])], axis=1)
                n = n * cos + rot * sin
            o_ref[rs, sl] = n.astype(BF16)


def _proj_qk(hx, w_qk, gain_row, ones_bd, tm, cos=None, sin=None, tiles_per_batch=None):
    rows = hx.shape[0]
    rope = cos is not None
    in_specs = [pl.BlockSpec((tm, D_MODEL), lambda i, j: (i, 0)),
                pl.BlockSpec((D_MODEL, PROJ_TN), lambda i, j: (0, j)),
                pl.BlockSpec((1, PROJ_TN), lambda i, j: (0, j)),
                pl.BlockSpec((NORM_GROUP, NORM_GROUP), lambda i, j: (0, 0))]
    args = [hx, w_qk, gain_row, ones_bd]
    if rope:
        in_specs += [pl.BlockSpec((tm, NORM_GROUP), lambda i, j: (i % tiles_per_batch, 0))] * 2
        args += [cos, sin]
    return pl.pallas_call(
        functools.partial(_proj_qk_kernel, rope=rope),
        grid=(rows // tm, QK_COLS // PROJ_TN),
        in_specs=in_specs,
        out_specs=pl.BlockSpec((tm, PROJ_TN), lambda i, j: (i, j)),
        out_shape=jax.ShapeDtypeStruct((rows, QK_COLS), BF16),
        compiler_params=_cparams(("parallel", "arbitrary")),
        name="proj_qk_rope" if rope else "proj_qk",
    )(*args)


def _proj_rest_kernel(hx_ref, w_ref, o_ref):
    j = pl.program_id(1)
    w = w_ref[...]
    rows = hx_ref.shape[0]
    chunk = min(rows, REST_ROW_CHUNK)
    for r0 in range(0, rows, chunk):
        acc = _dot(hx_ref[r0:r0 + chunk, :], w)
        sg = _sigmoid(acc)
        o_ref[r0:r0 + chunk, :] = jnp.where(j < _MG_BLOCKS, sg,
                                            jnp.where(j < _GP_BLOCKS_END, acc * sg, acc)).astype(BF16)


def _proj_rest(hx, w_rest, tm):
    rows = hx.shape[0]
    return pl.pallas_call(
        _proj_rest_kernel,
        grid=(rows // tm, REST_COLS // PROJ_TN),
        in_specs=[pl.BlockSpec((tm, D_MODEL), lambda i, j: (i, 0)),
                  pl.BlockSpec((D_MODEL, PROJ_TN), lambda i, j: (0, j))],
        out_specs=pl.BlockSpec((tm, PROJ_TN), lambda i, j: (i, j)),
        out_shape=jax.ShapeDtypeStruct((rows, REST_COLS), BF16),
        compiler_params=_cparams(("parallel", "arbitrary")),
        name="proj_rest",
    )(hx, w_rest)


def _stack_heads(q, first_head, n):
    return jnp.concatenate([q[:, (first_head + g) * HEAD_DIM:(first_head + g + 1) * HEAD_DIM] for g in range(n)],
                           axis=0)


def _softmax_rows(s, extra=None):
    m = jnp.max(s, axis=-1, keepdims=True)
    if extra is not None:
        m = jnp.maximum(m, extra)
    p = jnp.exp2(s - m)
    l = jnp.sum(p, axis=-1, keepdims=True)
    if extra is not None:
        l = l + jnp.exp2(extra - m)
    return p, l


def _with_ones_column(v, pad=HEAD_DIM):
    lane = lax.broadcasted_iota(jnp.int32, (v.shape[0], pad), 1)
    return jnp.concatenate([v, jnp.where(lane == 0, 1.0, 0.0).astype(v.dtype)], axis=1)


def _lane_partial_sums(p):
    part = p[:, :LANES]
    for c in range(1, p.shape[1] // LANES):
        part = part + p[:, c * LANES:(c + 1) * LANES]
    return part


def _diff_lambda(lam_ref, lam_init):
    lf = lam_ref[...]
    a = jnp.sum(lf[0:1] * lf[1:2], axis=-1, keepdims=True)
    b = jnp.sum(lf[2:3] * lf[3:4], axis=-1, keepdims=True)
    return jnp.exp(a) - jnp.exp(b) + lam_init


def _finish_diff(o, subln_ref, lam_init):
    y = o * lax.rsqrt(jnp.mean(o * o, axis=-1, keepdims=True) + EPS) * subln_ref[...]
    return y * (1.0 - lam_init)


def _win_kernel(sink_ref, q_ref, kp_ref, kc_ref, kn_ref, kx_ref, vp_ref, vc_ref, vn_ref, vx_ref, o_ref, *,
                tq, seq):
    i = pl.program_id(1)
    g = A_HEADS // A_KV
    q = q_ref[...]
    k_all = jnp.concatenate([kp_ref[...], kc_ref[...], kn_ref[...], kx_ref[...]], axis=0)
    v_all = jnp.concatenate([vp_ref[...], vc_ref[...], vn_ref[...], vx_ref[...]], axis=0)
    nw = 2 * tq
    t = lax.broadcasted_iota(jnp.int32, (g * tq, nw), 0) & (tq - 1)
    j = lax.broadcasted_iota(jnp.int32, (g * tq, nw), 1)
    rel = j - tq // 2 - t
    kpos = i * tq - tq // 2 + j
    bad = jnp.where(jnp.abs(rel) > A_WINDOW, 1, 0) + jnp.where(kpos < 0, 1, 0) + jnp.where(kpos >= seq, 1, 0)
    for kv in range(A_KV):
        qs = _stack_heads(q, kv * g, g)
        kk = k_all[:, kv * HEAD_DIM:(kv + 1) * HEAD_DIM]
        vv = v_all[:, kv * HEAD_DIM:(kv + 1) * HEAD_DIM]
        s = _dot_nt(qs, kk)
        s_win = jnp.where(bad > 0, NEG_INF, s[:, :nw])
        s_ctx = s[:, nw:]
        sink = jnp.concatenate([jnp.full((tq, 1), sink_ref[kv * g + h], F32) for h in range(g)], axis=0)
        m = jnp.maximum(jnp.maximum(jnp.max(s_win, axis=-1, keepdims=True),
                                    jnp.max(s_ctx, axis=-1, keepdims=True)), sink)
        pw = jnp.exp2(s_win - m)
        pc = jnp.exp2(s_ctx - m)
        l = jnp.sum(pw, axis=-1, keepdims=True) + jnp.sum(pc, axis=-1, keepdims=True) + jnp.exp2(sink - m)
        o = (_dot(pw.astype(BF16), vv[:nw]) + _dot(pc.astype(BF16), vv[nw:])) / l
        for h in range(g):
            c0 = (kv * g + h) * HEAD_DIM
            o_ref[:, c0:c0 + HEAD_DIM] = o[h * tq:(h + 1) * tq].astype(BF16)


def _win_bounded_kernel(sink_ref, q_ref, kp_ref, kc_ref, kn_ref, kx_ref, vp_ref, vc_ref, vn_ref, vx_ref, mask_ref,
                        o_ref, *, tq):
    g = A_HEADS // A_KV
    q = q_ref[...]
    mask_t = mask_ref[...]
    sink = sink_ref[...]
    k_all = jnp.concatenate([kp_ref[...], kc_ref[...], kn_ref[...], kx_ref[...]], axis=0)
    v_all = jnp.concatenate([vp_ref[...], vc_ref[...], vn_ref[...], vx_ref[...]], axis=0)
    for kv in range(A_KV):
        sl = slice(kv * HEAD_DIM, (kv + 1) * HEAD_DIM)
        qs = _stack_heads(q, kv * g, g)
        stab = [jnp.maximum(sink[:, kv * g + h:kv * g + h + 1], 0.0) for h in range(g)]
        shift_t = jnp.concatenate([mask_t - stab[h] for h in range(g)], axis=1)
        p_t = jnp.exp2(_dot_nt(k_all[:, sl], qs) + shift_t).astype(BF16)
        acc = lax.dot_general(_with_ones_column(v_all[:, sl], VT_PAD), p_t, (((0,), (0,)), ((), ())),
                              preferred_element_type=F32)
        for h in range(g):
            a = acc[:, h * tq:(h + 1) * tq]
            l = a[HEAD_DIM:HEAD_DIM + 1] + jnp.exp2(sink[:, kv * g + h:kv * g + h + 1] - stab[h])
            c0 = (kv * g + h) * HEAD_DIM
            o_ref[:, c0:c0 + HEAD_DIM] = jnp.transpose(a[:HEAD_DIM] * (1.0 / l)).astype(BF16)


def _window_mask_tables(seq, tq, ctx_len):
    nt = seq // tq
    tabs = []
    for i0 in (0, 1, nt - 1):
        t = np.arange(tq)[None, :]
        j = np.arange(2 * tq)[:, None]
        kpos = i0 * tq - tq // 2 + j
        ok = (np.abs(j - tq // 2 - t) <= A_WINDOW) & (kpos >= 0) & (kpos < seq)
        win = np.where(ok, 0.0, NEG_INF).astype(np.float32)
        tabs.append(np.concatenate([win, np.zeros((ctx_len, tq), np.float32)], axis=0))
    return jnp.asarray(np.stack(tabs, axis=0))


def _window_attention(qk_lat, rest_lat, qk_ctx, rest_ctx, sink2, bounded):
    bsz, seq, _ = qk_lat.shape
    ctx_len = qk_ctx.shape[1]
    tq = 2 * A_WINDOW
    nt = seq // tq
    half = tq // 2
    kcol = QK_OFF["ak"] // LANES
    vcol = REST_OFF["av"] // LANES
    n_half = seq // half

    def kv_specs(col):
        return [pl.BlockSpec((None, half, LANES), lambda b, i: (b, jnp.maximum(2 * i - 1, 0), col)),
                pl.BlockSpec((None, tq, LANES), lambda b, i: (b, i, col)),
                pl.BlockSpec((None, half, LANES), lambda b, i: (b, jnp.minimum(2 * i + 2, n_half - 1), col)),
                pl.BlockSpec((None, ctx_len, LANES), lambda b, i: (b, 0, col))]

    def tile_variant(i):
        return jnp.where(i == 0, 0, jnp.where(i == nt - 1, 2, 1))

    q_spec = pl.BlockSpec((None, tq, A_HEADS * HEAD_DIM), lambda b, i: (b, i, QK_OFF["aq"] // 512))
    kv_args = (qk_lat, qk_lat, qk_lat, qk_ctx, rest_lat, rest_lat, rest_lat, rest_ctx)
    if bounded:
        body = functools.partial(_win_bounded_kernel, tq=tq)
        in_specs = ([pl.BlockSpec((1, A_HEADS), lambda b, i: (0, 0)), q_spec] + kv_specs(kcol) + kv_specs(vcol)
                    + [pl.BlockSpec((None, 2 * tq + ctx_len, tq), lambda b, i: (tile_variant(i), 0, 0))])
        args = (sink2.reshape(1, A_HEADS), qk_lat) + kv_args + (_window_mask_tables(seq, tq, ctx_len),)
    else:
        body = functools.partial(_win_kernel, tq=tq, seq=seq)
        in_specs = [pl.BlockSpec(memory_space=pltpu.SMEM), q_spec] + kv_specs(kcol) + kv_specs(vcol)
        args = (sink2, qk_lat) + kv_args
    return pl.pallas_call(
        body,
        grid=(bsz, nt),
        in_specs=in_specs,
        out_specs=pl.BlockSpec((None, tq, BRANCH_W), lambda b, i: (b, i, 0)),
        out_shape=jax.ShapeDtypeStruct((bsz, seq, BRANCH_W), BF16),
        compiler_params=_cparams(("parallel", "parallel")),
        name="mixer_a_window_bounded" if bounded else "mixer_a_window",
    )(*args)


def _nbr_kernel(q_ref, kp_ref, kc_ref, kn_ref, kx_ref, vp_ref, vc_ref, vn_ref, vx_ref, bias_ref, rmask_ref, o_ref,
                *, tq, bounded):
    q = q_ref[...]
    k_all = jnp.concatenate([kp_ref[...], kc_ref[...], kn_ref[...], kx_ref[...]], axis=0)
    v_all = jnp.concatenate([vp_ref[...], vc_ref[...], vn_ref[...], vx_ref[...]], axis=0)
    rmask = rmask_ref[...]
    nw = 3 * tq
    for h in range(B_HEADS):
        sl = slice(h * HEAD_DIM, (h + 1) * HEAD_DIM)
        s = _dot_nt(q[:, sl], k_all[:, sl])
        s = jnp.concatenate([s[:, :nw] + (bias_ref[h] + rmask), s[:, nw:]], axis=1)
        if not bounded:
            s = s - jnp.max(s, axis=-1, keepdims=True)
        acc = _dot(jnp.exp2(s).astype(BF16), _with_ones_column(v_all[:, sl]))
        o_ref[:, sl] = (acc[:, :HEAD_DIM] * (1.0 / acc[:, HEAD_DIM:HEAD_DIM + 1])).astype(BF16)


def _nbr_bias_tables(rpb, seq, tq):
    rows = seq // GRID_W
    nt = seq // tq
    kr = min(NB_ROWS, rows)
    qr = tq // GRID_W
    n_heads = rpb.shape[0]
    assert NB_ROWS - 1 - qr - (qr - 1) >= 0 and NB_ROWS - 1 - qr + 3 * qr <= 2 * NB_ROWS - 1
    pad = GRID_W - 1
    rpb_pad = jnp.pad(rpb.astype(F32), ((0, 0), (0, 0), (pad, pad)))
    toep = jnp.stack([rpb_pad[:, :, pad + NB_COLS - 1 - c:pad + NB_COLS - 1 - c + GRID_W] for c in range(GRID_W)],
                     axis=2)
    c = np.arange(GRID_W)
    cstart = np.clip(c - NB_COLS // 2, 0, GRID_W - NB_COLS)
    col_ok = (c[None, :] >= cstart[:, None]) & (c[None, :] < cstart[:, None] + NB_COLS)
    toep = jnp.where(col_ok[None, None], toep, NEG_INF)
    a0 = NB_ROWS - 1 - qr
    per_rl = [jnp.transpose(toep[:, a0 - rl:a0 - rl + 3 * qr], (0, 2, 1, 3)) for rl in range(qr)]
    base = jnp.stack(per_rl, axis=1).reshape(n_heads, tq, 3 * tq)
    row_masks = []
    for i0 in (0, 1, nt - 1):
        r = i0 * qr + np.arange(qr)
        r2 = (i0 - 1) * qr + np.arange(3 * qr)
        rstart = np.clip(r - kr // 2, 0, rows - kr)
        row_ok = ((r2[None, :] >= rstart[:, None]) & (r2[None, :] < rstart[:, None] + kr)
                  & (r2[None, :] >= 0) & (r2[None, :] < rows))
        full = np.broadcast_to(row_ok[:, None, :, None], (qr, GRID_W, 3 * qr, GRID_W)).reshape(tq, 3 * tq)
        row_masks.append(np.where(full, 0.0, NEG_INF).astype(np.float32))
    return base, jnp.asarray(np.stack(row_masks, axis=0))


def _neighborhood_attention(qk_lat, rest_lat, qk_ctx, rest_ctx, rpb2, tq, bounded):
    bsz, seq, _ = qk_lat.shape
    ctx_len = qk_ctx.shape[1]
    nt = seq // tq
    width = B_HEADS * HEAD_DIM
    bias, row_masks = _nbr_bias_tables(rpb2, seq, tq)

    def kv_specs(col):
        return [pl.BlockSpec((None, tq, width), lambda b, i: (b, jnp.maximum(i - 1, 0), col)),
                pl.BlockSpec((None, tq, width), lambda b, i: (b, i, col)),
                pl.BlockSpec((None, tq, width), lambda b, i: (b, jnp.minimum(i + 1, nt - 1), col)),
                pl.BlockSpec((None, ctx_len, width), lambda b, i: (b, 0, col))]

    def tile_variant(i):
        return jnp.where(i == 0, 0, jnp.where(i == nt - 1, 2, 1))

    return pl.pallas_call(
        functools.partial(_nbr_kernel, tq=tq, bounded=bounded),
        grid=(bsz, nt),
        in_specs=[pl.BlockSpec((None, tq, width), lambda b, i: (b, i, QK_OFF["bq"] // width))]
                 + kv_specs(QK_OFF["bk"] // width) + kv_specs(REST_OFF["bv"] // width)
                 + [pl.BlockSpec(bias.shape, lambda b, i: (0, 0, 0)),
                    pl.BlockSpec((None,) + row_masks.shape[1:], lambda b, i: (tile_variant(i), 0, 0))],
        out_specs=pl.BlockSpec((None, tq, BRANCH_W), lambda b, i: (b, i, 0)),
        out_shape=jax.ShapeDtypeStruct((bsz, seq, BRANCH_W), BF16),
        compiler_params=_cparams(("parallel", "parallel")),
        name="mixer_b_neighbourhood_bounded" if bounded else "mixer_b_neighbourhood",
    )(qk_lat, qk_lat, qk_lat, qk_lat, qk_ctx, rest_lat, rest_lat, rest_lat, rest_ctx, bias, row_masks)


def _flash_update(s, v, m_ref, l_ref, acc_ref, idx):
    m_prev = m_ref[idx]
    m_new = jnp.maximum(m_prev, jnp.max(s, axis=-1, keepdims=True))
    alpha = jnp.exp2(m_prev - m_new)
    p = jnp.exp2(s - m_new)
    if l_ref is not None:
        l_ref[idx] = alpha * l_ref[idx] + _lane_partial_sums(p)
    acc_ref[idx] = alpha * acc_ref[idx] + _dot(p.astype(BF16), v)
    m_ref[idx] = m_new


def _dense_c_kernel(q_ref, kl_ref, vl_ref, kx_ref, vx_ref, o_ref, qs_ref, acc_ref, m_ref, *, tq, nk_lat):
    kt = pl.program_id(2)
    g = C_HEADS // C_KV

    @pl.when(kt == 0)
    def _():
        q = q_ref[...]
        for kv in range(C_KV):
            qs_ref[kv] = _stack_heads(q, kv * g, g)
        acc_ref[...] = jnp.zeros(acc_ref.shape, F32)
        m_ref[...] = jnp.full(m_ref.shape, -jnp.inf, F32)

    def step(k, v):
        for kv in range(C_KV):
            sl = slice(kv * HEAD_DIM, (kv + 1) * HEAD_DIM)
            _flash_update(_dot_nt(qs_ref[kv], k[:, sl]), _with_ones_column(v[:, sl]), m_ref, None, acc_ref, kv)

    @pl.when(kt < nk_lat)
    def _():
        step(kl_ref[...], vl_ref[...])

    @pl.when(kt == nk_lat)
    def _():
        step(kx_ref[...], vx_ref[...])
        for kv in range(C_KV):
            acc = acc_ref[kv]
            o = acc[:, :HEAD_DIM] * (1.0 / acc[:, HEAD_DIM:HEAD_DIM + 1])
            for h in range(g):
                c0 = (kv * g + h) * HEAD_DIM
                o_ref[:, c0:c0 + HEAD_DIM] = o[h * tq:(h + 1) * tq].astype(BF16)


def _dense_c_bounded_kernel(q_ref, kl_ref, vl_ref, kx_ref, vx_ref, o_ref, qs_ref, acc_ref, *, tq, nk_lat):
    kt = pl.program_id(2)
    g = C_HEADS // C_KV

    @pl.when(kt == 0)
    def _():
        q = q_ref[...]
        for kv in range(C_KV):
            qs_ref[kv] = _stack_heads(q, kv * g, g)
        acc_ref[...] = jnp.zeros(acc_ref.shape, F32)

    def step(k, v):
        n = k.shape[0]
        chunk = min(n, DENSE_KEY_CHUNK)
        for kv in range(C_KV):
            sl = slice(kv * HEAD_DIM, (kv + 1) * HEAD_DIM)
            upd = None
            for k0 in range(0, n, chunk):
                ks = slice(k0, k0 + chunk)
                p_t = jnp.exp2(_dot_nt(k[ks, sl], qs_ref[kv])).astype(BF16)
                term = lax.dot_general(_with_ones_column(v[ks, sl], VT_PAD), p_t, (((0,), (0,)), ((), ())),
                                       preferred_element_type=F32)
                upd = term if upd is None else upd + term
            acc_ref[kv] += upd

    @pl.when(kt < nk_lat)
    def _():
        step(kl_ref[...], vl_ref[...])

    @pl.when(kt == nk_lat)
    def _():
        step(kx_ref[...], vx_ref[...])
        for kv in range(C_KV):
            acc = acc_ref[kv]
            o = jnp.transpose(acc[:HEAD_DIM] * (1.0 / acc[HEAD_DIM:HEAD_DIM + 1]))
            for h in range(g):
                c0 = (kv * g + h) * HEAD_DIM
                o_ref[:, c0:c0 + HEAD_DIM] = o[h * tq:(h + 1) * tq].astype(BF16)


def _dense_gqa(qk_lat, rest_lat, qk_ctx, rest_ctx, tq, tk, bounded):
    bsz, seq, _ = qk_lat.shape
    ctx_len = qk_ctx.shape[1]
    nk_lat = seq // tk
    g = C_HEADS // C_KV
    kcol = QK_OFF["ck"] // LANES
    vcol = REST_OFF["cv"] // LANES
    if bounded:
        body = functools.partial(_dense_c_bounded_kernel, tq=tq, nk_lat=nk_lat)
        scratch = [pltpu.VMEM((C_KV, g * tq, HEAD_DIM), BF16),
                   pltpu.VMEM((C_KV, HEAD_DIM + VT_PAD, g * tq), F32)]
    else:
        body = functools.partial(_dense_c_kernel, tq=tq, nk_lat=nk_lat)
        scratch = [pltpu.VMEM((C_KV, g * tq, HEAD_DIM), BF16),
                   pltpu.VMEM((C_KV, g * tq, LANES), F32),
                   pltpu.VMEM((C_KV, g * tq, 1), F32)]
    return pl.pallas_call(
        body,
        grid=(bsz, seq // tq, nk_lat + 1),
        in_specs=[pl.BlockSpec((None, tq, C_HEADS * HEAD_DIM), lambda b, i, kt: (b, i, QK_OFF["cq"] // 512)),
                  pl.BlockSpec((None, tk, LANES), lambda b, i, kt: (b, jnp.minimum(kt, nk_lat - 1), kcol)),
                  pl.BlockSpec((None, tk, LANES), lambda b, i, kt: (b, jnp.minimum(kt, nk_lat - 1), vcol)),
                  pl.BlockSpec((None, ctx_len, LANES), lambda b, i, kt: (b, 0, kcol)),
                  pl.BlockSpec((None, ctx_len, LANES), lambda b, i, kt: (b, 0, vcol))],
        out_specs=pl.BlockSpec((None, tq, BRANCH_W), lambda b, i, kt: (b, i, 0)),
        out_shape=jax.ShapeDtypeStruct((bsz, seq, BRANCH_W), BF16),
        scratch_shapes=scratch,
        compiler_params=_cparams(("parallel", "parallel", "arbitrary")),
        name="mixer_c_dense_bounded" if bounded else "mixer_c_dense",
    )(qk_lat, qk_lat, rest_lat, qk_ctx, rest_ctx)


def _dense_d_kernel(q_ref, kl_ref, vl_ref, kx_ref, vx_ref, lam_ref, subln_ref, o_ref, qs_ref, l_ref, acc_ref,
                    m_ref, *, nk_lat, lam_init):
    kt = pl.program_id(2)
    n_sc = 2 * D_HEADS

    @pl.when(kt == 0)
    def _():
        q = q_ref[...]
        for idx in range(n_sc):
            qs_ref[idx] = q[:, idx * HEAD_DIM:(idx + 1) * HEAD_DIM]
        l_ref[...] = jnp.zeros(l_ref.shape, F32)
        acc_ref[...] = jnp.zeros(acc_ref.shape, F32)
        m_ref[...] = jnp.full(m_ref.shape, -jnp.inf, F32)

    def step(k, v):
        for idx in range(n_sc):
            h = idx // 2
            s = _dot_nt(qs_ref[idx], k[:, idx * HEAD_DIM:(idx + 1) * HEAD_DIM])
            _flash_update(s, v[:, h * 2 * HEAD_DIM:(h + 1) * 2 * HEAD_DIM], m_ref, l_ref, acc_ref, idx)

    @pl.when(kt < nk_lat)
    def _():
        step(kl_ref[...], vl_ref[...])

    @pl.when(kt == nk_lat)
    def _():
        step(kx_ref[...], vx_ref[...])
        lam = _diff_lambda(lam_ref, lam_init)
        for h in range(D_HEADS):
            l1 = jnp.sum(l_ref[2 * h], axis=-1, keepdims=True)
            l2 = jnp.sum(l_ref[2 * h + 1], axis=-1, keepdims=True)
            o = acc_ref[2 * h] / l1 - lam * (acc_ref[2 * h + 1] / l2)
            o_ref[:, h * 2 * HEAD_DIM:(h + 1) * 2 * HEAD_DIM] = _finish_diff(o, subln_ref, lam_init).astype(BF16)


def _dense_d_bounded_kernel(q_ref, kl_ref, vl_ref, kx_ref, vx_ref, lam_ref, subln_ref, o_ref, qs_ref, acc_ref, *,
                            nk_lat, lam_init):
    kt = pl.program_id(2)
    n_sc = 2 * D_HEADS
    vw = 2 * HEAD_DIM

    @pl.when(kt == 0)
    def _():
        q = q_ref[...]
        for idx in range(n_sc):
            qs_ref[idx] = q[:, idx * HEAD_DIM:(idx + 1) * HEAD_DIM]
        acc_ref[...] = jnp.zeros(acc_ref.shape, F32)

    def step(k, v):
        n = k.shape[0]
        chunk = min(n, 2 * DENSE_KEY_CHUNK)
        lane = lax.broadcasted_iota(jnp.int32, (chunk, VT_PAD), 1)
        ones_cols = jnp.where(lane == 0, 1.0, 0.0).astype(BF16)
        for h in range(D_HEADS):
            upd = [None, None]
            for k0 in range(0, n, chunk):
                ks = slice(k0, k0 + chunk)
                v1 = jnp.concatenate([v[ks, h * vw:(h + 1) * vw], ones_cols], axis=1)
                for c in range(2):
                    idx = 2 * h + c
                    p_t = jnp.exp2(_dot_nt(k[ks, idx * HEAD_DIM:(idx + 1) * HEAD_DIM], qs_ref[idx])).astype(BF16)
                    term = lax.dot_general(v1, p_t, (((0,), (0,)), ((), ())), preferred_element_type=F32)
                    upd[c] = term if upd[c] is None else upd[c] + term
            for c in range(2):
                acc_ref[2 * h + c] += upd[c]

    @pl.when(kt < nk_lat)
    def _():
        step(kl_ref[...], vl_ref[...])

    @pl.when(kt == nk_lat)
    def _():
        step(kx_ref[...], vx_ref[...])
        lam = _diff_lambda(lam_ref, lam_init)
        for h in range(D_HEADS):
            a1, a2 = acc_ref[2 * h], acc_ref[2 * h + 1]
            o_t = a1[:vw] * (1.0 / a1[vw:vw + 1]) - lam * (a2[:vw] * (1.0 / a2[vw:vw + 1]))
            o_ref[:, h * vw:(h + 1) * vw] = _finish_diff(jnp.transpose(o_t), subln_ref, lam_init).astype(BF16)


def _dense_diff(qk_lat, rest_lat, qk_ctx, rest_ctx, lam_d, subln, lam_init, tq, tk, bounded):
    bsz, seq, _ = qk_lat.shape
    ctx_len = qk_ctx.shape[1]
    nk_lat = seq // tk
    width = D_HEADS * 2 * HEAD_DIM
    kcol = QK_OFF["dk"] // width
    vcol = REST_OFF["dv"] // width
    if bounded:
        body = functools.partial(_dense_d_bounded_kernel, nk_lat=nk_lat, lam_init=lam_init)
        scratch = [pltpu.VMEM((2 * D_HEADS, tq, HEAD_DIM), BF16),
                   pltpu.VMEM((2 * D_HEADS, 2 * HEAD_DIM + VT_PAD, tq), F32)]
    else:
        body = functools.partial(_dense_d_kernel, nk_lat=nk_lat, lam_init=lam_init)
        scratch = [pltpu.VMEM((2 * D_HEADS, tq, HEAD_DIM), BF16),
                   pltpu.VMEM((2 * D_HEADS, tq, LANES), F32),
                   pltpu.VMEM((2 * D_HEADS, tq, 2 * HEAD_DIM), F32),
                   pltpu.VMEM((2 * D_HEADS, tq, 1), F32)]
    return pl.pallas_call(
        body,
        grid=(bsz, seq // tq, nk_lat + 1),
        in_specs=[pl.BlockSpec((None, tq, width), lambda b, i, kt: (b, i, QK_OFF["dq"] // width)),
                  pl.BlockSpec((None, tk, width), lambda b, i, kt: (b, jnp.minimum(kt, nk_lat - 1), kcol)),
                  pl.BlockSpec((None, tk, width), lambda b, i, kt: (b, jnp.minimum(kt, nk_lat - 1), vcol)),
                  pl.BlockSpec((None, ctx_len, width), lambda b, i, kt: (b, 0, kcol)),
                  pl.BlockSpec((None, ctx_len, width), lambda b, i, kt: (b, 0, vcol)),
                  pl.BlockSpec((4, HEAD_DIM), lambda b, i, kt: (0, 0)),
                  pl.BlockSpec((1, 2 * HEAD_DIM), lambda b, i, kt: (0, 0))],
        out_specs=pl.BlockSpec((None, tq, BRANCH_W), lambda b, i, kt: (b, i, 0)),
        out_shape=jax.ShapeDtypeStruct((bsz, seq, BRANCH_W), BF16),
        scratch_shapes=scratch,
        compiler_params=_cparams(("parallel", "parallel", "arbitrary")),
        name="mixer_d_diff_bounded" if bounded else "mixer_d_diff",
    )(qk_lat, qk_lat, rest_lat, qk_ctx, rest_ctx, lam_d, subln.reshape(1, 2 * HEAD_DIM))


def _ctx_kernel(sink_ref, qk_ref, v_ref, lam_ref, subln_ref, o_ref, *, lam_init):
    qk = qk_ref[...]
    vals = v_ref[...]
    ctx_len = qk.shape[0]
    v_off = {name: REST_OFF[name] - REST_OFF[_FIRST_VALUE] for name in ("av", "cv", "bv", "dv")}

    def cols(name, start, width):
        c0 = QK_OFF[name] + start
        return qk[:, c0:c0 + width]

    def gqa(qname, kname, vname, out_off, n_kv, with_sink):
        g = 8 // n_kv
        for kv in range(n_kv):
            qs = _stack_heads(cols(qname, 0, 8 * HEAD_DIM), kv * g, g)
            s = _dot_nt(qs, cols(kname, kv * HEAD_DIM, HEAD_DIM))
            extra = None
            if with_sink:
                extra = jnp.concatenate([jnp.full((ctx_len, 1), sink_ref[kv * g + h], F32) for h in range(g)], axis=0)
            p, l = _softmax_rows(s, extra)
            vv = vals[:, v_off[vname] + kv * HEAD_DIM:v_off[vname] + (kv + 1) * HEAD_DIM]
            o = _dot(p.astype(BF16), vv) / l
            for h in range(g):
                c0 = out_off + (kv * g + h) * HEAD_DIM
                o_ref[:, c0:c0 + HEAD_DIM] = o[h * ctx_len:(h + 1) * ctx_len].astype(BF16)

    gqa("aq", "ak", "av", 0 * BRANCH_W, A_KV, True)
    gqa("bq", "bk", "bv", 1 * BRANCH_W, B_HEADS, False)
    gqa("cq", "ck", "cv", 2 * BRANCH_W, C_KV, False)

    lam = _diff_lambda(lam_ref, lam_init)
    for h in range(D_HEADS):
        base = h * 2 * HEAD_DIM
        p1, l1 = _softmax_rows(_dot_nt(cols("dq", base, HEAD_DIM), cols("dk", base, HEAD_DIM)))
        p2, l2 = _softmax_rows(_dot_nt(cols("dq", base + HEAD_DIM, HEAD_DIM), cols("dk", base + HEAD_DIM, HEAD_DIM)))
        pd = p1 / l1 - lam * (p2 / l2)
        o = _dot(pd.astype(BF16), vals[:, v_off["dv"] + base:v_off["dv"] + base + 2 * HEAD_DIM])
        c0 = 3 * BRANCH_W + base
        o_ref[:, c0:c0 + 2 * HEAD_DIM] = _finish_diff(o, subln_ref, lam_init).astype(BF16)


def _ctx_attention(qk_ctx, v_ctx, sink2, lam_d, subln, lam_init):
    bsz, ctx_len, _ = qk_ctx.shape
    return pl.pallas_call(
        functools.partial(_ctx_kernel, lam_init=lam_init),
        grid=(bsz,),
        in_specs=[pl.BlockSpec(memory_space=pltpu.SMEM),
                  pl.BlockSpec((None, ctx_len, QK_COLS), lambda b: (b, 0, 0)),
                  pl.BlockSpec((None, ctx_len, v_ctx.shape[2]), lambda b: (b, 0, 0)),
                  pl.BlockSpec((4, HEAD_DIM), lambda b: (0, 0)),
                  pl.BlockSpec((1, 2 * HEAD_DIM), lambda b: (0, 0))],
        out_specs=pl.BlockSpec((None, ctx_len, N_BRANCH * BRANCH_W), lambda b: (b, 0, 0)),
        out_shape=jax.ShapeDtypeStruct((bsz, ctx_len, N_BRANCH * BRANCH_W), BF16),
        compiler_params=_cparams(("parallel",)),
        name="ctx_attention",
    )(sink2, qk_ctx, v_ctx, lam_d, subln.reshape(1, 2 * HEAD_DIM))


def _merge_kernel(x_ref, mod_ref, ya_ref, yb_ref, yc_ref, yd_ref, gp_ref, mg_ref, wbr_ref, wout_ref, o_ref):
    gate = mod_ref[...][:, 2 * D_MODEL:]
    merged = None
    for n, y_ref in enumerate((ya_ref, yb_ref, yc_ref, yd_ref)):
        yg = (y_ref[...].astype(F32) * gp_ref[:, n * BRANCH_W:(n + 1) * BRANCH_W].astype(F32)).astype(BF16)
        term = mg_ref[:, n * D_MODEL:(n + 1) * D_MODEL].astype(F32) * _dot(yg, wbr_ref[n])
        merged = term if merged is None else merged + term
    o_ref[...] = x_ref[...] + gate * _dot(merged.astype(BF16), wout_ref[...])


def _merge(x2, mod3, ys, y_cols, rest, w_br, w_out, tm, row_of_tile):
    rows = x2.shape[0]
    y_specs = [pl.BlockSpec((tm, BRANCH_W), (lambda i, c=c: (i, c))) for c in y_cols]
    return pl.pallas_call(
        _merge_kernel,
        grid=(rows // tm,),
        in_specs=[pl.BlockSpec((tm, D_MODEL), lambda i: (i, 0)),
                  pl.BlockSpec((None, 1, 3 * D_MODEL), lambda i: (row_of_tile(i), 0, 0))]
                 + y_specs
                 + [pl.BlockSpec((tm, N_BRANCH * BRANCH_W), lambda i: (i, REST_OFF["ag"] // (N_BRANCH * BRANCH_W))),
                    pl.BlockSpec((tm, N_BRANCH * D_MODEL), lambda i: (i, 0)),
                    pl.BlockSpec((N_BRANCH, BRANCH_W, D_MODEL), lambda i: (0, 0, 0)),
                    pl.BlockSpec((D_MODEL, D_MODEL), lambda i: (0, 0))],
        out_specs=pl.BlockSpec((tm, D_MODEL), lambda i: (i, 0)),
        out_shape=jax.ShapeDtypeStruct((rows, D_MODEL), F32),
        compiler_params=_cparams(("parallel",)),
        name="gated_merge",
    )(x2, mod3, *ys, rest, rest, w_br, w_out)


def _rope_tables(seq):
    t = jnp.arange(seq, dtype=jnp.int32)
    pos = jnp.stack([t // GRID_W, t % GRID_W], axis=-1).astype(F32)
    n_freq = HEAD_DIM // 4
    freqs = ROPE_THETA ** (-jnp.arange(n_freq, dtype=F32) / n_freq)
    ang = pos[:, :, None] * freqs[None, None, :]
    ang = jnp.concatenate([ang, ang], axis=-1).reshape(seq, HEAD_DIM)
    sign = np.where((np.arange(HEAD_DIM) % 32) < 16, -1.0, 1.0).astype(np.float32)
    reps = NORM_GROUP // HEAD_DIM
    return jnp.tile(jnp.cos(ang), (1, reps)), jnp.tile(jnp.sin(ang) * sign, (1, reps))


def _regroup_cols(w, order, total):
    parts = [w[:, _ORIG[n][0]:_ORIG[n][1]] for n in order]
    used = sum(p.shape[1] for p in parts)
    if total > used:
        parts.append(jnp.zeros((w.shape[0], total - used), w.dtype))
    return jnp.concatenate(parts, axis=1)


def _qk_gain_row(g):
    parts = []
    for name in _QK_ORDER:
        mixer = "abcd".index(name[0])
        is_q = name[1] == "q"
        width = _ORIG[name][1] - _ORIG[name][0]
        gain = g[mixer, 0] * (QK_SCALE * LOG2E) if is_q else g[mixer, 1]
        parts.append(jnp.tile(gain, width // HEAD_DIM))
    parts.append(jnp.ones((QK_COLS - _QK_USED,), g.dtype))
    return jnp.concatenate(parts).reshape(1, QK_COLS).astype(F32)


def kernel(x, c, ctx, c_ctx, norm_w, w_ada, b_ada, w_in, qk_gain, sink_a, rpb_b, lam_d, subln_d, w_br, w_out):
    bsz, seq, _ = x.shape
    ctx_len = ctx.shape[1]
    depth = w_ada.shape[0]
    assert seq % (2 * A_WINDOW) == 0 and ctx_len % LANES == 0 and bsz <= 6

    tm_lat = 2048
    tm_rest = 4096
    tiles_per_batch = seq // tm_lat
    tk_dense = 2048
    tq_bounded, tq_rowmax = 512, 256
    assert seq % tk_dense == 0
    tm_merge = 512
    merge_tiles_per_batch = seq // tm_merge

    cvec = jnp.concatenate([c, c_ctx[None, :], jnp.zeros((8 - bsz - 1, D_MODEL), F32)], axis=0)
    mod_all = _ada(cvec, w_ada, b_ada)
    cos, sin = _rope_tables(seq)
    ones_bd = jnp.asarray(np.kron(np.eye(NORM_GROUP // HEAD_DIM), np.ones((HEAD_DIM, HEAD_DIM))), BF16)

    x2 = x.reshape(bsz * seq, D_MODEL)
    c2 = ctx.reshape(bsz * ctx_len, D_MODEL)
    for l in range(depth):
        need_ctx = l < depth - 1
        lam_init = 0.8 - 0.6 * math.exp(-0.3 * l)
        mod3 = mod_all[l].reshape(8, 1, 3 * D_MODEL)
        w_qk = _regroup_cols(w_in[l], _QK_ORDER, QK_COLS).astype(BF16)
        w_rest = _regroup_cols(w_in[l], _REST_ORDER, REST_COLS).astype(BF16)
        gain_row = _qk_gain_row(qk_gain[l])

        hx = _prenorm(x2, norm_w[l], mod3, tm_lat, lambda i: i // tiles_per_batch)
        hc = _prenorm(c2, norm_w[l], mod3, bsz * ctx_len, lambda i: bsz)
        qk_lat = _proj_qk(hx, w_qk, gain_row, ones_bd, tm_lat, cos, sin, tiles_per_batch).reshape(bsz, seq, QK_COLS)
        qk_ctx = _proj_qk(hc, w_qk, gain_row, ones_bd, bsz * ctx_len).reshape(bsz, ctx_len, QK_COLS)
        rest_lat = _proj_rest(hx, w_rest, tm_rest).reshape(bsz, seq, REST_COLS)
        rest_ctx = _proj_rest(hc, w_rest, bsz * ctx_len).reshape(bsz, ctx_len, REST_COLS)

        sink2 = sink_a[l] * LOG2E
        rpb2 = rpb_b[l] * LOG2E
        tq_nbr = 2 * A_WINDOW
        proj = (qk_lat, rest_lat, qk_ctx, rest_ctx)

        def qk_bound(g):
            return HEAD_DIM * QK_SCALE * LOG2E * jnp.max(jnp.abs(g[0])) * jnp.max(jnp.abs(g[1])) * 1.02

        y_a = lax.cond(qk_bound(qk_gain[l, 0]) <= LOGIT_BOUND,
                       lambda: _window_attention(*proj, sink2, True),
                       lambda: _window_attention(*proj, sink2, False))
        y_b = lax.cond(qk_bound(qk_gain[l, 1]) + jnp.max(jnp.abs(rpb_b[l])) * LOG2E <= LOGIT_BOUND,
                       lambda: _neighborhood_attention(*proj, rpb2, tq_nbr, True),
                       lambda: _neighborhood_attention(*proj, rpb2, tq_nbr, False))
        y_c = lax.cond(qk_bound(qk_gain[l, 2]) <= LOGIT_BOUND,
                       lambda: _dense_gqa(*proj, tq_bounded, tk_dense, True),
                       lambda: _dense_gqa(*proj, tq_rowmax, tk_dense, False))
        y_d = lax.cond(qk_bound(qk_gain[l, 3]) <= LOGIT_BOUND,
                       lambda: _dense_diff(*proj, lam_d[l], subln_d[l], lam_init, tq_bounded, tk_dense, True),
                       lambda: _dense_diff(*proj, lam_d[l], subln_d[l], lam_init, tq_rowmax, tk_dense, False))

        w_br_l = w_br[l].astype(BF16)
        w_out_l = w_out[l].astype(BF16)
        ys = [y.reshape(bsz * seq, BRANCH_W) for y in (y_a, y_b, y_c, y_d)]
        x_new = _merge(x2, mod3, ys, (0, 0, 0, 0), rest_lat.reshape(bsz * seq, REST_COLS), w_br_l, w_out_l,
                       tm_merge, lambda i: i // merge_tiles_per_batch)
        if need_ctx:
            y_ctx = _ctx_attention(qk_ctx, rest_ctx[:, :, REST_OFF[_FIRST_VALUE]:], sink2, lam_d[l], subln_d[l],
                                   lam_init)
            y_ctx2 = y_ctx.reshape(bsz * ctx_len, N_BRANCH * BRANCH_W)
            c2 = _merge(c2, mod3, [y_ctx2] * 4, (0, 1, 2, 3), rest_ctx.reshape(bsz * ctx_len, REST_COLS),
                        w_br_l, w_out_l, bsz * ctx_len, lambda i: bsz)
        x2 = x_new
    return x2.reshape(bsz, seq, D_MODEL)
```

```python
import functools
import math

import numpy as np
import jax
import jax.numpy as jnp
from jax import lax
from jax.experimental import pallas as pl
from jax.experimental.pallas import tpu as pltpu

F32 = jnp.float32
BF16 = jnp.bfloat16

D_MODEL = 1024
GRID_W = 64
HEAD_DIM = 64
BRANCH_W = 512
N_BRANCH = 4
A_HEADS, A_KV, A_WINDOW = 8, 2, 128
B_HEADS, NB_ROWS, NB_COLS = 8, 8, 16
C_HEADS, C_KV = 8, 2
D_HEADS = 4
ROPE_THETA = 10000.0
EPS = 1e-6
NEG_INF = -1e30
QK_SCALE = HEAD_DIM ** -0.5
LOG2E = math.log2(math.e)
LOGIT_BOUND = 60.0

V7X_VMEM_LIMIT_BYTES = 56 * 1024 * 1024
LANES = 128

_ORIG = dict(aq=(0, 512), ak=(512, 640), av=(640, 768), ag=(768, 1280),
             bq=(1280, 1792), bk=(1792, 2304), bv=(2304, 2816), bg=(2816, 3328),
             cq=(3328, 3840), ck=(3840, 3968), cv=(3968, 4096), cg=(4096, 4608),
             dq=(4608, 5120), dk=(5120, 5632), dv=(5632, 6144), dg=(6144, 6656),
             mg=(6656, 10752))
_QK_ORDER = ("aq", "cq", "dq", "bq", "bk", "dk", "ak", "ck")
_REST_ORDER = ("mg", "ag", "bg", "cg", "dg", "dv", "bv", "av", "cv")
_FIRST_VALUE = "dv"


def _offsets(order):
    off, out = 0, {}
    for name in order:
        lo, hi = _ORIG[name]
        out[name] = off
        off += hi - lo
    return out, off


PROJ_TN = 512
PROJ_ROW_CHUNK = 512
REST_ROW_CHUNK = 1024
VT_PAD = 16
DENSE_KEY_CHUNK = 512
NORM_GROUP = 256
QK_OFF, _QK_USED = _offsets(_QK_ORDER)
REST_OFF, _REST_USED = _offsets(_REST_ORDER)
QK_COLS = -(-_QK_USED // PROJ_TN) * PROJ_TN
REST_COLS = -(-_REST_USED // PROJ_TN) * PROJ_TN
_ROPE_LO_END = QK_OFF["bq"] // PROJ_TN
_ROPE_HI_START = QK_OFF["dk"] // PROJ_TN
_MG_BLOCKS = (REST_OFF["ag"]) // PROJ_TN
_GP_BLOCKS_END = REST_OFF[_FIRST_VALUE] // PROJ_TN
assert QK_OFF["bq"] % PROJ_TN == 0 and QK_OFF["dk"] % PROJ_TN == 0
assert REST_OFF["ag"] % PROJ_TN == 0 and REST_OFF[_FIRST_VALUE] % PROJ_TN == 0


def _cparams(sem):
    return pltpu.CompilerParams(dimension_semantics=sem, vmem_limit_bytes=V7X_VMEM_LIMIT_BYTES)


def _sigmoid(x):
    return 0.5 * jnp.tanh(0.5 * x) + 0.5


def _dot_nt(a, b):
    return lax.dot_general(a, b, (((1,), (1,)), ((), ())), preferred_element_type=F32)


def _dot(a, b):
    return jnp.dot(a, b, preferred_element_type=F32)


def _ada_kernel(c_ref, w_ref, b_ref, o_ref):
    c = c_ref[...]
    o_ref[...] = _dot(c * _sigmoid(c), w_ref[...]) + b_ref[...]


def _ada(cvec, w_ada, b_ada):
    depth = w_ada.shape[0]
    tn = 512
    return pl.pallas_call(
        _ada_kernel,
        grid=(depth, 3 * D_MODEL // tn),
        in_specs=[pl.BlockSpec((8, D_MODEL), lambda l, j: (0, 0)),
                  pl.BlockSpec((None, D_MODEL, tn), lambda l, j: (l, 0, j)),
                  pl.BlockSpec((None, 1, tn), lambda l, j: (l, 0, j))],
        out_specs=pl.BlockSpec((None, 8, tn), lambda l, j: (l, 0, j)),
        out_shape=jax.ShapeDtypeStruct((depth, 8, 3 * D_MODEL), F32),
        compiler_params=_cparams(("parallel", "parallel")),
        name="ada_mod",
    )(cvec, w_ada, b_ada.reshape(depth, 1, 3 * D_MODEL))


def _prenorm_kernel(x_ref, nw_ref, mod_ref, o_ref):
    x = x_ref[...]
    mod = mod_ref[...]
    shift, scale = mod[:, :D_MODEL], mod[:, D_MODEL:2 * D_MODEL]
    y = x * lax.rsqrt(jnp.mean(x * x, axis=-1, keepdims=True) + EPS) * nw_ref[...]
    o_ref[...] = (y * (1.0 + scale) + shift).astype(BF16)


def _prenorm(x2, norm_w, mod3, tm, row_of_tile):
    rows = x2.shape[0]
    return pl.pallas_call(
        _prenorm_kernel,
        grid=(rows // tm,),
        in_specs=[pl.BlockSpec((tm, D_MODEL), lambda i: (i, 0)),
                  pl.BlockSpec((1, D_MODEL), lambda i: (0, 0)),
                  pl.BlockSpec((None, 1, 3 * D_MODEL), lambda i: (row_of_tile(i), 0, 0))],
        out_specs=pl.BlockSpec((tm, D_MODEL), lambda i: (i, 0)),
        out_shape=jax.ShapeDtypeStruct((rows, D_MODEL), BF16),
        compiler_params=_cparams(("parallel",)),
        name="prenorm",
    )(x2, norm_w.reshape(1, D_MODEL), mod3)


def _rot_half_unsigned(n):
    lane = lax.broadcasted_iota(jnp.int32, n.shape, 1)
    return jnp.where((lane & 31) < 16, pltpu.roll(n, LANES - 16, 1), pltpu.roll(n, 16, 1))


def _proj_qk_kernel(*refs, rope):
    if rope:
        hx_ref, w_ref, gain_ref, ones_ref, cos_ref, sin_ref, o_ref = refs
    else:
        hx_ref, w_ref, gain_ref, ones_ref, o_ref = refs
    w = w_ref[...]
    ones_bd = ones_ref[...]
    rows = hx_ref.shape[0]
    chunk = min(rows, PROJ_ROW_CHUNK)
    if rope:
        j = pl.program_id(1)
        use = jnp.logical_or(j < _ROPE_LO_END, j >= _ROPE_HI_START)
    for r0 in range(0, rows, chunk):
        rs = slice(r0, r0 + chunk)
        acc_all = _dot(hx_ref[rs, :], w)
        if rope:
            cos = jnp.where(use, cos_ref[rs, :], 1.0)
            sin = jnp.where(use, sin_ref[rs, :], 0.0)
        for c in range(PROJ_TN // NORM_GROUP):
            sl = slice(c * NORM_GROUP, (c + 1) * NORM_GROUP)
            acc = acc_all[:, sl]
            ss = _dot((acc * acc).astype(BF16), ones_bd)
            n = acc * lax.rsqrt(ss * (1.0 / HEAD_DIM) + EPS) * gain_ref[:, sl]
            if rope:
                rot = jnp.concatenate([_rot_half_unsigned(n[:, :LANES]), _rot_half_unsigned(n[:, LANES:])], axis=1)
                n = n * cos + rot * sin
            o_ref[rs, sl] = n.astype(BF16)


def _proj_qk(hx, w_qk, gain_row, ones_bd, tm, cos=None, sin=None, tiles_per_batch=None):
    rows = hx.shape[0]
    rope = cos is not None
    in_specs = [pl.BlockSpec((tm, D_MODEL), lambda i, j: (i, 0)),
                pl.BlockSpec((D_MODEL, PROJ_TN), lambda i, j: (0, j)),
                pl.BlockSpec((1, PROJ_TN), lambda i, j: (0, j)),
                pl.BlockSpec((NORM_GROUP, NORM_GROUP), lambda i, j: (0, 0))]
    args = [hx, w_qk, gain_row, ones_bd]
    if rope:
        in_specs += [pl.BlockSpec((tm, NORM_GROUP), lambda i, j: (i % tiles_per_batch, 0))] * 2
        args += [cos, sin]
    return pl.pallas_call(
        functools.partial(_proj_qk_kernel, rope=rope),
        grid=(rows // tm, QK_COLS // PROJ_TN),
        in_specs=in_specs,
        out_specs=pl.BlockSpec((tm, PROJ_TN), lambda i, j: (i, j)),
        out_shape=jax.ShapeDtypeStruct((rows, QK_COLS), BF16),
        compiler_params=_cparams(("parallel", "arbitrary")),
        name="proj_qk_rope" if rope else "proj_qk",
    )(*args)


def _proj_rest_kernel(hx_ref, w_ref, o_ref):
    j = pl.program_id(1)
    w = w_ref[...]
    rows = hx_ref.shape[0]
    chunk = min(rows, REST_ROW_CHUNK)
    for r0 in range(0, rows, chunk):
        acc = _dot(hx_ref[r0:r0 + chunk, :], w)
        sg = _sigmoid(acc)
        o_ref[r0:r0 + chunk, :] = jnp.where(j < _MG_BLOCKS, sg,
                                            jnp.where(j < _GP_BLOCKS_END, acc * sg, acc)).astype(BF16)


def _proj_rest(hx, w_rest, tm):
    rows = hx.shape[0]
    return pl.pallas_call(
        _proj_rest_kernel,
        grid=(rows // tm, REST_COLS // PROJ_TN),
        in_specs=[pl.BlockSpec((tm, D_MODEL), lambda i, j: (i, 0)),
                  pl.BlockSpec((D_MODEL, PROJ_TN), lambda i, j: (0, j))],
        out_specs=pl.BlockSpec((tm, PROJ_TN), lambda i, j: (i, j)),
        out_shape=jax.ShapeDtypeStruct((rows, REST_COLS), BF16),
        compiler_params=_cparams(("parallel", "arbitrary")),
        name="proj_rest",
    )(hx, w_rest)


def _stack_heads(q, first_head, n):
    return jnp.concatenate([q[:, (first_head + g) * HEAD_DIM:(first_head + g + 1) * HEAD_DIM] for g in range(n)],
                           axis=0)


def _softmax_rows(s, extra=None):
    m = jnp.max(s, axis=-1, keepdims=True)
    if extra is not None:
        m = jnp.maximum(m, extra)
    p = jnp.exp2(s - m)
    l = jnp.sum(p, axis=-1, keepdims=True)
    if extra is not None:
        l = l + jnp.exp2(extra - m)
    return p, l


def _with_ones_column(v, pad=HEAD_DIM):
    lane = lax.broadcasted_iota(jnp.int32, (v.shape[0], pad), 1)
    return jnp.concatenate([v, jnp.where(lane == 0, 1.0, 0.0).astype(v.dtype)], axis=1)


def _lane_partial_sums(p):
    part = p[:, :LANES]
    for c in range(1, p.shape[1] // LANES):
        part = part + p[:, c * LANES:(c + 1) * LANES]
    return part


def _diff_lambda(lam_ref, lam_init):
    lf = lam_ref[...]
    a = jnp.sum(lf[0:1] * lf[1:2], axis=-1, keepdims=True)
    b = jnp.sum(lf[2:3] * lf[3:4], axis=-1, keepdims=True)
    return jnp.exp(a) - jnp.exp(b) + lam_init


def _finish_diff(o, subln_ref, lam_init):
    y = o * lax.rsqrt(jnp.mean(o * o, axis=-1, keepdims=True) + EPS) * subln_ref[...]
    return y * (1.0 - lam_init)


def _win_kernel(sink_ref, q_ref, kp_ref, kc_ref, kn_ref, kx_ref, vp_ref, vc_ref, vn_ref, vx_ref, o_ref, *,
                tq, seq):
    i = pl.program_id(1)
    g = A_HEADS // A_KV
    q = q_ref[...]
    k_all = jnp.concatenate([kp_ref[...], kc_ref[...], kn_ref[...], kx_ref[...]], axis=0)
    v_all = jnp.concatenate([vp_ref[...], vc_ref[...], vn_ref[...], vx_ref[...]], axis=0)
    nw = 2 * tq
    t = lax.broadcasted_iota(jnp.int32, (g * tq, nw), 0) & (tq - 1)
    j = lax.broadcasted_iota(jnp.int32, (g * tq, nw), 1)
    rel = j - tq // 2 - t
    kpos = i * tq - tq // 2 + j
    bad = jnp.where(jnp.abs(rel) > A_WINDOW, 1, 0) + jnp.where(kpos < 0, 1, 0) + jnp.where(kpos >= seq, 1, 0)
    for kv in range(A_KV):
        qs = _stack_heads(q, kv * g, g)
        kk = k_all[:, kv * HEAD_DIM:(kv + 1) * HEAD_DIM]
        vv = v_all[:, kv * HEAD_DIM:(kv + 1) * HEAD_DIM]
        s = _dot_nt(qs, kk)
        s_win = jnp.where(bad > 0, NEG_INF, s[:, :nw])
        s_ctx = s[:, nw:]
        sink = jnp.concatenate([jnp.full((tq, 1), sink_ref[kv * g + h], F32) for h in range(g)], axis=0)
        m = jnp.maximum(jnp.maximum(jnp.max(s_win, axis=-1, keepdims=True),
                                    jnp.max(s_ctx, axis=-1, keepdims=True)), sink)
        pw = jnp.exp2(s_win - m)
        pc = jnp.exp2(s_ctx - m)
        l = jnp.sum(pw, axis=-1, keepdims=True) + jnp.sum(pc, axis=-1, keepdims=True) + jnp.exp2(sink - m)
        o = (_dot(pw.astype(BF16), vv[:nw]) + _dot(pc.astype(BF16), vv[nw:])) / l
        for h in range(g):
            c0 = (kv * g + h) * HEAD_DIM
            o_ref[:, c0:c0 + HEAD_DIM] = o[h * tq:(h + 1) * tq].astype(BF16)


def _win_bounded_kernel(sink_ref, q_ref, kp_ref, kc_ref, kn_ref, kx_ref, vp_ref, vc_ref, vn_ref, vx_ref, mask_ref,
                        o_ref, *, tq):
    g = A_HEADS // A_KV
    q = q_ref[...]
    mask_t = mask_ref[...]
    sink = sink_ref[...]
    k_all = jnp.concatenate([kp_ref[...], kc_ref[...], kn_ref[...], kx_ref[...]], axis=0)
    v_all = jnp.concatenate([vp_ref[...], vc_ref[...], vn_ref[...], vx_ref[...]], axis=0)
    for kv in range(A_KV):
        sl = slice(kv * HEAD_DIM, (kv + 1) * HEAD_DIM)
        qs = _stack_heads(q, kv * g, g)
        stab = [jnp.maximum(sink[:, kv * g + h:kv * g + h + 1], 0.0) for h in range(g)]
        shift_t = jnp.concatenate([mask_t - stab[h] for h in range(g)], axis=1)
        p_t = jnp.exp2(_dot_nt(k_all[:, sl], qs) + shift_t).astype(BF16)
        acc = lax.dot_general(_with_ones_column(v_all[:, sl], VT_PAD), p_t, (((0,), (0,)), ((), ())),
                              preferred_element_type=F32)
        for h in range(g):
            a = acc[:, h * tq:(h + 1) * tq]
            l = a[HEAD_DIM:HEAD_DIM + 1] + jnp.exp2(sink[:, kv * g + h:kv * g + h + 1] - stab[h])
            c0 = (kv * g + h) * HEAD_DIM
            o_ref[:, c0:c0 + HEAD_DIM] = jnp.transpose(a[:HEAD_DIM] * (1.0 / l)).astype(BF16)


def _window_mask_tables(seq, tq, ctx_len):
    nt = seq // tq
    tabs = []
    for i0 in (0, 1, nt - 1):
        t = np.arange(tq)[None, :]
        j = np.arange(2 * tq)[:, None]
        kpos = i0 * tq - tq // 2 + j
        ok = (np.abs(j - tq // 2 - t) <= A_WINDOW) & (kpos >= 0) & (kpos < seq)
        win = np.where(ok, 0.0, NEG_INF).astype(np.float32)
        tabs.append(np.concatenate([win, np.zeros((ctx_len, tq), np.float32)], axis=0))
    return jnp.asarray(np.stack(tabs, axis=0))


def _window_attention(qk_lat, rest_lat, qk_ctx, rest_ctx, sink2, bounded):
    bsz, seq, _ = qk_lat.shape
    ctx_len = qk_ctx.shape[1]
    tq = 2 * A_WINDOW
    nt = seq // tq
    half = tq // 2
    kcol = QK_OFF["ak"] // LANES
    vcol = REST_OFF["av"] // LANES
    n_half = seq // half

    def kv_specs(col):
        return [pl.BlockSpec((None, half, LANES), lambda b, i: (b, jnp.maximum(2 * i - 1, 0), col)),
                pl.BlockSpec((None, tq, LANES), lambda b, i: (b, i, col)),
                pl.BlockSpec((None, half, LANES), lambda b, i: (b, jnp.minimum(2 * i + 2, n_half - 1), col)),
                pl.BlockSpec((None, ctx_len, LANES), lambda b, i: (b, 0, col))]

    def tile_variant(i):
        return jnp.where(i == 0, 0, jnp.where(i == nt - 1, 2, 1))

    q_spec = pl.BlockSpec((None, tq, A_HEADS * HEAD_DIM), lambda b, i: (b, i, QK_OFF["aq"] // 512))
    kv_args = (qk_lat, qk_lat, qk_lat, qk_ctx, rest_lat, rest_lat, rest_lat, rest_ctx)
    if bounded:
        body = functools.partial(_win_bounded_kernel, tq=tq)
        in_specs = ([pl.BlockSpec((1, A_HEADS), lambda b, i: (0, 0)), q_spec] + kv_specs(kcol) + kv_specs(vcol)
                    + [pl.BlockSpec((None, 2 * tq + ctx_len, tq), lambda b, i: (tile_variant(i), 0, 0))])
        args = (sink2.reshape(1, A_HEADS), qk_lat) + kv_args + (_window_mask_tables(seq, tq, ctx_len),)
    else:
        body = functools.partial(_win_kernel, tq=tq, seq=seq)
        in_specs = [pl.BlockSpec(memory_space=pltpu.SMEM), q_spec] + kv_specs(kcol) + kv_specs(vcol)
        args = (sink2, qk_lat) + kv_args
    return pl.pallas_call(
        body,
        grid=(bsz, nt),
        in_specs=in_specs,
        out_specs=pl.BlockSpec((None, tq, BRANCH_W), lambda b, i: (b, i, 0)),
        out_shape=jax.ShapeDtypeStruct((bsz, seq, BRANCH_W), BF16),
        compiler_params=_cparams(("parallel", "parallel")),
        name="mixer_a_window_bounded" if bounded else "mixer_a_window",
    )(*args)


def _nbr_kernel(q_ref, kp_ref, kc_ref, kn_ref, kx_ref, vp_ref, vc_ref, vn_ref, vx_ref, bias_ref, rmask_ref, o_ref,
                *, tq, bounded):
    q = q_ref[...]
    k_all = jnp.concatenate([kp_ref[...], kc_ref[...], kn_ref[...], kx_ref[...]], axis=0)
    v_all = jnp.concatenate([vp_ref[...], vc_ref[...], vn_ref[...], vx_ref[...]], axis=0)
    rmask = rmask_ref[...]
    nw = 3 * tq
    for h in range(B_HEADS):
        sl = slice(h * HEAD_DIM, (h + 1) * HEAD_DIM)
        s = _dot_nt(q[:, sl], k_all[:, sl])
        s = jnp.concatenate([s[:, :nw] + (bias_ref[h] + rmask), s[:, nw:]], axis=1)
        if not bounded:
            s = s - jnp.max(s, axis=-1, keepdims=True)
        acc = _dot(jnp.exp2(s).astype(BF16), _with_ones_column(v_all[:, sl]))
        o_ref[:, sl] = (acc[:, :HEAD_DIM] * (1.0 / acc[:, HEAD_DIM:HEAD_DIM + 1])).astype(BF16)


def _nbr_bias_tables(rpb, seq, tq):
    rows = seq // GRID_W
    nt = seq // tq
    kr = min(NB_ROWS, rows)
    qr = tq // GRID_W
    n_heads = rpb.shape[0]
    assert NB_ROWS - 1 - qr - (qr - 1) >= 0 and NB_ROWS - 1 - qr + 3 * qr <= 2 * NB_ROWS - 1
    pad = GRID_W - 1
    rpb_pad = jnp.pad(rpb.astype(F32), ((0, 0), (0, 0), (pad, pad)))
    toep = jnp.stack([rpb_pad[:, :, pad + NB_COLS - 1 - c:pad + NB_COLS - 1 - c + GRID_W] for c in range(GRID_W)],
                     axis=2)
    c = np.arange(GRID_W)
    cstart = np.clip(c - NB_COLS // 2, 0, GRID_W - NB_COLS)
    col_ok = (c[None, :] >= cstart[:, None]) & (c[None, :] < cstart[:, None] + NB_COLS)
    toep = jnp.where(col_ok[None, None], toep, NEG_INF)
    a0 = NB_ROWS - 1 - qr
    per_rl = [jnp.transpose(toep[:, a0 - rl:a0 - rl + 3 * qr], (0, 2, 1, 3)) for rl in range(qr)]
    base = jnp.stack(per_rl, axis=1).reshape(n_heads, tq, 3 * tq)
    row_masks = []
    for i0 in (0, 1, nt - 1):
        r = i0 * qr + np.arange(qr)
        r2 = (i0 - 1) * qr + np.arange(3 * qr)
        rstart = np.clip(r - kr // 2, 0, rows - kr)
        row_ok = ((r2[None, :] >= rstart[:, None]) & (r2[None, :] < rstart[:, None] + kr)
                  & (r2[None, :] >= 0) & (r2[None, :] < rows))
        full = np.broadcast_to(row_ok[:, None, :, None], (qr, GRID_W, 3 * qr, GRID_W)).reshape(tq, 3 * tq)
        row_masks.append(np.where(full, 0.0, NEG_INF).astype(np.float32))
    return base, jnp.asarray(np.stack(row_masks, axis=0))


def _neighborhood_attention(qk_lat, rest_lat, qk_ctx, rest_ctx, rpb2, tq, bounded):
    bsz, seq, _ = qk_lat.shape
    ctx_len = qk_ctx.shape[1]
    nt = seq // tq
    width = B_HEADS * HEAD_DIM
    bias, row_masks = _nbr_bias_tables(rpb2, seq, tq)

    def kv_specs(col):
        return [pl.BlockSpec((None, tq, width), lambda b, i: (b, jnp.maximum(i - 1, 0), col)),
                pl.BlockSpec((None, tq, width), lambda b, i: (b, i, col)),
                pl.BlockSpec((None, tq, width), lambda b, i: (b, jnp.minimum(i + 1, nt - 1), col)),
                pl.BlockSpec((None, ctx_len, width), lambda b, i: (b, 0, col))]

    def tile_variant(i):
        return jnp.where(i == 0, 0, jnp.where(i == nt - 1, 2, 1))

    return pl.pallas_call(
        functools.partial(_nbr_kernel, tq=tq, bounded=bounded),
        grid=(bsz, nt),
        in_specs=[pl.BlockSpec((None, tq, width), lambda b, i: (b, i, QK_OFF["bq"] // width))]
                 + kv_specs(QK_OFF["bk"] // width) + kv_specs(REST_OFF["bv"] // width)
                 + [pl.BlockSpec(bias.shape, lambda b, i: (0, 0, 0)),
                    pl.BlockSpec((None,) + row_masks.shape[1:], lambda b, i: (tile_variant(i), 0, 0))],
        out_specs=pl.BlockSpec((None, tq, BRANCH_W), lambda b, i: (b, i, 0)),
        out_shape=jax.ShapeDtypeStruct((bsz, seq, BRANCH_W), BF16),
        compiler_params=_cparams(("parallel", "parallel")),
        name="mixer_b_neighbourhood_bounded" if bounded else "mixer_b_neighbourhood",
    )(qk_lat, qk_lat, qk_lat, qk_lat, qk_ctx, rest_lat, rest_lat, rest_lat, rest_ctx, bias, row_masks)


def _flash_update(s, v, m_ref, l_ref, acc_ref, idx):
    m_prev = m_ref[idx]
    m_new = jnp.maximum(m_prev, jnp.max(s, axis=-1, keepdims=True))
    alpha = jnp.exp2(m_prev - m_new)
    p = jnp.exp2(s - m_new)
    if l_ref is not None:
        l_ref[idx] = alpha * l_ref[idx] + _lane_partial_sums(p)
    acc_ref[idx] = alpha * acc_ref[idx] + _dot(p.astype(BF16), v)
    m_ref[idx] = m_new


def _dense_c_kernel(q_ref, kl_ref, vl_ref, kx_ref, vx_ref, o_ref, qs_ref, acc_ref, m_ref, *, tq, nk_lat):
    kt = pl.program_id(2)
    g = C_HEADS // C_KV

    @pl.when(kt == 0)
    def _():
        q = q_ref[...]
        for kv in range(C_KV):
            qs_ref[kv] = _stack_heads(q, kv * g, g)
        acc_ref[...] = jnp.zeros(acc_ref.shape, F32)
        m_ref[...] = jnp.full(m_ref.shape, -jnp.inf, F32)

    def step(k, v):
        for kv in range(C_KV):
            sl = slice(kv * HEAD_DIM, (kv + 1) * HEAD_DIM)
            _flash_update(_dot_nt(qs_ref[kv], k[:, sl]), _with_ones_column(v[:, sl]), m_ref, None, acc_ref, kv)

    @pl.when(kt < nk_lat)
    def _():
        step(kl_ref[...], vl_ref[...])

    @pl.when(kt == nk_lat)
    def _():
        step(kx_ref[...], vx_ref[...])
        for kv in range(C_KV):
            acc = acc_ref[kv]
            o = acc[:, :HEAD_DIM] * (1.0 / acc[:, HEAD_DIM:HEAD_DIM + 1])
            for h in range(g):
                c0 = (kv * g + h) * HEAD_DIM
                o_ref[:, c0:c0 + HEAD_DIM] = o[h * tq:(h + 1) * tq].astype(BF16)


def _dense_c_bounded_kernel(q_ref, kl_ref, vl_ref, kx_ref, vx_ref, o_ref, qs_ref, acc_ref, *, tq, nk_lat):
    kt = pl.program_id(2)
    g = C_HEADS // C_KV

    @pl.when(kt == 0)
    def _():
        q = q_ref[...]
        for kv in range(C_KV):
            qs_ref[kv] = _stack_heads(q, kv * g, g)
        acc_ref[...] = jnp.zeros(acc_ref.shape, F32)

    def step(k, v):
        n = k.shape[0]
        chunk = min(n, DENSE_KEY_CHUNK)
        for kv in range(C_KV):
            sl = slice(kv * HEAD_DIM, (kv + 1) * HEAD_DIM)
            upd = None
            for k0 in range(0, n, chunk):
                ks = slice(k0, k0 + chunk)
                p_t = jnp.exp2(_dot_nt(k[ks, sl], qs_ref[kv])).astype(BF16)
                term = lax.dot_general(_with_ones_column(v[ks, sl], VT_PAD), p_t, (((0,), (0,)), ((), ())),
                                       preferred_element_type=F32)
                upd = term if upd is None else upd + term
            acc_ref[kv] += upd

    @pl.when(kt < nk_lat)
    def _():
        step(kl_ref[...], vl_ref[...])

    @pl.when(kt == nk_lat)
    def _():
        step(kx_ref[...], vx_ref[...])
        for kv in range(C_KV):
            acc = acc_ref[kv]
            o = jnp.transpose(acc[:HEAD_DIM] * (1.0 / acc[HEAD_DIM:HEAD_DIM + 1]))
            for h in range(g):
                c0 = (kv * g + h) * HEAD_DIM
                o_ref[:, c0:c0 + HEAD_DIM] = o[h * tq:(h + 1) * tq].astype(BF16)


def _dense_gqa(qk_lat, rest_lat, qk_ctx, rest_ctx, tq, tk, bounded):
    bsz, seq, _ = qk_lat.shape
    ctx_len = qk_ctx.shape[1]
    nk_lat = seq // tk
    g = C_HEADS // C_KV
    kcol = QK_OFF["ck"] // LANES
    vcol = REST_OFF["cv"] // LANES
    if bounded:
        body = functools.partial(_dense_c_bounded_kernel, tq=tq, nk_lat=nk_lat)
        scratch = [pltpu.VMEM((C_KV, g * tq, HEAD_DIM), BF16),
                   pltpu.VMEM((C_KV, HEAD_DIM + VT_PAD, g * tq), F32)]
    else:
        body = functools.partial(_dense_c_kernel, tq=tq, nk_lat=nk_lat)
        scratch = [pltpu.VMEM((C_KV, g * tq, HEAD_DIM), BF16),
                   pltpu.VMEM((C_KV, g * tq, LANES), F32),
                   pltpu.VMEM((C_KV, g * tq, 1), F32)]
    return pl.pallas_call(
        body,
        grid=(bsz, seq // tq, nk_lat + 1),
        in_specs=[pl.BlockSpec((None, tq, C_HEADS * HEAD_DIM), lambda b, i, kt: (b, i, QK_OFF["cq"] // 512)),
                  pl.BlockSpec((None, tk, LANES), lambda b, i, kt: (b, jnp.minimum(kt, nk_lat - 1), kcol)),
                  pl.BlockSpec((None, tk, LANES), lambda b, i, kt: (b, jnp.minimum(kt, nk_lat - 1), vcol)),
                  pl.BlockSpec((None, ctx_len, LANES), lambda b, i, kt: (b, 0, kcol)),
                  pl.BlockSpec((None, ctx_len, LANES), lambda b, i, kt: (b, 0, vcol))],
        out_specs=pl.BlockSpec((None, tq, BRANCH_W), lambda b, i, kt: (b, i, 0)),
        out_shape=jax.ShapeDtypeStruct((bsz, seq, BRANCH_W), BF16),
        scratch_shapes=scratch,
        compiler_params=_cparams(("parallel", "parallel", "arbitrary")),
        name="mixer_c_dense_bounded" if bounded else "mixer_c_dense",
    )(qk_lat, qk_lat, rest_lat, qk_ctx, rest_ctx)


def _dense_d_kernel(q_ref, kl_ref, vl_ref, kx_ref, vx_ref, lam_ref, subln_ref, o_ref, qs_ref, l_ref, acc_ref,
                    m_ref, *, nk_lat, lam_init):
    kt = pl.program_id(2)
    n_sc = 2 * D_HEADS

    @pl.when(kt == 0)
    def _():
        q = q_ref[...]
        for idx in range(n_sc):
            qs_ref[idx] = q[:, idx * HEAD_DIM:(idx + 1) * HEAD_DIM]
        l_ref[...] = jnp.zeros(l_ref.shape, F32)
        acc_ref[...] = jnp.zeros(acc_ref.shape, F32)
        m_ref[...] = jnp.full(m_ref.shape, -jnp.inf, F32)

    def step(k, v):
        for idx in range(n_sc):
            h = idx // 2
            s = _dot_nt(qs_ref[idx], k[:, idx * HEAD_DIM:(idx + 1) * HEAD_DIM])
            _flash_update(s, v[:, h * 2 * HEAD_DIM:(h + 1) * 2 * HEAD_DIM], m_ref, l_ref, acc_ref, idx)

    @pl.when(kt < nk_lat)
    def _():
        step(kl_ref[...], vl_ref[...])

    @pl.when(kt == nk_lat)
    def _():
        step(kx_ref[...], vx_ref[...])
        lam = _diff_lambda(lam_ref, lam_init)
        for h in range(D_HEADS):
            l1 = jnp.sum(l_ref[2 * h], axis=-1, keepdims=True)
            l2 = jnp.sum(l_ref[2 * h + 1], axis=-1, keepdims=True)
            o = acc_ref[2 * h] / l1 - lam * (acc_ref[2 * h + 1] / l2)
            o_ref[:, h * 2 * HEAD_DIM:(h + 1) * 2 * HEAD_DIM] = _finish_diff(o, subln_ref, lam_init).astype(BF16)


def _dense_d_bounded_kernel(q_ref, kl_ref, vl_ref, kx_ref, vx_ref, lam_ref, subln_ref, o_ref, qs_ref, acc_ref, *,
                            nk_lat, lam_init):
    kt = pl.program_id(2)
    n_sc = 2 * D_HEADS
    vw = 2 * HEAD_DIM

    @pl.when(kt == 0)
    def _():
        q = q_ref[...]
        for idx in range(n_sc):
            qs_ref[idx] = q[:, idx * HEAD_DIM:(idx + 1) * HEAD_DIM]
        acc_ref[...] = jnp.zeros(acc_ref.shape, F32)

    def step(k, v):
        n = k.shape[0]
        chunk = min(n, 2 * DENSE_KEY_CHUNK)
        lane = lax.broadcasted_iota(jnp.int32, (chunk, VT_PAD), 1)
        ones_cols = jnp.where(lane == 0, 1.0, 0.0).astype(BF16)
        for h in range(D_HEADS):
            upd = [None, None]
            for k0 in range(0, n, chunk):
                ks = slice(k0, k0 + chunk)
                v1 = jnp.concatenate([v[ks, h * vw:(h + 1) * vw], ones_cols], axis=1)
                for c in range(2):
                    idx = 2 * h + c
                    p_t = jnp.exp2(_dot_nt(k[ks, idx * HEAD_DIM:(idx + 1) * HEAD_DIM], qs_ref[idx])).astype(BF16)
                    term = lax.dot_general(v1, p_t, (((0,), (0,)), ((), ())), preferred_element_type=F32)
                    upd[c] = term if upd[c] is None else upd[c] + term
            for c in range(2):
                acc_ref[2 * h + c] += upd[c]

    @pl.when(kt < nk_lat)
    def _():
        step(kl_ref[...], vl_ref[...])

    @pl.when(kt == nk_lat)
    def _():
        step(kx_ref[...], vx_ref[...])
        lam = _diff_lambda(lam_ref, lam_init)
        for h in range(D_HEADS):
            a1, a2 = acc_ref[2 * h], acc_ref[2 * h + 1]
            o_t = a1[:vw] * (1.0 / a1[vw:vw + 1]) - lam * (a2[:vw] * (1.0 / a2[vw:vw + 1]))
            o_ref[:, h * vw:(h + 1) * vw] = _finish_diff(jnp.transpose(o_t), subln_ref, lam_init).astype(BF16)


def _dense_diff(qk_lat, rest_lat, qk_ctx, rest_ctx, lam_d, subln, lam_init, tq, tk, bounded):
    bsz, seq, _ = qk_lat.shape
    ctx_len = qk_ctx.shape[1]
    nk_lat = seq // tk
    width = D_HEADS * 2 * HEAD_DIM
    kcol = QK_OFF["dk"] // width
    vcol = REST_OFF["dv"] // width
    if bounded:
        body = functools.partial(_dense_d_bounded_kernel, nk_lat=nk_lat, lam_init=lam_init)
        scratch = [pltpu.VMEM((2 * D_HEADS, tq, HEAD_DIM), BF16),
                   pltpu.VMEM((2 * D_HEADS, 2 * HEAD_DIM + VT_PAD, tq), F32)]
    else:
        body = functools.partial(_dense_d_kernel, nk_lat=nk_lat, lam_init=lam_init)
        scratch = [pltpu.VMEM((2 * D_HEADS, tq, HEAD_DIM), BF16),
                   pltpu.VMEM((2 * D_HEADS, tq, LANES), F32),
                   pltpu.VMEM((2 * D_HEADS, tq, 2 * HEAD_DIM), F32),
                   pltpu.VMEM((2 * D_HEADS, tq, 1), F32)]
    return pl.pallas_call(
        body,
        grid=(bsz, seq // tq, nk_lat + 1),
        in_specs=[pl.BlockSpec((None, tq, width), lambda b, i, kt: (b, i, QK_OFF["dq"] // width)),
                  pl.BlockSpec((None, tk, width), lambda b, i, kt: (b, jnp.minimum(kt, nk_lat - 1), kcol)),
                  pl.BlockSpec((None, tk, width), lambda b, i, kt: (b, jnp.minimum(kt, nk_lat - 1), vcol)),
                  pl.BlockSpec((None, ctx_len, width), lambda b, i, kt: (b, 0, kcol)),
                  pl.BlockSpec((None, ctx_len, width), lambda b, i, kt: (b, 0, vcol)),
                  pl.BlockSpec((4, HEAD_DIM), lambda b, i, kt: (0, 0)),
                  pl.BlockSpec((1, 2 * HEAD_DIM), lambda b, i, kt: (0, 0))],
        out_specs=pl.BlockSpec((None, tq, BRANCH_W), lambda b, i, kt: (b, i, 0)),
        out_shape=jax.ShapeDtypeStruct((bsz, seq, BRANCH_W), BF16),
        scratch_shapes=scratch,
        compiler_params=_cparams(("parallel", "parallel", "arbitrary")),
        name="mixer_d_diff_bounded" if bounded else "mixer_d_diff",
    )(qk_lat, qk_lat, rest_lat, qk_ctx, rest_ctx, lam_d, subln.reshape(1, 2 * HEAD_DIM))


def _ctx_kernel(sink_ref, qk_ref, v_ref, lam_ref, subln_ref, o_ref, *, lam_init):
    qk = qk_ref[...]
    vals = v_ref[...]
    ctx_len = qk.shape[0]
    v_off = {name: REST_OFF[name] - REST_OFF[_FIRST_VALUE] for name in ("av", "cv", "bv", "dv")}

    def cols(name, start, width):
        c0 = QK_OFF[name] + start
        return qk[:, c0:c0 + width]

    def gqa(qname, kname, vname, out_off, n_kv, with_sink):
        g = 8 // n_kv
        for kv in range(n_kv):
            qs = _stack_heads(cols(qname, 0, 8 * HEAD_DIM), kv * g, g)
            s = _dot_nt(qs, cols(kname, kv * HEAD_DIM, HEAD_DIM))
            extra = None
            if with_sink:
                extra = jnp.concatenate([jnp.full((ctx_len, 1), sink_ref[kv * g + h], F32) for h in range(g)], axis=0)
            p, l = _softmax_rows(s, extra)
            vv = vals[:, v_off[vname] + kv * HEAD_DIM:v_off[vname] + (kv + 1) * HEAD_DIM]
            o = _dot(p.astype(BF16), vv) / l
            for h in range(g):
                c0 = out_off + (kv * g + h) * HEAD_DIM
                o_ref[:, c0:c0 + HEAD_DIM] = o[h * ctx_len:(h + 1) * ctx_len].astype(BF16)

    gqa("aq", "ak", "av", 0 * BRANCH_W, A_KV, True)
    gqa("bq", "bk", "bv", 1 * BRANCH_W, B_HEADS, False)
    gqa("cq", "ck", "cv", 2 * BRANCH_W, C_KV, False)

    lam = _diff_lambda(lam_ref, lam_init)
    for h in range(D_HEADS):
        base = h * 2 * HEAD_DIM
        p1, l1 = _softmax_rows(_dot_nt(cols("dq", base, HEAD_DIM), cols("dk", base, HEAD_DIM)))
        p2, l2 = _softmax_rows(_dot_nt(cols("dq", base + HEAD_DIM, HEAD_DIM), cols("dk", base + HEAD_DIM, HEAD_DIM)))
        pd = p1 / l1 - lam * (p2 / l2)
        o = _dot(pd.astype(BF16), vals[:, v_off["dv"] + base:v_off["dv"] + base + 2 * HEAD_DIM])
        c0 = 3 * BRANCH_W + base
        o_ref[:, c0:c0 + 2 * HEAD_DIM] = _finish_diff(o, subln_ref, lam_init).astype(BF16)


def _ctx_attention(qk_ctx, v_ctx, sink2, lam_d, subln, lam_init):
    bsz, ctx_len, _ = qk_ctx.shape
    return pl.pallas_call(
        functools.partial(_ctx_kernel, lam_init=lam_init),
        grid=(bsz,),
        in_specs=[pl.BlockSpec(memory_space=pltpu.SMEM),
                  pl.BlockSpec((None, ctx_len, QK_COLS), lambda b: (b, 0, 0)),
                  pl.BlockSpec((None, ctx_len, v_ctx.shape[2]), lambda b: (b, 0, 0)),
                  pl.BlockSpec((4, HEAD_DIM), lambda b: (0, 0)),
                  pl.BlockSpec((1, 2 * HEAD_DIM), lambda b: (0, 0))],
        out_specs=pl.BlockSpec((None, ctx_len, N_BRANCH * BRANCH_W), lambda b: (b, 0, 0)),
        out_shape=jax.ShapeDtypeStruct((bsz, ctx_len, N_BRANCH * BRANCH_W), BF16),
        compiler_params=_cparams(("parallel",)),
        name="ctx_attention",
    )(sink2, qk_ctx, v_ctx, lam_d, subln.reshape(1, 2 * HEAD_DIM))


def _merge_kernel(x_ref, mod_ref, ya_ref, yb_ref, yc_ref, yd_ref, gp_ref, mg_ref, wbr_ref, wout_ref, o_ref):
    gate = mod_ref[...][:, 2 * D_MODEL:]
    merged = None
    for n, y_ref in enumerate((ya_ref, yb_ref, yc_ref, yd_ref)):
        yg = (y_ref[...].astype(F32) * gp_ref[:, n * BRANCH_W:(n + 1) * BRANCH_W].astype(F32)).astype(BF16)
        term = mg_ref[:, n * D_MODEL:(n + 1) * D_MODEL].astype(F32) * _dot(yg, wbr_ref[n])
        merged = term if merged is None else merged + term
    o_ref[...] = x_ref[...] + gate * _dot(merged.astype(BF16), wout_ref[...])


def _merge(x2, mod3, ys, y_cols, rest, w_br, w_out, tm, row_of_tile):
    rows = x2.shape[0]
    y_specs = [pl.BlockSpec((tm, BRANCH_W), (lambda i, c=c: (i, c))) for c in y_cols]
    return pl.pallas_call(
        _merge_kernel,
        grid=(rows // tm,),
        in_specs=[pl.BlockSpec((tm, D_MODEL), lambda i: (i, 0)),
                  pl.BlockSpec((None, 1, 3 * D_MODEL), lambda i: (row_of_tile(i), 0, 0))]
                 + y_specs
                 + [pl.BlockSpec((tm, N_BRANCH * BRANCH_W), lambda i: (i, REST_OFF["ag"] // (N_BRANCH * BRANCH_W))),
                    pl.BlockSpec((tm, N_BRANCH * D_MODEL), lambda i: (i, 0)),
                    pl.BlockSpec((N_BRANCH, BRANCH_W, D_MODEL), lambda i: (0, 0, 0)),
                    pl.BlockSpec((D_MODEL, D_MODEL), lambda i: (0, 0))],
        out_specs=pl.BlockSpec((tm, D_MODEL), lambda i: (i, 0)),
        out_shape=jax.ShapeDtypeStruct((rows, D_MODEL), F32),
        compiler_params=_cparams(("parallel",)),
        name="gated_merge",
    )(x2, mod3, *ys, rest, rest, w_br, w_out)


def _rope_tables(seq):
    t = np.arange(seq, dtype=np.int32)
    pos = np.stack([t // GRID_W, t % GRID_W], axis=-1).astype(np.float32)
    n_freq = HEAD_DIM // 4
    freqs = (np.float32(ROPE_THETA) ** (-np.arange(n_freq, dtype=np.float32) / np.float32(n_freq))).astype(np.float32)
    ang = (pos[:, :, None] * freqs[None, None, :]).astype(np.float32)
    ang = np.concatenate([ang, ang], axis=-1).reshape(seq, HEAD_DIM)
    sign = np.where((np.arange(HEAD_DIM) % 32) < 16, -1.0, 1.0).astype(np.float32)
    reps = NORM_GROUP // HEAD_DIM
    cos = np.tile(np.cos(ang).astype(np.float32), (1, reps))
    sin = np.tile(np.sin(ang).astype(np.float32) * sign, (1, reps))
    return jnp.asarray(cos), jnp.asarray(sin)


def _regroup_cols(w, order, total):
    parts = [w[:, _ORIG[n][0]:_ORIG[n][1]] for n in order]
    used = sum(p.shape[1] for p in parts)
    if total > used:
        parts.append(jnp.zeros((w.shape[0], total - used), w.dtype))
    return jnp.concatenate(parts, axis=1)


def _qk_gain_row(g):
    parts = []
    for name in _QK_ORDER:
        mixer = "abcd".index(name[0])
        is_q = name[1] == "q"
        width = _ORIG[name][1] - _ORIG[name][0]
        gain = g[mixer, 0] * (QK_SCALE * LOG2E) if is_q else g[mixer, 1]
        parts.append(jnp.tile(gain, width // HEAD_DIM))
    parts.append(jnp.ones((QK_COLS - _QK_USED,), g.dtype))
    return jnp.concatenate(parts).reshape(1, QK_COLS).astype(F32)


def kernel(x, c, ctx, c_ctx, norm_w, w_ada, b_ada, w_in, qk_gain, sink_a, rpb_b, lam_d, subln_d, w_br, w_out):
    bsz, seq, _ = x.shape
    ctx_len = ctx.shape[1]
    depth = w_ada.shape[0]
    assert seq % (2 * A_WINDOW) == 0 and ctx_len % LANES == 0 and bsz <= 6

    tm_lat = 2048
    tm_rest = 4096
    tiles_per_batch = seq // tm_lat
    tk_dense = 2048
    tq_bounded, tq_rowmax = 512, 256
    assert seq % tk_dense == 0
    tm_merge = 512
    merge_tiles_per_batch = seq // tm_merge

    cvec = jnp.concatenate([c, c_ctx[None, :], jnp.zeros((8 - bsz - 1, D_MODEL), F32)], axis=0)
    mod_all = _ada(cvec, w_ada, b_ada)
    cos, sin = _rope_tables(seq)
    ones_bd = jnp.asarray(np.kron(np.eye(NORM_GROUP // HEAD_DIM), np.ones((HEAD_DIM, HEAD_DIM))), BF16)

    x2 = x.reshape(bsz * seq, D_MODEL)
    c2 = ctx.reshape(bsz * ctx_len, D_MODEL)
    for l in range(depth):
        need_ctx = l < depth - 1
        lam_init = 0.8 - 0.6 * math.exp(-0.3 * l)
        mod3 = mod_all[l].reshape(8, 1, 3 * D_MODEL)
        w_qk = _regroup_cols(w_in[l], _QK_ORDER, QK_COLS).astype(BF16)
        w_rest = _regroup_cols(w_in[l], _REST_ORDER, REST_COLS).astype(BF16)
        gain_row = _qk_gain_row(qk_gain[l])

        hx = _prenorm(x2, norm_w[l], mod3, tm_lat, lambda i: i // tiles_per_batch)
        hc = _prenorm(c2, norm_w[l], mod3, bsz * ctx_len, lambda i: bsz)
        qk_lat = _proj_qk(hx, w_qk, gain_row, ones_bd, tm_lat, cos, sin, tiles_per_batch).reshape(bsz, seq, QK_COLS)
        qk_ctx = _proj_qk(hc, w_qk, gain_row, ones_bd, bsz * ctx_len).reshape(bsz, ctx_len, QK_COLS)
        rest_lat = _proj_rest(hx, w_rest, tm_rest).reshape(bsz, seq, REST_COLS)
        rest_ctx = _proj_rest(hc, w_rest, bsz * ctx_len).reshape(bsz, ctx_len, REST_COLS)

        sink2 = sink_a[l] * LOG2E
        rpb2 = rpb_b[l] * LOG2E
        tq_nbr = 2 * A_WINDOW
        proj = (qk_lat, rest_lat, qk_ctx, rest_ctx)

        def qk_bound(g):
            return HEAD_DIM * QK_SCALE * LOG2E * jnp.max(jnp.abs(g[0])) * jnp.max(jnp.abs(g[1])) * 1.02

        y_a = lax.cond(qk_bound(qk_gain[l, 0]) <= LOGIT_BOUND,
                       lambda: _window_attention(*proj, sink2, True),
                       lambda: _window_attention(*proj, sink2, False))
        y_b = lax.cond(qk_bound(qk_gain[l, 1]) + jnp.max(jnp.abs(rpb_b[l])) * LOG2E <= LOGIT_BOUND,
                       lambda: _neighborhood_attention(*proj, rpb2, tq_nbr, True),
                       lambda: _neighborhood_attention(*proj, rpb2, tq_nbr, False))
        y_c = lax.cond(qk_bound(qk_gain[l, 2]) <= LOGIT_BOUND,
                       lambda: _dense_gqa(*proj, 2 * tq_bounded, tk_dense, True),
                       lambda: _dense_gqa(*proj, tq_rowmax, tk_dense, False))
        y_d = lax.cond(qk_bound(qk_gain[l, 3]) <= LOGIT_BOUND,
                       lambda: _dense_diff(*proj, lam_d[l], subln_d[l], lam_init, 2 * tq_bounded, tk_dense, True),
                       lambda: _dense_diff(*proj, lam_d[l], subln_d[l], lam_init, tq_rowmax, tk_dense, False))

        w_br_l = w_br[l].astype(BF16)
        w_out_l = w_out[l].astype(BF16)
        ys = [y.reshape(bsz * seq, BRANCH_W) for y in (y_a, y_b, y_c, y_d)]
        x_new = _merge(x2, mod3, ys, (0, 0, 0, 0), rest_lat.reshape(bsz * seq, REST_COLS), w_br_l, w_out_l,
                       tm_merge, lambda i: i // merge_tiles_per_batch)
        if need_ctx:
            y_ctx = _ctx_attention(qk_ctx, rest_ctx[:, :, REST_OFF[_FIRST_VALUE]:], sink2, lam_d[l], subln_d[l],
                                   lam_init)
            y_ctx2 = y_ctx.reshape(bsz * ctx_len, N_BRANCH * BRANCH_W)
            c2 = _merge(c2, mod3, [y_ctx2] * 4, (0, 1, 2, 3), rest_ctx.reshape(bsz * ctx_len, REST_COLS),
                        w_br_l, w_out_l, bsz * ctx_len, lambda i: bsz)
        x2 = x_new
    return x2.reshape(bsz, seq, D_MODEL)
```

```python
import functools
import math

import numpy as np
import jax
import jax.numpy as jnp
from jax import lax
from jax.experimental import pallas as pl
from jax.experimental.pallas import tpu as pltpu

F32 = jnp.float32
BF16 = jnp.bfloat16

D_MODEL = 1024
GRID_W = 64
HEAD_DIM = 64
BRANCH_W = 512
N_BRANCH = 4
A_HEADS, A_KV, A_WINDOW = 8, 2, 128
B_HEADS, NB_ROWS, NB_COLS = 8, 8, 16
C_HEADS, C_KV = 8, 2
D_HEADS = 4
ROPE_THETA = 10000.0
EPS = 1e-6
NEG_INF = -1e30
QK_SCALE = HEAD_DIM ** -0.5
LOG2E = math.log2(math.e)
LOGIT_BOUND = 60.0

V7X_VMEM_LIMIT_BYTES = 56 * 1024 * 1024
LANES = 128

_ORIG = dict(aq=(0, 512), ak=(512, 640), av=(640, 768), ag=(768, 1280),
             bq=(1280, 1792), bk=(1792, 2304), bv=(2304, 2816), bg=(2816, 3328),
             cq=(3328, 3840), ck=(3840, 3968), cv=(3968, 4096), cg=(4096, 4608),
             dq=(4608, 5120), dk=(5120, 5632), dv=(5632, 6144), dg=(6144, 6656),
             mg=(6656, 10752))
_QK_ORDER = ("aq", "cq", "dq", "bq", "bk", "dk", "ak", "ck")
_REST_ORDER = ("mg", "ag", "bg", "cg", "dg", "dv", "bv", "av", "cv")
_FIRST_VALUE = "dv"


def _offsets(order):
    off, out = 0, {}
    for name in order:
        lo, hi = _ORIG[name]
        out[name] = off
        off += hi - lo
    return out, off


PROJ_TN = 512
PROJ_ROW_CHUNK = 512
REST_ROW_CHUNK = 1024
VT_PAD = 16
DENSE_KEY_CHUNK = 512
NORM_GROUP = 256
QK_OFF, _QK_USED = _offsets(_QK_ORDER)
REST_OFF, _REST_USED = _offsets(_REST_ORDER)
QK_COLS = -(-_QK_USED // PROJ_TN) * PROJ_TN
REST_COLS = -(-_REST_USED // PROJ_TN) * PROJ_TN
_ROPE_LO_END = QK_OFF["bq"] // PROJ_TN
_ROPE_HI_START = QK_OFF["dk"] // PROJ_TN
_MG_BLOCKS = (REST_OFF["ag"]) // PROJ_TN
_GP_BLOCKS_END = REST_OFF[_FIRST_VALUE] // PROJ_TN
assert QK_OFF["bq"] % PROJ_TN == 0 and QK_OFF["dk"] % PROJ_TN == 0
assert REST_OFF["ag"] % PROJ_TN == 0 and REST_OFF[_FIRST_VALUE] % PROJ_TN == 0


def _cparams(sem):
    return pltpu.CompilerParams(dimension_semantics=sem, vmem_limit_bytes=V7X_VMEM_LIMIT_BYTES)


def _sigmoid(x):
    return 0.5 * jnp.tanh(0.5 * x) + 0.5


def _dot_nt(a, b):
    return lax.dot_general(a, b, (((1,), (1,)), ((), ())), preferred_element_type=F32)


def _dot(a, b):
    return jnp.dot(a, b, preferred_element_type=F32)


def _ada_kernel(c_ref, w_ref, b_ref, o_ref):
    c = c_ref[...]
    o_ref[...] = _dot(c * _sigmoid(c), w_ref[...]) + b_ref[...]


def _ada(cvec, w_ada, b_ada):
    depth = w_ada.shape[0]
    tn = 512
    return pl.pallas_call(
        _ada_kernel,
        grid=(depth, 3 * D_MODEL // tn),
        in_specs=[pl.BlockSpec((8, D_MODEL), lambda l, j: (0, 0)),
                  pl.BlockSpec((None, D_MODEL, tn), lambda l, j: (l, 0, j)),
                  pl.BlockSpec((None, 1, tn), lambda l, j: (l, 0, j))],
        out_specs=pl.BlockSpec((None, 8, tn), lambda l, j: (l, 0, j)),
        out_shape=jax.ShapeDtypeStruct((depth, 8, 3 * D_MODEL), F32),
        compiler_params=_cparams(("parallel", "parallel")),
        name="ada_mod",
    )(cvec, w_ada, b_ada.reshape(depth, 1, 3 * D_MODEL))


def _prenorm_kernel(x_ref, nw_ref, mod_ref, o_ref):
    x = x_ref[...]
    mod = mod_ref[...]
    shift, scale = mod[:, :D_MODEL], mod[:, D_MODEL:2 * D_MODEL]
    y = x * lax.rsqrt(jnp.mean(x * x, axis=-1, keepdims=True) + EPS) * nw_ref[...]
    o_ref[...] = (y * (1.0 + scale) + shift).astype(BF16)


def _prenorm(x2, norm_w, mod3, tm, row_of_tile):
    rows = x2.shape[0]
    return pl.pallas_call(
        _prenorm_kernel,
        grid=(rows // tm,),
        in_specs=[pl.BlockSpec((tm, D_MODEL), lambda i: (i, 0)),
                  pl.BlockSpec((1, D_MODEL), lambda i: (0, 0)),
                  pl.BlockSpec((None, 1, 3 * D_MODEL), lambda i: (row_of_tile(i), 0, 0))],
        out_specs=pl.BlockSpec((tm, D_MODEL), lambda i: (i, 0)),
        out_shape=jax.ShapeDtypeStruct((rows, D_MODEL), BF16),
        compiler_params=_cparams(("parallel",)),
        name="prenorm",
    )(x2, norm_w.reshape(1, D_MODEL), mod3)


def _rot_half_unsigned(n):
    lane = lax.broadcasted_iota(jnp.int32, n.shape, 1)
    return jnp.where((lane & 31) < 16, pltpu.roll(n, LANES - 16, 1), pltpu.roll(n, 16, 1))


def _proj_qk_kernel(*refs, rope):
    if rope:
        hx_ref, w_ref, gain_ref, ones_ref, cos_ref, sin_ref, o_ref = refs
    else:
        hx_ref, w_ref, gain_ref, ones_ref, o_ref = refs
    w = w_ref[...]
    ones_bd = ones_ref[...]
    rows = hx_ref.shape[0]
    chunk = min(rows, PROJ_ROW_CHUNK)
    if rope:
        j = pl.program_id(1)
        use = jnp.logical_or(j < _ROPE_LO_END, j >= _ROPE_HI_START)
    for r0 in range(0, rows, chunk):
        rs = slice(r0, r0 + chunk)
        acc_all = _dot(hx_ref[rs, :], w)
        if rope:
            cos = jnp.where(use, cos_ref[rs, :], 1.0)
            sin = jnp.where(use, sin_ref[rs, :], 0.0)
        for c in range(PROJ_TN // NORM_GROUP):
            sl = slice(c * NORM_GROUP, (c + 1) * NORM_GROUP)
            acc = acc_all[:, sl]
            ss = _dot((acc * acc).astype(BF16), ones_bd)
            n = acc * lax.rsqrt(ss * (1.0 / HEAD_DIM) + EPS) * gain_ref[:, sl]
            if rope:
                rot = jnp.concatenate([_rot_half_unsigned(n[:, :LANES]), _rot_half_unsigned(n[:, LANES:])], axis=1)
                n = n * cos + rot * sin
            o_ref[rs, sl] = n.astype(BF16)


def _proj_qk(hx, w_qk, gain_row, ones_bd, tm, cos=None, sin=None, tiles_per_batch=None):
    rows = hx.shape[0]
    rope = cos is not None
    in_specs = [pl.BlockSpec((tm, D_MODEL), lambda i, j: (i, 0)),
                pl.BlockSpec((D_MODEL, PROJ_TN), lambda i, j: (0, j)),
                pl.BlockSpec((1, PROJ_TN), lambda i, j: (0, j)),
                pl.BlockSpec((NORM_GROUP, NORM_GROUP), lambda i, j: (0, 0))]
    args = [hx, w_qk, gain_row, ones_bd]
    if rope:
        in_specs += [pl.BlockSpec((tm, NORM_GROUP), lambda i, j: (i % tiles_per_batch, 0))] * 2
        args += [cos, sin]
    return pl.pallas_call(
        functools.partial(_proj_qk_kernel, rope=rope),
        grid=(rows // tm, QK_COLS // PROJ_TN),
        in_specs=in_specs,
        out_specs=pl.BlockSpec((tm, PROJ_TN), lambda i, j: (i, j)),
        out_shape=jax.ShapeDtypeStruct((rows, QK_COLS), BF16),
        compiler_params=_cparams(("parallel", "arbitrary")),
        name="proj_qk_rope" if rope else "proj_qk",
    )(*args)


def _proj_rest_kernel(hx_ref, w_ref, o_ref):
    j = pl.program_id(1)
    w = w_ref[...]
    rows = hx_ref.shape[0]
    chunk = min(rows, REST_ROW_CHUNK)
    for r0 in range(0, rows, chunk):
        acc = _dot(hx_ref[r0:r0 + chunk, :], w)
        sg = _sigmoid(acc)
        o_ref[r0:r0 + chunk, :] = jnp.where(j < _MG_BLOCKS, sg,
                                            jnp.where(j < _GP_BLOCKS_END, acc * sg, acc)).astype(BF16)


def _proj_rest(hx, w_rest, tm):
    rows = hx.shape[0]
    return pl.pallas_call(
        _proj_rest_kernel,
        grid=(rows // tm, REST_COLS // PROJ_TN),
        in_specs=[pl.BlockSpec((tm, D_MODEL), lambda i, j: (i, 0)),
                  pl.BlockSpec((D_MODEL, PROJ_TN), lambda i, j: (0, j))],
        out_specs=pl.BlockSpec((tm, PROJ_TN), lambda i, j: (i, j)),
        out_shape=jax.ShapeDtypeStruct((rows, REST_COLS), BF16),
        compiler_params=_cparams(("parallel", "arbitrary")),
        name="proj_rest",
    )(hx, w_rest)


def _stack_heads(q, first_head, n):
    return jnp.concatenate([q[:, (first_head + g) * HEAD_DIM:(first_head + g + 1) * HEAD_DIM] for g in range(n)],
                           axis=0)


def _softmax_rows(s, extra=None):
    m = jnp.max(s, axis=-1, keepdims=True)
    if extra is not None:
        m = jnp.maximum(m, extra)
    p = jnp.exp2(s - m)
    l = jnp.sum(p, axis=-1, keepdims=True)
    if extra is not None:
        l = l + jnp.exp2(extra - m)
    return p, l


def _with_ones_column(v, pad=HEAD_DIM):
    lane = lax.broadcasted_iota(jnp.int32, (v.shape[0], pad), 1)
    return jnp.concatenate([v, jnp.where(lane == 0, 1.0, 0.0).astype(v.dtype)], axis=1)


def _lane_partial_sums(p):
    part = p[:, :LANES]
    for c in range(1, p.shape[1] // LANES):
        part = part + p[:, c * LANES:(c + 1) * LANES]
    return part


def _diff_lambda(lam_ref, lam_init):
    lf = lam_ref[...]
    a = jnp.sum(lf[0:1] * lf[1:2], axis=-1, keepdims=True)
    b = jnp.sum(lf[2:3] * lf[3:4], axis=-1, keepdims=True)
    return jnp.exp(a) - jnp.exp(b) + lam_init


def _finish_diff(o, subln_ref, lam_init):
    y = o * lax.rsqrt(jnp.mean(o * o, axis=-1, keepdims=True) + EPS) * subln_ref[...]
    return y * (1.0 - lam_init)


def _win_kernel(sink_ref, q_ref, kp_ref, kc_ref, kn_ref, kx_ref, vp_ref, vc_ref, vn_ref, vx_ref, o_ref, *,
                tq, seq):
    i = pl.program_id(1)
    g = A_HEADS // A_KV
    q = q_ref[...]
    k_all = jnp.concatenate([kp_ref[...], kc_ref[...], kn_ref[...], kx_ref[...]], axis=0)
    v_all = jnp.concatenate([vp_ref[...], vc_ref[...], vn_ref[...], vx_ref[...]], axis=0)
    nw = 2 * tq
    t = lax.broadcasted_iota(jnp.int32, (g * tq, nw), 0) & (tq - 1)
    j = lax.broadcasted_iota(jnp.int32, (g * tq, nw), 1)
    rel = j - tq // 2 - t
    kpos = i * tq - tq // 2 + j
    bad = jnp.where(jnp.abs(rel) > A_WINDOW, 1, 0) + jnp.where(kpos < 0, 1, 0) + jnp.where(kpos >= seq, 1, 0)
    for kv in range(A_KV):
        qs = _stack_heads(q, kv * g, g)
        kk = k_all[:, kv * HEAD_DIM:(kv + 1) * HEAD_DIM]
        vv = v_all[:, kv * HEAD_DIM:(kv + 1) * HEAD_DIM]
        s = _dot_nt(qs, kk)
        s_win = jnp.where(bad > 0, NEG_INF, s[:, :nw])
        s_ctx = s[:, nw:]
        sink = jnp.concatenate([jnp.full((tq, 1), sink_ref[kv * g + h], F32) for h in range(g)], axis=0)
        m = jnp.maximum(jnp.maximum(jnp.max(s_win, axis=-1, keepdims=True),
                                    jnp.max(s_ctx, axis=-1, keepdims=True)), sink)
        pw = jnp.exp2(s_win - m)
        pc = jnp.exp2(s_ctx - m)
        l = jnp.sum(pw, axis=-1, keepdims=True) + jnp.sum(pc, axis=-1, keepdims=True) + jnp.exp2(sink - m)
        o = (_dot(pw.astype(BF16), vv[:nw]) + _dot(pc.astype(BF16), vv[nw:])) / l
        for h in range(g):
            c0 = (kv * g + h) * HEAD_DIM
            o_ref[:, c0:c0 + HEAD_DIM] = o[h * tq:(h + 1) * tq].astype(BF16)


def _win_bounded_kernel(sink_ref, q_ref, kp_ref, kc_ref, kn_ref, kx_ref, vp_ref, vc_ref, vn_ref, vx_ref, mask_ref,
                        o_ref, *, tq):
    g = A_HEADS // A_KV
    q = q_ref[...]
    mask_t = mask_ref[...]
    sink = sink_ref[...]
    k_all = jnp.concatenate([kp_ref[...], kc_ref[...], kn_ref[...], kx_ref[...]], axis=0)
    v_all = jnp.concatenate([vp_ref[...], vc_ref[...], vn_ref[...], vx_ref[...]], axis=0)
    for kv in range(A_KV):
        sl = slice(kv * HEAD_DIM, (kv + 1) * HEAD_DIM)
        qs = _stack_heads(q, kv * g, g)
        stab = [jnp.maximum(sink[:, kv * g + h:kv * g + h + 1], 0.0) for h in range(g)]
        shift_t = jnp.concatenate([mask_t - stab[h] for h in range(g)], axis=1)
        p_t = jnp.exp2(_dot_nt(k_all[:, sl], qs) + shift_t).astype(BF16)
        acc = lax.dot_general(_with_ones_column(v_all[:, sl], VT_PAD), p_t, (((0,), (0,)), ((), ())),
                              preferred_element_type=F32)
        for h in range(g):
            a = acc[:, h * tq:(h + 1) * tq]
            l = a[HEAD_DIM:HEAD_DIM + 1] + jnp.exp2(sink[:, kv * g + h:kv * g + h + 1] - stab[h])
            c0 = (kv * g + h) * HEAD_DIM
            o_ref[:, c0:c0 + HEAD_DIM] = jnp.transpose(a[:HEAD_DIM] * (1.0 / l)).astype(BF16)


def _window_mask_tables(seq, tq, ctx_len):
    nt = seq // tq
    tabs = []
    for i0 in (0, 1, nt - 1):
        t = np.arange(tq)[None, :]
        j = np.arange(2 * tq)[:, None]
        kpos = i0 * tq - tq // 2 + j
        ok = (np.abs(j - tq // 2 - t) <= A_WINDOW) & (kpos >= 0) & (kpos < seq)
        win = np.where(ok, 0.0, NEG_INF).astype(np.float32)
        tabs.append(np.concatenate([win, np.zeros((ctx_len, tq), np.float32)], axis=0))
    return jnp.asarray(np.stack(tabs, axis=0))


def _window_attention(qk_lat, rest_lat, qk_ctx, rest_ctx, sink2, bounded):
    bsz, seq, _ = qk_lat.shape
    ctx_len = qk_ctx.shape[1]
    tq = 2 * A_WINDOW
    nt = seq // tq
    half = tq // 2
    kcol = QK_OFF["ak"] // LANES
    vcol = REST_OFF["av"] // LANES
    n_half = seq // half

    def kv_specs(col):
        return [pl.BlockSpec((None, half, LANES), lambda b, i: (b, jnp.maximum(2 * i - 1, 0), col)),
                pl.BlockSpec((None, tq, LANES), lambda b, i: (b, i, col)),
                pl.BlockSpec((None, half, LANES), lambda b, i: (b, jnp.minimum(2 * i + 2, n_half - 1), col)),
                pl.BlockSpec((None, ctx_len, LANES), lambda b, i: (b, 0, col))]

    def tile_variant(i):
        return jnp.where(i == 0, 0, jnp.where(i == nt - 1, 2, 1))

    q_spec = pl.BlockSpec((None, tq, A_HEADS * HEAD_DIM), lambda b, i: (b, i, QK_OFF["aq"] // 512))
    kv_args = (qk_lat, qk_lat, qk_lat, qk_ctx, rest_lat, rest_lat, rest_lat, rest_ctx)
    if bounded:
        body = functools.partial(_win_bounded_kernel, tq=tq)
        in_specs = ([pl.BlockSpec((1, A_HEADS), lambda b, i: (0, 0)), q_spec] + kv_specs(kcol) + kv_specs(vcol)
                    + [pl.BlockSpec((None, 2 * tq + ctx_len, tq), lambda b, i: (tile_variant(i), 0, 0))])
        args = (sink2.reshape(1, A_HEADS), qk_lat) + kv_args + (_window_mask_tables(seq, tq, ctx_len),)
    else:
        body = functools.partial(_win_kernel, tq=tq, seq=seq)
        in_specs = [pl.BlockSpec(memory_space=pltpu.SMEM), q_spec] + kv_specs(kcol) + kv_specs(vcol)
        args = (sink2, qk_lat) + kv_args
    return pl.pallas_call(
        body,
        grid=(bsz, nt),
        in_specs=in_specs,
        out_specs=pl.BlockSpec((None, tq, BRANCH_W), lambda b, i: (b, i, 0)),
        out_shape=jax.ShapeDtypeStruct((bsz, seq, BRANCH_W), BF16),
        compiler_params=_cparams(("parallel", "parallel")),
        name="mixer_a_window_bounded" if bounded else "mixer_a_window",
    )(*args)


def _nbr_kernel(q_ref, kp_ref, kc_ref, kn_ref, kx_ref, vp_ref, vc_ref, vn_ref, vx_ref, bias_ref, rmask_ref, o_ref,
                *, tq, bounded):
    q = q_ref[...]
    k_all = jnp.concatenate([kp_ref[...], kc_ref[...], kn_ref[...], kx_ref[...]], axis=0)
    v_all = jnp.concatenate([vp_ref[...], vc_ref[...], vn_ref[...], vx_ref[...]], axis=0)
    rmask = rmask_ref[...]
    nw = 3 * tq
    for h in range(B_HEADS):
        sl = slice(h * HEAD_DIM, (h + 1) * HEAD_DIM)
        s = _dot_nt(q[:, sl], k_all[:, sl])
        s = jnp.concatenate([s[:, :nw] + (bias_ref[h] + rmask), s[:, nw:]], axis=1)
        if not bounded:
            s = s - jnp.max(s, axis=-1, keepdims=True)
        acc = _dot(jnp.exp2(s).astype(BF16), _with_ones_column(v_all[:, sl]))
        o_ref[:, sl] = (acc[:, :HEAD_DIM] * (1.0 / acc[:, HEAD_DIM:HEAD_DIM + 1])).astype(BF16)


def _nbr_bias_tables(rpb, seq, tq):
    rows = seq // GRID_W
    nt = seq // tq
    kr = min(NB_ROWS, rows)
    qr = tq // GRID_W
    n_heads = rpb.shape[0]
    assert NB_ROWS - 1 - qr - (qr - 1) >= 0 and NB_ROWS - 1 - qr + 3 * qr <= 2 * NB_ROWS - 1
    pad = GRID_W - 1
    rpb_pad = jnp.pad(rpb.astype(F32), ((0, 0), (0, 0), (pad, pad)))
    toep = jnp.stack([rpb_pad[:, :, pad + NB_COLS - 1 - c:pad + NB_COLS - 1 - c + GRID_W] for c in range(GRID_W)],
                     axis=2)
    c = np.arange(GRID_W)
    cstart = np.clip(c - NB_COLS // 2, 0, GRID_W - NB_COLS)
    col_ok = (c[None, :] >= cstart[:, None]) & (c[None, :] < cstart[:, None] + NB_COLS)
    toep = jnp.where(col_ok[None, None], toep, NEG_INF)
    a0 = NB_ROWS - 1 - qr
    per_rl = [jnp.transpose(toep[:, a0 - rl:a0 - rl + 3 * qr], (0, 2, 1, 3)) for rl in range(qr)]
    base = jnp.stack(per_rl, axis=1).reshape(n_heads, tq, 3 * tq)
    row_masks = []
    for i0 in (0, 1, nt - 1):
        r = i0 * qr + np.arange(qr)
        r2 = (i0 - 1) * qr + np.arange(3 * qr)
        rstart = np.clip(r - kr // 2, 0, rows - kr)
        row_ok = ((r2[None, :] >= rstart[:, None]) & (r2[None, :] < rstart[:, None] + kr)
                  & (r2[None, :] >= 0) & (r2[None, :] < rows))
        full = np.broadcast_to(row_ok[:, None, :, None], (qr, GRID_W, 3 * qr, GRID_W)).reshape(tq, 3 * tq)
        row_masks.append(np.where(full, 0.0, NEG_INF).astype(np.float32))
    return base, jnp.asarray(np.stack(row_masks, axis=0))


def _neighborhood_attention(qk_lat, rest_lat, qk_ctx, rest_ctx, rpb2, tq, bounded):
    bsz, seq, _ = qk_lat.shape
    ctx_len = qk_ctx.shape[1]
    nt = seq // tq
    width = B_HEADS * HEAD_DIM
    bias, row_masks = _nbr_bias_tables(rpb2, seq, tq)

    def kv_specs(col):
        return [pl.BlockSpec((None, tq, width), lambda b, i: (b, jnp.maximum(i - 1, 0), col)),
                pl.BlockSpec((None, tq, width), lambda b, i: (b, i, col)),
                pl.BlockSpec((None, tq, width), lambda b, i: (b, jnp.minimum(i + 1, nt - 1), col)),
                pl.BlockSpec((None, ctx_len, width), lambda b, i: (b, 0, col))]

    def tile_variant(i):
        return jnp.where(i == 0, 0, jnp.where(i == nt - 1, 2, 1))

    return pl.pallas_call(
        functools.partial(_nbr_kernel, tq=tq, bounded=bounded),
        grid=(bsz, nt),
        in_specs=[pl.BlockSpec((None, tq, width), lambda b, i: (b, i, QK_OFF["bq"] // width))]
                 + kv_specs(QK_OFF["bk"] // width) + kv_specs(REST_OFF["bv"] // width)
                 + [pl.BlockSpec(bias.shape, lambda b, i: (0, 0, 0)),
                    pl.BlockSpec((None,) + row_masks.shape[1:], lambda b, i: (tile_variant(i), 0, 0))],
        out_specs=pl.BlockSpec((None, tq, BRANCH_W), lambda b, i: (b, i, 0)),
        out_shape=jax.ShapeDtypeStruct((bsz, seq, BRANCH_W), BF16),
        compiler_params=_cparams(("parallel", "parallel")),
        name="mixer_b_neighbourhood_bounded" if bounded else "mixer_b_neighbourhood",
    )(qk_lat, qk_lat, qk_lat, qk_lat, qk_ctx, rest_lat, rest_lat, rest_lat, rest_ctx, bias, row_masks)


def _flash_update(s, v, m_ref, l_ref, acc_ref, idx):
    m_prev = m_ref[idx]
    m_new = jnp.maximum(m_prev, jnp.max(s, axis=-1, keepdims=True))
    alpha = jnp.exp2(m_prev - m_new)
    p = jnp.exp2(s - m_new)
    if l_ref is not None:
        l_ref[idx] = alpha * l_ref[idx] + _lane_partial_sums(p)
    acc_ref[idx] = alpha * acc_ref[idx] + _dot(p.astype(BF16), v)
    m_ref[idx] = m_new


def _dense_c_kernel(q_ref, kl_ref, vl_ref, kx_ref, vx_ref, o_ref, qs_ref, acc_ref, m_ref, *, tq, nk_lat):
    kt = pl.program_id(2)
    g = C_HEADS // C_KV

    @pl.when(kt == 0)
    def _():
        q = q_ref[...]
        for kv in range(C_KV):
            qs_ref[kv] = _stack_heads(q, kv * g, g)
        acc_ref[...] = jnp.zeros(acc_ref.shape, F32)
        m_ref[...] = jnp.full(m_ref.shape, -jnp.inf, F32)

    def step(k, v):
        for kv in range(C_KV):
            sl = slice(kv * HEAD_DIM, (kv + 1) * HEAD_DIM)
            _flash_update(_dot_nt(qs_ref[kv], k[:, sl]), _with_ones_column(v[:, sl]), m_ref, None, acc_ref, kv)

    @pl.when(kt < nk_lat)
    def _():
        step(kl_ref[...], vl_ref[...])

    @pl.when(kt == nk_lat)
    def _():
        step(kx_ref[...], vx_ref[...])
        for kv in range(C_KV):
            acc = acc_ref[kv]
            o = acc[:, :HEAD_DIM] * (1.0 / acc[:, HEAD_DIM:HEAD_DIM + 1])
            for h in range(g):
                c0 = (kv * g + h) * HEAD_DIM
                o_ref[:, c0:c0 + HEAD_DIM] = o[h * tq:(h + 1) * tq].astype(BF16)


def _dense_c_bounded_kernel(q_ref, kl_ref, vl_ref, kx_ref, vx_ref, o_ref, qs_ref, acc_ref, *, tq, nk_lat):
    kt = pl.program_id(2)
    g = C_HEADS // C_KV

    @pl.when(kt == 0)
    def _():
        q = q_ref[...].astype(F32)
        for kv in range(C_KV):
            qs_ref[kv] = jnp.transpose(_stack_heads(q, kv * g, g)).astype(BF16)
        acc_ref[...] = jnp.zeros(acc_ref.shape, F32)

    def step(k, v):
        n = k.shape[0]
        chunk = min(n, DENSE_KEY_CHUNK)
        for kv in range(C_KV):
            sl = slice(kv * HEAD_DIM, (kv + 1) * HEAD_DIM)
            upd = None
            for k0 in range(0, n, chunk):
                ks = slice(k0, k0 + chunk)
                p_t = jnp.exp2(_dot(k[ks, sl], qs_ref[kv])).astype(BF16)
                term = lax.dot_general(_with_ones_column(v[ks, sl], VT_PAD), p_t, (((0,), (0,)), ((), ())),
                                       preferred_element_type=F32)
                upd = term if upd is None else upd + term
            acc_ref[kv] += upd

    @pl.when(kt < nk_lat)
    def _():
        step(kl_ref[...], vl_ref[...])

    @pl.when(kt == nk_lat)
    def _():
        step(kx_ref[...], vx_ref[...])
        for kv in range(C_KV):
            acc = acc_ref[kv]
            o = jnp.transpose(acc[:HEAD_DIM] * (1.0 / acc[HEAD_DIM:HEAD_DIM + 1]))
            for h in range(g):
                c0 = (kv * g + h) * HEAD_DIM
                o_ref[:, c0:c0 + HEAD_DIM] = o[h * tq:(h + 1) * tq].astype(BF16)


def _dense_gqa(qk_lat, rest_lat, qk_ctx, rest_ctx, tq, tk, bounded):
    bsz, seq, _ = qk_lat.shape
    ctx_len = qk_ctx.shape[1]
    nk_lat = seq // tk
    g = C_HEADS // C_KV
    kcol = QK_OFF["ck"] // LANES
    vcol = REST_OFF["cv"] // LANES
    if bounded:
        body = functools.partial(_dense_c_bounded_kernel, tq=tq, nk_lat=nk_lat)
        scratch = [pltpu.VMEM((C_KV, HEAD_DIM, g * tq), BF16),
                   pltpu.VMEM((C_KV, HEAD_DIM + VT_PAD, g * tq), F32)]
    else:
        body = functools.partial(_dense_c_kernel, tq=tq, nk_lat=nk_lat)
        scratch = [pltpu.VMEM((C_KV, g * tq, HEAD_DIM), BF16),
                   pltpu.VMEM((C_KV, g * tq, LANES), F32),
                   pltpu.VMEM((C_KV, g * tq, 1), F32)]
    return pl.pallas_call(
        body,
        grid=(bsz, seq // tq, nk_lat + 1),
        in_specs=[pl.BlockSpec((None, tq, C_HEADS * HEAD_DIM), lambda b, i, kt: (b, i, QK_OFF["cq"] // 512)),
                  pl.BlockSpec((None, tk, LANES), lambda b, i, kt: (b, jnp.minimum(kt, nk_lat - 1), kcol)),
                  pl.BlockSpec((None, tk, LANES), lambda b, i, kt: (b, jnp.minimum(kt, nk_lat - 1), vcol)),
                  pl.BlockSpec((None, ctx_len, LANES), lambda b, i, kt: (b, 0, kcol)),
                  pl.BlockSpec((None, ctx_len, LANES), lambda b, i, kt: (b, 0, vcol))],
        out_specs=pl.BlockSpec((None, tq, BRANCH_W), lambda b, i, kt: (b, i, 0)),
        out_shape=jax.ShapeDtypeStruct((bsz, seq, BRANCH_W), BF16),
        scratch_shapes=scratch,
        compiler_params=_cparams(("parallel", "parallel", "arbitrary")),
        name="mixer_c_dense_bounded" if bounded else "mixer_c_dense",
    )(qk_lat, qk_lat, rest_lat, qk_ctx, rest_ctx)


def _dense_d_kernel(q_ref, kl_ref, vl_ref, kx_ref, vx_ref, lam_ref, subln_ref, o_ref, qs_ref, l_ref, acc_ref,
                    m_ref, *, nk_lat, lam_init):
    kt = pl.program_id(2)
    n_sc = 2 * D_HEADS

    @pl.when(kt == 0)
    def _():
        q = q_ref[...]
        for idx in range(n_sc):
            qs_ref[idx] = q[:, idx * HEAD_DIM:(idx + 1) * HEAD_DIM]
        l_ref[...] = jnp.zeros(l_ref.shape, F32)
        acc_ref[...] = jnp.zeros(acc_ref.shape, F32)
        m_ref[...] = jnp.full(m_ref.shape, -jnp.inf, F32)

    def step(k, v):
        for idx in range(n_sc):
            h = idx // 2
            s = _dot_nt(qs_ref[idx], k[:, idx * HEAD_DIM:(idx + 1) * HEAD_DIM])
            _flash_update(s, v[:, h * 2 * HEAD_DIM:(h + 1) * 2 * HEAD_DIM], m_ref, l_ref, acc_ref, idx)

    @pl.when(kt < nk_lat)
    def _():
        step(kl_ref[...], vl_ref[...])

    @pl.when(kt == nk_lat)
    def _():
        step(kx_ref[...], vx_ref[...])
        lam = _diff_lambda(lam_ref, lam_init)
        for h in range(D_HEADS):
            l1 = jnp.sum(l_ref[2 * h], axis=-1, keepdims=True)
            l2 = jnp.sum(l_ref[2 * h + 1], axis=-1, keepdims=True)
            o = acc_ref[2 * h] / l1 - lam * (acc_ref[2 * h + 1] / l2)
            o_ref[:, h * 2 * HEAD_DIM:(h + 1) * 2 * HEAD_DIM] = _finish_diff(o, subln_ref, lam_init).astype(BF16)


def _dense_d_bounded_kernel(q_ref, kl_ref, vl_ref, kx_ref, vx_ref, lam_ref, subln_ref, o_ref, qs_ref, acc_ref, *,
                            nk_lat, lam_init):
    kt = pl.program_id(2)
    n_sc = 2 * D_HEADS
    vw = 2 * HEAD_DIM

    @pl.when(kt == 0)
    def _():
        q = q_ref[...].astype(F32)
        for idx in range(n_sc):
            qs_ref[idx] = jnp.transpose(q[:, idx * HEAD_DIM:(idx + 1) * HEAD_DIM]).astype(BF16)
        acc_ref[...] = jnp.zeros(acc_ref.shape, F32)

    def step(k, v):
        n = k.shape[0]
        chunk = min(n, 2 * DENSE_KEY_CHUNK)
        lane = lax.broadcasted_iota(jnp.int32, (chunk, VT_PAD), 1)
        ones_cols = jnp.where(lane == 0, 1.0, 0.0).astype(BF16)
        for h in range(D_HEADS):
            upd = [None, None]
            for k0 in range(0, n, chunk):
                ks = slice(k0, k0 + chunk)
                v1 = jnp.concatenate([v[ks, h * vw:(h + 1) * vw], ones_cols], axis=1)
                for c in range(2):
                    idx = 2 * h + c
                    p_t = jnp.exp2(_dot(k[ks, idx * HEAD_DIM:(idx + 1) * HEAD_DIM], qs_ref[idx])).astype(BF16)
                    term = lax.dot_general(v1, p_t, (((0,), (0,)), ((), ())), preferred_element_type=F32)
                    upd[c] = term if upd[c] is None else upd[c] + term
            for c in range(2):
                acc_ref[2 * h + c] += upd[c]

    @pl.when(kt < nk_lat)
    def _():
        step(kl_ref[...], vl_ref[...])

    @pl.when(kt == nk_lat)
    def _():
        step(kx_ref[...], vx_ref[...])
        lam = _diff_lambda(lam_ref, lam_init)
        for h in range(D_HEADS):
            a1, a2 = acc_ref[2 * h], acc_ref[2 * h + 1]
            o_t = a1[:vw] * (1.0 / a1[vw:vw + 1]) - lam * (a2[:vw] * (1.0 / a2[vw:vw + 1]))
            o_ref[:, h * vw:(h + 1) * vw] = _finish_diff(jnp.transpose(o_t), subln_ref, lam_init).astype(BF16)


def _dense_diff(qk_lat, rest_lat, qk_ctx, rest_ctx, lam_d, subln, lam_init, tq, tk, bounded):
    bsz, seq, _ = qk_lat.shape
    ctx_len = qk_ctx.shape[1]
    nk_lat = seq // tk
    width = D_HEADS * 2 * HEAD_DIM
    kcol = QK_OFF["dk"] // width
    vcol = REST_OFF["dv"] // width
    if bounded:
        body = functools.partial(_dense_d_bounded_kernel, nk_lat=nk_lat, lam_init=lam_init)
        scratch = [pltpu.VMEM((2 * D_HEADS, HEAD_DIM, tq), BF16),
                   pltpu.VMEM((2 * D_HEADS, 2 * HEAD_DIM + VT_PAD, tq), F32)]
    else:
        body = functools.partial(_dense_d_kernel, nk_lat=nk_lat, lam_init=lam_init)
        scratch = [pltpu.VMEM((2 * D_HEADS, tq, HEAD_DIM), BF16),
                   pltpu.VMEM((2 * D_HEADS, tq, LANES), F32),
                   pltpu.VMEM((2 * D_HEADS, tq, 2 * HEAD_DIM), F32),
                   pltpu.VMEM((2 * D_HEADS, tq, 1), F32)]
    return pl.pallas_call(
        body,
        grid=(bsz, seq // tq, nk_lat + 1),
        in_specs=[pl.BlockSpec((None, tq, width), lambda b, i, kt: (b, i, QK_OFF["dq"] // width)),
                  pl.BlockSpec((None, tk, width), lambda b, i, kt: (b, jnp.minimum(kt, nk_lat - 1), kcol)),
                  pl.BlockSpec((None, tk, width), lambda b, i, kt: (b, jnp.minimum(kt, nk_lat - 1), vcol)),
                  pl.BlockSpec((None, ctx_len, width), lambda b, i, kt: (b, 0, kcol)),
                  pl.BlockSpec((None, ctx_len, width), lambda b, i, kt: (b, 0, vcol)),
                  pl.BlockSpec((4, HEAD_DIM), lambda b, i, kt: (0, 0)),
                  pl.BlockSpec((1, 2 * HEAD_DIM), lambda b, i, kt: (0, 0))],
        out_specs=pl.BlockSpec((None, tq, BRANCH_W), lambda b, i, kt: (b, i, 0)),
        out_shape=jax.ShapeDtypeStruct((bsz, seq, BRANCH_W), BF16),
        scratch_shapes=scratch,
        compiler_params=_cparams(("parallel", "parallel", "arbitrary")),
        name="mixer_d_diff_bounded" if bounded else "mixer_d_diff",
    )(qk_lat, qk_lat, rest_lat, qk_ctx, rest_ctx, lam_d, subln.reshape(1, 2 * HEAD_DIM))


def _ctx_kernel(sink_ref, qk_ref, v_ref, lam_ref, subln_ref, o_ref, *, lam_init):
    qk = qk_ref[...]
    vals = v_ref[...]
    ctx_len = qk.shape[0]
    v_off = {name: REST_OFF[name] - REST_OFF[_FIRST_VALUE] for name in ("av", "cv", "bv", "dv")}

    def cols(name, start, width):
        c0 = QK_OFF[name] + start
        return qk[:, c0:c0 + width]

    def gqa(qname, kname, vname, out_off, n_kv, with_sink):
        g = 8 // n_kv
        for kv in range(n_kv):
            qs = _stack_heads(cols(qname, 0, 8 * HEAD_DIM), kv * g, g)
            s = _dot_nt(qs, cols(kname, kv * HEAD_DIM, HEAD_DIM))
            extra = None
            if with_sink:
                extra = jnp.concatenate([jnp.full((ctx_len, 1), sink_ref[kv * g + h], F32) for h in range(g)], axis=0)
            p, l = _softmax_rows(s, extra)
            vv = vals[:, v_off[vname] + kv * HEAD_DIM:v_off[vname] + (kv + 1) * HEAD_DIM]
            o = _dot(p.astype(BF16), vv) / l
            for h in range(g):
                c0 = out_off + (kv * g + h) * HEAD_DIM
                o_ref[:, c0:c0 + HEAD_DIM] = o[h * ctx_len:(h + 1) * ctx_len].astype(BF16)

    gqa("aq", "ak", "av", 0 * BRANCH_W, A_KV, True)
    gqa("bq", "bk", "bv", 1 * BRANCH_W, B_HEADS, False)
    gqa("cq", "ck", "cv", 2 * BRANCH_W, C_KV, False)

    lam = _diff_lambda(lam_ref, lam_init)
    for h in range(D_HEADS):
        base = h * 2 * HEAD_DIM
        p1, l1 = _softmax_rows(_dot_nt(cols("dq", base, HEAD_DIM), cols("dk", base, HEAD_DIM)))
        p2, l2 = _softmax_rows(_dot_nt(cols("dq", base + HEAD_DIM, HEAD_DIM), cols("dk", base + HEAD_DIM, HEAD_DIM)))
        pd = p1 / l1 - lam * (p2 / l2)
        o = _dot(pd.astype(BF16), vals[:, v_off["dv"] + base:v_off["dv"] + base + 2 * HEAD_DIM])
        c0 = 3 * BRANCH_W + base
        o_ref[:, c0:c0 + 2 * HEAD_DIM] = _finish_diff(o, subln_ref, lam_init).astype(BF16)


def _ctx_attention(qk_ctx, v_ctx, sink2, lam_d, subln, lam_init):
    bsz, ctx_len, _ = qk_ctx.shape
    return pl.pallas_call(
        functools.partial(_ctx_kernel, lam_init=lam_init),
        grid=(bsz,),
        in_specs=[pl.BlockSpec(memory_space=pltpu.SMEM),
                  pl.BlockSpec((None, ctx_len, QK_COLS), lambda b: (b, 0, 0)),
                  pl.BlockSpec((None, ctx_len, v_ctx.shape[2]), lambda b: (b, 0, 0)),
                  pl.BlockSpec((4, HEAD_DIM), lambda b: (0, 0)),
                  pl.BlockSpec((1, 2 * HEAD_DIM), lambda b: (0, 0))],
        out_specs=pl.BlockSpec((None, ctx_len, N_BRANCH * BRANCH_W), lambda b: (b, 0, 0)),
        out_shape=jax.ShapeDtypeStruct((bsz, ctx_len, N_BRANCH * BRANCH_W), BF16),
        compiler_params=_cparams(("parallel",)),
        name="ctx_attention",
    )(sink2, qk_ctx, v_ctx, lam_d, subln.reshape(1, 2 * HEAD_DIM))


def _merge_kernel(x_ref, mod_ref, ya_ref, yb_ref, yc_ref, yd_ref, gp_ref, mg_ref, wbr_ref, wout_ref, o_ref):
    gate = mod_ref[...][:, 2 * D_MODEL:]
    merged = None
    for n, y_ref in enumerate((ya_ref, yb_ref, yc_ref, yd_ref)):
        yg = (y_ref[...].astype(F32) * gp_ref[:, n * BRANCH_W:(n + 1) * BRANCH_W].astype(F32)).astype(BF16)
        term = mg_ref[:, n * D_MODEL:(n + 1) * D_MODEL].astype(F32) * _dot(yg, wbr_ref[n])
        merged = term if merged is None else merged + term
    o_ref[...] = x_ref[...] + gate * _dot(merged.astype(BF16), wout_ref[...])


def _merge(x2, mod3, ys, y_cols, rest, w_br, w_out, tm, row_of_tile):
    rows = x2.shape[0]
    y_specs = [pl.BlockSpec((tm, BRANCH_W), (lambda i, c=c: (i, c))) for c in y_cols]
    return pl.pallas_call(
        _merge_kernel,
        grid=(rows // tm,),
        in_specs=[pl.BlockSpec((tm, D_MODEL), lambda i: (i, 0)),
                  pl.BlockSpec((None, 1, 3 * D_MODEL), lambda i: (row_of_tile(i), 0, 0))]
                 + y_specs
                 + [pl.BlockSpec((tm, N_BRANCH * BRANCH_W), lambda i: (i, REST_OFF["ag"] // (N_BRANCH * BRANCH_W))),
                    pl.BlockSpec((tm, N_BRANCH * D_MODEL), lambda i: (i, 0)),
                    pl.BlockSpec((N_BRANCH, BRANCH_W, D_MODEL), lambda i: (0, 0, 0)),
                    pl.BlockSpec((D_MODEL, D_MODEL), lambda i: (0, 0))],
        out_specs=pl.BlockSpec((tm, D_MODEL), lambda i: (i, 0)),
        out_shape=jax.ShapeDtypeStruct((rows, D_MODEL), F32),
        compiler_params=_cparams(("parallel",)),
        name="gated_merge",
    )(x2, mod3, *ys, rest, rest, w_br, w_out)


def _rope_tables(seq):
    t = np.arange(seq, dtype=np.int32)
    pos = np.stack([t // GRID_W, t % GRID_W], axis=-1).astype(np.float32)
    n_freq = HEAD_DIM // 4
    freqs = (np.float32(ROPE_THETA) ** (-np.arange(n_freq, dtype=np.float32) / np.float32(n_freq))).astype(np.float32)
    ang = (pos[:, :, None] * freqs[None, None, :]).astype(np.float32)
    ang = np.concatenate([ang, ang], axis=-1).reshape(seq, HEAD_DIM)
    sign = np.where((np.arange(HEAD_DIM) % 32) < 16, -1.0, 1.0).astype(np.float32)
    reps = NORM_GROUP // HEAD_DIM
    cos = np.tile(np.cos(ang).astype(np.float32), (1, reps))
    sin = np.tile(np.sin(ang).astype(np.float32) * sign, (1, reps))
    return jnp.asarray(cos), jnp.asarray(sin)


def _regroup_cols(w, order, total):
    parts = [w[:, _ORIG[n][0]:_ORIG[n][1]] for n in order]
    used = sum(p.shape[1] for p in parts)
    if total > used:
        parts.append(jnp.zeros((w.shape[0], total - used), w.dtype))
    return jnp.concatenate(parts, axis=1)


def _qk_gain_row(g):
    parts = []
    for name in _QK_ORDER:
        mixer = "abcd".index(name[0])
        is_q = name[1] == "q"
        width = _ORIG[name][1] - _ORIG[name][0]
        gain = g[mixer, 0] * (QK_SCALE * LOG2E) if is_q else g[mixer, 1]
        parts.append(jnp.tile(gain, width // HEAD_DIM))
    parts.append(jnp.ones((QK_COLS - _QK_USED,), g.dtype))
    return jnp.concatenate(parts).reshape(1, QK_COLS).astype(F32)


def kernel(x, c, ctx, c_ctx, norm_w, w_ada, b_ada, w_in, qk_gain, sink_a, rpb_b, lam_d, subln_d, w_br, w_out):
    bsz, seq, _ = x.shape
    ctx_len = ctx.shape[1]
    depth = w_ada.shape[0]
    assert seq % (2 * A_WINDOW) == 0 and ctx_len % LANES == 0 and bsz <= 6

    tm_lat = 2048
    tm_rest = 4096
    tiles_per_batch = seq // tm_lat
    tk_dense = 2048
    tq_bounded, tq_rowmax = 512, 256
    assert seq % tk_dense == 0
    tm_merge = 512
    merge_tiles_per_batch = seq // tm_merge

    cvec = jnp.concatenate([c, c_ctx[None, :], jnp.zeros((8 - bsz - 1, D_MODEL), F32)], axis=0)
    mod_all = _ada(cvec, w_ada, b_ada)
    cos, sin = _rope_tables(seq)
    ones_bd = jnp.asarray(np.kron(np.eye(NORM_GROUP // HEAD_DIM), np.ones((HEAD_DIM, HEAD_DIM))), BF16)

    x2 = x.reshape(bsz * seq, D_MODEL)
    c2 = ctx.reshape(bsz * ctx_len, D_MODEL)
    for l in range(depth):
        need_ctx = l < depth - 1
        lam_init = 0.8 - 0.6 * math.exp(-0.3 * l)
        mod3 = mod_all[l].reshape(8, 1, 3 * D_MODEL)
        w_qk = _regroup_cols(w_in[l], _QK_ORDER, QK_COLS).astype(BF16)
        w_rest = _regroup_cols(w_in[l], _REST_ORDER, REST_COLS).astype(BF16)
        gain_row = _qk_gain_row(qk_gain[l])

        hx = _prenorm(x2, norm_w[l], mod3, tm_lat, lambda i: i // tiles_per_batch)
        hc = _prenorm(c2, norm_w[l], mod3, bsz * ctx_len, lambda i: bsz)
        qk_lat = _proj_qk(hx, w_qk, gain_row, ones_bd, tm_lat, cos, sin, tiles_per_batch).reshape(bsz, seq, QK_COLS)
        qk_ctx = _proj_qk(hc, w_qk, gain_row, ones_bd, bsz * ctx_len).reshape(bsz, ctx_len, QK_COLS)
        rest_lat = _proj_rest(hx, w_rest, tm_rest).reshape(bsz, seq, REST_COLS)
        rest_ctx = _proj_rest(hc, w_rest, bsz * ctx_len).reshape(bsz, ctx_len, REST_COLS)

        sink2 = sink_a[l] * LOG2E
        rpb2 = rpb_b[l] * LOG2E
        tq_nbr = 2 * A_WINDOW
        proj = (qk_lat, rest_lat, qk_ctx, rest_ctx)

        def qk_bound(g):
            return HEAD_DIM * QK_SCALE * LOG2E * jnp.max(jnp.abs(g[0])) * jnp.max(jnp.abs(g[1])) * 1.02

        y_a = lax.cond(qk_bound(qk_gain[l, 0]) <= LOGIT_BOUND,
                       lambda: _window_attention(*proj, sink2, True),
                       lambda: _window_attention(*proj, sink2, False))
        y_b = lax.cond(qk_bound(qk_gain[l, 1]) + jnp.max(jnp.abs(rpb_b[l])) * LOG2E <= LOGIT_BOUND,
                       lambda: _neighborhood_attention(*proj, rpb2, tq_nbr, True),
                       lambda: _neighborhood_attention(*proj, rpb2, tq_nbr, False))
        y_c = lax.cond(qk_bound(qk_gain[l, 2]) <= LOGIT_BOUND,
                       lambda: _dense_gqa(*proj, 2 * tq_bounded, tk_dense, True),
                       lambda: _dense_gqa(*proj, tq_rowmax, tk_dense, False))
        y_d = lax.cond(qk_bound(qk_gain[l, 3]) <= LOGIT_BOUND,
                       lambda: _dense_diff(*proj, lam_d[l], subln_d[l], lam_init, 2 * tq_bounded, tk_dense, True),
                       lambda: _dense_diff(*proj, lam_d[l], subln_d[l], lam_init, tq_rowmax, tk_dense, False))

        w_br_l = w_br[l].astype(BF16)
        w_out_l = w_out[l].astype(BF16)
        ys = [y.reshape(bsz * seq, BRANCH_W) for y in (y_a, y_b, y_c, y_d)]
        x_new = _merge(x2, mod3, ys, (0, 0, 0, 0), rest_lat.reshape(bsz * seq, REST_COLS), w_br_l, w_out_l,
                       tm_merge, lambda i: i // merge_tiles_per_batch)
        if need_ctx:
            y_ctx = _ctx_attention(qk_ctx, rest_ctx[:, :, REST_OFF[_FIRST_VALUE]:], sink2, lam_d[l], subln_d[l],
                                   lam_init)
            y_ctx2 = y_ctx.reshape(bsz * ctx_len, N_BRANCH * BRANCH_W)
            c2 = _merge(c2, mod3, [y_ctx2] * 4, (0, 1, 2, 3), rest_ctx.reshape(bsz * ctx_len, REST_COLS),
                        w_br_l, w_out_l, bsz * ctx_len, lambda i: bsz)
        x2 = x_new
    return x2.reshape(bsz, seq, D_MODEL)
```

```python
import functools
import math

import numpy as np
import jax
import jax.numpy as jnp
from jax import lax
from jax.experimental import pallas as pl
from jax.experimental.pallas import tpu as pltpu

F32 = jnp.float32
BF16 = jnp.bfloat16

D_MODEL = 1024
GRID_W = 64
HEAD_DIM = 64
BRANCH_W = 512
N_BRANCH = 4
A_HEADS, A_KV, A_WINDOW = 8, 2, 128
B_HEADS, NB_ROWS, NB_COLS = 8, 8, 16
C_HEADS, C_KV = 8, 2
D_HEADS = 4
ROPE_THETA = 10000.0
EPS = 1e-6
NEG_INF = -1e30
QK_SCALE = HEAD_DIM ** -0.5
LOG2E = math.log2(math.e)
LOGIT_BOUND = 60.0

V7X_VMEM_LIMIT_BYTES = 56 * 1024 * 1024
LANES = 128

_ORIG = dict(aq=(0, 512), ak=(512, 640), av=(640, 768), ag=(768, 1280),
             bq=(1280, 1792), bk=(1792, 2304), bv=(2304, 2816), bg=(2816, 3328),
             cq=(3328, 3840), ck=(3840, 3968), cv=(3968, 4096), cg=(4096, 4608),
             dq=(4608, 5120), dk=(5120, 5632), dv=(5632, 6144), dg=(6144, 6656),
             mg=(6656, 10752))
_QK_ORDER = ("aq", "cq", "dq", "bq", "bk", "dk", "ak", "ck")
_REST_ORDER = ("mg", "ag", "bg", "cg", "dg", "dv", "bv", "av", "cv")
_FIRST_VALUE = "dv"


def _offsets(order):
    off, out = 0, {}
    for name in order:
        lo, hi = _ORIG[name]
        out[name] = off
        off += hi - lo
    return out, off


PROJ_TN = 512
PROJ_ROW_CHUNK = 512
REST_ROW_CHUNK = 1024
VT_PAD = 16
DENSE_KEY_CHUNK = 512
NORM_GROUP = 256
QK_OFF, _QK_USED = _offsets(_QK_ORDER)
REST_OFF, _REST_USED = _offsets(_REST_ORDER)
QK_COLS = -(-_QK_USED // PROJ_TN) * PROJ_TN
REST_COLS = -(-_REST_USED // PROJ_TN) * PROJ_TN
_ROPE_LO_END = QK_OFF["bq"] // PROJ_TN
_ROPE_HI_START = QK_OFF["dk"] // PROJ_TN
_MG_BLOCKS = (REST_OFF["ag"]) // PROJ_TN
_GP_BLOCKS_END = REST_OFF[_FIRST_VALUE] // PROJ_TN
assert QK_OFF["bq"] % PROJ_TN == 0 and QK_OFF["dk"] % PROJ_TN == 0
assert REST_OFF["ag"] % PROJ_TN == 0 and REST_OFF[_FIRST_VALUE] % PROJ_TN == 0


def _cparams(sem):
    return pltpu.CompilerParams(dimension_semantics=sem, vmem_limit_bytes=V7X_VMEM_LIMIT_BYTES)


def _sigmoid(x):
    return 0.5 * jnp.tanh(0.5 * x) + 0.5


def _dot_nt(a, b):
    return lax.dot_general(a, b, (((1,), (1,)), ((), ())), preferred_element_type=F32)


def _dot(a, b):
    return jnp.dot(a, b, preferred_element_type=F32)


def _ada_kernel(c_ref, w_ref, b_ref, o_ref):
    c = c_ref[...]
    o_ref[...] = _dot(c * _sigmoid(c), w_ref[...]) + b_ref[...]


def _ada(cvec, w_ada, b_ada):
    depth = w_ada.shape[0]
    tn = 512
    return pl.pallas_call(
        _ada_kernel,
        grid=(depth, 3 * D_MODEL // tn),
        in_specs=[pl.BlockSpec((8, D_MODEL), lambda l, j: (0, 0)),
                  pl.BlockSpec((None, D_MODEL, tn), lambda l, j: (l, 0, j)),
                  pl.BlockSpec((None, 1, tn), lambda l, j: (l, 0, j))],
        out_specs=pl.BlockSpec((None, 8, tn), lambda l, j: (l, 0, j)),
        out_shape=jax.ShapeDtypeStruct((depth, 8, 3 * D_MODEL), F32),
        compiler_params=_cparams(("parallel", "parallel")),
        name="ada_mod",
    )(cvec, w_ada, b_ada.reshape(depth, 1, 3 * D_MODEL))


def _prenorm_kernel(x_ref, nw_ref, mod_ref, o_ref):
    x = x_ref[...]
    mod = mod_ref[...]
    shift, scale = mod[:, :D_MODEL], mod[:, D_MODEL:2 * D_MODEL]
    y = x * lax.rsqrt(jnp.mean(x * x, axis=-1, keepdims=True) + EPS) * nw_ref[...]
    o_ref[...] = (y * (1.0 + scale) + shift).astype(BF16)


def _prenorm(x2, norm_w, mod3, tm, row_of_tile):
    rows = x2.shape[0]
    return pl.pallas_call(
        _prenorm_kernel,
        grid=(rows // tm,),
        in_specs=[pl.BlockSpec((tm, D_MODEL), lambda i: (i, 0)),
                  pl.BlockSpec((1, D_MODEL), lambda i: (0, 0)),
                  pl.BlockSpec((None, 1, 3 * D_MODEL), lambda i: (row_of_tile(i), 0, 0))],
        out_specs=pl.BlockSpec((tm, D_MODEL), lambda i: (i, 0)),
        out_shape=jax.ShapeDtypeStruct((rows, D_MODEL), BF16),
        compiler_params=_cparams(("parallel",)),
        name="prenorm",
    )(x2, norm_w.reshape(1, D_MODEL), mod3)


def _rot_half_unsigned(n):
    lane = lax.broadcasted_iota(jnp.int32, n.shape, 1)
    return jnp.where((lane & 31) < 16, pltpu.roll(n, LANES - 16, 1), pltpu.roll(n, 16, 1))


def _proj_qk_kernel(*refs, rope):
    if rope:
        hx_ref, w_ref, gain_ref, ones_ref, cos_ref, sin_ref, o_ref = refs
    else:
        hx_ref, w_ref, gain_ref, ones_ref, o_ref = refs
    w = w_ref[...]
    ones_bd = ones_ref[...]
    rows = hx_ref.shape[0]
    chunk = min(rows, PROJ_ROW_CHUNK)
    if rope:
        j = pl.program_id(1)
        use = jnp.logical_or(j < _ROPE_LO_END, j >= _ROPE_HI_START)
    for r0 in range(0, rows, chunk):
        rs = slice(r0, r0 + chunk)
        acc_all = _dot(hx_ref[rs, :], w)
        if rope:
            cos = jnp.where(use, cos_ref[rs, :], 1.0)
            sin = jnp.where(use, sin_ref[rs, :], 0.0)
        for c in range(PROJ_TN // NORM_GROUP):
            sl = slice(c * NORM_GROUP, (c + 1) * NORM_GROUP)
            acc = acc_all[:, sl]
            ss = _dot((acc * acc).astype(BF16), ones_bd)
            n = acc * lax.rsqrt(ss * (1.0 / HEAD_DIM) + EPS) * gain_ref[:, sl]
            if rope:
                rot = jnp.concatenate([_rot_half_unsigned(n[:, :LANES]), _rot_half_unsigned(n[:, LANES:])], axis=1)
                n = n * cos + rot * sin
            o_ref[rs, sl] = n.astype(BF16)


def _proj_qk(hx, w_qk, gain_row, ones_bd, tm, cos=None, sin=None, tiles_per_batch=None):
    rows = hx.shape[0]
    rope = cos is not None
    in_specs = [pl.BlockSpec((tm, D_MODEL), lambda i, j: (i, 0)),
                pl.BlockSpec((D_MODEL, PROJ_TN), lambda i, j: (0, j)),
                pl.BlockSpec((1, PROJ_TN), lambda i, j: (0, j)),
                pl.BlockSpec((NORM_GROUP, NORM_GROUP), lambda i, j: (0, 0))]
    args = [hx, w_qk, gain_row, ones_bd]
    if rope:
        in_specs += [pl.BlockSpec((tm, NORM_GROUP), lambda i, j: (i % tiles_per_batch, 0))] * 2
        args += [cos, sin]
    return pl.pallas_call(
        functools.partial(_proj_qk_kernel, rope=rope),
        grid=(rows // tm, QK_COLS // PROJ_TN),
        in_specs=in_specs,
        out_specs=pl.BlockSpec((tm, PROJ_TN), lambda i, j: (i, j)),
        out_shape=jax.ShapeDtypeStruct((rows, QK_COLS), BF16),
        compiler_params=_cparams(("parallel", "arbitrary")),
        name="proj_qk_rope" if rope else "proj_qk",
    )(*args)


def _proj_rest_kernel(hx_ref, w_ref, o_ref):
    j = pl.program_id(1)
    w = w_ref[...]
    rows = hx_ref.shape[0]
    chunk = min(rows, REST_ROW_CHUNK)
    for r0 in range(0, rows, chunk):
        acc = _dot(hx_ref[r0:r0 + chunk, :], w)
        sg = _sigmoid(acc)
        o_ref[r0:r0 + chunk, :] = jnp.where(j < _MG_BLOCKS, sg,
                                            jnp.where(j < _GP_BLOCKS_END, acc * sg, acc)).astype(BF16)


def _proj_rest(hx, w_rest, tm):
    rows = hx.shape[0]
    return pl.pallas_call(
        _proj_rest_kernel,
        grid=(rows // tm, REST_COLS // PROJ_TN),
        in_specs=[pl.BlockSpec((tm, D_MODEL), lambda i, j: (i, 0)),
                  pl.BlockSpec((D_MODEL, PROJ_TN), lambda i, j: (0, j))],
        out_specs=pl.BlockSpec((tm, PROJ_TN), lambda i, j: (i, j)),
        out_shape=jax.ShapeDtypeStruct((rows, REST_COLS), BF16),
        compiler_params=_cparams(("parallel", "arbitrary")),
        name="proj_rest",
    )(hx, w_rest)


def _stack_heads(q, first_head, n):
    return jnp.concatenate([q[:, (first_head + g) * HEAD_DIM:(first_head + g + 1) * HEAD_DIM] for g in range(n)],
                           axis=0)


def _softmax_rows(s, extra=None):
    m = jnp.max(s, axis=-1, keepdims=True)
    if extra is not None:
        m = jnp.maximum(m, extra)
    p = jnp.exp2(s - m)
    l = jnp.sum(p, axis=-1, keepdims=True)
    if extra is not None:
        l = l + jnp.exp2(extra - m)
    return p, l


def _with_ones_column(v, pad=HEAD_DIM):
    lane = lax.broadcasted_iota(jnp.int32, (v.shape[0], pad), 1)
    return jnp.concatenate([v, jnp.where(lane == 0, 1.0, 0.0).astype(v.dtype)], axis=1)


def _lane_partial_sums(p):
    part = p[:, :LANES]
    for c in range(1, p.shape[1] // LANES):
        part = part + p[:, c * LANES:(c + 1) * LANES]
    return part


def _diff_lambda(lam_ref, lam_init):
    lf = lam_ref[...]
    a = jnp.sum(lf[0:1] * lf[1:2], axis=-1, keepdims=True)
    b = jnp.sum(lf[2:3] * lf[3:4], axis=-1, keepdims=True)
    return jnp.exp(a) - jnp.exp(b) + lam_init


def _finish_diff(o, subln_ref, lam_init):
    y = o * lax.rsqrt(jnp.mean(o * o, axis=-1, keepdims=True) + EPS) * subln_ref[...]
    return y * (1.0 - lam_init)


def _win_kernel(sink_ref, q_ref, kp_ref, kc_ref, kn_ref, kx_ref, vp_ref, vc_ref, vn_ref, vx_ref, o_ref, *,
                tq, seq):
    i = pl.program_id(1)
    g = A_HEADS // A_KV
    q = q_ref[...]
    k_all = jnp.concatenate([kp_ref[...], kc_ref[...], kn_ref[...], kx_ref[...]], axis=0)
    v_all = jnp.concatenate([vp_ref[...], vc_ref[...], vn_ref[...], vx_ref[...]], axis=0)
    nw = 2 * tq
    t = lax.broadcasted_iota(jnp.int32, (g * tq, nw), 0) & (tq - 1)
    j = lax.broadcasted_iota(jnp.int32, (g * tq, nw), 1)
    rel = j - tq // 2 - t
    kpos = i * tq - tq // 2 + j
    bad = jnp.where(jnp.abs(rel) > A_WINDOW, 1, 0) + jnp.where(kpos < 0, 1, 0) + jnp.where(kpos >= seq, 1, 0)
    for kv in range(A_KV):
        qs = _stack_heads(q, kv * g, g)
        kk = k_all[:, kv * HEAD_DIM:(kv + 1) * HEAD_DIM]
        vv = v_all[:, kv * HEAD_DIM:(kv + 1) * HEAD_DIM]
        s = _dot_nt(qs, kk)
        s_win = jnp.where(bad > 0, NEG_INF, s[:, :nw])
        s_ctx = s[:, nw:]
        sink = jnp.concatenate([jnp.full((tq, 1), sink_ref[kv * g + h], F32) for h in range(g)], axis=0)
        m = jnp.maximum(jnp.maximum(jnp.max(s_win, axis=-1, keepdims=True),
                                    jnp.max(s_ctx, axis=-1, keepdims=True)), sink)
        pw = jnp.exp2(s_win - m)
        pc = jnp.exp2(s_ctx - m)
        l = jnp.sum(pw, axis=-1, keepdims=True) + jnp.sum(pc, axis=-1, keepdims=True) + jnp.exp2(sink - m)
        o = (_dot(pw.astype(BF16), vv[:nw]) + _dot(pc.astype(BF16), vv[nw:])) / l
        for h in range(g):
            c0 = (kv * g + h) * HEAD_DIM
            o_ref[:, c0:c0 + HEAD_DIM] = o[h * tq:(h + 1) * tq].astype(BF16)


def _win_bounded_kernel(sink_ref, q_ref, kp_ref, kc_ref, kn_ref, kx_ref, vp_ref, vc_ref, vn_ref, vx_ref, mask_ref,
                        o_ref, *, tq):
    g = A_HEADS // A_KV
    q = q_ref[...]
    mask_t = mask_ref[...]
    sink = sink_ref[...]
    k_all = jnp.concatenate([kp_ref[...], kc_ref[...], kn_ref[...], kx_ref[...]], axis=0)
    v_all = jnp.concatenate([vp_ref[...], vc_ref[...], vn_ref[...], vx_ref[...]], axis=0)
    for kv in range(A_KV):
        sl = slice(kv * HEAD_DIM, (kv + 1) * HEAD_DIM)
        qs = _stack_heads(q, kv * g, g)
        stab = [jnp.maximum(sink[:, kv * g + h:kv * g + h + 1], 0.0) for h in range(g)]
        shift_t = jnp.concatenate([mask_t - stab[h] for h in range(g)], axis=1)
        p_t = jnp.exp2(_dot_nt(k_all[:, sl], qs) + shift_t).astype(BF16)
        acc = lax.dot_general(_with_ones_column(v_all[:, sl], VT_PAD), p_t, (((0,), (0,)), ((), ())),
                              preferred_element_type=F32)
        for h in range(g):
            a = acc[:, h * tq:(h + 1) * tq]
            l = a[HEAD_DIM:HEAD_DIM + 1] + jnp.exp2(sink[:, kv * g + h:kv * g + h + 1] - stab[h])
            c0 = (kv * g + h) * HEAD_DIM
            o_ref[:, c0:c0 + HEAD_DIM] = jnp.transpose(a[:HEAD_DIM] * (1.0 / l)).astype(BF16)


def _window_mask_tables(seq, tq, ctx_len):
    nt = seq // tq
    tabs = []
    for i0 in (0, 1, nt - 1):
        t = np.arange(tq)[None, :]
        j = np.arange(2 * tq)[:, None]
        kpos = i0 * tq - tq // 2 + j
        ok = (np.abs(j - tq // 2 - t) <= A_WINDOW) & (kpos >= 0) & (kpos < seq)
        win = np.where(ok, 0.0, NEG_INF).astype(np.float32)
        tabs.append(np.concatenate([win, np.zeros((ctx_len, tq), np.float32)], axis=0))
    return jnp.asarray(np.stack(tabs, axis=0))


def _window_attention(qk_lat, rest_lat, qk_ctx, rest_ctx, sink2, bounded):
    bsz, seq, _ = qk_lat.shape
    ctx_len = qk_ctx.shape[1]
    tq = 2 * A_WINDOW
    nt = seq // tq
    half = tq // 2
    kcol = QK_OFF["ak"] // LANES
    vcol = REST_OFF["av"] // LANES
    n_half = seq // half

    def kv_specs(col):
        return [pl.BlockSpec((None, half, LANES), lambda b, i: (b, jnp.maximum(2 * i - 1, 0), col)),
                pl.BlockSpec((None, tq, LANES), lambda b, i: (b, i, col)),
                pl.BlockSpec((None, half, LANES), lambda b, i: (b, jnp.minimum(2 * i + 2, n_half - 1), col)),
                pl.BlockSpec((None, ctx_len, LANES), lambda b, i: (b, 0, col))]

    def tile_variant(i):
        return jnp.where(i == 0, 0, jnp.where(i == nt - 1, 2, 1))

    q_spec = pl.BlockSpec((None, tq, A_HEADS * HEAD_DIM), lambda b, i: (b, i, QK_OFF["aq"] // 512))
    kv_args = (qk_lat, qk_lat, qk_lat, qk_ctx, rest_lat, rest_lat, rest_lat, rest_ctx)
    if bounded:
        body = functools.partial(_win_bounded_kernel, tq=tq)
        in_specs = ([pl.BlockSpec((1, A_HEADS), lambda b, i: (0, 0)), q_spec] + kv_specs(kcol) + kv_specs(vcol)
                    + [pl.BlockSpec((None, 2 * tq + ctx_len, tq), lambda b, i: (tile_variant(i), 0, 0))])
        args = (sink2.reshape(1, A_HEADS), qk_lat) + kv_args + (_window_mask_tables(seq, tq, ctx_len),)
    else:
        body = functools.partial(_win_kernel, tq=tq, seq=seq)
        in_specs = [pl.BlockSpec(memory_space=pltpu.SMEM), q_spec] + kv_specs(kcol) + kv_specs(vcol)
        args = (sink2, qk_lat) + kv_args
    return pl.pallas_call(
        body,
        grid=(bsz, nt),
        in_specs=in_specs,
        out_specs=pl.BlockSpec((None, tq, BRANCH_W), lambda b, i: (b, i, 0)),
        out_shape=jax.ShapeDtypeStruct((bsz, seq, BRANCH_W), BF16),
        compiler_params=_cparams(("parallel", "parallel")),
        name="mixer_a_window_bounded" if bounded else "mixer_a_window",
    )(*args)


def _nbr_kernel(q_ref, kp_ref, kc_ref, kn_ref, kx_ref, vp_ref, vc_ref, vn_ref, vx_ref, bias_ref, rmask_ref, o_ref,
                *, tq, bounded):
    q = q_ref[...]
    k_all = jnp.concatenate([kp_ref[...], kc_ref[...], kn_ref[...], kx_ref[...]], axis=0)
    v_all = jnp.concatenate([vp_ref[...], vc_ref[...], vn_ref[...], vx_ref[...]], axis=0)
    rmask = rmask_ref[...]
    nw = 3 * tq
    for h in range(B_HEADS):
        sl = slice(h * HEAD_DIM, (h + 1) * HEAD_DIM)
        s = _dot_nt(q[:, sl], k_all[:, sl])
        s = jnp.concatenate([s[:, :nw] + (bias_ref[h] + rmask), s[:, nw:]], axis=1)
        if not bounded:
            s = s - jnp.max(s, axis=-1, keepdims=True)
        acc = _dot(jnp.exp2(s).astype(BF16), _with_ones_column(v_all[:, sl]))
        o_ref[:, sl] = (acc[:, :HEAD_DIM] * (1.0 / acc[:, HEAD_DIM:HEAD_DIM + 1])).astype(BF16)


def _nbr_bias_tables(rpb, seq, tq):
    rows = seq // GRID_W
    nt = seq // tq
    kr = min(NB_ROWS, rows)
    qr = tq // GRID_W
    n_heads = rpb.shape[0]
    assert NB_ROWS - 1 - qr - (qr - 1) >= 0 and NB_ROWS - 1 - qr + 3 * qr <= 2 * NB_ROWS - 1
    pad = GRID_W - 1
    rpb_pad = jnp.pad(rpb.astype(F32), ((0, 0), (0, 0), (pad, pad)))
    toep = jnp.stack([rpb_pad[:, :, pad + NB_COLS - 1 - c:pad + NB_COLS - 1 - c + GRID_W] for c in range(GRID_W)],
                     axis=2)
    c = np.arange(GRID_W)
    cstart = np.clip(c - NB_COLS // 2, 0, GRID_W - NB_COLS)
    col_ok = (c[None, :] >= cstart[:, None]) & (c[None, :] < cstart[:, None] + NB_COLS)
    toep = jnp.where(col_ok[None, None], toep, NEG_INF)
    a0 = NB_ROWS - 1 - qr
    per_rl = [jnp.transpose(toep[:, a0 - rl:a0 - rl + 3 * qr], (0, 2, 1, 3)) for rl in range(qr)]
    base = jnp.stack(per_rl, axis=1).reshape(n_heads, tq, 3 * tq)
    row_masks = []
    for i0 in (0, 1, nt - 1):
        r = i0 * qr + np.arange(qr)
        r2 = (i0 - 1) * qr + np.arange(3 * qr)
        rstart = np.clip(r - kr // 2, 0, rows - kr)
        row_ok = ((r2[None, :] >= rstart[:, None]) & (r2[None, :] < rstart[:, None] + kr)
                  & (r2[None, :] >= 0) & (r2[None, :] < rows))
        full = np.broadcast_to(row_ok[:, None, :, None], (qr, GRID_W, 3 * qr, GRID_W)).reshape(tq, 3 * tq)
        row_masks.append(np.where(full, 0.0, NEG_INF).astype(np.float32))
    return base, jnp.asarray(np.stack(row_masks, axis=0))


def _neighborhood_attention(qk_lat, rest_lat, qk_ctx, rest_ctx, rpb2, tq, bounded):
    bsz, seq, _ = qk_lat.shape
    ctx_len = qk_ctx.shape[1]
    nt = seq // tq
    width = B_HEADS * HEAD_DIM
    bias, row_masks = _nbr_bias_tables(rpb2, seq, tq)

    def kv_specs(col):
        return [pl.BlockSpec((None, tq, width), lambda b, i: (b, jnp.maximum(i - 1, 0), col)),
                pl.BlockSpec((None, tq, width), lambda b, i: (b, i, col)),
                pl.BlockSpec((None, tq, width), lambda b, i: (b, jnp.minimum(i + 1, nt - 1), col)),
                pl.BlockSpec((None, ctx_len, width), lambda b, i: (b, 0, col))]

    def tile_variant(i):
        return jnp.where(i == 0, 0, jnp.where(i == nt - 1, 2, 1))

    return pl.pallas_call(
        functools.partial(_nbr_kernel, tq=tq, bounded=bounded),
        grid=(bsz, nt),
        in_specs=[pl.BlockSpec((None, tq, width), lambda b, i: (b, i, QK_OFF["bq"] // width))]
                 + kv_specs(QK_OFF["bk"] // width) + kv_specs(REST_OFF["bv"] // width)
                 + [pl.BlockSpec(bias.shape, lambda b, i: (0, 0, 0)),
                    pl.BlockSpec((None,) + row_masks.shape[1:], lambda b, i: (tile_variant(i), 0, 0))],
        out_specs=pl.BlockSpec((None, tq, BRANCH_W), lambda b, i: (b, i, 0)),
        out_shape=jax.ShapeDtypeStruct((bsz, seq, BRANCH_W), BF16),
        compiler_params=_cparams(("parallel", "parallel")),
        name="mixer_b_neighbourhood_bounded" if bounded else "mixer_b_neighbourhood",
    )(qk_lat, qk_lat, qk_lat, qk_lat, qk_ctx, rest_lat, rest_lat, rest_lat, rest_ctx, bias, row_masks)


def _flash_update(s, v, m_ref, l_ref, acc_ref, idx):
    m_prev = m_ref[idx]
    m_new = jnp.maximum(m_prev, jnp.max(s, axis=-1, keepdims=True))
    alpha = jnp.exp2(m_prev - m_new)
    p = jnp.exp2(s - m_new)
    if l_ref is not None:
        l_ref[idx] = alpha * l_ref[idx] + _lane_partial_sums(p)
    acc_ref[idx] = alpha * acc_ref[idx] + _dot(p.astype(BF16), v)
    m_ref[idx] = m_new


def _dense_c_kernel(q_ref, kl_ref, vl_ref, kx_ref, vx_ref, o_ref, qs_ref, acc_ref, m_ref, *, tq, nk_lat):
    kt = pl.program_id(2)
    g = C_HEADS // C_KV

    @pl.when(kt == 0)
    def _():
        q = q_ref[...]
        for kv in range(C_KV):
            qs_ref[kv] = _stack_heads(q, kv * g, g)
        acc_ref[...] = jnp.zeros(acc_ref.shape, F32)
        m_ref[...] = jnp.full(m_ref.shape, -jnp.inf, F32)

    def step(k, v):
        for kv in range(C_KV):
            sl = slice(kv * HEAD_DIM, (kv + 1) * HEAD_DIM)
            _flash_update(_dot_nt(qs_ref[kv], k[:, sl]), _with_ones_column(v[:, sl]), m_ref, None, acc_ref, kv)

    @pl.when(kt < nk_lat)
    def _():
        step(kl_ref[...], vl_ref[...])

    @pl.when(kt == nk_lat)
    def _():
        step(kx_ref[...], vx_ref[...])
        for kv in range(C_KV):
            acc = acc_ref[kv]
            o = acc[:, :HEAD_DIM] * (1.0 / acc[:, HEAD_DIM:HEAD_DIM + 1])
            for h in range(g):
                c0 = (kv * g + h) * HEAD_DIM
                o_ref[:, c0:c0 + HEAD_DIM] = o[h * tq:(h + 1) * tq].astype(BF16)


def _dense_c_bounded_kernel(q_ref, kl_ref, vl_ref, kx_ref, vx_ref, o_ref, qs_ref, acc_ref, *, tq, nk_lat):
    kt = pl.program_id(2)
    g = C_HEADS // C_KV

    @pl.when(kt == 0)
    def _():
        q = q_ref[...]
        for kv in range(C_KV):
            qs_ref[kv] = _stack_heads(q, kv * g, g)
        acc_ref[...] = jnp.zeros(acc_ref.shape, F32)

    def step(k, v):
        n = k.shape[0]
        chunk = min(n, DENSE_KEY_CHUNK)
        for kv in range(C_KV):
            sl = slice(kv * HEAD_DIM, (kv + 1) * HEAD_DIM)
            upd = None
            for k0 in range(0, n, chunk):
                ks = slice(k0, k0 + chunk)
                p_t = jnp.exp2(_dot_nt(k[ks, sl], qs_ref[kv])).astype(BF16)
                term = lax.dot_general(_with_ones_column(v[ks, sl], VT_PAD), p_t, (((0,), (0,)), ((), ())),
                                       preferred_element_type=F32)
                upd = term if upd is None else upd + term
            acc_ref[kv] += upd

    @pl.when(kt < nk_lat)
    def _():
        step(kl_ref[...], vl_ref[...])

    @pl.when(kt == nk_lat)
    def _():
        step(kx_ref[...], vx_ref[...])
        for kv in range(C_KV):
            acc = acc_ref[kv]
            o = jnp.transpose(acc[:HEAD_DIM] * (1.0 / acc[HEAD_DIM:HEAD_DIM + 1]))
            for h in range(g):
                c0 = (kv * g + h) * HEAD_DIM
                o_ref[:, c0:c0 + HEAD_DIM] = o[h * tq:(h + 1) * tq].astype(BF16)


def _dense_gqa(qk_lat, rest_lat, qk_ctx, rest_ctx, tq, tk, bounded):
    bsz, seq, _ = qk_lat.shape
    ctx_len = qk_ctx.shape[1]
    nk_lat = seq // tk
    g = C_HEADS // C_KV
    kcol = QK_OFF["ck"] // LANES
    vcol = REST_OFF["cv"] // LANES
    if bounded:
        body = functools.partial(_dense_c_bounded_kernel, tq=tq, nk_lat=nk_lat)
        scratch = [pltpu.VMEM((C_KV, g * tq, HEAD_DIM), BF16),
                   pltpu.VMEM((C_KV, HEAD_DIM + VT_PAD, g * tq), F32)]
    else:
        body = functools.partial(_dense_c_kernel, tq=tq, nk_lat=nk_lat)
        scratch = [pltpu.VMEM((C_KV, g * tq, HEAD_DIM), BF16),
                   pltpu.VMEM((C_KV, g * tq, LANES), F32),
                   pltpu.VMEM((C_KV, g * tq, 1), F32)]
    return pl.pallas_call(
        body,
        grid=(bsz, seq // tq, nk_lat + 1),
        in_specs=[pl.BlockSpec((None, tq, C_HEADS * HEAD_DIM), lambda b, i, kt: (b, i, QK_OFF["cq"] // 512)),
                  pl.BlockSpec((None, tk, LANES), lambda b, i, kt: (b, jnp.minimum(kt, nk_lat - 1), 0)),
                  pl.BlockSpec((None, tk, LANES), lambda b, i, kt: (b, jnp.minimum(kt, nk_lat - 1), 0)),
                  pl.BlockSpec((None, ctx_len, LANES), lambda b, i, kt: (b, 0, kcol)),
                  pl.BlockSpec((None, ctx_len, LANES), lambda b, i, kt: (b, 0, vcol))],
        out_specs=pl.BlockSpec((None, tq, BRANCH_W), lambda b, i, kt: (b, i, 0)),
        out_shape=jax.ShapeDtypeStruct((bsz, seq, BRANCH_W), BF16),
        scratch_shapes=scratch,
        compiler_params=_cparams(("parallel", "parallel", "arbitrary")),
        name="mixer_c_dense_bounded" if bounded else "mixer_c_dense",
    )(qk_lat, qk_lat[:, :, kcol * LANES:(kcol + 1) * LANES], rest_lat[:, :, vcol * LANES:(vcol + 1) * LANES],
      qk_ctx, rest_ctx)


def _dense_d_kernel(q_ref, kl_ref, vl_ref, kx_ref, vx_ref, lam_ref, subln_ref, o_ref, qs_ref, l_ref, acc_ref,
                    m_ref, *, nk_lat, lam_init):
    kt = pl.program_id(2)
    n_sc = 2 * D_HEADS

    @pl.when(kt == 0)
    def _():
        q = q_ref[...]
        for idx in range(n_sc):
            qs_ref[idx] = q[:, idx * HEAD_DIM:(idx + 1) * HEAD_DIM]
        l_ref[...] = jnp.zeros(l_ref.shape, F32)
        acc_ref[...] = jnp.zeros(acc_ref.shape, F32)
        m_ref[...] = jnp.full(m_ref.shape, -jnp.inf, F32)

    def step(k, v):
        for idx in range(n_sc):
            h = idx // 2
            s = _dot_nt(qs_ref[idx], k[:, idx * HEAD_DIM:(idx + 1) * HEAD_DIM])
            _flash_update(s, v[:, h * 2 * HEAD_DIM:(h + 1) * 2 * HEAD_DIM], m_ref, l_ref, acc_ref, idx)

    @pl.when(kt < nk_lat)
    def _():
        step(kl_ref[...], vl_ref[...])

    @pl.when(kt == nk_lat)
    def _():
        step(kx_ref[...], vx_ref[...])
        lam = _diff_lambda(lam_ref, lam_init)
        for h in range(D_HEADS):
            l1 = jnp.sum(l_ref[2 * h], axis=-1, keepdims=True)
            l2 = jnp.sum(l_ref[2 * h + 1], axis=-1, keepdims=True)
            o = acc_ref[2 * h] / l1 - lam * (acc_ref[2 * h + 1] / l2)
            o_ref[:, h * 2 * HEAD_DIM:(h + 1) * 2 * HEAD_DIM] = _finish_diff(o, subln_ref, lam_init).astype(BF16)


def _dense_d_bounded_kernel(q_ref, kl_ref, vl_ref, kx_ref, vx_ref, lam_ref, subln_ref, o_ref, qs_ref, acc_ref, *,
                            nk_lat, lam_init):
    kt = pl.program_id(2)
    n_sc = 2 * D_HEADS
    vw = 2 * HEAD_DIM

    @pl.when(kt == 0)
    def _():
        q = q_ref[...]
        for idx in range(n_sc):
            qs_ref[idx] = q[:, idx * HEAD_DIM:(idx + 1) * HEAD_DIM]
        acc_ref[...] = jnp.zeros(acc_ref.shape, F32)

    def step(k, v):
        n = k.shape[0]
        chunk = min(n, 2 * DENSE_KEY_CHUNK)
        lane = lax.broadcasted_iota(jnp.int32, (chunk, VT_PAD), 1)
        ones_cols = jnp.where(lane == 0, 1.0, 0.0).astype(BF16)
        for h in range(D_HEADS):
            upd = [None, None]
            for k0 in range(0, n, chunk):
                ks = slice(k0, k0 + chunk)
                v1 = jnp.concatenate([v[ks, h * vw:(h + 1) * vw], ones_cols], axis=1)
                for c in range(2):
                    idx = 2 * h + c
                    p_t = jnp.exp2(_dot_nt(k[ks, idx * HEAD_DIM:(idx + 1) * HEAD_DIM], qs_ref[idx])).astype(BF16)
                    term = lax.dot_general(v1, p_t, (((0,), (0,)), ((), ())), preferred_element_type=F32)
                    upd[c] = term if upd[c] is None else upd[c] + term
            for c in range(2):
                acc_ref[2 * h + c] += upd[c]

    @pl.when(kt < nk_lat)
    def _():
        step(kl_ref[...], vl_ref[...])

    @pl.when(kt == nk_lat)
    def _():
        step(kx_ref[...], vx_ref[...])
        lam = _diff_lambda(lam_ref, lam_init)
        for h in range(D_HEADS):
            a1, a2 = acc_ref[2 * h], acc_ref[2 * h + 1]
            o_t = a1[:vw] * (1.0 / a1[vw:vw + 1]) - lam * (a2[:vw] * (1.0 / a2[vw:vw + 1]))
            o_ref[:, h * vw:(h + 1) * vw] = _finish_diff(jnp.transpose(o_t), subln_ref, lam_init).astype(BF16)


def _dense_diff(qk_lat, rest_lat, qk_ctx, rest_ctx, lam_d, subln, lam_init, tq, tk, bounded):
    bsz, seq, _ = qk_lat.shape
    ctx_len = qk_ctx.shape[1]
    nk_lat = seq // tk
    width = D_HEADS * 2 * HEAD_DIM
    kcol = QK_OFF["dk"] // width
    vcol = REST_OFF["dv"] // width
    if bounded:
        body = functools.partial(_dense_d_bounded_kernel, nk_lat=nk_lat, lam_init=lam_init)
        scratch = [pltpu.VMEM((2 * D_HEADS, tq, HEAD_DIM), BF16),
                   pltpu.VMEM((2 * D_HEADS, 2 * HEAD_DIM + VT_PAD, tq), F32)]
    else:
        body = functools.partial(_dense_d_kernel, nk_lat=nk_lat, lam_init=lam_init)
        scratch = [pltpu.VMEM((2 * D_HEADS, tq, HEAD_DIM), BF16),
                   pltpu.VMEM((2 * D_HEADS, tq, LANES), F32),
                   pltpu.VMEM((2 * D_HEADS, tq, 2 * HEAD_DIM), F32),
                   pltpu.VMEM((2 * D_HEADS, tq, 1), F32)]
    return pl.pallas_call(
        body,
        grid=(bsz, seq // tq, nk_lat + 1),
        in_specs=[pl.BlockSpec((None, tq, width), lambda b, i, kt: (b, i, QK_OFF["dq"] // width)),
                  pl.BlockSpec((None, tk, width), lambda b, i, kt: (b, jnp.minimum(kt, nk_lat - 1), kcol)),
                  pl.BlockSpec((None, tk, width), lambda b, i, kt: (b, jnp.minimum(kt, nk_lat - 1), vcol)),
                  pl.BlockSpec((None, ctx_len, width), lambda b, i, kt: (b, 0, kcol)),
                  pl.BlockSpec((None, ctx_len, width), lambda b, i, kt: (b, 0, vcol)),
                  pl.BlockSpec((4, HEAD_DIM), lambda b, i, kt: (0, 0)),
                  pl.BlockSpec((1, 2 * HEAD_DIM), lambda b, i, kt: (0, 0))],
        out_specs=pl.BlockSpec((None, tq, BRANCH_W), lambda b, i, kt: (b, i, 0)),
        out_shape=jax.ShapeDtypeStruct((bsz, seq, BRANCH_W), BF16),
        scratch_shapes=scratch,
        compiler_params=_cparams(("parallel", "parallel", "arbitrary")),
        name="mixer_d_diff_bounded" if bounded else "mixer_d_diff",
    )(qk_lat, qk_lat, rest_lat, qk_ctx, rest_ctx, lam_d, subln.reshape(1, 2 * HEAD_DIM))


def _ctx_kernel(sink_ref, qk_ref, v_ref, lam_ref, subln_ref, o_ref, *, lam_init):
    qk = qk_ref[...]
    vals = v_ref[...]
    ctx_len = qk.shape[0]
    v_off = {name: REST_OFF[name] - REST_OFF[_FIRST_VALUE] for name in ("av", "cv", "bv", "dv")}

    def cols(name, start, width):
        c0 = QK_OFF[name] + start
        return qk[:, c0:c0 + width]

    def gqa(qname, kname, vname, out_off, n_kv, with_sink):
        g = 8 // n_kv
        for kv in range(n_kv):
            qs = _stack_heads(cols(qname, 0, 8 * HEAD_DIM), kv * g, g)
            s = _dot_nt(qs, cols(kname, kv * HEAD_DIM, HEAD_DIM))
            extra = None
            if with_sink:
                extra = jnp.concatenate([jnp.full((ctx_len, 1), sink_ref[kv * g + h], F32) for h in range(g)], axis=0)
            p, l = _softmax_rows(s, extra)
            vv = vals[:, v_off[vname] + kv * HEAD_DIM:v_off[vname] + (kv + 1) * HEAD_DIM]
            o = _dot(p.astype(BF16), vv) / l
            for h in range(g):
                c0 = out_off + (kv * g + h) * HEAD_DIM
                o_ref[:, c0:c0 + HEAD_DIM] = o[h * ctx_len:(h + 1) * ctx_len].astype(BF16)

    gqa("aq", "ak", "av", 0 * BRANCH_W, A_KV, True)
    gqa("bq", "bk", "bv", 1 * BRANCH_W, B_HEADS, False)
    gqa("cq", "ck", "cv", 2 * BRANCH_W, C_KV, False)

    lam = _diff_lambda(lam_ref, lam_init)
    for h in range(D_HEADS):
        base = h * 2 * HEAD_DIM
        p1, l1 = _softmax_rows(_dot_nt(cols("dq", base, HEAD_DIM), cols("dk", base, HEAD_DIM)))
        p2, l2 = _softmax_rows(_dot_nt(cols("dq", base + HEAD_DIM, HEAD_DIM), cols("dk", base + HEAD_DIM, HEAD_DIM)))
        pd = p1 / l1 - lam * (p2 / l2)
        o = _dot(pd.astype(BF16), vals[:, v_off["dv"] + base:v_off["dv"] + base + 2 * HEAD_DIM])
        c0 = 3 * BRANCH_W + base
        o_ref[:, c0:c0 + 2 * HEAD_DIM] = _finish_diff(o, subln_ref, lam_init).astype(BF16)


def _ctx_attention(qk_ctx, v_ctx, sink2, lam_d, subln, lam_init):
    bsz, ctx_len, _ = qk_ctx.shape
    return pl.pallas_call(
        functools.partial(_ctx_kernel, lam_init=lam_init),
        grid=(bsz,),
        in_specs=[pl.BlockSpec(memory_space=pltpu.SMEM),
                  pl.BlockSpec((None, ctx_len, QK_COLS), lambda b: (b, 0, 0)),
                  pl.BlockSpec((None, ctx_len, v_ctx.shape[2]), lambda b: (b, 0, 0)),
                  pl.BlockSpec((4, HEAD_DIM), lambda b: (0, 0)),
                  pl.BlockSpec((1, 2 * HEAD_DIM), lambda b: (0, 0))],
        out_specs=pl.BlockSpec((None, ctx_len, N_BRANCH * BRANCH_W), lambda b: (b, 0, 0)),
        out_shape=jax.ShapeDtypeStruct((bsz, ctx_len, N_BRANCH * BRANCH_W), BF16),
        compiler_params=_cparams(("parallel",)),
        name="ctx_attention",
    )(sink2, qk_ctx, v_ctx, lam_d, subln.reshape(1, 2 * HEAD_DIM))


def _merge_kernel(x_ref, mod_ref, ya_ref, yb_ref, yc_ref, yd_ref, gp_ref, mg_ref, wbr_ref, wout_ref, o_ref):
    gate = mod_ref[...][:, 2 * D_MODEL:]
    merged = None
    for n, y_ref in enumerate((ya_ref, yb_ref, yc_ref, yd_ref)):
        yg = (y_ref[...].astype(F32) * gp_ref[:, n * BRANCH_W:(n + 1) * BRANCH_W].astype(F32)).astype(BF16)
        term = mg_ref[:, n * D_MODEL:(n + 1) * D_MODEL].astype(F32) * _dot(yg, wbr_ref[n])
        merged = term if merged is None else merged + term
    o_ref[...] = x_ref[...] + gate * _dot(merged.astype(BF16), wout_ref[...])


def _merge(x2, mod3, ys, y_cols, rest, w_br, w_out, tm, row_of_tile):
    rows = x2.shape[0]
    y_specs = [pl.BlockSpec((tm, BRANCH_W), (lambda i, c=c: (i, c))) for c in y_cols]
    return pl.pallas_call(
        _merge_kernel,
        grid=(rows // tm,),
        in_specs=[pl.BlockSpec((tm, D_MODEL), lambda i: (i, 0)),
                  pl.BlockSpec((None, 1, 3 * D_MODEL), lambda i: (row_of_tile(i), 0, 0))]
                 + y_specs
                 + [pl.BlockSpec((tm, N_BRANCH * BRANCH_W), lambda i: (i, REST_OFF["ag"] // (N_BRANCH * BRANCH_W))),
                    pl.BlockSpec((tm, N_BRANCH * D_MODEL), lambda i: (i, 0)),
                    pl.BlockSpec((N_BRANCH, BRANCH_W, D_MODEL), lambda i: (0, 0, 0)),
                    pl.BlockSpec((D_MODEL, D_MODEL), lambda i: (0, 0))],
        out_specs=pl.BlockSpec((tm, D_MODEL), lambda i: (i, 0)),
        out_shape=jax.ShapeDtypeStruct((rows, D_MODEL), F32),
        compiler_params=_cparams(("parallel",)),
        name="gated_merge",
    )(x2, mod3, *ys, rest, rest, w_br, w_out)


def _rope_tables(seq):
    t = np.arange(seq, dtype=np.int32)
    pos = np.stack([t // GRID_W, t % GRID_W], axis=-1).astype(np.float32)
    n_freq = HEAD_DIM // 4
    freqs = (np.float32(ROPE_THETA) ** (-np.arange(n_freq, dtype=np.float32) / np.float32(n_freq))).astype(np.float32)
    ang = (pos[:, :, None] * freqs[None, None, :]).astype(np.float32)
    ang = np.concatenate([ang, ang], axis=-1).reshape(seq, HEAD_DIM)
    sign = np.where((np.arange(HEAD_DIM) % 32) < 16, -1.0, 1.0).astype(np.float32)
    reps = NORM_GROUP // HEAD_DIM
    cos = np.tile(np.cos(ang).astype(np.float32), (1, reps))
    sin = np.tile(np.sin(ang).astype(np.float32) * sign, (1, reps))
    return jnp.asarray(cos), jnp.asarray(sin)


def _regroup_cols(w, order, total):
    parts = [w[:, _ORIG[n][0]:_ORIG[n][1]] for n in order]
    used = sum(p.shape[1] for p in parts)
    if total > used:
        parts.append(jnp.zeros((w.shape[0], total - used), w.dtype))
    return jnp.concatenate(parts, axis=1)


def _qk_gain_row(g):
    parts = []
    for name in _QK_ORDER:
        mixer = "abcd".index(name[0])
        is_q = name[1] == "q"
        width = _ORIG[name][1] - _ORIG[name][0]
        gain = g[mixer, 0] * (QK_SCALE * LOG2E) if is_q else g[mixer, 1]
        parts.append(jnp.tile(gain, width // HEAD_DIM))
    parts.append(jnp.ones((QK_COLS - _QK_USED,), g.dtype))
    return jnp.concatenate(parts).reshape(1, QK_COLS).astype(F32)


def kernel(x, c, ctx, c_ctx, norm_w, w_ada, b_ada, w_in, qk_gain, sink_a, rpb_b, lam_d, subln_d, w_br, w_out):
    bsz, seq, _ = x.shape
    ctx_len = ctx.shape[1]
    depth = w_ada.shape[0]
    assert seq % (2 * A_WINDOW) == 0 and ctx_len % LANES == 0 and bsz <= 6

    tm_lat = 2048
    tm_rest = 4096
    tiles_per_batch = seq // tm_lat
    tk_dense = 2048
    tq_bounded, tq_rowmax = 512, 256
    assert seq % tk_dense == 0
    tm_merge = 512
    merge_tiles_per_batch = seq // tm_merge

    cvec = jnp.concatenate([c, c_ctx[None, :], jnp.zeros((8 - bsz - 1, D_MODEL), F32)], axis=0)
    mod_all = _ada(cvec, w_ada, b_ada)
    cos, sin = _rope_tables(seq)
    ones_bd = jnp.asarray(np.kron(np.eye(NORM_GROUP // HEAD_DIM), np.ones((HEAD_DIM, HEAD_DIM))), BF16)

    x2 = x.reshape(bsz * seq, D_MODEL)
    c2 = ctx.reshape(bsz * ctx_len, D_MODEL)
    for l in range(depth):
        need_ctx = l < depth - 1
        lam_init = 0.8 - 0.6 * math.exp(-0.3 * l)
        mod3 = mod_all[l].reshape(8, 1, 3 * D_MODEL)
        w_qk = _regroup_cols(w_in[l], _QK_ORDER, QK_COLS).astype(BF16)
        w_rest = _regroup_cols(w_in[l], _REST_ORDER, REST_COLS).astype(BF16)
        gain_row = _qk_gain_row(qk_gain[l])

        hx = _prenorm(x2, norm_w[l], mod3, tm_lat, lambda i: i // tiles_per_batch)
        hc = _prenorm(c2, norm_w[l], mod3, bsz * ctx_len, lambda i: bsz)
        qk_lat = _proj_qk(hx, w_qk, gain_row, ones_bd, tm_lat, cos, sin, tiles_per_batch).reshape(bsz, seq, QK_COLS)
        qk_ctx = _proj_qk(hc, w_qk, gain_row, ones_bd, bsz * ctx_len).reshape(bsz, ctx_len, QK_COLS)
        rest_lat = _proj_rest(hx, w_rest, tm_rest).reshape(bsz, seq, REST_COLS)
        rest_ctx = _proj_rest(hc, w_rest, bsz * ctx_len).reshape(bsz, ctx_len, REST_COLS)

        sink2 = sink_a[l] * LOG2E
        rpb2 = rpb_b[l] * LOG2E
        tq_nbr = 2 * A_WINDOW
        proj = (qk_lat, rest_lat, qk_ctx, rest_ctx)

        def qk_bound(g):
            return HEAD_DIM * QK_SCALE * LOG2E * jnp.max(jnp.abs(g[0])) * jnp.max(jnp.abs(g[1])) * 1.02

        y_a = lax.cond(qk_bound(qk_gain[l, 0]) <= LOGIT_BOUND,
                       lambda: _window_attention(*proj, sink2, True),
                       lambda: _window_attention(*proj, sink2, False))
        y_b = lax.cond(qk_bound(qk_gain[l, 1]) + jnp.max(jnp.abs(rpb_b[l])) * LOG2E <= LOGIT_BOUND,
                       lambda: _neighborhood_attention(*proj, rpb2, tq_nbr, True),
                       lambda: _neighborhood_attention(*proj, rpb2, tq_nbr, False))
        y_c = lax.cond(qk_bound(qk_gain[l, 2]) <= LOGIT_BOUND,
                       lambda: _dense_gqa(*proj, 2 * tq_bounded, tk_dense, True),
                       lambda: _dense_gqa(*proj, tq_rowmax, tk_dense, False))
        y_d = lax.cond(qk_bound(qk_gain[l, 3]) <= LOGIT_BOUND,
                       lambda: _dense_diff(*proj, lam_d[l], subln_d[l], lam_init, 2 * tq_bounded, tk_dense, True),
                       lambda: _dense_diff(*proj, lam_d[l], subln_d[l], lam_init, tq_rowmax, tk_dense, False))

        w_br_l = w_br[l].astype(BF16)
        w_out_l = w_out[l].astype(BF16)
        ys = [y.reshape(bsz * seq, BRANCH_W) for y in (y_a, y_b, y_c, y_d)]
        x_new = _merge(x2, mod3, ys, (0, 0, 0, 0), rest_lat.reshape(bsz * seq, REST_COLS), w_br_l, w_out_l,
                       tm_merge, lambda i: i // merge_tiles_per_batch)
        if need_ctx:
            y_ctx = _ctx_attention(qk_ctx, rest_ctx[:, :, REST_OFF[_FIRST_VALUE]:], sink2, lam_d[l], subln_d[l],
                                   lam_init)
            y_ctx2 = y_ctx.reshape(bsz * ctx_len, N_BRANCH * BRANCH_W)
            c2 = _merge(c2, mod3, [y_ctx2] * 4, (0, 1, 2, 3), rest_ctx.reshape(bsz * ctx_len, REST_COLS),
                        w_br_l, w_out_l, bsz * ctx_len, lambda i: bsz)
        x2 = x_new
    return x2.reshape(bsz, seq, D_MODEL)
```

```python
import functools
import math

import numpy as np
import jax
import jax.numpy as jnp
from jax import lax
from jax.experimental import pallas as pl
from jax.experimental.pallas import tpu as pltpu

F32 = jnp.float32
BF16 = jnp.bfloat16

D_MODEL = 1024
GRID_W = 64
HEAD_DIM = 64
BRANCH_W = 512
N_BRANCH = 4
A_HEADS, A_KV, A_WINDOW = 8, 2, 128
B_HEADS, NB_ROWS, NB_COLS = 8, 8, 16
C_HEADS, C_KV = 8, 2
D_HEADS = 4
ROPE_THETA = 10000.0
EPS = 1e-6
NEG_INF = -1e30
QK_SCALE = HEAD_DIM ** -0.5
LOG2E = math.log2(math.e)
LOGIT_BOUND = 60.0

V7X_VMEM_LIMIT_BYTES = 56 * 1024 * 1024
LANES = 128

_ORIG = dict(aq=(0, 512), ak=(512, 640), av=(640, 768), ag=(768, 1280),
             bq=(1280, 1792), bk=(1792, 2304), bv=(2304, 2816), bg=(2816, 3328),
             cq=(3328, 3840), ck=(3840, 3968), cv=(3968, 4096), cg=(4096, 4608),
             dq=(4608, 5120), dk=(5120, 5632), dv=(5632, 6144), dg=(6144, 6656),
             mg=(6656, 10752))
_QK_ORDER = ("aq", "cq", "dq", "bq", "bk", "dk", "ak", "ck")
_REST_ORDER = ("mg", "ag", "bg", "cg", "dg", "dv", "bv", "av", "cv")
_FIRST_VALUE = "dv"


def _offsets(order):
    off, out = 0, {}
    for name in order:
        lo, hi = _ORIG[name]
        out[name] = off
        off += hi - lo
    return out, off


PROJ_TN = 512
PROJ_ROW_CHUNK = 512
REST_ROW_CHUNK = 1024
VT_PAD = 16
DENSE_KEY_CHUNK = 512
TM_QK, TM_REST, TM_MERGE = 2048, 4096, 512
TQ_NBR = (NB_ROWS // 2) * GRID_W
TQ_DENSE, TQ_DENSE_ROWMAX, TK_DENSE = 1024, 256, 2048
NORM_GROUP = 256
QK_OFF, _QK_USED = _offsets(_QK_ORDER)
REST_OFF, _REST_USED = _offsets(_REST_ORDER)
QK_COLS = -(-_QK_USED // PROJ_TN) * PROJ_TN
REST_COLS = -(-_REST_USED // PROJ_TN) * PROJ_TN
_ROPE_LO_END = QK_OFF["bq"] // PROJ_TN
_ROPE_HI_START = QK_OFF["dk"] // PROJ_TN
_MG_BLOCKS = (REST_OFF["ag"]) // PROJ_TN
_GP_BLOCKS_END = REST_OFF[_FIRST_VALUE] // PROJ_TN
assert QK_OFF["bq"] % PROJ_TN == 0 and QK_OFF["dk"] % PROJ_TN == 0
assert REST_OFF["ag"] % PROJ_TN == 0 and REST_OFF[_FIRST_VALUE] % PROJ_TN == 0


def _cparams(sem):
    return pltpu.CompilerParams(dimension_semantics=sem, vmem_limit_bytes=V7X_VMEM_LIMIT_BYTES)


def _sigmoid(x):
    return 0.5 * jnp.tanh(0.5 * x) + 0.5


def _dot_nt(a, b):
    return lax.dot_general(a, b, (((1,), (1,)), ((), ())), preferred_element_type=F32)


def _dot(a, b):
    return jnp.dot(a, b, preferred_element_type=F32)


def _ada_kernel(c_ref, w_ref, b_ref, o_ref):
    c = c_ref[...]
    o_ref[...] = _dot(c * _sigmoid(c), w_ref[...]) + b_ref[...]


def _ada(cvec, w_ada, b_ada):
    depth = w_ada.shape[0]
    tn = 512
    return pl.pallas_call(
        _ada_kernel,
        grid=(depth, 3 * D_MODEL // tn),
        in_specs=[pl.BlockSpec((8, D_MODEL), lambda l, j: (0, 0)),
                  pl.BlockSpec((None, D_MODEL, tn), lambda l, j: (l, 0, j)),
                  pl.BlockSpec((None, 1, tn), lambda l, j: (l, 0, j))],
        out_specs=pl.BlockSpec((None, 8, tn), lambda l, j: (l, 0, j)),
        out_shape=jax.ShapeDtypeStruct((depth, 8, 3 * D_MODEL), F32),
        compiler_params=_cparams(("parallel", "parallel")),
        name="ada_mod",
    )(cvec, w_ada, b_ada.reshape(depth, 1, 3 * D_MODEL))


def _prenorm_kernel(x_ref, nw_ref, mod_ref, o_ref):
    x = x_ref[...]
    mod = mod_ref[...]
    shift, scale = mod[:, :D_MODEL], mod[:, D_MODEL:2 * D_MODEL]
    y = x * lax.rsqrt(jnp.mean(x * x, axis=-1, keepdims=True) + EPS) * nw_ref[...]
    o_ref[...] = (y * (1.0 + scale) + shift).astype(BF16)


def _prenorm(x2, norm_w, mod3, tm, row_of_tile):
    rows = x2.shape[0]
    return pl.pallas_call(
        _prenorm_kernel,
        grid=(rows // tm,),
        in_specs=[pl.BlockSpec((tm, D_MODEL), lambda i: (i, 0)),
                  pl.BlockSpec((1, D_MODEL), lambda i: (0, 0)),
                  pl.BlockSpec((None, 1, 3 * D_MODEL), lambda i: (row_of_tile(i), 0, 0))],
        out_specs=pl.BlockSpec((tm, D_MODEL), lambda i: (i, 0)),
        out_shape=jax.ShapeDtypeStruct((rows, D_MODEL), BF16),
        compiler_params=_cparams(("parallel",)),
        name="prenorm",
    )(x2, norm_w.reshape(1, D_MODEL), mod3)


def _rot_half_unsigned(n):
    lane = lax.broadcasted_iota(jnp.int32, n.shape, 1)
    return jnp.where((lane & 31) < 16, pltpu.roll(n, LANES - 16, 1), pltpu.roll(n, 16, 1))


def _proj_qk_kernel(*refs, rope):
    if rope:
        hx_ref, w_ref, gain_ref, ones_ref, cos_ref, sin_ref, o_ref = refs
    else:
        hx_ref, w_ref, gain_ref, ones_ref, o_ref = refs
    w = w_ref[...]
    ones_bd = ones_ref[...]
    rows = hx_ref.shape[0]
    chunk = min(rows, PROJ_ROW_CHUNK)
    if rope:
        j = pl.program_id(1)
        use = jnp.logical_or(j < _ROPE_LO_END, j >= _ROPE_HI_START)
    for r0 in range(0, rows, chunk):
        rs = slice(r0, r0 + chunk)
        acc_all = _dot(hx_ref[rs, :], w)
        if rope:
            cos = jnp.where(use, cos_ref[rs, :], 1.0)
            sin = jnp.where(use, sin_ref[rs, :], 0.0)
        for c in range(PROJ_TN // NORM_GROUP):
            sl = slice(c * NORM_GROUP, (c + 1) * NORM_GROUP)
            acc = acc_all[:, sl]
            ss = _dot((acc * acc).astype(BF16), ones_bd)
            n = acc * lax.rsqrt(ss * (1.0 / HEAD_DIM) + EPS) * gain_ref[:, sl]
            if rope:
                rot = jnp.concatenate([_rot_half_unsigned(n[:, :LANES]), _rot_half_unsigned(n[:, LANES:])], axis=1)
                n = n * cos + rot * sin
            o_ref[rs, sl] = n.astype(BF16)


def _proj_qk(hx, w_qk, gain_row, ones_bd, tm, cos=None, sin=None, tiles_per_batch=None):
    rows = hx.shape[0]
    rope = cos is not None
    in_specs = [pl.BlockSpec((tm, D_MODEL), lambda i, j: (i, 0)),
                pl.BlockSpec((D_MODEL, PROJ_TN), lambda i, j: (0, j)),
                pl.BlockSpec((1, PROJ_TN), lambda i, j: (0, j)),
                pl.BlockSpec((NORM_GROUP, NORM_GROUP), lambda i, j: (0, 0))]
    args = [hx, w_qk, gain_row, ones_bd]
    if rope:
        in_specs += [pl.BlockSpec((tm, NORM_GROUP), lambda i, j: (i % tiles_per_batch, 0))] * 2
        args += [cos, sin]
    return pl.pallas_call(
        functools.partial(_proj_qk_kernel, rope=rope),
        grid=(rows // tm, QK_COLS // PROJ_TN),
        in_specs=in_specs,
        out_specs=pl.BlockSpec((tm, PROJ_TN), lambda i, j: (i, j)),
        out_shape=jax.ShapeDtypeStruct((rows, QK_COLS), BF16),
        compiler_params=_cparams(("parallel", "arbitrary")),
        name="proj_qk_rope" if rope else "proj_qk",
    )(*args)


def _proj_rest_kernel(hx_ref, w_ref, o_ref):
    j = pl.program_id(1)
    w = w_ref[...]
    rows = hx_ref.shape[0]
    chunk = min(rows, REST_ROW_CHUNK)
    for r0 in range(0, rows, chunk):
        acc = _dot(hx_ref[r0:r0 + chunk, :], w)
        sg = _sigmoid(acc)
        o_ref[r0:r0 + chunk, :] = jnp.where(j < _MG_BLOCKS, sg,
                                            jnp.where(j < _GP_BLOCKS_END, acc * sg, acc)).astype(BF16)


def _proj_rest(hx, w_rest, tm):
    rows = hx.shape[0]
    return pl.pallas_call(
        _proj_rest_kernel,
        grid=(rows // tm, REST_COLS // PROJ_TN),
        in_specs=[pl.BlockSpec((tm, D_MODEL), lambda i, j: (i, 0)),
                  pl.BlockSpec((D_MODEL, PROJ_TN), lambda i, j: (0, j))],
        out_specs=pl.BlockSpec((tm, PROJ_TN), lambda i, j: (i, j)),
        out_shape=jax.ShapeDtypeStruct((rows, REST_COLS), BF16),
        compiler_params=_cparams(("parallel", "arbitrary")),
        name="proj_rest",
    )(hx, w_rest)


def _stack_heads(q, first_head, n):
    return jnp.concatenate([q[:, (first_head + g) * HEAD_DIM:(first_head + g + 1) * HEAD_DIM] for g in range(n)],
                           axis=0)


def _softmax_rows(s, extra=None):
    m = jnp.max(s, axis=-1, keepdims=True)
    if extra is not None:
        m = jnp.maximum(m, extra)
    p = jnp.exp2(s - m)
    l = jnp.sum(p, axis=-1, keepdims=True)
    if extra is not None:
        l = l + jnp.exp2(extra - m)
    return p, l


def _with_ones_column(v, pad=HEAD_DIM):
    lane = lax.broadcasted_iota(jnp.int32, (v.shape[0], pad), 1)
    return jnp.concatenate([v, jnp.where(lane == 0, 1.0, 0.0).astype(v.dtype)], axis=1)


def _lane_partial_sums(p):
    part = p[:, :LANES]
    for c in range(1, p.shape[1] // LANES):
        part = part + p[:, c * LANES:(c + 1) * LANES]
    return part


def _diff_lambda(lam_ref, lam_init):
    lf = lam_ref[...]
    a = jnp.sum(lf[0:1] * lf[1:2], axis=-1, keepdims=True)
    b = jnp.sum(lf[2:3] * lf[3:4], axis=-1, keepdims=True)
    return jnp.exp(a) - jnp.exp(b) + lam_init


def _finish_diff(o, subln_ref, lam_init):
    y = o * lax.rsqrt(jnp.mean(o * o, axis=-1, keepdims=True) + EPS) * subln_ref[...]
    return y * (1.0 - lam_init)


def _win_kernel(sink_ref, q_ref, kp_ref, kc_ref, kn_ref, kx_ref, vp_ref, vc_ref, vn_ref, vx_ref, o_ref, *,
                tq, seq):
    i = pl.program_id(1)
    g = A_HEADS // A_KV
    q = q_ref[...]
    k_all = jnp.concatenate([kp_ref[...], kc_ref[...], kn_ref[...], kx_ref[...]], axis=0)
    v_all = jnp.concatenate([vp_ref[...], vc_ref[...], vn_ref[...], vx_ref[...]], axis=0)
    nw = 2 * tq
    t = lax.broadcasted_iota(jnp.int32, (g * tq, nw), 0) & (tq - 1)
    j = lax.broadcasted_iota(jnp.int32, (g * tq, nw), 1)
    rel = j - tq // 2 - t
    kpos = i * tq - tq // 2 + j
    bad = jnp.where(jnp.abs(rel) > A_WINDOW, 1, 0) + jnp.where(kpos < 0, 1, 0) + jnp.where(kpos >= seq, 1, 0)
    for kv in range(A_KV):
        qs = _stack_heads(q, kv * g, g)
        kk = k_all[:, kv * HEAD_DIM:(kv + 1) * HEAD_DIM]
        vv = v_all[:, kv * HEAD_DIM:(kv + 1) * HEAD_DIM]
        s = _dot_nt(qs, kk)
        s_win = jnp.where(bad > 0, NEG_INF, s[:, :nw])
        s_ctx = s[:, nw:]
        sink = jnp.concatenate([jnp.full((tq, 1), sink_ref[kv * g + h], F32) for h in range(g)], axis=0)
        m = jnp.maximum(jnp.maximum(jnp.max(s_win, axis=-1, keepdims=True),
                                    jnp.max(s_ctx, axis=-1, keepdims=True)), sink)
        pw = jnp.exp2(s_win - m)
        pc = jnp.exp2(s_ctx - m)
        l = jnp.sum(pw, axis=-1, keepdims=True) + jnp.sum(pc, axis=-1, keepdims=True) + jnp.exp2(sink - m)
        o = (_dot(pw.astype(BF16), vv[:nw]) + _dot(pc.astype(BF16), vv[nw:])) / l
        for h in range(g):
            c0 = (kv * g + h) * HEAD_DIM
            o_ref[:, c0:c0 + HEAD_DIM] = o[h * tq:(h + 1) * tq].astype(BF16)


def _win_bounded_kernel(sink_ref, q_ref, kp_ref, kc_ref, kn_ref, kx_ref, vp_ref, vc_ref, vn_ref, vx_ref, mask_ref,
                        o_ref, *, tq):
    g = A_HEADS // A_KV
    q = q_ref[...]
    mask_t = mask_ref[...]
    sink = sink_ref[...]
    k_all = jnp.concatenate([kp_ref[...], kc_ref[...], kn_ref[...], kx_ref[...]], axis=0)
    v_all = jnp.concatenate([vp_ref[...], vc_ref[...], vn_ref[...], vx_ref[...]], axis=0)
    for kv in range(A_KV):
        sl = slice(kv * HEAD_DIM, (kv + 1) * HEAD_DIM)
        qs = _stack_heads(q, kv * g, g)
        stab = [jnp.maximum(sink[:, kv * g + h:kv * g + h + 1], 0.0) for h in range(g)]
        shift_t = jnp.concatenate([mask_t - stab[h] for h in range(g)], axis=1)
        p_t = jnp.exp2(_dot_nt(k_all[:, sl], qs) + shift_t).astype(BF16)
        acc = lax.dot_general(_with_ones_column(v_all[:, sl], VT_PAD), p_t, (((0,), (0,)), ((), ())),
                              preferred_element_type=F32)
        for h in range(g):
            a = acc[:, h * tq:(h + 1) * tq]
            l = a[HEAD_DIM:HEAD_DIM + 1] + jnp.exp2(sink[:, kv * g + h:kv * g + h + 1] - stab[h])
            c0 = (kv * g + h) * HEAD_DIM
            o_ref[:, c0:c0 + HEAD_DIM] = jnp.transpose(a[:HEAD_DIM] * (1.0 / l)).astype(BF16)


def _window_mask_tables(seq, tq, ctx_len):
    nt = seq // tq
    tabs = []
    for i0 in (0, 1, nt - 1):
        t = np.arange(tq)[None, :]
        j = np.arange(2 * tq)[:, None]
        kpos = i0 * tq - tq // 2 + j
        ok = (np.abs(j - tq // 2 - t) <= A_WINDOW) & (kpos >= 0) & (kpos < seq)
        win = np.where(ok, 0.0, NEG_INF).astype(np.float32)
        tabs.append(np.concatenate([win, np.zeros((ctx_len, tq), np.float32)], axis=0))
    return jnp.asarray(np.stack(tabs, axis=0))


def _window_attention(qk_lat, rest_lat, qk_ctx, rest_ctx, sink2, bounded):
    bsz, seq, _ = qk_lat.shape
    ctx_len = qk_ctx.shape[1]
    tq = 2 * A_WINDOW
    nt = seq // tq
    half = tq // 2
    kcol = QK_OFF["ak"] // LANES
    vcol = REST_OFF["av"] // LANES
    n_half = seq // half

    def kv_specs(col):
        return [pl.BlockSpec((None, half, LANES), lambda b, i: (b, jnp.maximum(2 * i - 1, 0), col)),
                pl.BlockSpec((None, tq, LANES), lambda b, i: (b, i, col)),
                pl.BlockSpec((None, half, LANES), lambda b, i: (b, jnp.minimum(2 * i + 2, n_half - 1), col)),
                pl.BlockSpec((None, ctx_len, LANES), lambda b, i: (b, 0, col))]

    def tile_variant(i):
        return jnp.where(i == 0, 0, jnp.where(i == nt - 1, 2, 1))

    q_spec = pl.BlockSpec((None, tq, A_HEADS * HEAD_DIM), lambda b, i: (b, i, QK_OFF["aq"] // 512))
    kv_args = (qk_lat, qk_lat, qk_lat, qk_ctx, rest_lat, rest_lat, rest_lat, rest_ctx)
    if bounded:
        body = functools.partial(_win_bounded_kernel, tq=tq)
        in_specs = ([pl.BlockSpec((1, A_HEADS), lambda b, i: (0, 0)), q_spec] + kv_specs(kcol) + kv_specs(vcol)
                    + [pl.BlockSpec((None, 2 * tq + ctx_len, tq), lambda b, i: (tile_variant(i), 0, 0))])
        args = (sink2.reshape(1, A_HEADS), qk_lat) + kv_args + (_window_mask_tables(seq, tq, ctx_len),)
    else:
        body = functools.partial(_win_kernel, tq=tq, seq=seq)
        in_specs = [pl.BlockSpec(memory_space=pltpu.SMEM), q_spec] + kv_specs(kcol) + kv_specs(vcol)
        args = (sink2, qk_lat) + kv_args
    return pl.pallas_call(
        body,
        grid=(bsz, nt),
        in_specs=in_specs,
        out_specs=pl.BlockSpec((None, tq, BRANCH_W), lambda b, i: (b, i, 0)),
        out_shape=jax.ShapeDtypeStruct((bsz, seq, BRANCH_W), BF16),
        compiler_params=_cparams(("parallel", "parallel")),
        name="mixer_a_window_bounded" if bounded else "mixer_a_window",
    )(*args)


def _nbr_kernel(q_ref, kp_ref, kc_ref, kn_ref, kx_ref, vp_ref, vc_ref, vn_ref, vx_ref, bias_ref, rmask_ref, o_ref,
                *, tq, bounded):
    q = q_ref[...]
    k_all = jnp.concatenate([kp_ref[...], kc_ref[...], kn_ref[...], kx_ref[...]], axis=0)
    v_all = jnp.concatenate([vp_ref[...], vc_ref[...], vn_ref[...], vx_ref[...]], axis=0)
    rmask = rmask_ref[...]
    nw = 3 * tq
    for h in range(B_HEADS):
        sl = slice(h * HEAD_DIM, (h + 1) * HEAD_DIM)
        s = _dot_nt(q[:, sl], k_all[:, sl])
        s = jnp.concatenate([s[:, :nw] + (bias_ref[h] + rmask), s[:, nw:]], axis=1)
        if not bounded:
            s = s - jnp.max(s, axis=-1, keepdims=True)
        acc = _dot(jnp.exp2(s).astype(BF16), _with_ones_column(v_all[:, sl]))
        o_ref[:, sl] = (acc[:, :HEAD_DIM] * (1.0 / acc[:, HEAD_DIM:HEAD_DIM + 1])).astype(BF16)


def _nbr_bias_tables(rpb, seq, tq):
    rows = seq // GRID_W
    nt = seq // tq
    kr = min(NB_ROWS, rows)
    qr = tq // GRID_W
    n_heads = rpb.shape[0]
    assert NB_ROWS - 1 - qr - (qr - 1) >= 0 and NB_ROWS - 1 - qr + 3 * qr <= 2 * NB_ROWS - 1
    pad = GRID_W - 1
    rpb_pad = jnp.pad(rpb.astype(F32), ((0, 0), (0, 0), (pad, pad)))
    toep = jnp.stack([rpb_pad[:, :, pad + NB_COLS - 1 - c:pad + NB_COLS - 1 - c + GRID_W] for c in range(GRID_W)],
                     axis=2)
    c = np.arange(GRID_W)
    cstart = np.clip(c - NB_COLS // 2, 0, GRID_W - NB_COLS)
    col_ok = (c[None, :] >= cstart[:, None]) & (c[None, :] < cstart[:, None] + NB_COLS)
    toep = jnp.where(col_ok[None, None], toep, NEG_INF)
    a0 = NB_ROWS - 1 - qr
    per_rl = [jnp.transpose(toep[:, a0 - rl:a0 - rl + 3 * qr], (0, 2, 1, 3)) for rl in range(qr)]
    base = jnp.stack(per_rl, axis=1).reshape(n_heads, tq, 3 * tq)
    row_masks = []
    for i0 in (0, 1, nt - 1):
        r = i0 * qr + np.arange(qr)
        r2 = (i0 - 1) * qr + np.arange(3 * qr)
        rstart = np.clip(r - kr // 2, 0, rows - kr)
        row_ok = ((r2[None, :] >= rstart[:, None]) & (r2[None, :] < rstart[:, None] + kr)
                  & (r2[None, :] >= 0) & (r2[None, :] < rows))
        full = np.broadcast_to(row_ok[:, None, :, None], (qr, GRID_W, 3 * qr, GRID_W)).reshape(tq, 3 * tq)
        row_masks.append(np.where(full, 0.0, NEG_INF).astype(np.float32))
    return base, jnp.asarray(np.stack(row_masks, axis=0))


def _neighborhood_attention(qk_lat, rest_lat, qk_ctx, rest_ctx, rpb2, tq, bounded):
    bsz, seq, _ = qk_lat.shape
    ctx_len = qk_ctx.shape[1]
    nt = seq // tq
    width = B_HEADS * HEAD_DIM
    bias, row_masks = _nbr_bias_tables(rpb2, seq, tq)

    def kv_specs(col):
        return [pl.BlockSpec((None, tq, width), lambda b, i: (b, jnp.maximum(i - 1, 0), col)),
                pl.BlockSpec((None, tq, width), lambda b, i: (b, i, col)),
                pl.BlockSpec((None, tq, width), lambda b, i: (b, jnp.minimum(i + 1, nt - 1), col)),
                pl.BlockSpec((None, ctx_len, width), lambda b, i: (b, 0, col))]

    def tile_variant(i):
        return jnp.where(i == 0, 0, jnp.where(i == nt - 1, 2, 1))

    return pl.pallas_call(
        functools.partial(_nbr_kernel, tq=tq, bounded=bounded),
        grid=(bsz, nt),
        in_specs=[pl.BlockSpec((None, tq, width), lambda b, i: (b, i, QK_OFF["bq"] // width))]
                 + kv_specs(QK_OFF["bk"] // width) + kv_specs(REST_OFF["bv"] // width)
                 + [pl.BlockSpec(bias.shape, lambda b, i: (0, 0, 0)),
                    pl.BlockSpec((None,) + row_masks.shape[1:], lambda b, i: (tile_variant(i), 0, 0))],
        out_specs=pl.BlockSpec((None, tq, BRANCH_W), lambda b, i: (b, i, 0)),
        out_shape=jax.ShapeDtypeStruct((bsz, seq, BRANCH_W), BF16),
        compiler_params=_cparams(("parallel", "parallel")),
        name="mixer_b_neighbourhood_bounded" if bounded else "mixer_b_neighbourhood",
    )(qk_lat, qk_lat, qk_lat, qk_lat, qk_ctx, rest_lat, rest_lat, rest_lat, rest_ctx, bias, row_masks)


def _flash_update(s, v, m_ref, l_ref, acc_ref, idx):
    m_prev = m_ref[idx]
    m_new = jnp.maximum(m_prev, jnp.max(s, axis=-1, keepdims=True))
    alpha = jnp.exp2(m_prev - m_new)
    p = jnp.exp2(s - m_new)
    if l_ref is not None:
        l_ref[idx] = alpha * l_ref[idx] + _lane_partial_sums(p)
    acc_ref[idx] = alpha * acc_ref[idx] + _dot(p.astype(BF16), v)
    m_ref[idx] = m_new


def _dense_c_kernel(q_ref, kl_ref, vl_ref, kx_ref, vx_ref, o_ref, qs_ref, acc_ref, m_ref, *, tq, nk_lat):
    kt = pl.program_id(2)
    g = C_HEADS // C_KV

    @pl.when(kt == 0)
    def _():
        q = q_ref[...]
        for kv in range(C_KV):
            qs_ref[kv] = _stack_heads(q, kv * g, g)
        acc_ref[...] = jnp.zeros(acc_ref.shape, F32)
        m_ref[...] = jnp.full(m_ref.shape, -jnp.inf, F32)

    def step(k, v):
        for kv in range(C_KV):
            sl = slice(kv * HEAD_DIM, (kv + 1) * HEAD_DIM)
            _flash_update(_dot_nt(qs_ref[kv], k[:, sl]), _with_ones_column(v[:, sl]), m_ref, None, acc_ref, kv)

    @pl.when(kt < nk_lat)
    def _():
        step(kl_ref[...], vl_ref[...])

    @pl.when(kt == nk_lat)
    def _():
        step(kx_ref[...], vx_ref[...])
        for kv in range(C_KV):
            acc = acc_ref[kv]
            o = acc[:, :HEAD_DIM] * (1.0 / acc[:, HEAD_DIM:HEAD_DIM + 1])
            for h in range(g):
                c0 = (kv * g + h) * HEAD_DIM
                o_ref[:, c0:c0 + HEAD_DIM] = o[h * tq:(h + 1) * tq].astype(BF16)


def _dense_c_bounded_kernel(q_ref, kl_ref, vl_ref, kx_ref, vx_ref, o_ref, qs_ref, acc_ref, *, tq, nk_lat):
    kt = pl.program_id(2)
    g = C_HEADS // C_KV

    @pl.when(kt == 0)
    def _():
        q = q_ref[...]
        for kv in range(C_KV):
            qs_ref[kv] = _stack_heads(q, kv * g, g)
        acc_ref[...] = jnp.zeros(acc_ref.shape, F32)

    def step(k, v):
        n = k.shape[0]
        chunk = min(n, DENSE_KEY_CHUNK)
        for kv in range(C_KV):
            sl = slice(kv * HEAD_DIM, (kv + 1) * HEAD_DIM)
            upd = None
            for k0 in range(0, n, chunk):
                ks = slice(k0, k0 + chunk)
                p_t = jnp.exp2(_dot_nt(k[ks, sl], qs_ref[kv])).astype(BF16)
                term = lax.dot_general(_with_ones_column(v[ks, sl], VT_PAD), p_t, (((0,), (0,)), ((), ())),
                                       preferred_element_type=F32)
                upd = term if upd is None else upd + term
            acc_ref[kv] += upd

    @pl.when(kt < nk_lat)
    def _():
        step(kl_ref[...], vl_ref[...])

    @pl.when(kt == nk_lat)
    def _():
        step(kx_ref[...], vx_ref[...])
        for kv in range(C_KV):
            acc = acc_ref[kv]
            o = jnp.transpose(acc[:HEAD_DIM] * (1.0 / acc[HEAD_DIM:HEAD_DIM + 1]))
            for h in range(g):
                c0 = (kv * g + h) * HEAD_DIM
                o_ref[:, c0:c0 + HEAD_DIM] = o[h * tq:(h + 1) * tq].astype(BF16)


def _dense_gqa(qk_lat, rest_lat, qk_ctx, rest_ctx, tq, tk, bounded):
    bsz, seq, _ = qk_lat.shape
    ctx_len = qk_ctx.shape[1]
    nk_lat = seq // tk
    g = C_HEADS // C_KV
    kcol = QK_OFF["ck"] // LANES
    vcol = REST_OFF["cv"] // LANES
    if bounded:
        body = functools.partial(_dense_c_bounded_kernel, tq=tq, nk_lat=nk_lat)
        scratch = [pltpu.VMEM((C_KV, g * tq, HEAD_DIM), BF16),
                   pltpu.VMEM((C_KV, HEAD_DIM + VT_PAD, g * tq), F32)]
    else:
        body = functools.partial(_dense_c_kernel, tq=tq, nk_lat=nk_lat)
        scratch = [pltpu.VMEM((C_KV, g * tq, HEAD_DIM), BF16),
                   pltpu.VMEM((C_KV, g * tq, LANES), F32),
                   pltpu.VMEM((C_KV, g * tq, 1), F32)]
    return pl.pallas_call(
        body,
        grid=(bsz, seq // tq, nk_lat + 1),
        in_specs=[pl.BlockSpec((None, tq, C_HEADS * HEAD_DIM), lambda b, i, kt: (b, i, QK_OFF["cq"] // 512)),
                  pl.BlockSpec((None, tk, LANES), lambda b, i, kt: (b, jnp.minimum(kt, nk_lat - 1), kcol)),
                  pl.BlockSpec((None, tk, LANES), lambda b, i, kt: (b, jnp.minimum(kt, nk_lat - 1), vcol)),
                  pl.BlockSpec((None, ctx_len, LANES), lambda b, i, kt: (b, 0, kcol)),
                  pl.BlockSpec((None, ctx_len, LANES), lambda b, i, kt: (b, 0, vcol))],
        out_specs=pl.BlockSpec((None, tq, BRANCH_W), lambda b, i, kt: (b, i, 0)),
        out_shape=jax.ShapeDtypeStruct((bsz, seq, BRANCH_W), BF16),
        scratch_shapes=scratch,
        compiler_params=_cparams(("parallel", "parallel", "arbitrary")),
        name="mixer_c_dense_bounded" if bounded else "mixer_c_dense",
    )(qk_lat, qk_lat, rest_lat, qk_ctx, rest_ctx)


def _dense_d_kernel(q_ref, kl_ref, vl_ref, kx_ref, vx_ref, lam_ref, subln_ref, o_ref, qs_ref, l_ref, acc_ref,
                    m_ref, *, nk_lat, lam_init):
    kt = pl.program_id(2)
    n_sc = 2 * D_HEADS

    @pl.when(kt == 0)
    def _():
        q = q_ref[...]
        for idx in range(n_sc):
            qs_ref[idx] = q[:, idx * HEAD_DIM:(idx + 1) * HEAD_DIM]
        l_ref[...] = jnp.zeros(l_ref.shape, F32)
        acc_ref[...] = jnp.zeros(acc_ref.shape, F32)
        m_ref[...] = jnp.full(m_ref.shape, -jnp.inf, F32)

    def step(k, v):
        for idx in range(n_sc):
            h = idx // 2
            s = _dot_nt(qs_ref[idx], k[:, idx * HEAD_DIM:(idx + 1) * HEAD_DIM])
            _flash_update(s, v[:, h * 2 * HEAD_DIM:(h + 1) * 2 * HEAD_DIM], m_ref, l_ref, acc_ref, idx)

    @pl.when(kt < nk_lat)
    def _():
        step(kl_ref[...], vl_ref[...])

    @pl.when(kt == nk_lat)
    def _():
        step(kx_ref[...], vx_ref[...])
        lam = _diff_lambda(lam_ref, lam_init)
        for h in range(D_HEADS):
            l1 = jnp.sum(l_ref[2 * h], axis=-1, keepdims=True)
            l2 = jnp.sum(l_ref[2 * h + 1], axis=-1, keepdims=True)
            o = acc_ref[2 * h] / l1 - lam * (acc_ref[2 * h + 1] / l2)
            o_ref[:, h * 2 * HEAD_DIM:(h + 1) * 2 * HEAD_DIM] = _finish_diff(o, subln_ref, lam_init).astype(BF16)


def _dense_d_bounded_kernel(q_ref, kl_ref, vl_ref, kx_ref, vx_ref, lam_ref, subln_ref, o_ref, qs_ref, acc_ref, *,
                            nk_lat, lam_init):
    kt = pl.program_id(2)
    n_sc = 2 * D_HEADS
    vw = 2 * HEAD_DIM

    @pl.when(kt == 0)
    def _():
        q = q_ref[...]
        for idx in range(n_sc):
            qs_ref[idx] = q[:, idx * HEAD_DIM:(idx + 1) * HEAD_DIM]
        acc_ref[...] = jnp.zeros(acc_ref.shape, F32)

    def step(k, v):
        n = k.shape[0]
        chunk = min(n, 2 * DENSE_KEY_CHUNK)
        lane = lax.broadcasted_iota(jnp.int32, (chunk, VT_PAD), 1)
        ones_cols = jnp.where(lane == 0, 1.0, 0.0).astype(BF16)
        for h in range(D_HEADS):
            upd = [None, None]
            for k0 in range(0, n, chunk):
                ks = slice(k0, k0 + chunk)
                v1 = jnp.concatenate([v[ks, h * vw:(h + 1) * vw], ones_cols], axis=1)
                for c in range(2):
                    idx = 2 * h + c
                    p_t = jnp.exp2(_dot_nt(k[ks, idx * HEAD_DIM:(idx + 1) * HEAD_DIM], qs_ref[idx])).astype(BF16)
                    term = lax.dot_general(v1, p_t, (((0,), (0,)), ((), ())), preferred_element_type=F32)
                    upd[c] = term if upd[c] is None else upd[c] + term
            for c in range(2):
                acc_ref[2 * h + c] += upd[c]

    @pl.when(kt < nk_lat)
    def _():
        step(kl_ref[...], vl_ref[...])

    @pl.when(kt == nk_lat)
    def _():
        step(kx_ref[...], vx_ref[...])
        lam = _diff_lambda(lam_ref, lam_init)
        for h in range(D_HEADS):
            a1, a2 = acc_ref[2 * h], acc_ref[2 * h + 1]
            o_t = a1[:vw] * (1.0 / a1[vw:vw + 1]) - lam * (a2[:vw] * (1.0 / a2[vw:vw + 1]))
            o_ref[:, h * vw:(h + 1) * vw] = _finish_diff(jnp.transpose(o_t), subln_ref, lam_init).astype(BF16)


def _dense_diff(qk_lat, rest_lat, qk_ctx, rest_ctx, lam_d, subln, lam_init, tq, tk, bounded):
    bsz, seq, _ = qk_lat.shape
    ctx_len = qk_ctx.shape[1]
    nk_lat = seq // tk
    width = D_HEADS * 2 * HEAD_DIM
    kcol = QK_OFF["dk"] // width
    vcol = REST_OFF["dv"] // width
    if bounded:
        body = functools.partial(_dense_d_bounded_kernel, nk_lat=nk_lat, lam_init=lam_init)
        scratch = [pltpu.VMEM((2 * D_HEADS, tq, HEAD_DIM), BF16),
                   pltpu.VMEM((2 * D_HEADS, 2 * HEAD_DIM + VT_PAD, tq), F32)]
    else:
        body = functools.partial(_dense_d_kernel, nk_lat=nk_lat, lam_init=lam_init)
        scratch = [pltpu.VMEM((2 * D_HEADS, tq, HEAD_DIM), BF16),
                   pltpu.VMEM((2 * D_HEADS, tq, LANES), F32),
                   pltpu.VMEM((2 * D_HEADS, tq, 2 * HEAD_DIM), F32),
                   pltpu.VMEM((2 * D_HEADS, tq, 1), F32)]
    return pl.pallas_call(
        body,
        grid=(bsz, seq // tq, nk_lat + 1),
        in_specs=[pl.BlockSpec((None, tq, width), lambda b, i, kt: (b, i, QK_OFF["dq"] // width)),
                  pl.BlockSpec((None, tk, width), lambda b, i, kt: (b, jnp.minimum(kt, nk_lat - 1), kcol)),
                  pl.BlockSpec((None, tk, width), lambda b, i, kt: (b, jnp.minimum(kt, nk_lat - 1), vcol)),
                  pl.BlockSpec((None, ctx_len, width), lambda b, i, kt: (b, 0, kcol)),
                  pl.BlockSpec((None, ctx_len, width), lambda b, i, kt: (b, 0, vcol)),
                  pl.BlockSpec((4, HEAD_DIM), lambda b, i, kt: (0, 0)),
                  pl.BlockSpec((1, 2 * HEAD_DIM), lambda b, i, kt: (0, 0))],
        out_specs=pl.BlockSpec((None, tq, BRANCH_W), lambda b, i, kt: (b, i, 0)),
        out_shape=jax.ShapeDtypeStruct((bsz, seq, BRANCH_W), BF16),
        scratch_shapes=scratch,
        compiler_params=_cparams(("parallel", "parallel", "arbitrary")),
        name="mixer_d_diff_bounded" if bounded else "mixer_d_diff",
    )(qk_lat, qk_lat, rest_lat, qk_ctx, rest_ctx, lam_d, subln.reshape(1, 2 * HEAD_DIM))


def _ctx_kernel(sink_ref, qk_ref, v_ref, lam_ref, subln_ref, o_ref, *, lam_init):
    qk = qk_ref[...]
    vals = v_ref[...]
    ctx_len = qk.shape[0]
    v_off = {name: REST_OFF[name] - REST_OFF[_FIRST_VALUE] for name in ("av", "cv", "bv", "dv")}

    def cols(name, start, width):
        c0 = QK_OFF[name] + start
        return qk[:, c0:c0 + width]

    def gqa(qname, kname, vname, out_off, n_kv, with_sink):
        g = 8 // n_kv
        for kv in range(n_kv):
            qs = _stack_heads(cols(qname, 0, 8 * HEAD_DIM), kv * g, g)
            s = _dot_nt(qs, cols(kname, kv * HEAD_DIM, HEAD_DIM))
            extra = None
            if with_sink:
                extra = jnp.concatenate([jnp.full((ctx_len, 1), sink_ref[kv * g + h], F32) for h in range(g)], axis=0)
            p, l = _softmax_rows(s, extra)
            vv = vals[:, v_off[vname] + kv * HEAD_DIM:v_off[vname] + (kv + 1) * HEAD_DIM]
            o = _dot(p.astype(BF16), vv) / l
            for h in range(g):
                c0 = out_off + (kv * g + h) * HEAD_DIM
                o_ref[:, c0:c0 + HEAD_DIM] = o[h * ctx_len:(h + 1) * ctx_len].astype(BF16)

    gqa("aq", "ak", "av", 0 * BRANCH_W, A_KV, True)
    gqa("bq", "bk", "bv", 1 * BRANCH_W, B_HEADS, False)
    gqa("cq", "ck", "cv", 2 * BRANCH_W, C_KV, False)

    lam = _diff_lambda(lam_ref, lam_init)
    for h in range(D_HEADS):
        base = h * 2 * HEAD_DIM
        p1, l1 = _softmax_rows(_dot_nt(cols("dq", base, HEAD_DIM), cols("dk", base, HEAD_DIM)))
        p2, l2 = _softmax_rows(_dot_nt(cols("dq", base + HEAD_DIM, HEAD_DIM), cols("dk", base + HEAD_DIM, HEAD_DIM)))
        pd = p1 / l1 - lam * (p2 / l2)
        o = _dot(pd.astype(BF16), vals[:, v_off["dv"] + base:v_off["dv"] + base + 2 * HEAD_DIM])
        c0 = 3 * BRANCH_W + base
        o_ref[:, c0:c0 + 2 * HEAD_DIM] = _finish_diff(o, subln_ref, lam_init).astype(BF16)


def _ctx_attention(qk_ctx, v_ctx, sink2, lam_d, subln, lam_init):
    bsz, ctx_len, _ = qk_ctx.shape
    return pl.pallas_call(
        functools.partial(_ctx_kernel, lam_init=lam_init),
        grid=(bsz,),
        in_specs=[pl.BlockSpec(memory_space=pltpu.SMEM),
                  pl.BlockSpec((None, ctx_len, QK_COLS), lambda b: (b, 0, 0)),
                  pl.BlockSpec((None, ctx_len, v_ctx.shape[2]), lambda b: (b, 0, 0)),
                  pl.BlockSpec((4, HEAD_DIM), lambda b: (0, 0)),
                  pl.BlockSpec((1, 2 * HEAD_DIM), lambda b: (0, 0))],
        out_specs=pl.BlockSpec((None, ctx_len, N_BRANCH * BRANCH_W), lambda b: (b, 0, 0)),
        out_shape=jax.ShapeDtypeStruct((bsz, ctx_len, N_BRANCH * BRANCH_W), BF16),
        compiler_params=_cparams(("parallel",)),
        name="ctx_attention",
    )(sink2, qk_ctx, v_ctx, lam_d, subln.reshape(1, 2 * HEAD_DIM))


def _merge_kernel(x_ref, mod_ref, ya_ref, yb_ref, yc_ref, yd_ref, gp_ref, mg_ref, wbr_ref, wout_ref, o_ref):
    gate = mod_ref[...][:, 2 * D_MODEL:]
    merged = None
    for n, y_ref in enumerate((ya_ref, yb_ref, yc_ref, yd_ref)):
        yg = (y_ref[...].astype(F32) * gp_ref[:, n * BRANCH_W:(n + 1) * BRANCH_W].astype(F32)).astype(BF16)
        term = mg_ref[:, n * D_MODEL:(n + 1) * D_MODEL].astype(F32) * _dot(yg, wbr_ref[n])
        merged = term if merged is None else merged + term
    o_ref[...] = x_ref[...] + gate * _dot(merged.astype(BF16), wout_ref[...])


def _merge(x2, mod3, ys, y_cols, rest, w_br, w_out, tm, row_of_tile):
    rows = x2.shape[0]
    y_specs = [pl.BlockSpec((tm, BRANCH_W), (lambda i, c=c: (i, c))) for c in y_cols]
    return pl.pallas_call(
        _merge_kernel,
        grid=(rows // tm,),
        in_specs=[pl.BlockSpec((tm, D_MODEL), lambda i: (i, 0)),
                  pl.BlockSpec((None, 1, 3 * D_MODEL), lambda i: (row_of_tile(i), 0, 0))]
                 + y_specs
                 + [pl.BlockSpec((tm, N_BRANCH * BRANCH_W), lambda i: (i, REST_OFF["ag"] // (N_BRANCH * BRANCH_W))),
                    pl.BlockSpec((tm, N_BRANCH * D_MODEL), lambda i: (i, 0)),
                    pl.BlockSpec((N_BRANCH, BRANCH_W, D_MODEL), lambda i: (0, 0, 0)),
                    pl.BlockSpec((D_MODEL, D_MODEL), lambda i: (0, 0))],
        out_specs=pl.BlockSpec((tm, D_MODEL), lambda i: (i, 0)),
        out_shape=jax.ShapeDtypeStruct((rows, D_MODEL), F32),
        compiler_params=_cparams(("parallel",)),
        name="gated_merge",
    )(x2, mod3, *ys, rest, rest, w_br, w_out)


def _rope_tables(seq):
    t = np.arange(seq, dtype=np.int32)
    pos = np.stack([t // GRID_W, t % GRID_W], axis=-1).astype(np.float32)
    n_freq = HEAD_DIM // 4
    freqs = (np.float32(ROPE_THETA) ** (-np.arange(n_freq, dtype=np.float32) / np.float32(n_freq))).astype(np.float32)
    ang = (pos[:, :, None] * freqs[None, None, :]).astype(np.float32)
    ang = np.concatenate([ang, ang], axis=-1).reshape(seq, HEAD_DIM)
    sign = np.where((np.arange(HEAD_DIM) % 32) < 16, -1.0, 1.0).astype(np.float32)
    reps = NORM_GROUP // HEAD_DIM
    cos = np.tile(np.cos(ang).astype(np.float32), (1, reps))
    sin = np.tile(np.sin(ang).astype(np.float32) * sign, (1, reps))
    return jnp.asarray(cos), jnp.asarray(sin)


def _regroup_cols(w, order, total):
    parts = [w[:, _ORIG[n][0]:_ORIG[n][1]] for n in order]
    used = sum(p.shape[1] for p in parts)
    if total > used:
        parts.append(jnp.zeros((w.shape[0], total - used), w.dtype))
    return jnp.concatenate(parts, axis=1)


def _qk_gain_row(g):
    parts = []
    for name in _QK_ORDER:
        mixer = "abcd".index(name[0])
        is_q = name[1] == "q"
        width = _ORIG[name][1] - _ORIG[name][0]
        gain = g[mixer, 0] * (QK_SCALE * LOG2E) if is_q else g[mixer, 1]
        parts.append(jnp.tile(gain, width // HEAD_DIM))
    parts.append(jnp.ones((QK_COLS - _QK_USED,), g.dtype))
    return jnp.concatenate(parts).reshape(1, QK_COLS).astype(F32)


def kernel(x, c, ctx, c_ctx, norm_w, w_ada, b_ada, w_in, qk_gain, sink_a, rpb_b, lam_d, subln_d, w_br, w_out):
    bsz, seq, _ = x.shape
    ctx_len = ctx.shape[1]
    depth = w_ada.shape[0]
    assert seq % (2 * A_WINDOW) == 0 and ctx_len % LANES == 0 and bsz <= 6

    assert seq % TM_REST == 0 and seq % TK_DENSE == 0 and seq % TQ_DENSE == 0
    tiles_per_batch = seq // TM_QK
    merge_tiles_per_batch = seq // TM_MERGE

    cvec = jnp.concatenate([c, c_ctx[None, :], jnp.zeros((8 - bsz - 1, D_MODEL), F32)], axis=0)
    mod_all = _ada(cvec, w_ada, b_ada)
    cos, sin = _rope_tables(seq)
    ones_bd = jnp.asarray(np.kron(np.eye(NORM_GROUP // HEAD_DIM), np.ones((HEAD_DIM, HEAD_DIM))), BF16)

    x2 = x.reshape(bsz * seq, D_MODEL)
    c2 = ctx.reshape(bsz * ctx_len, D_MODEL)
    for l in range(depth):
        need_ctx = l < depth - 1
        lam_init = 0.8 - 0.6 * math.exp(-0.3 * l)
        mod3 = mod_all[l].reshape(8, 1, 3 * D_MODEL)
        w_qk = _regroup_cols(w_in[l], _QK_ORDER, QK_COLS).astype(BF16)
        w_rest = _regroup_cols(w_in[l], _REST_ORDER, REST_COLS).astype(BF16)
        gain_row = _qk_gain_row(qk_gain[l])

        hx = _prenorm(x2, norm_w[l], mod3, TM_QK, lambda i: i // tiles_per_batch)
        hc = _prenorm(c2, norm_w[l], mod3, bsz * ctx_len, lambda i: bsz)
        qk_lat = _proj_qk(hx, w_qk, gain_row, ones_bd, TM_QK, cos, sin, tiles_per_batch).reshape(bsz, seq, QK_COLS)
        qk_ctx = _proj_qk(hc, w_qk, gain_row, ones_bd, bsz * ctx_len).reshape(bsz, ctx_len, QK_COLS)
        rest_lat = _proj_rest(hx, w_rest, TM_REST).reshape(bsz, seq, REST_COLS)
        rest_ctx = _proj_rest(hc, w_rest, bsz * ctx_len).reshape(bsz, ctx_len, REST_COLS)

        sink2 = sink_a[l] * LOG2E
        rpb2 = rpb_b[l] * LOG2E
        proj = (qk_lat, rest_lat, qk_ctx, rest_ctx)

        def qk_bound(g):
            return HEAD_DIM * QK_SCALE * LOG2E * jnp.max(jnp.abs(g[0])) * jnp.max(jnp.abs(g[1])) * 1.02

        y_a = lax.cond(qk_bound(qk_gain[l, 0]) <= LOGIT_BOUND,
                       lambda: _window_attention(*proj, sink2, True),
                       lambda: _window_attention(*proj, sink2, False))
        y_b = lax.cond(qk_bound(qk_gain[l, 1]) + jnp.max(jnp.abs(rpb_b[l])) * LOG2E <= LOGIT_BOUND,
                       lambda: _neighborhood_attention(*proj, rpb2, TQ_NBR, True),
                       lambda: _neighborhood_attention(*proj, rpb2, TQ_NBR, False))
        y_c = lax.cond(qk_bound(qk_gain[l, 2]) <= LOGIT_BOUND,
                       lambda: _dense_gqa(*proj, TQ_DENSE, TK_DENSE, True),
                       lambda: _dense_gqa(*proj, TQ_DENSE_ROWMAX, TK_DENSE, False))
        y_d = lax.cond(qk_bound(qk_gain[l, 3]) <= LOGIT_BOUND,
                       lambda: _dense_diff(*proj, lam_d[l], subln_d[l], lam_init, TQ_DENSE, TK_DENSE, True),
                       lambda: _dense_diff(*proj, lam_d[l], subln_d[l], lam_init, TQ_DENSE_ROWMAX, TK_DENSE, False))

        w_br_l = w_br[l].astype(BF16)
        w_out_l = w_out[l].astype(BF16)
        ys = [y.reshape(bsz * seq, BRANCH_W) for y in (y_a, y_b, y_c, y_d)]
        x_new = _merge(x2, mod3, ys, (0, 0, 0, 0), rest_lat.reshape(bsz * seq, REST_COLS), w_br_l, w_out_l,
                       TM_MERGE, lambda i: i // merge_tiles_per_batch)
        if need_ctx:
            y_ctx = _ctx_attention(qk_ctx, rest_ctx[:, :, REST_OFF[_FIRST_VALUE]:], sink2, lam_d[l], subln_d[l],
                                   lam_init)
            y_ctx2 = y_ctx.reshape(bsz * ctx_len, N_BRANCH * BRANCH_W)
            c2 = _merge(c2, mod3, [y_ctx2] * 4, (0, 1, 2, 3), rest_ctx.reshape(bsz * ctx_len, REST_COLS),
                        w_br_l, w_out_l, bsz * ctx_len, lambda i: bsz)
        x2 = x_new
    return x2.reshape(bsz, seq, D_MODEL)
```

```python
import functools
import math

import numpy as np
import jax
import jax.numpy as jnp
from jax import lax
from jax.experimental import pallas as pl
from jax.experimental.pallas import tpu as pltpu

F32 = jnp.float32
BF16 = jnp.bfloat16

D_MODEL = 1024
GRID_W = 64
HEAD_DIM = 64
BRANCH_W = 512
N_BRANCH = 4
A_HEADS, A_KV, A_WINDOW = 8, 2, 128
B_HEADS, NB_ROWS, NB_COLS = 8, 8, 16
C_HEADS, C_KV = 8, 2
D_HEADS = 4
ROPE_THETA = 10000.0
EPS = 1e-6
NEG_INF = -1e30
QK_SCALE = HEAD_DIM ** -0.5
LOG2E = math.log2(math.e)
LOGIT_BOUND = 60.0

V7X_VMEM_LIMIT_BYTES = 56 * 1024 * 1024
LANES = 128

_ORIG = dict(aq=(0, 512), ak=(512, 640), av=(640, 768), ag=(768, 1280),
             bq=(1280, 1792), bk=(1792, 2304), bv=(2304, 2816), bg=(2816, 3328),
             cq=(3328, 3840), ck=(3840, 3968), cv=(3968, 4096), cg=(4096, 4608),
             dq=(4608, 5120), dk=(5120, 5632), dv=(5632, 6144), dg=(6144, 6656),
             mg=(6656, 10752))
_QK_ORDER = ("aq", "cq", "dq", "bq", "bk", "dk", "ak", "ck")
_REST_ORDER = ("mg", "ag", "bg", "cg", "dg", "dv", "bv", "av", "cv")
_FIRST_VALUE = "dv"


def _offsets(order):
    off, out = 0, {}
    for name in order:
        lo, hi = _ORIG[name]
        out[name] = off
        off += hi - lo
    return out, off


PROJ_TN = 512
PROJ_ROW_CHUNK = 512
REST_ROW_CHUNK = 1024
VT_PAD = 16
DENSE_KEY_CHUNK = 512
TM_QK, TM_REST, TM_MERGE = 2048, 4096, 512
TQ_NBR = (NB_ROWS // 2) * GRID_W
TQ_DENSE, TQ_DENSE_ROWMAX, TK_DENSE = 1024, 256, 2048
NORM_GROUP = 256
QK_OFF, _QK_USED = _offsets(_QK_ORDER)
REST_OFF, _REST_USED = _offsets(_REST_ORDER)
QK_COLS = -(-_QK_USED // PROJ_TN) * PROJ_TN
REST_COLS = -(-_REST_USED // PROJ_TN) * PROJ_TN
_ROPE_LO_END = QK_OFF["bq"] // PROJ_TN
_ROPE_HI_START = QK_OFF["dk"] // PROJ_TN
_MG_BLOCKS = (REST_OFF["ag"]) // PROJ_TN
_GP_BLOCKS_END = REST_OFF[_FIRST_VALUE] // PROJ_TN
assert QK_OFF["bq"] % PROJ_TN == 0 and QK_OFF["dk"] % PROJ_TN == 0
assert REST_OFF["ag"] % PROJ_TN == 0 and REST_OFF[_FIRST_VALUE] % PROJ_TN == 0


def _cparams(sem):
    return pltpu.CompilerParams(dimension_semantics=sem, vmem_limit_bytes=V7X_VMEM_LIMIT_BYTES)


def _sigmoid(x):
    return 0.5 * jnp.tanh(0.5 * x) + 0.5


def _dot_nt(a, b):
    return lax.dot_general(a, b, (((1,), (1,)), ((), ())), preferred_element_type=F32)


def _dot(a, b):
    return jnp.dot(a, b, preferred_element_type=F32)


def _ada_kernel(c_ref, w_ref, b_ref, o_ref):
    c = c_ref[...]
    o_ref[...] = _dot(c * _sigmoid(c), w_ref[...]) + b_ref[...]


def _ada(cvec, w_ada, b_ada):
    depth = w_ada.shape[0]
    tn = 512
    return pl.pallas_call(
        _ada_kernel,
        grid=(depth, 3 * D_MODEL // tn),
        in_specs=[pl.BlockSpec((8, D_MODEL), lambda l, j: (0, 0)),
                  pl.BlockSpec((None, D_MODEL, tn), lambda l, j: (l, 0, j)),
                  pl.BlockSpec((None, 1, tn), lambda l, j: (l, 0, j))],
        out_specs=pl.BlockSpec((None, 8, tn), lambda l, j: (l, 0, j)),
        out_shape=jax.ShapeDtypeStruct((depth, 8, 3 * D_MODEL), F32),
        compiler_params=_cparams(("parallel", "parallel")),
        name="ada_mod",
    )(cvec, w_ada, b_ada.reshape(depth, 1, 3 * D_MODEL))


def _prenorm_kernel(x_ref, nw_ref, mod_ref, o_ref):
    x = x_ref[...]
    mod = mod_ref[...]
    shift, scale = mod[:, :D_MODEL], mod[:, D_MODEL:2 * D_MODEL]
    y = x * lax.rsqrt(jnp.mean(x * x, axis=-1, keepdims=True) + EPS) * nw_ref[...]
    o_ref[...] = (y * (1.0 + scale) + shift).astype(BF16)


def _prenorm(x2, norm_w, mod3, tm, row_of_tile):
    rows = x2.shape[0]
    return pl.pallas_call(
        _prenorm_kernel,
        grid=(rows // tm,),
        in_specs=[pl.BlockSpec((tm, D_MODEL), lambda i: (i, 0)),
                  pl.BlockSpec((1, D_MODEL), lambda i: (0, 0)),
                  pl.BlockSpec((None, 1, 3 * D_MODEL), lambda i: (row_of_tile(i), 0, 0))],
        out_specs=pl.BlockSpec((tm, D_MODEL), lambda i: (i, 0)),
        out_shape=jax.ShapeDtypeStruct((rows, D_MODEL), BF16),
        compiler_params=_cparams(("parallel",)),
        name="prenorm",
    )(x2, norm_w.reshape(1, D_MODEL), mod3)


def _rot_half_unsigned(n):
    lane = lax.broadcasted_iota(jnp.int32, n.shape, 1)
    return jnp.where((lane & 31) < 16, pltpu.roll(n, LANES - 16, 1), pltpu.roll(n, 16, 1))


def _proj_qk_kernel(*refs, rope):
    if rope:
        hx_ref, w_ref, gain_ref, ones_ref, cos_ref, sin_ref, o_ref = refs
    else:
        hx_ref, w_ref, gain_ref, ones_ref, o_ref = refs
    w = w_ref[...]
    ones_bd = ones_ref[...]
    rows = hx_ref.shape[0]
    chunk = min(rows, PROJ_ROW_CHUNK)

    def block(rotary):
        for r0 in range(0, rows, chunk):
            rs = slice(r0, r0 + chunk)
            acc_all = _dot(hx_ref[rs, :], w)
            for c in range(PROJ_TN // NORM_GROUP):
                sl = slice(c * NORM_GROUP, (c + 1) * NORM_GROUP)
                acc = acc_all[:, sl]
                ss = _dot((acc * acc).astype(BF16), ones_bd)
                n = acc * lax.rsqrt(ss * (1.0 / HEAD_DIM) + EPS) * gain_ref[:, sl]
                if rotary:
                    rot = jnp.concatenate([_rot_half_unsigned(n[:, :LANES]), _rot_half_unsigned(n[:, LANES:])],
                                          axis=1)
                    n = n * cos_ref[rs, :] + rot * sin_ref[rs, :]
                o_ref[rs, sl] = n.astype(BF16)

    if rope:
        j = pl.program_id(1)
        use = jnp.logical_or(j < _ROPE_LO_END, j >= _ROPE_HI_START)
        pl.when(use)(lambda: block(True))
        pl.when(jnp.logical_not(use))(lambda: block(False))
    else:
        block(False)


def _proj_qk(hx, w_qk, gain_row, ones_bd, tm, cos=None, sin=None, tiles_per_batch=None):
    rows = hx.shape[0]
    rope = cos is not None
    in_specs = [pl.BlockSpec((tm, D_MODEL), lambda i, j: (i, 0)),
                pl.BlockSpec((D_MODEL, PROJ_TN), lambda i, j: (0, j)),
                pl.BlockSpec((1, PROJ_TN), lambda i, j: (0, j)),
                pl.BlockSpec((NORM_GROUP, NORM_GROUP), lambda i, j: (0, 0))]
    args = [hx, w_qk, gain_row, ones_bd]
    if rope:
        in_specs += [pl.BlockSpec((tm, NORM_GROUP), lambda i, j: (i % tiles_per_batch, 0))] * 2
        args += [cos, sin]
    return pl.pallas_call(
        functools.partial(_proj_qk_kernel, rope=rope),
        grid=(rows // tm, QK_COLS // PROJ_TN),
        in_specs=in_specs,
        out_specs=pl.BlockSpec((tm, PROJ_TN), lambda i, j: (i, j)),
        out_shape=jax.ShapeDtypeStruct((rows, QK_COLS), BF16),
        compiler_params=_cparams(("parallel", "arbitrary")),
        name="proj_qk_rope" if rope else "proj_qk",
    )(*args)


def _proj_rest_kernel(hx_ref, w_ref, o_ref):
    j = pl.program_id(1)
    w = w_ref[...]
    rows = hx_ref.shape[0]
    chunk = min(rows, REST_ROW_CHUNK)
    def block(act):
        for r0 in range(0, rows, chunk):
            o_ref[r0:r0 + chunk, :] = act(_dot(hx_ref[r0:r0 + chunk, :], w)).astype(BF16)

    pl.when(j < _MG_BLOCKS)(lambda: block(_sigmoid))
    pl.when(jnp.logical_and(j >= _MG_BLOCKS, j < _GP_BLOCKS_END))(lambda: block(lambda a: a * _sigmoid(a)))
    pl.when(j >= _GP_BLOCKS_END)(lambda: block(lambda a: a))


def _proj_rest(hx, w_rest, tm):
    rows = hx.shape[0]
    return pl.pallas_call(
        _proj_rest_kernel,
        grid=(rows // tm, REST_COLS // PROJ_TN),
        in_specs=[pl.BlockSpec((tm, D_MODEL), lambda i, j: (i, 0)),
                  pl.BlockSpec((D_MODEL, PROJ_TN), lambda i, j: (0, j))],
        out_specs=pl.BlockSpec((tm, PROJ_TN), lambda i, j: (i, j)),
        out_shape=jax.ShapeDtypeStruct((rows, REST_COLS), BF16),
        compiler_params=_cparams(("parallel", "arbitrary")),
        name="proj_rest",
    )(hx, w_rest)


def _stack_heads(q, first_head, n):
    return jnp.concatenate([q[:, (first_head + g) * HEAD_DIM:(first_head + g + 1) * HEAD_DIM] for g in range(n)],
                           axis=0)


def _softmax_rows(s, extra=None):
    m = jnp.max(s, axis=-1, keepdims=True)
    if extra is not None:
        m = jnp.maximum(m, extra)
    p = jnp.exp2(s - m)
    l = jnp.sum(p, axis=-1, keepdims=True)
    if extra is not None:
        l = l + jnp.exp2(extra - m)
    return p, l


def _with_ones_column(v, pad=HEAD_DIM):
    lane = lax.broadcasted_iota(jnp.int32, (v.shape[0], pad), 1)
    return jnp.concatenate([v, jnp.where(lane == 0, 1.0, 0.0).astype(v.dtype)], axis=1)


def _lane_partial_sums(p):
    part = p[:, :LANES]
    for c in range(1, p.shape[1] // LANES):
        part = part + p[:, c * LANES:(c + 1) * LANES]
    return part


def _diff_lambda(lam_ref, lam_init):
    lf = lam_ref[...]
    a = jnp.sum(lf[0:1] * lf[1:2], axis=-1, keepdims=True)
    b = jnp.sum(lf[2:3] * lf[3:4], axis=-1, keepdims=True)
    return jnp.exp(a) - jnp.exp(b) + lam_init


def _finish_diff(o, subln_ref, lam_init):
    y = o * lax.rsqrt(jnp.mean(o * o, axis=-1, keepdims=True) + EPS) * subln_ref[...]
    return y * (1.0 - lam_init)


def _win_kernel(sink_ref, q_ref, kp_ref, kc_ref, kn_ref, kx_ref, vp_ref, vc_ref, vn_ref, vx_ref, o_ref, *,
                tq, seq):
    i = pl.program_id(1)
    g = A_HEADS // A_KV
    q = q_ref[...]
    k_all = jnp.concatenate([kp_ref[...], kc_ref[...], kn_ref[...], kx_ref[...]], axis=0)
    v_all = jnp.concatenate([vp_ref[...], vc_ref[...], vn_ref[...], vx_ref[...]], axis=0)
    nw = 2 * tq
    t = lax.broadcasted_iota(jnp.int32, (g * tq, nw), 0) & (tq - 1)
    j = lax.broadcasted_iota(jnp.int32, (g * tq, nw), 1)
    rel = j - tq // 2 - t
    kpos = i * tq - tq // 2 + j
    bad = jnp.where(jnp.abs(rel) > A_WINDOW, 1, 0) + jnp.where(kpos < 0, 1, 0) + jnp.where(kpos >= seq, 1, 0)
    for kv in range(A_KV):
        qs = _stack_heads(q, kv * g, g)
        kk = k_all[:, kv * HEAD_DIM:(kv + 1) * HEAD_DIM]
        vv = v_all[:, kv * HEAD_DIM:(kv + 1) * HEAD_DIM]
        s = _dot_nt(qs, kk)
        s_win = jnp.where(bad > 0, NEG_INF, s[:, :nw])
        s_ctx = s[:, nw:]
        sink = jnp.concatenate([jnp.full((tq, 1), sink_ref[kv * g + h], F32) for h in range(g)], axis=0)
        m = jnp.maximum(jnp.maximum(jnp.max(s_win, axis=-1, keepdims=True),
                                    jnp.max(s_ctx, axis=-1, keepdims=True)), sink)
        pw = jnp.exp2(s_win - m)
        pc = jnp.exp2(s_ctx - m)
        l = jnp.sum(pw, axis=-1, keepdims=True) + jnp.sum(pc, axis=-1, keepdims=True) + jnp.exp2(sink - m)
        o = (_dot(pw.astype(BF16), vv[:nw]) + _dot(pc.astype(BF16), vv[nw:])) / l
        for h in range(g):
            c0 = (kv * g + h) * HEAD_DIM
            o_ref[:, c0:c0 + HEAD_DIM] = o[h * tq:(h + 1) * tq].astype(BF16)


def _win_bounded_kernel(sink_ref, q_ref, kp_ref, kc_ref, kn_ref, kx_ref, vp_ref, vc_ref, vn_ref, vx_ref, mask_ref,
                        o_ref, *, tq):
    g = A_HEADS // A_KV
    q = q_ref[...]
    mask_t = mask_ref[...]
    sink = sink_ref[...]
    k_all = jnp.concatenate([kp_ref[...], kc_ref[...], kn_ref[...], kx_ref[...]], axis=0)
    v_all = jnp.concatenate([vp_ref[...], vc_ref[...], vn_ref[...], vx_ref[...]], axis=0)
    for kv in range(A_KV):
        sl = slice(kv * HEAD_DIM, (kv + 1) * HEAD_DIM)
        qs = _stack_heads(q, kv * g, g)
        stab = [jnp.maximum(sink[:, kv * g + h:kv * g + h + 1], 0.0) for h in range(g)]
        shift_t = jnp.concatenate([mask_t - stab[h] for h in range(g)], axis=1)
        p_t = jnp.exp2(_dot_nt(k_all[:, sl], qs) + shift_t).astype(BF16)
        acc = lax.dot_general(_with_ones_column(v_all[:, sl], VT_PAD), p_t, (((0,), (0,)), ((), ())),
                              preferred_element_type=F32)
        for h in range(g):
            a = acc[:, h * tq:(h + 1) * tq]
            l = a[HEAD_DIM:HEAD_DIM + 1] + jnp.exp2(sink[:, kv * g + h:kv * g + h + 1] - stab[h])
            c0 = (kv * g + h) * HEAD_DIM
            o_ref[:, c0:c0 + HEAD_DIM] = jnp.transpose(a[:HEAD_DIM] * (1.0 / l)).astype(BF16)


def _window_mask_tables(seq, tq, ctx_len):
    nt = seq // tq
    tabs = []
    for i0 in (0, 1, nt - 1):
        t = np.arange(tq)[None, :]
        j = np.arange(2 * tq)[:, None]
        kpos = i0 * tq - tq // 2 + j
        ok = (np.abs(j - tq // 2 - t) <= A_WINDOW) & (kpos >= 0) & (kpos < seq)
        win = np.where(ok, 0.0, NEG_INF).astype(np.float32)
        tabs.append(np.concatenate([win, np.zeros((ctx_len, tq), np.float32)], axis=0))
    return jnp.asarray(np.stack(tabs, axis=0))


def _window_attention(qk_lat, rest_lat, qk_ctx, rest_ctx, sink2, bounded):
    bsz, seq, _ = qk_lat.shape
    ctx_len = qk_ctx.shape[1]
    tq = 2 * A_WINDOW
    nt = seq // tq
    half = tq // 2
    kcol = QK_OFF["ak"] // LANES
    vcol = REST_OFF["av"] // LANES
    n_half = seq // half

    def kv_specs(col):
        return [pl.BlockSpec((None, half, LANES), lambda b, i: (b, jnp.maximum(2 * i - 1, 0), col)),
                pl.BlockSpec((None, tq, LANES), lambda b, i: (b, i, col)),
                pl.BlockSpec((None, half, LANES), lambda b, i: (b, jnp.minimum(2 * i + 2, n_half - 1), col)),
                pl.BlockSpec((None, ctx_len, LANES), lambda b, i: (b, 0, col))]

    def tile_variant(i):
        return jnp.where(i == 0, 0, jnp.where(i == nt - 1, 2, 1))

    q_spec = pl.BlockSpec((None, tq, A_HEADS * HEAD_DIM), lambda b, i: (b, i, QK_OFF["aq"] // 512))
    kv_args = (qk_lat, qk_lat, qk_lat, qk_ctx, rest_lat, rest_lat, rest_lat, rest_ctx)
    if bounded:
        body = functools.partial(_win_bounded_kernel, tq=tq)
        in_specs = ([pl.BlockSpec((1, A_HEADS), lambda b, i: (0, 0)), q_spec] + kv_specs(kcol) + kv_specs(vcol)
                    + [pl.BlockSpec((None, 2 * tq + ctx_len, tq), lambda b, i: (tile_variant(i), 0, 0))])
        args = (sink2.reshape(1, A_HEADS), qk_lat) + kv_args + (_window_mask_tables(seq, tq, ctx_len),)
    else:
        body = functools.partial(_win_kernel, tq=tq, seq=seq)
        in_specs = [pl.BlockSpec(memory_space=pltpu.SMEM), q_spec] + kv_specs(kcol) + kv_specs(vcol)
        args = (sink2, qk_lat) + kv_args
    return pl.pallas_call(
        body,
        grid=(bsz, nt),
        in_specs=in_specs,
        out_specs=pl.BlockSpec((None, tq, BRANCH_W), lambda b, i: (b, i, 0)),
        out_shape=jax.ShapeDtypeStruct((bsz, seq, BRANCH_W), BF16),
        compiler_params=_cparams(("parallel", "parallel")),
        name="mixer_a_window_bounded" if bounded else "mixer_a_window",
    )(*args)


def _nbr_kernel(q_ref, kp_ref, kc_ref, kn_ref, kx_ref, vp_ref, vc_ref, vn_ref, vx_ref, bias_ref, rmask_ref, o_ref,
                *, tq, bounded):
    q = q_ref[...]
    k_all = jnp.concatenate([kp_ref[...], kc_ref[...], kn_ref[...], kx_ref[...]], axis=0)
    v_all = jnp.concatenate([vp_ref[...], vc_ref[...], vn_ref[...], vx_ref[...]], axis=0)
    rmask = rmask_ref[...]
    nw = 3 * tq
    for h in range(B_HEADS):
        sl = slice(h * HEAD_DIM, (h + 1) * HEAD_DIM)
        s = _dot_nt(q[:, sl], k_all[:, sl])
        s = jnp.concatenate([s[:, :nw] + (bias_ref[h] + rmask), s[:, nw:]], axis=1)
        if not bounded:
            s = s - jnp.max(s, axis=-1, keepdims=True)
        acc = _dot(jnp.exp2(s).astype(BF16), _with_ones_column(v_all[:, sl]))
        o_ref[:, sl] = (acc[:, :HEAD_DIM] * (1.0 / acc[:, HEAD_DIM:HEAD_DIM + 1])).astype(BF16)


def _nbr_bias_tables(rpb, seq, tq):
    rows = seq // GRID_W
    nt = seq // tq
    kr = min(NB_ROWS, rows)
    qr = tq // GRID_W
    n_heads = rpb.shape[0]
    assert NB_ROWS - 1 - qr - (qr - 1) >= 0 and NB_ROWS - 1 - qr + 3 * qr <= 2 * NB_ROWS - 1
    pad = GRID_W - 1
    rpb_pad = jnp.pad(rpb.astype(F32), ((0, 0), (0, 0), (pad, pad)))
    toep = jnp.stack([rpb_pad[:, :, pad + NB_COLS - 1 - c:pad + NB_COLS - 1 - c + GRID_W] for c in range(GRID_W)],
                     axis=2)
    c = np.arange(GRID_W)
    cstart = np.clip(c - NB_COLS // 2, 0, GRID_W - NB_COLS)
    col_ok = (c[None, :] >= cstart[:, None]) & (c[None, :] < cstart[:, None] + NB_COLS)
    toep = jnp.where(col_ok[None, None], toep, NEG_INF)
    a0 = NB_ROWS - 1 - qr
    per_rl = [jnp.transpose(toep[:, a0 - rl:a0 - rl + 3 * qr], (0, 2, 1, 3)) for rl in range(qr)]
    base = jnp.stack(per_rl, axis=1).reshape(n_heads, tq, 3 * tq)
    row_masks = []
    for i0 in (0, 1, nt - 1):
        r = i0 * qr + np.arange(qr)
        r2 = (i0 - 1) * qr + np.arange(3 * qr)
        rstart = np.clip(r - kr // 2, 0, rows - kr)
        row_ok = ((r2[None, :] >= rstart[:, None]) & (r2[None, :] < rstart[:, None] + kr)
                  & (r2[None, :] >= 0) & (r2[None, :] < rows))
        full = np.broadcast_to(row_ok[:, None, :, None], (qr, GRID_W, 3 * qr, GRID_W)).reshape(tq, 3 * tq)
        row_masks.append(np.where(full, 0.0, NEG_INF).astype(np.float32))
    return base, jnp.asarray(np.stack(row_masks, axis=0))


def _neighborhood_attention(qk_lat, rest_lat, qk_ctx, rest_ctx, rpb2, tq, bounded):
    bsz, seq, _ = qk_lat.shape
    ctx_len = qk_ctx.shape[1]
    nt = seq // tq
    width = B_HEADS * HEAD_DIM
    bias, row_masks = _nbr_bias_tables(rpb2, seq, tq)

    def kv_specs(col):
        return [pl.BlockSpec((None, tq, width), lambda b, i: (b, jnp.maximum(i - 1, 0), col)),
                pl.BlockSpec((None, tq, width), lambda b, i: (b, i, col)),
                pl.BlockSpec((None, tq, width), lambda b, i: (b, jnp.minimum(i + 1, nt - 1), col)),
                pl.BlockSpec((None, ctx_len, width), lambda b, i: (b, 0, col))]

    def tile_variant(i):
        return jnp.where(i == 0, 0, jnp.where(i == nt - 1, 2, 1))

    return pl.pallas_call(
        functools.partial(_nbr_kernel, tq=tq, bounded=bounded),
        grid=(bsz, nt),
        in_specs=[pl.BlockSpec((None, tq, width), lambda b, i: (b, i, QK_OFF["bq"] // width))]
                 + kv_specs(QK_OFF["bk"] // width) + kv_specs(REST_OFF["bv"] // width)
                 + [pl.BlockSpec(bias.shape, lambda b, i: (0, 0, 0)),
                    pl.BlockSpec((None,) + row_masks.shape[1:], lambda b, i: (tile_variant(i), 0, 0))],
        out_specs=pl.BlockSpec((None, tq, BRANCH_W), lambda b, i: (b, i, 0)),
        out_shape=jax.ShapeDtypeStruct((bsz, seq, BRANCH_W), BF16),
        compiler_params=_cparams(("parallel", "parallel")),
        name="mixer_b_neighbourhood_bounded" if bounded else "mixer_b_neighbourhood",
    )(qk_lat, qk_lat, qk_lat, qk_lat, qk_ctx, rest_lat, rest_lat, rest_lat, rest_ctx, bias, row_masks)


def _flash_update(s, v, m_ref, l_ref, acc_ref, idx):
    m_prev = m_ref[idx]
    m_new = jnp.maximum(m_prev, jnp.max(s, axis=-1, keepdims=True))
    alpha = jnp.exp2(m_prev - m_new)
    p = jnp.exp2(s - m_new)
    if l_ref is not None:
        l_ref[idx] = alpha * l_ref[idx] + _lane_partial_sums(p)
    acc_ref[idx] = alpha * acc_ref[idx] + _dot(p.astype(BF16), v)
    m_ref[idx] = m_new


def _dense_c_kernel(q_ref, kl_ref, vl_ref, kx_ref, vx_ref, o_ref, qs_ref, acc_ref, m_ref, *, tq, nk_lat):
    kt = pl.program_id(2)
    g = C_HEADS // C_KV

    @pl.when(kt == 0)
    def _():
        q = q_ref[...]
        for kv in range(C_KV):
            qs_ref[kv] = _stack_heads(q, kv * g, g)
        acc_ref[...] = jnp.zeros(acc_ref.shape, F32)
        m_ref[...] = jnp.full(m_ref.shape, -jnp.inf, F32)

    def step(k, v):
        for kv in range(C_KV):
            sl = slice(kv * HEAD_DIM, (kv + 1) * HEAD_DIM)
            _flash_update(_dot_nt(qs_ref[kv], k[:, sl]), _with_ones_column(v[:, sl]), m_ref, None, acc_ref, kv)

    @pl.when(kt < nk_lat)
    def _():
        step(kl_ref[...], vl_ref[...])

    @pl.when(kt == nk_lat)
    def _():
        step(kx_ref[...], vx_ref[...])
        for kv in range(C_KV):
            acc = acc_ref[kv]
            o = acc[:, :HEAD_DIM] * (1.0 / acc[:, HEAD_DIM:HEAD_DIM + 1])
            for h in range(g):
                c0 = (kv * g + h) * HEAD_DIM
                o_ref[:, c0:c0 + HEAD_DIM] = o[h * tq:(h + 1) * tq].astype(BF16)


def _dense_c_bounded_kernel(q_ref, kl_ref, vl_ref, kx_ref, vx_ref, o_ref, qs_ref, acc_ref, *, tq, nk_lat):
    kt = pl.program_id(2)
    g = C_HEADS // C_KV

    @pl.when(kt == 0)
    def _():
        q = q_ref[...]
        for kv in range(C_KV):
            qs_ref[kv] = _stack_heads(q, kv * g, g)
        acc_ref[...] = jnp.zeros(acc_ref.shape, F32)

    def step(k, v):
        n = k.shape[0]
        chunk = min(n, DENSE_KEY_CHUNK)
        for kv in range(C_KV):
            sl = slice(kv * HEAD_DIM, (kv + 1) * HEAD_DIM)
            upd = None
            for k0 in range(0, n, chunk):
                ks = slice(k0, k0 + chunk)
                p_t = jnp.exp2(_dot_nt(k[ks, sl], qs_ref[kv])).astype(BF16)
                term = lax.dot_general(_with_ones_column(v[ks, sl], VT_PAD), p_t, (((0,), (0,)), ((), ())),
                                       preferred_element_type=F32)
                upd = term if upd is None else upd + term
            acc_ref[kv] += upd

    @pl.when(kt < nk_lat)
    def _():
        step(kl_ref[...], vl_ref[...])

    @pl.when(kt == nk_lat)
    def _():
        step(kx_ref[...], vx_ref[...])
        for kv in range(C_KV):
            acc = acc_ref[kv]
            o = jnp.transpose(acc[:HEAD_DIM] * (1.0 / acc[HEAD_DIM:HEAD_DIM + 1]))
            for h in range(g):
                c0 = (kv * g + h) * HEAD_DIM
                o_ref[:, c0:c0 + HEAD_DIM] = o[h * tq:(h + 1) * tq].astype(BF16)


def _dense_gqa(qk_lat, rest_lat, qk_ctx, rest_ctx, tq, tk, bounded):
    bsz, seq, _ = qk_lat.shape
    ctx_len = qk_ctx.shape[1]
    nk_lat = seq // tk
    g = C_HEADS // C_KV
    kcol = QK_OFF["ck"] // LANES
    vcol = REST_OFF["cv"] // LANES
    if bounded:
        body = functools.partial(_dense_c_bounded_kernel, tq=tq, nk_lat=nk_lat)
        scratch = [pltpu.VMEM((C_KV, g * tq, HEAD_DIM), BF16),
                   pltpu.VMEM((C_KV, HEAD_DIM + VT_PAD, g * tq), F32)]
    else:
        body = functools.partial(_dense_c_kernel, tq=tq, nk_lat=nk_lat)
        scratch = [pltpu.VMEM((C_KV, g * tq, HEAD_DIM), BF16),
                   pltpu.VMEM((C_KV, g * tq, LANES), F32),
                   pltpu.VMEM((C_KV, g * tq, 1), F32)]
    return pl.pallas_call(
        body,
        grid=(bsz, seq // tq, nk_lat + 1),
        in_specs=[pl.BlockSpec((None, tq, C_HEADS * HEAD_DIM), lambda b, i, kt: (b, i, QK_OFF["cq"] // 512)),
                  pl.BlockSpec((None, tk, LANES), lambda b, i, kt: (b, jnp.minimum(kt, nk_lat - 1), kcol)),
                  pl.BlockSpec((None, tk, LANES), lambda b, i, kt: (b, jnp.minimum(kt, nk_lat - 1), vcol)),
                  pl.BlockSpec((None, ctx_len, LANES), lambda b, i, kt: (b, 0, kcol)),
                  pl.BlockSpec((None, ctx_len, LANES), lambda b, i, kt: (b, 0, vcol))],
        out_specs=pl.BlockSpec((None, tq, BRANCH_W), lambda b, i, kt: (b, i, 0)),
        out_shape=jax.ShapeDtypeStruct((bsz, seq, BRANCH_W), BF16),
        scratch_shapes=scratch,
        compiler_params=_cparams(("parallel", "parallel", "arbitrary")),
        name="mixer_c_dense_bounded" if bounded else "mixer_c_dense",
    )(qk_lat, qk_lat, rest_lat, qk_ctx, rest_ctx)


def _dense_d_kernel(q_ref, kl_ref, vl_ref, kx_ref, vx_ref, lam_ref, subln_ref, o_ref, qs_ref, l_ref, acc_ref,
                    m_ref, *, nk_lat, lam_init):
    kt = pl.program_id(2)
    n_sc = 2 * D_HEADS

    @pl.when(kt == 0)
    def _():
        q = q_ref[...]
        for idx in range(n_sc):
            qs_ref[idx] = q[:, idx * HEAD_DIM:(idx + 1) * HEAD_DIM]
        l_ref[...] = jnp.zeros(l_ref.shape, F32)
        acc_ref[...] = jnp.zeros(acc_ref.shape, F32)
        m_ref[...] = jnp.full(m_ref.shape, -jnp.inf, F32)

    def step(k, v):
        for idx in range(n_sc):
            h = idx // 2
            s = _dot_nt(qs_ref[idx], k[:, idx * HEAD_DIM:(idx + 1) * HEAD_DIM])
            _flash_update(s, v[:, h * 2 * HEAD_DIM:(h + 1) * 2 * HEAD_DIM], m_ref, l_ref, acc_ref, idx)

    @pl.when(kt < nk_lat)
    def _():
        step(kl_ref[...], vl_ref[...])

    @pl.when(kt == nk_lat)
    def _():
        step(kx_ref[...], vx_ref[...])
        lam = _diff_lambda(lam_ref, lam_init)
        for h in range(D_HEADS):
            l1 = jnp.sum(l_ref[2 * h], axis=-1, keepdims=True)
            l2 = jnp.sum(l_ref[2 * h + 1], axis=-1, keepdims=True)
            o = acc_ref[2 * h] / l1 - lam * (acc_ref[2 * h + 1] / l2)
            o_ref[:, h * 2 * HEAD_DIM:(h + 1) * 2 * HEAD_DIM] = _finish_diff(o, subln_ref, lam_init).astype(BF16)


def _dense_d_bounded_kernel(q_ref, kl_ref, vl_ref, kx_ref, vx_ref, lam_ref, subln_ref, o_ref, qs_ref, acc_ref, *,
                            nk_lat, lam_init):
    kt = pl.program_id(2)
    n_sc = 2 * D_HEADS
    vw = 2 * HEAD_DIM

    @pl.when(kt == 0)
    def _():
        q = q_ref[...]
        for idx in range(n_sc):
            qs_ref[idx] = q[:, idx * HEAD_DIM:(idx + 1) * HEAD_DIM]
        acc_ref[...] = jnp.zeros(acc_ref.shape, F32)

    def step(k, v):
        n = k.shape[0]
        chunk = min(n, 2 * DENSE_KEY_CHUNK)
        lane = lax.broadcasted_iota(jnp.int32, (chunk, VT_PAD), 1)
        ones_cols = jnp.where(lane == 0, 1.0, 0.0).astype(BF16)
        for h in range(D_HEADS):
            upd = [None, None]
            for k0 in range(0, n, chunk):
                ks = slice(k0, k0 + chunk)
                v1 = jnp.concatenate([v[ks, h * vw:(h + 1) * vw], ones_cols], axis=1)
                for c in range(2):
                    idx = 2 * h + c
                    p_t = jnp.exp2(_dot_nt(k[ks, idx * HEAD_DIM:(idx + 1) * HEAD_DIM], qs_ref[idx])).astype(BF16)
                    term = lax.dot_general(v1, p_t, (((0,), (0,)), ((), ())), preferred_element_type=F32)
                    upd[c] = term if upd[c] is None else upd[c] + term
            for c in range(2):
                acc_ref[2 * h + c] += upd[c]

    @pl.when(kt < nk_lat)
    def _():
        step(kl_ref[...], vl_ref[...])

    @pl.when(kt == nk_lat)
    def _():
        step(kx_ref[...], vx_ref[...])
        lam = _diff_lambda(lam_ref, lam_init)
        for h in range(D_HEADS):
            a1, a2 = acc_ref[2 * h], acc_ref[2 * h + 1]
            o_t = a1[:vw] * (1.0 / a1[vw:vw + 1]) - lam * (a2[:vw] * (1.0 / a2[vw:vw + 1]))
            o_ref[:, h * vw:(h + 1) * vw] = _finish_diff(jnp.transpose(o_t), subln_ref, lam_init).astype(BF16)


def _dense_diff(qk_lat, rest_lat, qk_ctx, rest_ctx, lam_d, subln, lam_init, tq, tk, bounded):
    bsz, seq, _ = qk_lat.shape
    ctx_len = qk_ctx.shape[1]
    nk_lat = seq // tk
    width = D_HEADS * 2 * HEAD_DIM
    kcol = QK_OFF["dk"] // width
    vcol = REST_OFF["dv"] // width
    if bounded:
        body = functools.partial(_dense_d_bounded_kernel, nk_lat=nk_lat, lam_init=lam_init)
        scratch = [pltpu.VMEM((2 * D_HEADS, tq, HEAD_DIM), BF16),
                   pltpu.VMEM((2 * D_HEADS, 2 * HEAD_DIM + VT_PAD, tq), F32)]
    else:
        body = functools.partial(_dense_d_kernel, nk_lat=nk_lat, lam_init=lam_init)
        scratch = [pltpu.VMEM((2 * D_HEADS, tq, HEAD_DIM), BF16),
                   pltpu.VMEM((2 * D_HEADS, tq, LANES), F32),
                   pltpu.VMEM((2 * D_HEADS, tq, 2 * HEAD_DIM), F32),
                   pltpu.VMEM((2 * D_HEADS, tq, 1), F32)]
    return pl.pallas_call(
        body,
        grid=(bsz, seq // tq, nk_lat + 1),
        in_specs=[pl.BlockSpec((None, tq, width), lambda b, i, kt: (b, i, QK_OFF["dq"] // width)),
                  pl.BlockSpec((None, tk, width), lambda b, i, kt: (b, jnp.minimum(kt, nk_lat - 1), kcol)),
                  pl.BlockSpec((None, tk, width), lambda b, i, kt: (b, jnp.minimum(kt, nk_lat - 1), vcol)),
                  pl.BlockSpec((None, ctx_len, width), lambda b, i, kt: (b, 0, kcol)),
                  pl.BlockSpec((None, ctx_len, width), lambda b, i, kt: (b, 0, vcol)),
                  pl.BlockSpec((4, HEAD_DIM), lambda b, i, kt: (0, 0)),
                  pl.BlockSpec((1, 2 * HEAD_DIM), lambda b, i, kt: (0, 0))],
        out_specs=pl.BlockSpec((None, tq, BRANCH_W), lambda b, i, kt: (b, i, 0)),
        out_shape=jax.ShapeDtypeStruct((bsz, seq, BRANCH_W), BF16),
        scratch_shapes=scratch,
        compiler_params=_cparams(("parallel", "parallel", "arbitrary")),
        name="mixer_d_diff_bounded" if bounded else "mixer_d_diff",
    )(qk_lat, qk_lat, rest_lat, qk_ctx, rest_ctx, lam_d, subln.reshape(1, 2 * HEAD_DIM))


def _ctx_kernel(sink_ref, qk_ref, v_ref, lam_ref, subln_ref, o_ref, *, lam_init):
    qk = qk_ref[...]
    vals = v_ref[...]
    ctx_len = qk.shape[0]
    v_off = {name: REST_OFF[name] - REST_OFF[_FIRST_VALUE] for name in ("av", "cv", "bv", "dv")}

    def cols(name, start, width):
        c0 = QK_OFF[name] + start
        return qk[:, c0:c0 + width]

    def gqa(qname, kname, vname, out_off, n_kv, with_sink):
        g = 8 // n_kv
        for kv in range(n_kv):
            qs = _stack_heads(cols(qname, 0, 8 * HEAD_DIM), kv * g, g)
            s = _dot_nt(qs, cols(kname, kv * HEAD_DIM, HEAD_DIM))
            extra = None
            if with_sink:
                extra = jnp.concatenate([jnp.full((ctx_len, 1), sink_ref[kv * g + h], F32) for h in range(g)], axis=0)
            p, l = _softmax_rows(s, extra)
            vv = vals[:, v_off[vname] + kv * HEAD_DIM:v_off[vname] + (kv + 1) * HEAD_DIM]
            o = _dot(p.astype(BF16), vv) / l
            for h in range(g):
                c0 = out_off + (kv * g + h) * HEAD_DIM
                o_ref[:, c0:c0 + HEAD_DIM] = o[h * ctx_len:(h + 1) * ctx_len].astype(BF16)

    gqa("aq", "ak", "av", 0 * BRANCH_W, A_KV, True)
    gqa("bq", "bk", "bv", 1 * BRANCH_W, B_HEADS, False)
    gqa("cq", "ck", "cv", 2 * BRANCH_W, C_KV, False)

    lam = _diff_lambda(lam_ref, lam_init)
    for h in range(D_HEADS):
        base = h * 2 * HEAD_DIM
        p1, l1 = _softmax_rows(_dot_nt(cols("dq", base, HEAD_DIM), cols("dk", base, HEAD_DIM)))
        p2, l2 = _softmax_rows(_dot_nt(cols("dq", base + HEAD_DIM, HEAD_DIM), cols("dk", base + HEAD_DIM, HEAD_DIM)))
        pd = p1 / l1 - lam * (p2 / l2)
        o = _dot(pd.astype(BF16), vals[:, v_off["dv"] + base:v_off["dv"] + base + 2 * HEAD_DIM])
        c0 = 3 * BRANCH_W + base
        o_ref[:, c0:c0 + 2 * HEAD_DIM] = _finish_diff(o, subln_ref, lam_init).astype(BF16)


def _ctx_attention(qk_ctx, v_ctx, sink2, lam_d, subln, lam_init):
    bsz, ctx_len, _ = qk_ctx.shape
    return pl.pallas_call(
        functools.partial(_ctx_kernel, lam_init=lam_init),
        grid=(bsz,),
        in_specs=[pl.BlockSpec(memory_space=pltpu.SMEM),
                  pl.BlockSpec((None, ctx_len, QK_COLS), lambda b: (b, 0, 0)),
                  pl.BlockSpec((None, ctx_len, v_ctx.shape[2]), lambda b: (b, 0, 0)),
                  pl.BlockSpec((4, HEAD_DIM), lambda b: (0, 0)),
                  pl.BlockSpec((1, 2 * HEAD_DIM), lambda b: (0, 0))],
        out_specs=pl.BlockSpec((None, ctx_len, N_BRANCH * BRANCH_W), lambda b: (b, 0, 0)),
        out_shape=jax.ShapeDtypeStruct((bsz, ctx_len, N_BRANCH * BRANCH_W), BF16),
        compiler_params=_cparams(("parallel",)),
        name="ctx_attention",
    )(sink2, qk_ctx, v_ctx, lam_d, subln.reshape(1, 2 * HEAD_DIM))


def _merge_kernel(x_ref, mod_ref, ya_ref, yb_ref, yc_ref, yd_ref, gp_ref, mg_ref, wbr_ref, wout_ref, o_ref):
    gate = mod_ref[...][:, 2 * D_MODEL:]
    merged = None
    for n, y_ref in enumerate((ya_ref, yb_ref, yc_ref, yd_ref)):
        yg = (y_ref[...].astype(F32) * gp_ref[:, n * BRANCH_W:(n + 1) * BRANCH_W].astype(F32)).astype(BF16)
        term = mg_ref[:, n * D_MODEL:(n + 1) * D_MODEL].astype(F32) * _dot(yg, wbr_ref[n])
        merged = term if merged is None else merged + term
    o_ref[...] = x_ref[...] + gate * _dot(merged.astype(BF16), wout_ref[...])


def _merge(x2, mod3, ys, y_cols, rest, w_br, w_out, tm, row_of_tile):
    rows = x2.shape[0]
    y_specs = [pl.BlockSpec((tm, BRANCH_W), (lambda i, c=c: (i, c))) for c in y_cols]
    return pl.pallas_call(
        _merge_kernel,
        grid=(rows // tm,),
        in_specs=[pl.BlockSpec((tm, D_MODEL), lambda i: (i, 0)),
                  pl.BlockSpec((None, 1, 3 * D_MODEL), lambda i: (row_of_tile(i), 0, 0))]
                 + y_specs
                 + [pl.BlockSpec((tm, N_BRANCH * BRANCH_W), lambda i: (i, REST_OFF["ag"] // (N_BRANCH * BRANCH_W))),
                    pl.BlockSpec((tm, N_BRANCH * D_MODEL), lambda i: (i, 0)),
                    pl.BlockSpec((N_BRANCH, BRANCH_W, D_MODEL), lambda i: (0, 0, 0)),
                    pl.BlockSpec((D_MODEL, D_MODEL), lambda i: (0, 0))],
        out_specs=pl.BlockSpec((tm, D_MODEL), lambda i: (i, 0)),
        out_shape=jax.ShapeDtypeStruct((rows, D_MODEL), F32),
        compiler_params=_cparams(("parallel",)),
        name="gated_merge",
    )(x2, mod3, *ys, rest, rest, w_br, w_out)


def _rope_tables(seq):
    t = np.arange(seq, dtype=np.int32)
    pos = np.stack([t // GRID_W, t % GRID_W], axis=-1).astype(np.float32)
    n_freq = HEAD_DIM // 4
    freqs = (np.float32(ROPE_THETA) ** (-np.arange(n_freq, dtype=np.float32) / np.float32(n_freq))).astype(np.float32)
    ang = (pos[:, :, None] * freqs[None, None, :]).astype(np.float32)
    ang = np.concatenate([ang, ang], axis=-1).reshape(seq, HEAD_DIM)
    sign = np.where((np.arange(HEAD_DIM) % 32) < 16, -1.0, 1.0).astype(np.float32)
    reps = NORM_GROUP // HEAD_DIM
    cos = np.tile(np.cos(ang).astype(np.float32), (1, reps))
    sin = np.tile(np.sin(ang).astype(np.float32) * sign, (1, reps))
    return jnp.asarray(cos), jnp.asarray(sin)


def _regroup_cols(w, order, total):
    parts = [w[:, _ORIG[n][0]:_ORIG[n][1]] for n in order]
    used = sum(p.shape[1] for p in parts)
    if total > used:
        parts.append(jnp.zeros((w.shape[0], total - used), w.dtype))
    return jnp.concatenate(parts, axis=1)


def _qk_gain_row(g):
    parts = []
    for name in _QK_ORDER:
        mixer = "abcd".index(name[0])
        is_q = name[1] == "q"
        width = _ORIG[name][1] - _ORIG[name][0]
        gain = g[mixer, 0] * (QK_SCALE * LOG2E) if is_q else g[mixer, 1]
        parts.append(jnp.tile(gain, width // HEAD_DIM))
    parts.append(jnp.ones((QK_COLS - _QK_USED,), g.dtype))
    return jnp.concatenate(parts).reshape(1, QK_COLS).astype(F32)


def kernel(x, c, ctx, c_ctx, norm_w, w_ada, b_ada, w_in, qk_gain, sink_a, rpb_b, lam_d, subln_d, w_br, w_out):
    bsz, seq, _ = x.shape
    ctx_len = ctx.shape[1]
    depth = w_ada.shape[0]
    assert seq % (2 * A_WINDOW) == 0 and ctx_len % LANES == 0 and bsz <= 6

    assert seq % TM_REST == 0 and seq % TK_DENSE == 0 and seq % TQ_DENSE == 0
    tiles_per_batch = seq // TM_QK
    merge_tiles_per_batch = seq // TM_MERGE

    cvec = jnp.concatenate([c, c_ctx[None, :], jnp.zeros((8 - bsz - 1, D_MODEL), F32)], axis=0)
    mod_all = _ada(cvec, w_ada, b_ada)
    cos, sin = _rope_tables(seq)
    ones_bd = jnp.asarray(np.kron(np.eye(NORM_GROUP // HEAD_DIM), np.ones((HEAD_DIM, HEAD_DIM))), BF16)

    x2 = x.reshape(bsz * seq, D_MODEL)
    c2 = ctx.reshape(bsz * ctx_len, D_MODEL)
    for l in range(depth):
        need_ctx = l < depth - 1
        lam_init = 0.8 - 0.6 * math.exp(-0.3 * l)
        mod3 = mod_all[l].reshape(8, 1, 3 * D_MODEL)
        w_qk = _regroup_cols(w_in[l], _QK_ORDER, QK_COLS).astype(BF16)
        w_rest = _regroup_cols(w_in[l], _REST_ORDER, REST_COLS).astype(BF16)
        gain_row = _qk_gain_row(qk_gain[l])

        hx = _prenorm(x2, norm_w[l], mod3, TM_QK, lambda i: i // tiles_per_batch)
        hc = _prenorm(c2, norm_w[l], mod3, bsz * ctx_len, lambda i: bsz)
        qk_lat = _proj_qk(hx, w_qk, gain_row, ones_bd, TM_QK, cos, sin, tiles_per_batch).reshape(bsz, seq, QK_COLS)
        qk_ctx = _proj_qk(hc, w_qk, gain_row, ones_bd, bsz * ctx_len).reshape(bsz, ctx_len, QK_COLS)
        rest_lat = _proj_rest(hx, w_rest, TM_REST).reshape(bsz, seq, REST_COLS)
        rest_ctx = _proj_rest(hc, w_rest, bsz * ctx_len).reshape(bsz, ctx_len, REST_COLS)

        sink2 = sink_a[l] * LOG2E
        rpb2 = rpb_b[l] * LOG2E
        proj = (qk_lat, rest_lat, qk_ctx, rest_ctx)

        def qk_bound(g):
            return HEAD_DIM * QK_SCALE * LOG2E * jnp.max(jnp.abs(g[0])) * jnp.max(jnp.abs(g[1])) * 1.02

        y_a = lax.cond(qk_bound(qk_gain[l, 0]) <= LOGIT_BOUND,
                       lambda: _window_attention(*proj, sink2, True),
                       lambda: _window_attention(*proj, sink2, False))
        y_b = lax.cond(qk_bound(qk_gain[l, 1]) + jnp.max(jnp.abs(rpb_b[l])) * LOG2E <= LOGIT_BOUND,
                       lambda: _neighborhood_attention(*proj, rpb2, TQ_NBR, True),
                       lambda: _neighborhood_attention(*proj, rpb2, TQ_NBR, False))
        y_c = lax.cond(qk_bound(qk_gain[l, 2]) <= LOGIT_BOUND,
                       lambda: _dense_gqa(*proj, TQ_DENSE, TK_DENSE, True),
                       lambda: _dense_gqa(*proj, TQ_DENSE_ROWMAX, TK_DENSE, False))
        y_d = lax.cond(qk_bound(qk_gain[l, 3]) <= LOGIT_BOUND,
                       lambda: _dense_diff(*proj, lam_d[l], subln_d[l], lam_init, TQ_DENSE, TK_DENSE, True),
                       lambda: _dense_diff(*proj, lam_d[l], subln_d[l], lam_init, TQ_DENSE_ROWMAX, TK_DENSE, False))

        w_br_l = w_br[l].astype(BF16)
        w_out_l = w_out[l].astype(BF16)
        ys = [y.reshape(bsz * seq, BRANCH_W) for y in (y_a, y_b, y_c, y_d)]
        x_new = _merge(x2, mod3, ys, (0, 0, 0, 0), rest_lat.reshape(bsz * seq, REST_COLS), w_br_l, w_out_l,
                       TM_MERGE, lambda i: i // merge_tiles_per_batch)
        if need_ctx:
            y_ctx = _ctx_attention(qk_ctx, rest_ctx[:, :, REST_OFF[_FIRST_VALUE]:], sink2, lam_d[l], subln_d[l],
                                   lam_init)
            y_ctx2 = y_ctx.reshape(bsz * ctx_len, N_BRANCH * BRANCH_W)
            c2 = _merge(c2, mod3, [y_ctx2] * 4, (0, 1, 2, 3), rest_ctx.reshape(bsz * ctx_len, REST_COLS),
                        w_br_l, w_out_l, bsz * ctx_len, lambda i: bsz)
        x2 = x_new
    return x2.reshape(bsz, seq, D_MODEL)
```
